```python
import math
import jax, jax.numpy as jnp
from jax import lax
import numpy as np

D_MODEL = 4096
BATCH = 4
SEQ = 4096
DEPTH = 1

CONV_DIM = 2048
CONV_WIDTH = 31
N_HEADS = 16
N_KV = 4
HPG = N_HEADS // N_KV
HEAD_DIM = 128
CMP_LEN = 32
CMP_STRIDE = 16
CMP_HID = 512
SLC_BLOCK = 64
N_SEL = 16
WINDOW = 512
Q_BLOCK = 64
REL_BUCKETS = 32
REL_MAX_DIST = 128
D_FF = -(-8 * D_MODEL // (3 * 256)) * 256
NORM_EPS = 1e-6
NEG = -1e30
FORCE_SCORE = 1e9

Q_WIDTH = N_HEADS * HEAD_DIM
KV_WIDTH = N_KV * HEAD_DIM
IN_SIZES = (CONV_DIM, CONV_DIM, Q_WIDTH,
            KV_WIDTH, KV_WIDTH, KV_WIDTH, KV_WIDTH, KV_WIDTH, KV_WIDTH,
            3 * N_HEADS, 2 * D_MODEL)
N_IN = sum(IN_SIZES)
IN_OFFSETS = tuple(int(v) for v in np.cumsum(IN_SIZES)[:-1])

kernel_name = "hybrid_conformer_nsa_gated_block"


def rms_norm(x, g):
    x32 = x.astype(jnp.float32)
    y = x32 * lax.rsqrt(jnp.mean(x32 * x32, axis=-1, keepdims=True) + NORM_EPS)
    return (y * g.astype(jnp.float32)).astype(x.dtype)


def layer_norm(x, g, b):
    x32 = x.astype(jnp.float32)
    mu = jnp.mean(x32, axis=-1, keepdims=True)
    xc = x32 - mu
    y = xc * lax.rsqrt(jnp.mean(xc * xc, axis=-1, keepdims=True) + NORM_EPS)
    return (y * g.astype(jnp.float32) + b.astype(jnp.float32)).astype(x.dtype)


def t5_bucket(dist):
    n = jnp.maximum(dist, 0)
    max_exact = REL_BUCKETS // 2
    nf = jnp.maximum(n, 1).astype(jnp.float32)
    large = max_exact + (jnp.log(nf / max_exact) / math.log(REL_MAX_DIST / max_exact)
                         * (REL_BUCKETS - max_exact)).astype(jnp.int32)
    large = jnp.minimum(large, REL_BUCKETS - 1)
    return jnp.where(n < max_exact, n, large)


def bias_qk(rel_bias, dist):
    q_len, k_len = dist.shape
    b = rel_bias[t5_bucket(dist)].astype(jnp.float32)
    return b.reshape(q_len, k_len, N_KV, HPG).transpose(2, 3, 0, 1)


def masked_softmax(logits, mask):
    logits = jnp.where(mask, logits.astype(jnp.float32), NEG)
    p = jax.nn.softmax(logits, axis=-1)
    return jnp.where(mask, p, 0.0)


def conv_module(a, b_gate, w_dw, b_dw, ln_g, ln_b, w_out):
    u = a * jax.nn.sigmoid(b_gate)
    u = lax.conv_general_dilated(u, w_dw[:, None, :].astype(u.dtype), (1,), [(CONV_WIDTH - 1, 0)],
                                 dimension_numbers=('NWC', 'WIO', 'NWC'),
                                 feature_group_count=CONV_DIM) + b_dw
    u = layer_norm(u, ln_g, ln_b)
    return jax.nn.silu(u) @ w_out


def compress(k, pe, w1, w2):
    s = k.shape[2]
    nc = (s - CMP_LEN) // CMP_STRIDE + 1
    idx = jnp.arange(nc)[:, None] * CMP_STRIDE + jnp.arange(CMP_LEN)[None, :]
    blk = k[:, :, idx] + pe
    flat = blk.reshape(blk.shape[0], blk.shape[1], nc, CMP_LEN * HEAD_DIM)
    return jax.nn.gelu(flat @ w1) @ w2


def nsa_attention(q, gates, kc, vc, ks, vs, kw, vw, pe_k, pe_v, wk1, wk2, wv1, wv2, rel_bias):
    b, s = q.shape[0], q.shape[1]
    n_qb = s // Q_BLOCK
    to_g = lambda t: t.transpose(0, 2, 1, 3)
    kc, vc, ks, vs, kw, vw = to_g(kc), to_g(vc), to_g(ks), to_g(vs), to_g(kw), to_g(vw)

    k_cmp = compress(kc, pe_k, wk1, wk2)
    v_cmp = compress(vc, pe_v, wv1, wv2)
    nc = k_cmp.shape[2]
    cmp_start = jnp.arange(nc) * CMP_STRIDE
    cmp_end = cmp_start + CMP_LEN - 1

    n_slc = s // SLC_BLOCK
    n_sel = min(N_SEL, n_slc)
    slc_start = jnp.arange(n_slc) * SLC_BLOCK
    overlap = ((cmp_start[:, None] < slc_start[None, :] + SLC_BLOCK)
               & (cmp_start[:, None] + CMP_LEN > slc_start[None, :])).astype(jnp.float32)
    ks_blk = ks.reshape(b, N_KV, n_slc, SLC_BLOCK, HEAD_DIM)
    vs_blk = vs.reshape(b, N_KV, n_slc, SLC_BLOCK, HEAD_DIM)
    kw_pad = jnp.pad(kw, ((0, 0), (0, 0), (WINDOW, 0), (0, 0)))
    vw_pad = jnp.pad(vw, ((0, 0), (0, 0), (WINDOW, 0), (0, 0)))

    qb = (q * (HEAD_DIM ** -0.5)).reshape(b, n_qb, Q_BLOCK, N_KV, HPG, HEAD_DIM).transpose(1, 0, 3, 4, 2, 5)
    gb = gates.reshape(b, n_qb, Q_BLOCK, N_KV, HPG, 3).transpose(1, 0, 3, 4, 2, 5)
    bi = jnp.arange(b)[:, None, None, None]
    gi = jnp.arange(N_KV)[None, :, None, None]
    table_g = rel_bias.reshape(REL_BUCKETS, N_KV, HPG)
    sidx = jnp.arange(n_slc)

    def block(args):
        qj, gj, j = args
        t = j * Q_BLOCK + jnp.arange(Q_BLOCK)

        lc = jnp.einsum('bghqd,bgnd->bghqn', qj, k_cmp).astype(jnp.float32) \
            + bias_qk(rel_bias, t[:, None] - cmp_end[None, :])
        p_cmp = masked_softmax(lc, cmp_end[None, :] <= t[:, None])
        o_cmp = jnp.einsum('bghqn,bgnd->bghqd', p_cmp.astype(v_cmp.dtype), v_cmp)

        imp = jnp.einsum('bghqn,ns->bgqs', p_cmp, overlap)
        cur = t // SLC_BLOCK
        forced = (sidx[None, :] == 0) | (sidx[None, :] == cur[:, None]) | (sidx[None, :] == cur[:, None] - 1)
        causal_blk = slc_start[None, :] <= t[:, None]
        score = jnp.where(forced, FORCE_SCORE, jnp.where(causal_blk, imp, -FORCE_SCORE))
        _, sel = lax.top_k(score, n_sel)
        k_sel = ks_blk[bi, gi, sel].reshape(b, N_KV, Q_BLOCK, n_sel * SLC_BLOCK, HEAD_DIM)
        v_sel = vs_blk[bi, gi, sel].reshape(b, N_KV, Q_BLOCK, n_sel * SLC_BLOCK, HEAD_DIM)
        pos = (sel[..., None] * SLC_BLOCK + jnp.arange(SLC_BLOCK)).reshape(b, N_KV, Q_BLOCK, n_sel * SLC_BLOCK)
        dist = t[None, None, :, None] - pos
        b_sel = table_g[t5_bucket(dist), gi].astype(jnp.float32).transpose(0, 1, 4, 2, 3)
        ls = jnp.einsum('bghqd,bgqkd->bghqk', qj, k_sel).astype(jnp.float32) + b_sel
        p_s = masked_softmax(ls, (dist >= 0)[:, :, None])
        o_slc = jnp.einsum('bghqk,bgqkd->bghqd', p_s.astype(v_sel.dtype), v_sel)

        start = j * Q_BLOCK
        k_win = lax.dynamic_slice_in_dim(kw_pad, start, WINDOW + Q_BLOCK, axis=2)
        v_win = lax.dynamic_slice_in_dim(vw_pad, start, WINDOW + Q_BLOCK, axis=2)
        pos_w = start - WINDOW + jnp.arange(WINDOW + Q_BLOCK)
        dist_w = t[:, None] - pos_w[None, :]
        mask_w = (pos_w[None, :] >= 0) & (dist_w >= 0) & (dist_w < WINDOW)
        lw = jnp.einsum('bghqd,bgkd->bghqk', qj, k_win).astype(jnp.float32) + bias_qk(rel_bias, dist_w)
        p_w = masked_softmax(lw, mask_w)
        o_win = jnp.einsum('bghqk,bgkd->bghqd', p_w.astype(v_win.dtype), v_win)

        g = jax.nn.sigmoid(gj.astype(jnp.float32)).astype(qj.dtype)
        return g[..., 0:1] * o_cmp + g[..., 1:2] * o_slc + g[..., 2:3] * o_win

    o = lax.map(block, (qb, gb, jnp.arange(n_qb)))
    return o.transpose(1, 0, 4, 2, 3, 5).reshape(b, s, Q_WIDTH)


def setup_inputs(seed: int = 0) -> dict:
    key = jax.random.key(seed)
    ks = jax.random.split(key, 32)
    nrm = lambda k, shape, sc: jax.random.normal(k, shape, jnp.float32) * sc
    L = DEPTH
    return {
        "x": nrm(ks[0], (BATCH, SEQ, D_MODEL), 1.0),
        "c": nrm(ks[1], (BATCH, D_MODEL), 1.0),
        "w_ada": nrm(ks[2], (L, D_MODEL, 6 * D_MODEL), D_MODEL ** -0.5),
        "b_ada": nrm(ks[3], (L, 6 * D_MODEL), 0.01),
        "g_pre_mix": 1.0 + nrm(ks[4], (L, D_MODEL), 0.01),
        "g_post_mix": 1.0 + nrm(ks[5], (L, D_MODEL), 0.01),
        "g_pre_ffn": 1.0 + nrm(ks[6], (L, D_MODEL), 0.01),
        "g_post_ffn": 1.0 + nrm(ks[7], (L, D_MODEL), 0.01),
        "w_in": nrm(ks[8], (L, D_MODEL, N_IN), D_MODEL ** -0.5),
        "w_dw": nrm(ks[9], (L, CONV_WIDTH, CONV_DIM), CONV_WIDTH ** -0.5),
        "b_dw": nrm(ks[10], (L, CONV_DIM), 0.01),
        "conv_ln_g": 1.0 + nrm(ks[11], (L, CONV_DIM), 0.01),
        "conv_ln_b": nrm(ks[12], (L, CONV_DIM), 0.01),
        "w_conv_out": nrm(ks[13], (L, CONV_DIM, D_MODEL), CONV_DIM ** -0.5),
        "cmp_pe_k": nrm(ks[14], (L, CMP_LEN, HEAD_DIM), 0.1),
        "cmp_pe_v": nrm(ks[15], (L, CMP_LEN, HEAD_DIM), 0.1),
        "w_cmp_k1": nrm(ks[16], (L, CMP_LEN * HEAD_DIM, CMP_HID), (CMP_LEN * HEAD_DIM) ** -0.5),
        "w_cmp_k2": nrm(ks[17], (L, CMP_HID, HEAD_DIM), CMP_HID ** -0.5),
        "w_cmp_v1": nrm(ks[18], (L, CMP_LEN * HEAD_DIM, CMP_HID), (CMP_LEN * HEAD_DIM) ** -0.5),
        "w_cmp_v2": nrm(ks[19], (L, CMP_HID, HEAD_DIM), CMP_HID ** -0.5),
        "rel_bias": nrm(ks[20], (REL_BUCKETS, N_HEADS), 0.1),
        "w_nsa_out": nrm(ks[21], (L, Q_WIDTH, D_MODEL), Q_WIDTH ** -0.5),
        "w_out": nrm(ks[22], (L, D_MODEL, D_MODEL), D_MODEL ** -0.5),
        "w_ffn_gate": nrm(ks[23], (L, D_MODEL, D_FF), D_MODEL ** -0.5),
        "w_ffn_up": nrm(ks[24], (L, D_MODEL, D_FF), D_MODEL ** -0.5),
        "w_ffn_down": nrm(ks[25], (L, D_FF, D_MODEL), D_FF ** -0.5),
    }


def reference(x, c, w_ada, b_ada, g_pre_mix, g_post_mix, g_pre_ffn, g_post_ffn, w_in, w_dw, b_dw,
              conv_ln_g, conv_ln_b, w_conv_out, cmp_pe_k, cmp_pe_v, w_cmp_k1, w_cmp_k2, w_cmp_v1, w_cmp_v2,
              rel_bias, w_nsa_out, w_out, w_ffn_gate, w_ffn_up, w_ffn_down):
    b, s, _ = x.shape
    for l in range(DEPTH):
        mod = jax.nn.silu(c) @ w_ada[l] + b_ada[l]
        shift1, scale1, gate1, shift2, scale2, gate2 = [m[:, None, :] for m in jnp.split(mod, 6, axis=-1)]

        h = rms_norm(x, g_pre_mix[l]) * (1.0 + scale1) + shift1
        proj = h @ w_in[l]
        ca, cb, q, kc, vc, ksl, vsl, kw, vw, ng, mg = jnp.split(proj, IN_OFFSETS, axis=-1)
        kv = lambda t: t.reshape(b, s, N_KV, HEAD_DIM)
        y_a = conv_module(ca, cb, w_dw[l], b_dw[l], conv_ln_g[l], conv_ln_b[l], w_conv_out[l])
        o_b = nsa_attention(q.reshape(b, s, N_HEADS, HEAD_DIM), ng.reshape(b, s, N_HEADS, 3),
                            kv(kc), kv(vc), kv(ksl), kv(vsl), kv(kw), kv(vw),
                            cmp_pe_k[l], cmp_pe_v[l], w_cmp_k1[l], w_cmp_k2[l], w_cmp_v1[l], w_cmp_v2[l],
                            rel_bias)
        y_b = o_b @ w_nsa_out[l]
        ga, gb = jnp.split(mg, 2, axis=-1)
        mix = (jax.nn.sigmoid(ga) * y_a + jax.nn.sigmoid(gb) * y_b) @ w_out[l]
        x = x + gate1 * rms_norm(mix, g_post_mix[l])

        h2 = rms_norm(x, g_pre_ffn[l]) * (1.0 + scale2) + shift2
        f = (jax.nn.silu(h2 @ w_ffn_gate[l]) * (h2 @ w_ffn_up[l])) @ w_ffn_down[l]
        x = x + gate2 * rms_norm(f, g_post_ffn[l])
    return x
```

```python
import functools
import math

import numpy as np
import jax
import jax.numpy as jnp
from jax import lax
from jax.experimental import pallas as pl
from jax.experimental.pallas import tpu as pltpu

F32 = jnp.float32
BF = jnp.bfloat16

CONV_DIM = 2048
CONV_WIDTH = 31
N_HEADS = 16
N_KV = 4
HPG = N_HEADS // N_KV
HEAD_DIM = 128
CMP_LEN = 32
CMP_STRIDE = 16
CMP_HID = 512
SLC_BLOCK = 64
N_SEL = 16
WINDOW = 512
Q_BLOCK = 64
REL_BUCKETS = 32
REL_MAX_DIST = 128
NORM_EPS = 1e-6
NEG = -1e30
FORCE_SCORE = 1e9

Q_WIDTH = N_HEADS * HEAD_DIM
KV_WIDTH = N_KV * HEAD_DIM
Q_OFF = 2 * CONV_DIM
KC_OFF = Q_OFF + Q_WIDTH
NG_OFF = KC_OFF + 6 * KV_WIDTH
MG_OFF = NG_OFF + 3 * N_HEADS

LANES = 128
VMEM_LIMIT = 56 * 1024 * 1024

CONV_HALO = 32
SLC_LOCAL_BLOCKS = 4
SLC_PAD = (SLC_LOCAL_BLOCKS - 1) * SLC_BLOCK
SLC_CHUNK_BLOCKS = 8
WIN_KEYS = 640
WIN_PAD = WIN_KEYS - Q_BLOCK
CMP_BAND = 16
CMP_BAND_LO = 12
PAD_FLAG_COL = 64


def _t5_thresholds():
    max_exact = REL_BUCKETS // 2
    thr = []
    for k in range(1, REL_BUCKETS - max_exact):
        n = max_exact
        while max_exact + int(math.log(n / max_exact) / math.log(REL_MAX_DIST / max_exact)
                              * (REL_BUCKETS - max_exact)) < max_exact + k:
            n += 1
        thr.append(n)
    return tuple(thr)


T5_THRESHOLDS = _t5_thresholds()


def _cparams(*sem):
    return pltpu.CompilerParams(dimension_semantics=sem, vmem_limit_bytes=VMEM_LIMIT)


def _silu(v):
    return v * jax.nn.sigmoid(v)


def _dot(a, b):
    return jnp.dot(a, b, preferred_element_type=F32)


def _dot_nt(a, b):
    return lax.dot_general(a, b, (((1,), (1,)), ((), ())), preferred_element_type=F32)


def _rms(v, g):
    return v * lax.rsqrt(jnp.mean(v * v, axis=-1, keepdims=True) + NORM_EPS) * g


def _mod_kernel(c_ref, w_ref, b_ref, o_ref):
    a = _silu(c_ref[...]).astype(BF)
    o_ref[...] = _dot(a, w_ref[...].astype(BF)) + b_ref[...]


def _modulation(c8, w_ada, b_ada, tn=512):
    rows, d = c8.shape
    n = w_ada.shape[1]
    return pl.pallas_call(
        _mod_kernel,
        grid=(n // tn,),
        in_specs=[pl.BlockSpec((rows, d), lambda j: (0, 0)),
                  pl.BlockSpec((d, tn), lambda j: (0, j)),
                  pl.BlockSpec((1, tn), lambda j: (0, j))],
        out_specs=pl.BlockSpec((rows, tn), lambda j: (0, j)),
        out_shape=jax.ShapeDtypeStruct((rows, n), F32),
        compiler_params=_cparams("parallel"),
    )(c8, w_ada, b_ada)


def _prenorm_kernel(x_ref, g_ref, mod_ref, h_ref, *, shift_row, scale_row):
    y = _rms(x_ref[0], g_ref[...])
    h = y * (1.0 + mod_ref[0, scale_row:scale_row + 1, :]) + mod_ref[0, shift_row:shift_row + 1, :]
    h_ref[0] = h.astype(BF)


def _prenorm(x, g, mod3, shift_row, scale_row, tr=256):
    b, s, d = x.shape
    return pl.pallas_call(
        functools.partial(_prenorm_kernel, shift_row=shift_row, scale_row=scale_row),
        grid=(b, s // tr),
        in_specs=[pl.BlockSpec((1, tr, d), lambda i, r: (i, r, 0)),
                  pl.BlockSpec((1, d), lambda i, r: (0, 0)),
                  pl.BlockSpec((1, 6, d), lambda i, r: (i, 0, 0))],
        out_specs=pl.BlockSpec((1, tr, d), lambda i, r: (i, r, 0)),
        out_shape=jax.ShapeDtypeStruct((b, s, d), BF),
        compiler_params=_cparams("parallel", "parallel"),
    )(x, g, mod3)


def _post_mix_kernel(y_ref, x_ref, gp_ref, gn_ref, mod_ref, xo_ref, h_ref, *, gate_row, shift_row, scale_row):
    x1 = x_ref[0] + mod_ref[0, gate_row:gate_row + 1, :] * _rms(y_ref[0], gp_ref[...])
    xo_ref[0] = x1
    h = _rms(x1, gn_ref[...]) * (1.0 + mod_ref[0, scale_row:scale_row + 1, :]) \
        + mod_ref[0, shift_row:shift_row + 1, :]
    h_ref[0] = h.astype(BF)


def _post_mix(y, x, g_post, g_next, mod3, gate_row, shift_row, scale_row, tr=256):
    b, s, d = x.shape
    row = pl.BlockSpec((1, tr, d), lambda i, r: (i, r, 0))
    vec = pl.BlockSpec((1, d), lambda i, r: (0, 0))
    return pl.pallas_call(
        functools.partial(_post_mix_kernel, gate_row=gate_row, shift_row=shift_row, scale_row=scale_row),
        grid=(b, s // tr),
        in_specs=[row, row, vec, vec, pl.BlockSpec((1, 6, d), lambda i, r: (i, 0, 0))],
        out_specs=[row, row],
        out_shape=[jax.ShapeDtypeStruct((b, s, d), F32), jax.ShapeDtypeStruct((b, s, d), BF)],
        compiler_params=_cparams("parallel", "parallel"),
    )(y, x, g_post, g_next, mod3)


def _post_ffn_kernel(y_ref, x_ref, gp_ref, mod_ref, xo_ref, *, gate_row):
    xo_ref[0] = x_ref[0] + mod_ref[0, gate_row:gate_row + 1, :] * _rms(y_ref[0], gp_ref[...])


def _post_ffn(y, x, g_post, mod3, gate_row, tr=256):
    b, s, d = x.shape
    row = pl.BlockSpec((1, tr, d), lambda i, r: (i, r, 0))
    return pl.pallas_call(
        functools.partial(_post_ffn_kernel, gate_row=gate_row),
        grid=(b, s // tr),
        in_specs=[row, row, pl.BlockSpec((1, d), lambda i, r: (0, 0)),
                  pl.BlockSpec((1, 6, d), lambda i, r: (i, 0, 0))],
        out_specs=row,
        out_shape=jax.ShapeDtypeStruct((b, s, d), F32),
        compiler_params=_cparams("parallel", "parallel"),
    )(y, x, g_post, mod3)


def _mm_scale_kernel(a_ref, w_ref, s_ref, o_ref):
    o_ref[...] = (_dot(a_ref[...], w_ref[...]) * s_ref[...]).astype(o_ref.dtype)


def _mm_kernel(a_ref, w_ref, o_ref):
    o_ref[...] = _dot(a_ref[...], w_ref[...]).astype(o_ref.dtype)


def _matmul(a, w, out_dtype, tm, tn, col_scale=None):
    m, k = a.shape
    n = w.shape[1]
    in_specs = [pl.BlockSpec((tm, k), lambda i, j: (i, 0)),
                pl.BlockSpec((k, tn), lambda i, j: (0, j))]
    args = [a, w]
    body = _mm_kernel
    if col_scale is not None:
        in_specs.append(pl.BlockSpec((1, tn), lambda i, j: (0, j)))
        args.append(col_scale)
        body = _mm_scale_kernel
    return pl.pallas_call(
        body,
        grid=(m // tm, n // tn),
        in_specs=in_specs,
        out_specs=pl.BlockSpec((tm, tn), lambda i, j: (i, j)),
        out_shape=jax.ShapeDtypeStruct((m, n), out_dtype),
        compiler_params=_cparams("parallel", "arbitrary"),
    )(*args)


def _mix_kernel(a1_ref, a2_ref, w1_ref, w2_ref, ga_ref, gb_ref, o_ref):
    ya = _dot(a1_ref[...], w1_ref[...])
    yb = _dot(a2_ref[...], w2_ref[...])
    o = jax.nn.sigmoid(ga_ref[...].astype(F32)) * ya + jax.nn.sigmoid(gb_ref[...].astype(F32)) * yb
    o_ref[...] = o.astype(o_ref.dtype)


def _gated_mix(u_conv, o_nsa, w_conv_out, w_nsa_out, gates, d_model, tm=1024, tn=512):
    m, ka = u_conv.shape
    kb = o_nsa.shape[1]
    gb_block = d_model // tn
    return pl.pallas_call(
        _mix_kernel,
        grid=(m // tm, d_model // tn),
        in_specs=[pl.BlockSpec((tm, ka), lambda i, j: (i, 0)),
                  pl.BlockSpec((tm, kb), lambda i, j: (i, 0)),
                  pl.BlockSpec((ka, tn), lambda i, j: (0, j)),
                  pl.BlockSpec((kb, tn), lambda i, j: (0, j)),
                  pl.BlockSpec((tm, tn), lambda i, j: (i, j)),
                  pl.BlockSpec((tm, tn), lambda i, j: (i, j + gb_block))],
        out_specs=pl.BlockSpec((tm, tn), lambda i, j: (i, j)),
        out_shape=jax.ShapeDtypeStruct((m, d_model), BF),
        compiler_params=_cparams("parallel", "arbitrary"),
    )(u_conv, o_nsa, w_conv_out, w_nsa_out, gates, gates)


def _ffn_up_kernel(a_ref, wg_ref, wu_ref, o_ref):
    a = a_ref[...]
    o_ref[...] = (_silu(_dot(a, wg_ref[...])) * _dot(a, wu_ref[...])).astype(o_ref.dtype)


def _ffn_up(h, w_gate, w_up, tm=1024, tn=256):
    m, k = h.shape
    n = w_gate.shape[1]
    wspec = pl.BlockSpec((k, tn), lambda i, j: (0, j))
    return pl.pallas_call(
        _ffn_up_kernel,
        grid=(m // tm, n // tn),
        in_specs=[pl.BlockSpec((tm, k), lambda i, j: (i, 0)), wspec, wspec],
        out_specs=pl.BlockSpec((tm, tn), lambda i, j: (i, j)),
        out_shape=jax.ShapeDtypeStruct((m, n), BF),
        compiler_params=_cparams("parallel", "arbitrary"),
    )(h, w_gate, w_up)


def _conv_kernel(a_ref, b_ref, ah_ref, bh_ref, w_ref, bdw_ref, g_ref, bb_ref, o_ref, u_ref, v_ref, *, ts):
    i = pl.program_id(1)
    u_ref[CONV_HALO:CONV_HALO + ts, :] = a_ref[0].astype(F32) * jax.nn.sigmoid(b_ref[0].astype(F32))
    uh = ah_ref[0].astype(F32) * jax.nn.sigmoid(bh_ref[0].astype(F32))
    u_ref[0:CONV_HALO, :] = jnp.where(i > 0, uh, 0.0)

    rc, cc = 128, 256
    first = CONV_HALO - (CONV_WIDTH - 1)
    for r0 in range(0, ts, rc):
        for c0 in range(0, CONV_DIM, cc):
            acc = jnp.zeros((rc, cc), F32)
            for k in range(CONV_WIDTH):
                acc = acc + u_ref[r0 + first + k:r0 + first + k + rc, c0:c0 + cc] * w_ref[k:k + 1, c0:c0 + cc]
            v_ref[r0:r0 + rc, c0:c0 + cc] = acc + bdw_ref[:, c0:c0 + cc]

    v = v_ref[...]
    mu = jnp.mean(v, axis=-1, keepdims=True)
    vc = v - mu
    y = vc * lax.rsqrt(jnp.mean(vc * vc, axis=-1, keepdims=True) + NORM_EPS) * g_ref[...] + bb_ref[...]
    o_ref[0] = _silu(y).astype(o_ref.dtype)


def _conv_module(proj, w_dw, b_dw, ln_g, ln_b, ts=256):
    b, s, _ = proj.shape
    hb = ts // CONV_HALO
    cur_a = pl.BlockSpec((1, ts, CONV_DIM), lambda i, r: (i, r, 0))
    cur_b = pl.BlockSpec((1, ts, CONV_DIM), lambda i, r: (i, r, 1))
    halo_a = pl.BlockSpec((1, CONV_HALO, CONV_DIM), lambda i, r: (i, jnp.maximum(r * hb - 1, 0), 0))
    halo_b = pl.BlockSpec((1, CONV_HALO, CONV_DIM), lambda i, r: (i, jnp.maximum(r * hb - 1, 0), 1))
    vec = pl.BlockSpec((1, CONV_DIM), lambda i, r: (0, 0))
    return pl.pallas_call(
        functools.partial(_conv_kernel, ts=ts),
        grid=(b, s // ts),
        in_specs=[cur_a, cur_b, halo_a, halo_b,
                  pl.BlockSpec((CONV_WIDTH, CONV_DIM), lambda i, r: (0, 0)), vec, vec, vec],
        out_specs=pl.BlockSpec((1, ts, CONV_DIM), lambda i, r: (i, r, 0)),
        out_shape=jax.ShapeDtypeStruct((b, s, CONV_DIM), BF),
        scratch_shapes=[pltpu.VMEM((CONV_HALO + ts, CONV_DIM), F32), pltpu.VMEM((ts, CONV_DIM), F32)],
        compiler_params=_cparams("parallel", "arbitrary"),
    )(proj, proj, proj, proj, w_dw, b_dw, ln_g, ln_b)


def _compress_kernel(k_ref, pe_ref, w1_ref, w2_ref, o_ref):
    k16 = k_ref[0]
    half = CMP_STRIDE * HEAD_DIM
    lo = _dot(k16, w1_ref[0:half, :])
    hi = _dot(k16, w1_ref[half:2 * half, :])
    pe = _dot(pe_ref[...], w1_ref[...])
    rows = k16.shape[0]
    hid = lo + pltpu.roll(hi, rows - 1, axis=0) + pe[0:1, :]
    act = 0.5 * hid * (1.0 + jnp.tanh(math.sqrt(2.0 / math.pi) * (hid + 0.044715 * hid * hid * hid)))
    o_ref[0] = _dot(act.astype(BF), w2_ref[...])


def _compress(k16, pe8, w1, w2):
    n, rows, width = k16.shape
    return pl.pallas_call(
        _compress_kernel,
        grid=(n,),
        in_specs=[pl.BlockSpec((1, rows, width), lambda i: (i, 0, 0)),
                  pl.BlockSpec(pe8.shape, lambda i: (0, 0)),
                  pl.BlockSpec(w1.shape, lambda i: (0, 0)),
                  pl.BlockSpec(w2.shape, lambda i: (0, 0))],
        out_specs=pl.BlockSpec((1, rows, HEAD_DIM), lambda i: (i, 0, 0)),
        out_shape=jax.ShapeDtypeStruct((n, rows, HEAD_DIM), F32),
        compiler_params=_cparams("parallel"),
    )(k16, pe8, w1, w2)


def _bias_tables_kernel(rb_ref, wb_ref, sb_ref, cf_ref):
    h = pl.program_id(0)
    last = rb_ref[REL_BUCKETS - 1, h]

    def rel(dist):
        n = jnp.maximum(dist, 0)
        bucket = jnp.full(n.shape, REL_BUCKETS // 2, jnp.int32)
        for t in T5_THRESHOLDS:
            bucket = bucket + (n >= t).astype(jnp.int32)
        bucket = jnp.where(n < REL_BUCKETS // 2, n, jnp.minimum(bucket, REL_BUCKETS - 1))
        out = jnp.zeros(n.shape, F32)
        for bk in range(REL_BUCKETS - 1):
            out = jnp.where(bucket == bk, rb_ref[bk, h] - last, out)
        return out

    def dist_of(shape, offset, key_step):
        qi = lax.broadcasted_iota(jnp.int32, shape, 0)
        kl = lax.broadcasted_iota(jnp.int32, shape, 1)
        return qi + offset - key_step * kl, kl

    dw, _ = dist_of((Q_BLOCK, WIN_KEYS), WIN_PAD, 1)
    wb_ref[0] = jnp.where((dw >= 0) & (dw < WINDOW), rel(dw), NEG)

    ds_, _ = dist_of((Q_BLOCK, SLC_LOCAL_BLOCKS * SLC_BLOCK), SLC_PAD, 1)
    sb_ref[0] = jnp.where(ds_ >= 0, rel(ds_), NEG)

    qi = lax.broadcasted_iota(jnp.int32, (Q_BLOCK, LANES), 0)
    col = lax.broadcasted_iota(jnp.int32, (Q_BLOCK, LANES), 1)
    mi = col & (CMP_BAND - 1)
    part = col >> (CMP_BAND.bit_length() - 1)
    dc = qi - CMP_STRIDE * (mi - CMP_BAND_LO) - (CMP_LEN - 1)
    val = jnp.where(dc >= 0, rel(dc), 0.0)
    p0 = val.astype(BF).astype(F32)
    r1 = val - p0
    p1 = r1.astype(BF).astype(F32)
    p2 = r1 - p1
    parts = jnp.where(part == 0, p0, jnp.where(part == 1, p1, jnp.where(part == 2, p2, 0.0)))
    cf_ref[0] = parts.astype(BF)


def _bias_tables(rel_bias):
    smem = pl.BlockSpec(memory_space=pltpu.SMEM)
    lk = SLC_LOCAL_BLOCKS * SLC_BLOCK
    return pl.pallas_call(
        _bias_tables_kernel,
        grid=(N_HEADS,),
        in_specs=[smem],
        out_specs=[pl.BlockSpec((1, Q_BLOCK, WIN_KEYS), lambda h: (h, 0, 0)),
                   pl.BlockSpec((1, Q_BLOCK, lk), lambda h: (h, 0, 0)),
                   pl.BlockSpec((1, Q_BLOCK, LANES), lambda h: (h, 0, 0))],
        out_shape=[jax.ShapeDtypeStruct((N_HEADS, Q_BLOCK, WIN_KEYS), F32),
                   jax.ShapeDtypeStruct((N_HEADS, Q_BLOCK, lk), F32),
                   jax.ShapeDtypeStruct((N_HEADS, Q_BLOCK, LANES), BF)],
        compiler_params=_cparams("arbitrary"),
    )(rel_bias)


def _softmax_parts(s):
    m = jnp.max(s, axis=-1, keepdims=True)
    e = jnp.exp(s - m)
    return m, e, jnp.sum(e, axis=-1, keepdims=True)


def _attn_kernel(q_ref, ks_ref, vs_ref, kw_ref, vw_ref, kc_ref, vc_ref, ng_ref,
                 wb_ref, sb_ref, cf_ref, ov_ref, o_ref,
                 kaug_ref, vsp_ref, kwp_ref, vwp_ref, *, seq):
    j = pl.program_id(2)
    n_slc = seq // SLC_BLOCK
    ncp = seq // CMP_STRIDE
    rows = HPG * Q_BLOCK
    rank_w = 2 * n_slc
    n_sel = min(N_SEL, n_slc)

    @pl.when(j == 0)
    def _stage_keys():
        col = lax.broadcasted_iota(jnp.int32, (SLC_PAD, LANES), 1)
        kaug_ref[0:SLC_PAD, 0:HEAD_DIM] = jnp.zeros((SLC_PAD, HEAD_DIM), BF)
        kaug_ref[0:SLC_PAD, HEAD_DIM:] = jnp.where(col == PAD_FLAG_COL, NEG, 0.0).astype(BF)
        kaug_ref[SLC_PAD:, 0:HEAD_DIM] = ks_ref[0]
        blk = lax.broadcasted_iota(jnp.int32, (seq, LANES), 0) >> (SLC_BLOCK.bit_length() - 1)
        colk = lax.broadcasted_iota(jnp.int32, (seq, LANES), 1)
        kaug_ref[SLC_PAD:, HEAD_DIM:] = jnp.where(colk == blk, NEG, 0.0).astype(BF)
        vsp_ref[0:SLC_PAD, :] = jnp.zeros((SLC_PAD, HEAD_DIM), BF)
        vsp_ref[SLC_PAD:, :] = vs_ref[0]
        kwp_ref[0:WIN_PAD, :] = jnp.zeros((WIN_PAD, HEAD_DIM), BF)
        kwp_ref[WIN_PAD:, :] = kw_ref[0]
        vwp_ref[0:WIN_PAD, :] = jnp.zeros((WIN_PAD, HEAD_DIM), BF)
        vwp_ref[WIN_PAD:, :] = vw_ref[0]

    q4 = jnp.concatenate([q_ref[0, :, hh * HEAD_DIM:(hh + 1) * HEAD_DIM] for hh in range(HPG)], axis=0)

    kc = kc_ref[0].astype(BF)
    vc = vc_ref[0].astype(BF)
    r_io = lax.broadcasted_iota(jnp.int32, (LANES, ncp), 0)
    n_io = lax.broadcasted_iota(jnp.int32, (LANES, ncp), 1)
    band = (r_io < 3 * CMP_BAND) & (n_io == 4 * j + (r_io & (CMP_BAND - 1)) - CMP_BAND_LO)
    shift = jnp.where(band, 1.0, 0.0).astype(BF)
    cf4 = cf_ref[...].reshape(rows, LANES)
    lc = _dot_nt(q4, kc) + _dot(cf4, shift)
    qi_c = lax.broadcasted_iota(jnp.int32, (Q_BLOCK, ncp), 0)
    n_c = lax.broadcasted_iota(jnp.int32, (Q_BLOCK, ncp), 1)
    vis = (CMP_STRIDE * n_c + (CMP_LEN - 1)) <= (Q_BLOCK * j + qi_c)
    lc3 = jnp.where(vis[None], lc.reshape(HPG, Q_BLOCK, ncp), NEG)
    _, e_c, l_c = _softmax_parts(lc3)
    p_c = jnp.where(vis[None], e_c / l_c, 0.0)
    o_cmp = _dot(p_c.reshape(rows, ncp).astype(BF), vc)

    psum = p_c[0]
    for hh in range(1, HPG):
        psum = psum + p_c[hh]
    p0 = psum.astype(BF)
    r1 = psum - p0.astype(F32)
    p1 = r1.astype(BF)
    p2 = (r1 - p1.astype(F32)).astype(BF)
    ov = ov_ref[...]
    imp = _dot(p0, ov) + _dot(p1, ov) + _dot(p2, ov)

    lane = lax.broadcasted_iota(jnp.int32, (Q_BLOCK, rank_w), 1)
    blk = jnp.where(lane >= n_slc, lane - n_slc, lane)
    forced = (blk == 0) | (blk == j) | (blk == j - 1)
    causal = blk <= j
    score = jnp.where(forced, FORCE_SCORE, jnp.where(causal, imp, -FORCE_SCORE))
    rank = jnp.zeros((Q_BLOCK, rank_w), jnp.int32)
    for r in range(1, n_slc):
        other = pltpu.roll(score, r, axis=1)
        beats = (other > score) | ((other == score) & (lane >= r))
        rank = rank + beats.astype(jnp.int32)
    first_half = lane < n_slc
    unselected = first_half & jnp.logical_not((rank < n_sel) & causal)
    local = blk >= j - (SLC_LOCAL_BLOCKS - 1)
    flag = lane == PAD_FLAG_COL
    if rank_w < LANES:
        raise NotImplementedError("sequence too short for the selection layout")
    mask_loc = jnp.where(unselected | flag, 1.0, 0.0).astype(BF)
    mask_far = jnp.where(unselected | (first_half & local) | flag, 1.0, 0.0).astype(BF)
    qa_loc = jnp.concatenate([q4, jnp.concatenate([mask_loc] * HPG, axis=0)], axis=1)
    qa_far = jnp.concatenate([q4, jnp.concatenate([mask_far] * HPG, axis=0)], axis=1)

    r0 = pl.multiple_of(j * Q_BLOCK, Q_BLOCK)
    lk = SLC_LOCAL_BLOCKS * SLC_BLOCK
    s_loc = _dot_nt(qa_loc, kaug_ref[pl.ds(r0, lk), :]) + sb_ref[...].reshape(rows, lk)
    m_s, e_s, l_s = _softmax_parts(s_loc)
    acc_s = _dot(e_s.astype(BF), vsp_ref[pl.ds(r0, lk), :])

    chunk = SLC_CHUNK_BLOCKS * SLC_BLOCK

    def far_step(c, carry):
        m_o, l_o, acc_o = carry
        rr = pl.multiple_of(SLC_PAD + c * chunk, SLC_BLOCK)
        s = _dot_nt(qa_far, kaug_ref[pl.ds(rr, chunk), :])
        m_n = jnp.maximum(m_o, jnp.max(s, axis=-1, keepdims=True))
        a = jnp.exp(m_o - m_n)
        e = jnp.exp(s - m_n)
        l_n = a * l_o + jnp.sum(e, axis=-1, keepdims=True)
        acc_n = a * acc_o + _dot(e.astype(BF), vsp_ref[pl.ds(rr, chunk), :])
        return m_n, l_n, acc_n

    n_far = (j + SLC_CHUNK_BLOCKS - SLC_LOCAL_BLOCKS) // SLC_CHUNK_BLOCKS
    m_s, l_s, acc_s = lax.fori_loop(0, n_far, far_step, (m_s, l_s, acc_s))
    o_slc = acc_s / l_s

    key_w = lax.broadcasted_iota(jnp.int32, (1, WIN_KEYS), 1)
    before_start = jnp.where(key_w < WIN_PAD - Q_BLOCK * j, NEG, 0.0)
    s_w = _dot_nt(q4, kwp_ref[pl.ds(r0, WIN_KEYS), :]) + wb_ref[...].reshape(rows, WIN_KEYS) + before_start
    _, e_w, l_w = _softmax_parts(s_w)
    o_win = _dot(e_w.astype(BF), vwp_ref[pl.ds(r0, WIN_KEYS), :]) / l_w

    gts = jax.nn.sigmoid(ng_ref[0].astype(F32))
    for hh in range(HPG):
        sl = slice(hh * Q_BLOCK, (hh + 1) * Q_BLOCK)
        o_h = (gts[:, 3 * hh:3 * hh + 1] * o_cmp[sl]
               + gts[:, 3 * hh + 1:3 * hh + 2] * o_slc[sl]
               + gts[:, 3 * hh + 2:3 * hh + 3] * o_win[sl])
        o_ref[0, :, hh * HEAD_DIM:(hh + 1) * HEAD_DIM] = o_h.astype(o_ref.dtype)


def _nsa_attention(proj, gates, k_cmp, v_cmp, wb, sb, cf, ov):
    b, s, _ = proj.shape
    q_blk = Q_OFF // (HPG * HEAD_DIM)
    kv_blk = KC_OFF // HEAD_DIM
    ng_blk = (gates.shape[2] - N_KV * LANES) // LANES
    lk = SLC_LOCAL_BLOCKS * SLC_BLOCK
    ncp = s // CMP_STRIDE

    def kv_spec(idx):
        return pl.BlockSpec((1, s, HEAD_DIM), lambda i, g, j: (i, 0, kv_blk + idx * N_KV + g))

    cmp_spec = pl.BlockSpec((1, ncp, HEAD_DIM), lambda i, g, j: (i * N_KV + g, 0, 0))
    return pl.pallas_call(
        functools.partial(_attn_kernel, seq=s),
        grid=(b, N_KV, s // Q_BLOCK),
        in_specs=[pl.BlockSpec((1, Q_BLOCK, HPG * HEAD_DIM), lambda i, g, j: (i, j, q_blk + g)),
                  kv_spec(2), kv_spec(3), kv_spec(4), kv_spec(5),
                  cmp_spec, cmp_spec,
                  pl.BlockSpec((1, Q_BLOCK, LANES), lambda i, g, j: (i, j, ng_blk + g)),
                  pl.BlockSpec((HPG, Q_BLOCK, WIN_KEYS), lambda i, g, j: (g, 0, 0)),
                  pl.BlockSpec((HPG, Q_BLOCK, lk), lambda i, g, j: (g, 0, 0)),
                  pl.BlockSpec((HPG, Q_BLOCK, LANES), lambda i, g, j: (g, 0, 0)),
                  pl.BlockSpec(ov.shape, lambda i, g, j: (0, 0))],
        out_specs=pl.BlockSpec((1, Q_BLOCK, HPG * HEAD_DIM), lambda i, g, j: (i, j, g)),
        out_shape=jax.ShapeDtypeStruct((b, s, Q_WIDTH), BF),
        scratch_shapes=[pltpu.VMEM((SLC_PAD + s, 2 * HEAD_DIM), BF),
                        pltpu.VMEM((SLC_PAD + s, HEAD_DIM), BF),
                        pltpu.VMEM((WIN_PAD + s, HEAD_DIM), BF),
                        pltpu.VMEM((WIN_PAD + s, HEAD_DIM), BF)],
        compiler_params=_cparams("parallel", "parallel", "arbitrary"),
    )(proj, proj, proj, proj, proj, k_cmp, v_cmp, gates, wb, sb, cf, ov)


def _overlap_matrix(s):
    ncp = s // CMP_STRIDE
    n_slc = s // SLC_BLOCK
    nc = (s - CMP_LEN) // CMP_STRIDE + 1
    i = np.arange(ncp)[:, None]
    jj = np.arange(n_slc)[None, :]
    ov = ((i * CMP_STRIDE < (jj + 1) * SLC_BLOCK) & (i * CMP_STRIDE + CMP_LEN > jj * SLC_BLOCK) & (i < nc))
    return jnp.asarray(np.concatenate([ov, ov], axis=1), BF)


def kernel(x, c, w_ada, b_ada, g_pre_mix, g_post_mix, g_pre_ffn, g_post_ffn, w_in, w_dw, b_dw, conv_ln_g, conv_ln_b, w_conv_out, cmp_pe_k, cmp_pe_v, w_cmp_k1, w_cmp_k2, w_cmp_v1, w_cmp_v2, rel_bias, w_nsa_out, w_out, w_ffn_gate, w_ffn_up, w_ffn_down):
    b, s, d = x.shape
    m = b * s
    depth = w_ada.shape[0]
    c8 = jnp.zeros((8, d), F32).at[:b].set(c)
    wb, sb, cf = _bias_tables(rel_bias)
    ov = _overlap_matrix(s)
    col_scale = jnp.ones((1, NG_OFF), F32).at[:, Q_OFF:KC_OFF].set(HEAD_DIM ** -0.5)

    for l in range(depth):
        mod = _modulation(c8, w_ada[l], b_ada[l][None, :])
        mod3 = mod[:b].reshape(b, 6, d)

        w_main = w_in[l][:, :NG_OFF].astype(BF)
        w_ng = w_in[l][:, NG_OFF:MG_OFF].reshape(d, N_KV, 3 * HPG)
        w_ng = jnp.pad(w_ng, ((0, 0), (0, 0), (0, LANES - 3 * HPG))).reshape(d, N_KV * LANES)
        w_gates = jnp.concatenate([w_in[l][:, MG_OFF:], w_ng], axis=1).astype(BF)

        h = _prenorm(x, g_pre_mix[l][None, :], mod3, 0, 1).reshape(m, d)
        proj = _matmul(h, w_main, BF, 1024, 512, col_scale).reshape(b, s, NG_OFF)
        gates = _matmul(h, w_gates, BF, 1024, 512)

        u_conv = _conv_module(proj, w_dw[l], b_dw[l][None, :], conv_ln_g[l][None, :], conv_ln_b[l][None, :])

        def grouped(off):
            t = proj[:, :, off:off + KV_WIDTH].reshape(b, s // CMP_STRIDE, CMP_STRIDE, N_KV, HEAD_DIM)
            return t.transpose(0, 3, 1, 2, 4).reshape(b * N_KV, s // CMP_STRIDE, CMP_STRIDE * HEAD_DIM)

        pe_k = jnp.broadcast_to(cmp_pe_k[l].reshape(1, -1), (8, CMP_LEN * HEAD_DIM)).astype(BF)
        pe_v = jnp.broadcast_to(cmp_pe_v[l].reshape(1, -1), (8, CMP_LEN * HEAD_DIM)).astype(BF)
        k_cmp = _compress(grouped(KC_OFF), pe_k, w_cmp_k1[l].astype(BF), w_cmp_k2[l].astype(BF))
        v_cmp = _compress(grouped(KC_OFF + KV_WIDTH), pe_v, w_cmp_v1[l].astype(BF), w_cmp_v2[l].astype(BF))

        o_nsa = _nsa_attention(proj, gates.reshape(b, s, -1), k_cmp, v_cmp, wb, sb, cf, ov)

        z = _gated_mix(u_conv.reshape(m, CONV_DIM), o_nsa.reshape(m, Q_WIDTH),
                       w_conv_out[l].astype(BF), w_nsa_out[l].astype(BF), gates, d)
        mix = _matmul(z, w_out[l].astype(BF), F32, 1024, 512).reshape(b, s, d)
        x, h2 = _post_mix(mix, x, g_post_mix[l][None, :], g_pre_ffn[l][None, :], mod3, 2, 3, 4)

        ff = _ffn_up(h2.reshape(m, d), w_ffn_gate[l].astype(BF), w_ffn_up[l].astype(BF))
        f = _matmul(ff, w_ffn_down[l].astype(BF), F32, 512, 256).reshape(b, s, d)
        x = _post_ffn(f, x, g_post_ffn[l][None, :], mod3, 5)
    return x
```

```python
import functools
import math

import numpy as np
import jax
import jax.numpy as jnp
from jax import lax
from jax.experimental import pallas as pl
from jax.experimental.pallas import tpu as pltpu

F32 = jnp.float32
BF = jnp.bfloat16

CONV_DIM = 2048
CONV_WIDTH = 31
N_HEADS = 16
N_KV = 4
HPG = N_HEADS // N_KV
HEAD_DIM = 128
CMP_LEN = 32
CMP_STRIDE = 16
CMP_HID = 512
SLC_BLOCK = 64
N_SEL = 16
WINDOW = 512
Q_BLOCK = 64
REL_BUCKETS = 32
REL_MAX_DIST = 128
NORM_EPS = 1e-6
NEG = -1e30
FORCE_SCORE = 1e9

Q_WIDTH = N_HEADS * HEAD_DIM
KV_WIDTH = N_KV * HEAD_DIM
Q_OFF = 2 * CONV_DIM
KC_OFF = Q_OFF + Q_WIDTH
NG_OFF = KC_OFF + 6 * KV_WIDTH
MG_OFF = NG_OFF + 3 * N_HEADS

LANES = 128
VMEM_LIMIT = 56 * 1024 * 1024

CONV_HALO = 32
Q_TILE = 2 * Q_BLOCK
SLC_PAD = 2 * SLC_BLOCK
SLC_LOCAL_KEYS = SLC_PAD + Q_TILE
SLC_CHUNK_BLOCKS = 16
WIN_KEYS = WINDOW + Q_TILE
WIN_PAD = WINDOW
CMP_PER_TILE = Q_TILE // CMP_STRIDE
CMP_BAND = 16
CMP_BAND_LO = 9
PAD_FLAG_COL = 64
KEY_FORCED = 0x7F000000
KEY_FUTURE = -2


def _t5_thresholds():
    max_exact = REL_BUCKETS // 2
    thr = []
    for k in range(1, REL_BUCKETS - max_exact):
        n = max_exact
        while max_exact + int(math.log(n / max_exact) / math.log(REL_MAX_DIST / max_exact)
                              * (REL_BUCKETS - max_exact)) < max_exact + k:
            n += 1
        thr.append(n)
    return tuple(thr)


T5_THRESHOLDS = _t5_thresholds()


def _cparams(*sem):
    return pltpu.CompilerParams(dimension_semantics=sem, vmem_limit_bytes=VMEM_LIMIT)


def _silu(v):
    return v * jax.nn.sigmoid(v)


def _dot(a, b):
    return jnp.dot(a, b, preferred_element_type=F32)


def _dot_nt(a, b):
    return lax.dot_general(a, b, (((1,), (1,)), ((), ())), preferred_element_type=F32)


def _rms(v, g):
    return v * lax.rsqrt(jnp.mean(v * v, axis=-1, keepdims=True) + NORM_EPS) * g


def _mod_kernel(c_ref, w_ref, b_ref, o_ref):
    a = _silu(c_ref[...]).astype(BF)
    o_ref[...] = _dot(a, w_ref[...].astype(BF)) + b_ref[...]


def _modulation(c8, w_ada, b_ada, tn=512):
    rows, d = c8.shape
    n = w_ada.shape[1]
    return pl.pallas_call(
        _mod_kernel,
        grid=(n // tn,),
        in_specs=[pl.BlockSpec((rows, d), lambda j: (0, 0)),
                  pl.BlockSpec((d, tn), lambda j: (0, j)),
                  pl.BlockSpec((1, tn), lambda j: (0, j))],
        out_specs=pl.BlockSpec((rows, tn), lambda j: (0, j)),
        out_shape=jax.ShapeDtypeStruct((rows, n), F32),
        compiler_params=_cparams("parallel"),
    )(c8, w_ada, b_ada)


def _prenorm_kernel(x_ref, g_ref, mod_ref, h_ref, *, shift_row, scale_row):
    y = _rms(x_ref[0], g_ref[...])
    h = y * (1.0 + mod_ref[0, scale_row:scale_row + 1, :]) + mod_ref[0, shift_row:shift_row + 1, :]
    h_ref[0] = h.astype(BF)


def _prenorm(x, g, mod3, shift_row, scale_row, tr=256):
    b, s, d = x.shape
    return pl.pallas_call(
        functools.partial(_prenorm_kernel, shift_row=shift_row, scale_row=scale_row),
        grid=(b, s // tr),
        in_specs=[pl.BlockSpec((1, tr, d), lambda i, r: (i, r, 0)),
                  pl.BlockSpec((1, d), lambda i, r: (0, 0)),
                  pl.BlockSpec((1, 6, d), lambda i, r: (i, 0, 0))],
        out_specs=pl.BlockSpec((1, tr, d), lambda i, r: (i, r, 0)),
        out_shape=jax.ShapeDtypeStruct((b, s, d), BF),
        compiler_params=_cparams("parallel", "parallel"),
    )(x, g, mod3)


def _post_mix_kernel(y_ref, x_ref, gp_ref, gn_ref, mod_ref, xo_ref, h_ref, *, gate_row, shift_row, scale_row):
    x1 = x_ref[0] + mod_ref[0, gate_row:gate_row + 1, :] * _rms(y_ref[0], gp_ref[...])
    xo_ref[0] = x1
    h = _rms(x1, gn_ref[...]) * (1.0 + mod_ref[0, scale_row:scale_row + 1, :]) \
        + mod_ref[0, shift_row:shift_row + 1, :]
    h_ref[0] = h.astype(BF)


def _post_mix(y, x, g_post, g_next, mod3, gate_row, shift_row, scale_row, tr=256):
    b, s, d = x.shape
    row = pl.BlockSpec((1, tr, d), lambda i, r: (i, r, 0))
    vec = pl.BlockSpec((1, d), lambda i, r: (0, 0))
    return pl.pallas_call(
        functools.partial(_post_mix_kernel, gate_row=gate_row, shift_row=shift_row, scale_row=scale_row),
        grid=(b, s // tr),
        in_specs=[row, row, vec, vec, pl.BlockSpec((1, 6, d), lambda i, r: (i, 0, 0))],
        out_specs=[row, row],
        out_shape=[jax.ShapeDtypeStruct((b, s, d), F32), jax.ShapeDtypeStruct((b, s, d), BF)],
        compiler_params=_cparams("parallel", "parallel"),
    )(y, x, g_post, g_next, mod3)


def _post_ffn_kernel(y_ref, x_ref, gp_ref, mod_ref, xo_ref, *, gate_row):
    xo_ref[0] = x_ref[0] + mod_ref[0, gate_row:gate_row + 1, :] * _rms(y_ref[0], gp_ref[...])


def _post_ffn(y, x, g_post, mod3, gate_row, tr=256):
    b, s, d = x.shape
    row = pl.BlockSpec((1, tr, d), lambda i, r: (i, r, 0))
    return pl.pallas_call(
        functools.partial(_post_ffn_kernel, gate_row=gate_row),
        grid=(b, s // tr),
        in_specs=[row, row, pl.BlockSpec((1, d), lambda i, r: (0, 0)),
                  pl.BlockSpec((1, 6, d), lambda i, r: (i, 0, 0))],
        out_specs=row,
        out_shape=jax.ShapeDtypeStruct((b, s, d), F32),
        compiler_params=_cparams("parallel", "parallel"),
    )(y, x, g_post, mod3)


def _mm_scale_kernel(a_ref, w_ref, s_ref, o_ref):
    o_ref[...] = (_dot(a_ref[...], w_ref[...]) * s_ref[...]).astype(o_ref.dtype)


def _mm_kernel(a_ref, w_ref, o_ref):
    o_ref[...] = _dot(a_ref[...], w_ref[...]).astype(o_ref.dtype)


def _matmul(a, w, out_dtype, tm, tn, col_scale=None):
    m, k = a.shape
    n = w.shape[1]
    in_specs = [pl.BlockSpec((tm, k), lambda i, j: (i, 0)),
                pl.BlockSpec((k, tn), lambda i, j: (0, j))]
    args = [a, w]
    body = _mm_kernel
    if col_scale is not None:
        in_specs.append(pl.BlockSpec((1, tn), lambda i, j: (0, j)))
        args.append(col_scale)
        body = _mm_scale_kernel
    return pl.pallas_call(
        body,
        grid=(m // tm, n // tn),
        in_specs=in_specs,
        out_specs=pl.BlockSpec((tm, tn), lambda i, j: (i, j)),
        out_shape=jax.ShapeDtypeStruct((m, n), out_dtype),
        compiler_params=_cparams("parallel", "arbitrary"),
    )(*args)


def _mix_kernel(a1_ref, a2_ref, w1_ref, w2_ref, ga_ref, gb_ref, o_ref):
    ya = _dot(a1_ref[...], w1_ref[...])
    yb = _dot(a2_ref[...], w2_ref[...])
    o = jax.nn.sigmoid(ga_ref[...].astype(F32)) * ya + jax.nn.sigmoid(gb_ref[...].astype(F32)) * yb
    o_ref[...] = o.astype(o_ref.dtype)


def _gated_mix(u_conv, o_nsa, w_conv_out, w_nsa_out, gates, d_model, tm=1024, tn=512):
    m, ka = u_conv.shape
    kb = o_nsa.shape[1]
    gb_block = d_model // tn
    return pl.pallas_call(
        _mix_kernel,
        grid=(m // tm, d_model // tn),
        in_specs=[pl.BlockSpec((tm, ka), lambda i, j: (i, 0)),
                  pl.BlockSpec((tm, kb), lambda i, j: (i, 0)),
                  pl.BlockSpec((ka, tn), lambda i, j: (0, j)),
                  pl.BlockSpec((kb, tn), lambda i, j: (0, j)),
                  pl.BlockSpec((tm, tn), lambda i, j: (i, j)),
                  pl.BlockSpec((tm, tn), lambda i, j: (i, j + gb_block))],
        out_specs=pl.BlockSpec((tm, tn), lambda i, j: (i, j)),
        out_shape=jax.ShapeDtypeStruct((m, d_model), BF),
        compiler_params=_cparams("parallel", "arbitrary"),
    )(u_conv, o_nsa, w_conv_out, w_nsa_out, gates, gates)


def _ffn_up_kernel(a_ref, wg_ref, wu_ref, o_ref):
    a = a_ref[...]
    o_ref[...] = (_silu(_dot(a, wg_ref[...])) * _dot(a, wu_ref[...])).astype(o_ref.dtype)


def _ffn_up(h, w_gate, w_up, tm=1024, tn=256):
    m, k = h.shape
    n = w_gate.shape[1]
    wspec = pl.BlockSpec((k, tn), lambda i, j: (0, j))
    return pl.pallas_call(
        _ffn_up_kernel,
        grid=(m // tm, n // tn),
        in_specs=[pl.BlockSpec((tm, k), lambda i, j: (i, 0)), wspec, wspec],
        out_specs=pl.BlockSpec((tm, tn), lambda i, j: (i, j)),
        out_shape=jax.ShapeDtypeStruct((m, n), BF),
        compiler_params=_cparams("parallel", "arbitrary"),
    )(h, w_gate, w_up)


def _conv_kernel(a_ref, b_ref, ah_ref, bh_ref, w_ref, bdw_ref, g_ref, bb_ref, o_ref, u_ref, v_ref, *, ts):
    i = pl.program_id(1)
    u_ref[CONV_HALO:CONV_HALO + ts, :] = a_ref[0].astype(F32) * jax.nn.sigmoid(b_ref[0].astype(F32))
    uh = ah_ref[0].astype(F32) * jax.nn.sigmoid(bh_ref[0].astype(F32))
    u_ref[0:CONV_HALO, :] = jnp.where(i > 0, uh, 0.0)

    rc, cc = 128, 256
    first = CONV_HALO - (CONV_WIDTH - 1)
    for r0 in range(0, ts, rc):
        for c0 in range(0, CONV_DIM, cc):
            acc = jnp.zeros((rc, cc), F32)
            for k in range(CONV_WIDTH):
                acc = acc + u_ref[r0 + first + k:r0 + first + k + rc, c0:c0 + cc] * w_ref[k:k + 1, c0:c0 + cc]
            v_ref[r0:r0 + rc, c0:c0 + cc] = acc + bdw_ref[:, c0:c0 + cc]

    v = v_ref[...]
    mu = jnp.mean(v, axis=-1, keepdims=True)
    vc = v - mu
    y = vc * lax.rsqrt(jnp.mean(vc * vc, axis=-1, keepdims=True) + NORM_EPS) * g_ref[...] + bb_ref[...]
    o_ref[0] = _silu(y).astype(o_ref.dtype)


def _conv_module(proj, w_dw, b_dw, ln_g, ln_b, ts=256):
    b, s, _ = proj.shape
    hb = ts // CONV_HALO
    cur_a = pl.BlockSpec((1, ts, CONV_DIM), lambda i, r: (i, r, 0))
    cur_b = pl.BlockSpec((1, ts, CONV_DIM), lambda i, r: (i, r, 1))
    halo_a = pl.BlockSpec((1, CONV_HALO, CONV_DIM), lambda i, r: (i, jnp.maximum(r * hb - 1, 0), 0))
    halo_b = pl.BlockSpec((1, CONV_HALO, CONV_DIM), lambda i, r: (i, jnp.maximum(r * hb - 1, 0), 1))
    vec = pl.BlockSpec((1, CONV_DIM), lambda i, r: (0, 0))
    return pl.pallas_call(
        functools.partial(_conv_kernel, ts=ts),
        grid=(b, s // ts),
        in_specs=[cur_a, cur_b, halo_a, halo_b,
                  pl.BlockSpec((CONV_WIDTH, CONV_DIM), lambda i, r: (0, 0)), vec, vec, vec],
        out_specs=pl.BlockSpec((1, ts, CONV_DIM), lambda i, r: (i, r, 0)),
        out_shape=jax.ShapeDtypeStruct((b, s, CONV_DIM), BF),
        scratch_shapes=[pltpu.VMEM((CONV_HALO + ts, CONV_DIM), F32), pltpu.VMEM((ts, CONV_DIM), F32)],
        compiler_params=_cparams("parallel", "arbitrary"),
    )(proj, proj, proj, proj, w_dw, b_dw, ln_g, ln_b)


def _compress_kernel(k_ref, pe_ref, w1_ref, w2_ref, o_ref):
    k16 = k_ref[0]
    half = CMP_STRIDE * HEAD_DIM
    lo = _dot(k16, w1_ref[0:half, :])
    hi = _dot(k16, w1_ref[half:2 * half, :])
    pe = _dot(pe_ref[...], w1_ref[...])
    rows = k16.shape[0]
    hid = lo + pltpu.roll(hi, rows - 1, axis=0) + pe[0:1, :]
    act = 0.5 * hid * (1.0 + jnp.tanh(math.sqrt(2.0 / math.pi) * (hid + 0.044715 * hid * hid * hid)))
    o_ref[0] = _dot(act.astype(BF), w2_ref[...])


def _compress(k16, pe8, w1, w2):
    n, rows, width = k16.shape
    return pl.pallas_call(
        _compress_kernel,
        grid=(n,),
        in_specs=[pl.BlockSpec((1, rows, width), lambda i: (i, 0, 0)),
                  pl.BlockSpec(pe8.shape, lambda i: (0, 0)),
                  pl.BlockSpec(w1.shape, lambda i: (0, 0)),
                  pl.BlockSpec(w2.shape, lambda i: (0, 0))],
        out_specs=pl.BlockSpec((1, rows, HEAD_DIM), lambda i: (i, 0, 0)),
        out_shape=jax.ShapeDtypeStruct((n, rows, HEAD_DIM), F32),
        compiler_params=_cparams("parallel"),
    )(k16, pe8, w1, w2)


def _bias_tables_kernel(rb_ref, wb_ref, sb_ref, cf_ref):
    h = pl.program_id(0)
    last = rb_ref[REL_BUCKETS - 1, h]

    def rel(dist):
        n = jnp.maximum(dist, 0)
        bucket = jnp.full(n.shape, REL_BUCKETS // 2, jnp.int32)
        for t in T5_THRESHOLDS:
            bucket = bucket + (n >= t).astype(jnp.int32)
        bucket = jnp.where(n < REL_BUCKETS // 2, n, jnp.minimum(bucket, REL_BUCKETS - 1))
        out = jnp.zeros(n.shape, F32)
        for bk in range(REL_BUCKETS - 1):
            out = jnp.where(bucket == bk, rb_ref[bk, h] - last, out)
        return out

    def dist_of(shape, offset):
        qi = lax.broadcasted_iota(jnp.int32, shape, 0)
        kl = lax.broadcasted_iota(jnp.int32, shape, 1)
        return qi + offset - kl

    dw = dist_of((Q_TILE, WIN_KEYS), WIN_PAD)
    wb_ref[0] = jnp.where((dw >= 0) & (dw < WINDOW), rel(dw), NEG)

    ds_ = dist_of((Q_TILE, SLC_LOCAL_KEYS), SLC_PAD)
    sb_ref[0] = jnp.where(ds_ >= 0, rel(ds_), NEG)

    qi = lax.broadcasted_iota(jnp.int32, (Q_TILE, LANES), 0)
    col = lax.broadcasted_iota(jnp.int32, (Q_TILE, LANES), 1)
    mi = col & (CMP_BAND - 1)
    part = col >> (CMP_BAND.bit_length() - 1)
    dc = qi - CMP_STRIDE * (mi - CMP_BAND_LO) - (CMP_LEN - 1)
    val = jnp.where(dc >= 0, rel(dc), 0.0)
    p0 = val.astype(BF).astype(F32)
    r1 = val - p0
    p1 = r1.astype(BF).astype(F32)
    p2 = r1 - p1
    parts = jnp.where(part == 0, p0, jnp.where(part == 1, p1, jnp.where(part == 2, p2, 0.0)))
    cf_ref[0] = parts.astype(BF)


def _bias_tables(rel_bias):
    smem = pl.BlockSpec(memory_space=pltpu.SMEM)
    lk = SLC_LOCAL_KEYS
    return pl.pallas_call(
        _bias_tables_kernel,
        grid=(N_HEADS,),
        in_specs=[smem],
        out_specs=[pl.BlockSpec((1, Q_TILE, WIN_KEYS), lambda h: (h, 0, 0)),
                   pl.BlockSpec((1, Q_TILE, lk), lambda h: (h, 0, 0)),
                   pl.BlockSpec((1, Q_TILE, LANES), lambda h: (h, 0, 0))],
        out_shape=[jax.ShapeDtypeStruct((N_HEADS, Q_TILE, WIN_KEYS), F32),
                   jax.ShapeDtypeStruct((N_HEADS, Q_TILE, lk), F32),
                   jax.ShapeDtypeStruct((N_HEADS, Q_TILE, LANES), BF)],
        compiler_params=_cparams("arbitrary"),
    )(rel_bias)


def _softmax_parts(s):
    m = jnp.max(s, axis=-1, keepdims=True)
    e = jnp.exp(s - m)
    return m, e, jnp.sum(e, axis=-1, keepdims=True)


def _lane_tiles(v, op):
    out = v[:, 0:LANES]
    for t in range(1, v.shape[1] // LANES):
        out = op(out, v[:, t * LANES:(t + 1) * LANES])
    return out


def _attn_kernel(q_ref, ks_ref, vs_ref, kw_ref, vw_ref, kc_ref, vc_ref, ng_ref,
                 wb_ref, sb_ref, cf_ref, ovt_ref, eye_ref, o_ref,
                 kaug_ref, vsp_ref, kwp_ref, vwp_ref, key_ref, sloc_ref, sfar_ref, *, seq):
    tile = pl.program_id(2)
    n_slc = seq // SLC_BLOCK
    ncp = seq // CMP_STRIDE
    rows = HPG * Q_TILE
    n_sel = min(N_SEL, n_slc)
    if n_slc != PAD_FLAG_COL or Q_TILE != LANES:
        raise NotImplementedError("selection layout expects 64 selection blocks and 128-query tiles")

    @pl.when(tile == 0)
    def _stage_keys():
        col = lax.broadcasted_iota(jnp.int32, (SLC_PAD, LANES), 1)
        kaug_ref[0:SLC_PAD, 0:HEAD_DIM] = jnp.zeros((SLC_PAD, HEAD_DIM), BF)
        kaug_ref[0:SLC_PAD, HEAD_DIM:] = jnp.where(col == PAD_FLAG_COL, NEG, 0.0).astype(BF)
        kaug_ref[SLC_PAD:, 0:HEAD_DIM] = ks_ref[0]
        blk = lax.broadcasted_iota(jnp.int32, (seq, LANES), 0) >> (SLC_BLOCK.bit_length() - 1)
        colk = lax.broadcasted_iota(jnp.int32, (seq, LANES), 1)
        kaug_ref[SLC_PAD:, HEAD_DIM:] = jnp.where(colk == blk, NEG, 0.0).astype(BF)
        vsp_ref[0:SLC_PAD, :] = jnp.zeros((SLC_PAD, HEAD_DIM), BF)
        vsp_ref[SLC_PAD:, :] = vs_ref[0]
        kwp_ref[0:WIN_PAD, :] = jnp.zeros((WIN_PAD, HEAD_DIM), BF)
        kwp_ref[WIN_PAD:, :] = kw_ref[0]
        vwp_ref[0:WIN_PAD, :] = jnp.zeros((WIN_PAD, HEAD_DIM), BF)
        vwp_ref[WIN_PAD:, :] = vw_ref[0]

    q4 = jnp.concatenate([q_ref[0, :, hh * HEAD_DIM:(hh + 1) * HEAD_DIM] for hh in range(HPG)], axis=0)
    t0 = tile * Q_TILE

    kc = kc_ref[0].astype(BF)
    vc = vc_ref[0].astype(BF)
    r_io = lax.broadcasted_iota(jnp.int32, (LANES, ncp), 0)
    n_io = lax.broadcasted_iota(jnp.int32, (LANES, ncp), 1)
    band = (r_io < 3 * CMP_BAND) & (n_io == CMP_PER_TILE * tile + (r_io & (CMP_BAND - 1)) - CMP_BAND_LO)
    shift = jnp.where(band, 1.0, 0.0).astype(BF)
    cf4 = cf_ref[...].reshape(rows, LANES)
    lc = _dot_nt(q4, kc) + _dot(cf4, shift)
    qi_c = lax.broadcasted_iota(jnp.int32, (Q_TILE, ncp), 0)
    n_c = lax.broadcasted_iota(jnp.int32, (Q_TILE, ncp), 1)
    vis = (CMP_STRIDE * n_c + (CMP_LEN - 1)) <= (t0 + qi_c)
    lc3 = jnp.where(vis[None], lc.reshape(HPG, Q_TILE, ncp), NEG)
    _, e_c, l_c = _softmax_parts(lc3)
    p_c = jnp.where(vis[None], e_c * (1.0 / l_c), 0.0)
    o_cmp = _dot(p_c.reshape(rows, ncp).astype(BF), vc)

    psum = p_c[0]
    for hh in range(1, HPG):
        psum = psum + p_c[hh]
    p0 = psum.astype(BF)
    r1 = psum - p0.astype(F32)
    p1 = r1.astype(BF)
    p2 = (r1 - p1.astype(F32)).astype(BF)
    ovt = ovt_ref[...]
    imp_t = _dot_nt(ovt, p0) + _dot_nt(ovt, p1) + _dot_nt(ovt, p2)

    blk_t = lax.broadcasted_iota(jnp.int32, (n_slc, Q_TILE), 0)
    q_t = lax.broadcasted_iota(jnp.int32, (n_slc, Q_TILE), 1)
    cur_t = (t0 + q_t) >> (SLC_BLOCK.bit_length() - 1)
    forced_t = (blk_t == 0) | (blk_t == cur_t) | (blk_t == cur_t - 1)
    causal_t = blk_t <= cur_t
    key = jnp.where(forced_t, KEY_FORCED, jnp.where(causal_t, lax.bitcast_convert_type(imp_t, jnp.int32), KEY_FUTURE))
    key_ref[...] = key
    groups = [key[8 * a:8 * a + 8, :] for a in range(n_slc // 8)]
    ranks = [jnp.zeros((8, Q_TILE), jnp.int32) for _ in groups]
    sub = lax.broadcasted_iota(jnp.int32, (8, Q_TILE), 0)
    for bb in range(n_slc):
        kb = key_ref[bb:bb + 1, :]
        kb1 = kb + 1
        for a in range(n_slc // 8):
            if 8 * a > bb:
                ahead = kb1
            elif 8 * a + 7 < bb:
                ahead = kb
            else:
                ahead = jnp.where(sub > bb - 8 * a, kb1, kb)
            ranks[a] = ranks[a] + (ahead > groups[a]).astype(jnp.int32)
    rank = jnp.concatenate(ranks, axis=0)
    sel_t = (rank < n_sel) & causal_t
    first_local = (t0 - SLC_PAD) >> (SLC_BLOCK.bit_length() - 1)
    far_t = blk_t < first_local
    flag_rows = jnp.where(blk_t == 0, 1.0, 0.0)
    stack = jnp.concatenate([jnp.where(sel_t, 0.0, 1.0), flag_rows,
                             jnp.where(sel_t & far_t, 0.0, 1.0), flag_rows], axis=0).astype(BF)
    qmask = _dot_nt(eye_ref[...], stack).astype(BF)
    qa_loc = jnp.concatenate([q4, qmask[:, 0:LANES]], axis=1)
    qa_far = jnp.concatenate([q4, qmask[:, LANES:]], axis=1)

    r0 = pl.multiple_of(t0, Q_TILE)
    key_w = lax.broadcasted_iota(jnp.int32, (1, WIN_KEYS), 1)
    before_start = jnp.where(key_w < WIN_PAD - t0, NEG, 0.0)
    s_w = _dot_nt(q4, kwp_ref[pl.ds(r0, WIN_KEYS), :]) + wb_ref[...].reshape(rows, WIN_KEYS) + before_start
    _, e_w, l_w = _softmax_parts(s_w)
    o_win = _dot(e_w.astype(BF), vwp_ref[pl.ds(r0, WIN_KEYS), :]) * (1.0 / l_w)

    lk = SLC_LOCAL_KEYS
    chunk = SLC_CHUNK_BLOCKS * SLC_BLOCK
    n_far = (first_local + SLC_CHUNK_BLOCKS - 1) // SLC_CHUNK_BLOCKS

    s_loc = _dot_nt(qa_loc, kaug_ref[pl.ds(r0, lk), :]) + sb_ref[...].reshape(rows, lk)
    sloc_ref[...] = s_loc

    def far_logits(c, m_part):
        rr = pl.multiple_of(SLC_PAD + c * chunk, SLC_BLOCK)
        s = _dot_nt(qa_far, kaug_ref[pl.ds(rr, chunk), :])
        sfar_ref[c] = s
        return jnp.maximum(m_part, _lane_tiles(s, jnp.maximum))

    m_part = lax.fori_loop(0, n_far, far_logits, _lane_tiles(s_loc, jnp.maximum))
    m_s = jnp.max(m_part, axis=-1, keepdims=True)

    e_loc = jnp.exp(sloc_ref[...] - m_s)
    acc_s = _dot(e_loc.astype(BF), vsp_ref[pl.ds(r0, lk), :])

    def far_values(c, carry):
        l_part, acc = carry
        rr = pl.multiple_of(SLC_PAD + c * chunk, SLC_BLOCK)
        e = jnp.exp(sfar_ref[c] - m_s)
        return l_part + _lane_tiles(e, jnp.add), acc + _dot(e.astype(BF), vsp_ref[pl.ds(rr, chunk), :])

    l_part, acc_s = lax.fori_loop(0, n_far, far_values, (_lane_tiles(e_loc, jnp.add), acc_s))
    o_slc = acc_s * (1.0 / jnp.sum(l_part, axis=-1, keepdims=True))

    gts = jax.nn.sigmoid(ng_ref[0].astype(F32))
    for hh in range(HPG):
        sl = slice(hh * Q_TILE, (hh + 1) * Q_TILE)
        o_h = (gts[:, 3 * hh:3 * hh + 1] * o_cmp[sl]
               + gts[:, 3 * hh + 1:3 * hh + 2] * o_slc[sl]
               + gts[:, 3 * hh + 2:3 * hh + 3] * o_win[sl])
        o_ref[0, :, hh * HEAD_DIM:(hh + 1) * HEAD_DIM] = o_h.astype(o_ref.dtype)


def _nsa_attention(proj, gates, k_cmp, v_cmp, wb, sb, cf):
    b, s, _ = proj.shape
    ovt = _overlap_matrix(s)
    eye = jnp.asarray(np.tile(np.eye(Q_TILE), (HPG, 1)), BF)
    n_chunks = -(-(s // SLC_BLOCK) // SLC_CHUNK_BLOCKS)
    q_blk = Q_OFF // (HPG * HEAD_DIM)
    kv_blk = KC_OFF // HEAD_DIM
    ng_blk = (gates.shape[2] - N_KV * LANES) // LANES
    lk = SLC_LOCAL_KEYS
    ncp = s // CMP_STRIDE

    def kv_spec(idx):
        return pl.BlockSpec((1, s, HEAD_DIM), lambda i, g, j: (i, 0, kv_blk + idx * N_KV + g))

    cmp_spec = pl.BlockSpec((1, ncp, HEAD_DIM), lambda i, g, j: (i * N_KV + g, 0, 0))
    return pl.pallas_call(
        functools.partial(_attn_kernel, seq=s),
        grid=(b, N_KV, s // Q_TILE),
        in_specs=[pl.BlockSpec((1, Q_TILE, HPG * HEAD_DIM), lambda i, g, j: (i, j, q_blk + g)),
                  kv_spec(2), kv_spec(3), kv_spec(4), kv_spec(5),
                  cmp_spec, cmp_spec,
                  pl.BlockSpec((1, Q_TILE, LANES), lambda i, g, j: (i, j, ng_blk + g)),
                  pl.BlockSpec((HPG, Q_TILE, WIN_KEYS), lambda i, g, j: (g, 0, 0)),
                  pl.BlockSpec((HPG, Q_TILE, lk), lambda i, g, j: (g, 0, 0)),
                  pl.BlockSpec((HPG, Q_TILE, LANES), lambda i, g, j: (g, 0, 0)),
                  pl.BlockSpec(ovt.shape, lambda i, g, j: (0, 0)),
                  pl.BlockSpec(eye.shape, lambda i, g, j: (0, 0))],
        out_specs=pl.BlockSpec((1, Q_TILE, HPG * HEAD_DIM), lambda i, g, j: (i, j, g)),
        out_shape=jax.ShapeDtypeStruct((b, s, Q_WIDTH), BF),
        scratch_shapes=[pltpu.VMEM((SLC_PAD + s, 2 * HEAD_DIM), BF),
                        pltpu.VMEM((SLC_PAD + s, HEAD_DIM), BF),
                        pltpu.VMEM((WIN_PAD + s, HEAD_DIM), BF),
                        pltpu.VMEM((WIN_PAD + s, HEAD_DIM), BF),
                        pltpu.VMEM((s // SLC_BLOCK, Q_TILE), jnp.int32),
                        pltpu.VMEM((HPG * Q_TILE, lk), F32),
                        pltpu.VMEM((n_chunks, HPG * Q_TILE, SLC_CHUNK_BLOCKS * SLC_BLOCK), F32)],
        compiler_params=_cparams("parallel", "parallel", "arbitrary"),
    )(proj, proj, proj, proj, proj, k_cmp, v_cmp, gates, wb, sb, cf, ovt, eye)


def _overlap_matrix(s):
    ncp = s // CMP_STRIDE
    n_slc = s // SLC_BLOCK
    nc = (s - CMP_LEN) // CMP_STRIDE + 1
    i = np.arange(ncp)[None, :]
    jj = np.arange(n_slc)[:, None]
    ov = ((i * CMP_STRIDE < (jj + 1) * SLC_BLOCK) & (i * CMP_STRIDE + CMP_LEN > jj * SLC_BLOCK) & (i < nc))
    return jnp.asarray(ov, BF)


def kernel(x, c, w_ada, b_ada, g_pre_mix, g_post_mix, g_pre_ffn, g_post_ffn, w_in, w_dw, b_dw, conv_ln_g, conv_ln_b, w_conv_out, cmp_pe_k, cmp_pe_v, w_cmp_k1, w_cmp_k2, w_cmp_v1, w_cmp_v2, rel_bias, w_nsa_out, w_out, w_ffn_gate, w_ffn_up, w_ffn_down):
    b, s, d = x.shape
    m = b * s
    depth = w_ada.shape[0]
    c8 = jnp.zeros((8, d), F32).at[:b].set(c)
    wb, sb, cf = _bias_tables(rel_bias)
    col_scale = jnp.ones((1, NG_OFF), F32).at[:, Q_OFF:KC_OFF].set(HEAD_DIM ** -0.5)

    for l in range(depth):
        mod = _modulation(c8, w_ada[l], b_ada[l][None, :])
        mod3 = mod[:b].reshape(b, 6, d)

        w_main = w_in[l][:, :NG_OFF].astype(BF)
        w_ng = w_in[l][:, NG_OFF:MG_OFF].reshape(d, N_KV, 3 * HPG)
        w_ng = jnp.pad(w_ng, ((0, 0), (0, 0), (0, LANES - 3 * HPG))).reshape(d, N_KV * LANES)
        w_gates = jnp.concatenate([w_in[l][:, MG_OFF:], w_ng], axis=1).astype(BF)

        h = _prenorm(x, g_pre_mix[l][None, :], mod3, 0, 1).reshape(m, d)
        proj = _matmul(h, w_main, BF, 1024, 512, col_scale).reshape(b, s, NG_OFF)
        gates = _matmul(h, w_gates, BF, 1024, 512)

        u_conv = _conv_module(proj, w_dw[l], b_dw[l][None, :], conv_ln_g[l][None, :], conv_ln_b[l][None, :])

        def grouped(off):
            t = proj[:, :, off:off + KV_WIDTH].reshape(b, s // CMP_STRIDE, CMP_STRIDE, N_KV, HEAD_DIM)
            return t.transpose(0, 3, 1, 2, 4).reshape(b * N_KV, s // CMP_STRIDE, CMP_STRIDE * HEAD_DIM)

        pe_k = jnp.broadcast_to(cmp_pe_k[l].reshape(1, -1), (8, CMP_LEN * HEAD_DIM)).astype(BF)
        pe_v = jnp.broadcast_to(cmp_pe_v[l].reshape(1, -1), (8, CMP_LEN * HEAD_DIM)).astype(BF)
        k_cmp = _compress(grouped(KC_OFF), pe_k, w_cmp_k1[l].astype(BF), w_cmp_k2[l].astype(BF))
        v_cmp = _compress(grouped(KC_OFF + KV_WIDTH), pe_v, w_cmp_v1[l].astype(BF), w_cmp_v2[l].astype(BF))

        o_nsa = _nsa_attention(proj, gates.reshape(b, s, -1), k_cmp, v_cmp, wb, sb, cf)

        z = _gated_mix(u_conv.reshape(m, CONV_DIM), o_nsa.reshape(m, Q_WIDTH),
                       w_conv_out[l].astype(BF), w_nsa_out[l].astype(BF), gates, d)
        mix = _matmul(z, w_out[l].astype(BF), F32, 1024, 512).reshape(b, s, d)
        x, h2 = _post_mix(mix, x, g_post_mix[l][None, :], g_pre_ffn[l][None, :], mod3, 2, 3, 4)

        ff = _ffn_up(h2.reshape(m, d), w_ffn_gate[l].astype(BF), w_ffn_up[l].astype(BF))
        f = _matmul(ff, w_ffn_down[l].astype(BF), F32, 512, 256).reshape(b, s, d)
        x = _post_ffn(f, x, g_post_ffn[l][None, :], mod3, 5)
    return x
```

```python
import functools
import math

import numpy as np
import jax
import jax.numpy as jnp
from jax import lax
from jax.experimental import pallas as pl
from jax.experimental.pallas import tpu as pltpu

F32 = jnp.float32
BF = jnp.bfloat16

CONV_DIM = 2048
CONV_WIDTH = 31
N_HEADS = 16
N_KV = 4
HPG = N_HEADS // N_KV
HEAD_DIM = 128
CMP_LEN = 32
CMP_STRIDE = 16
CMP_HID = 512
SLC_BLOCK = 64
N_SEL = 16
WINDOW = 512
Q_BLOCK = 64
REL_BUCKETS = 32
REL_MAX_DIST = 128
NORM_EPS = 1e-6
NEG = -1e30
LOG2E = 1.4426950408889634

Q_WIDTH = N_HEADS * HEAD_DIM
KV_WIDTH = N_KV * HEAD_DIM
Q_OFF = 2 * CONV_DIM
KC_OFF = Q_OFF + Q_WIDTH
NG_OFF = KC_OFF + 6 * KV_WIDTH
MG_OFF = NG_OFF + 3 * N_HEADS

LANES = 128
SUBLANES = 8
VMEM_LIMIT = 56 * 1024 * 1024

CONV_HALO = 32
CONV_ROWS = 64
CONV_LANES = 256

Q_TILE = 2 * Q_BLOCK
SLC_PAD = 2 * SLC_BLOCK
SLC_LOCAL_KEYS = SLC_PAD + Q_TILE
SLC_CHUNK_BLOCKS = 16
WIN_KEYS = WINDOW + Q_TILE
WIN_PAD = WINDOW
CMP_PER_TILE = Q_TILE // CMP_STRIDE
CMP_BAND = 16
CMP_BAND_LO = 9
PAD_FLAG_COL = 64
KEY_FORCED = 0x7F000000
KEY_FUTURE = -2


def _t5_thresholds():
    max_exact = REL_BUCKETS // 2
    thr = []
    for k in range(1, REL_BUCKETS - max_exact):
        n = max_exact
        while max_exact + int(math.log(n / max_exact) / math.log(REL_MAX_DIST / max_exact)
                              * (REL_BUCKETS - max_exact)) < max_exact + k:
            n += 1
        thr.append(n)
    return tuple(thr)


T5_THRESHOLDS = _t5_thresholds()


def _cparams(*sem):
    return pltpu.CompilerParams(dimension_semantics=sem, vmem_limit_bytes=VMEM_LIMIT)


def _silu(v):
    return v * jax.nn.sigmoid(v)


def _dot(a, b):
    return jnp.dot(a, b, preferred_element_type=F32)


def _dot_nt(a, b):
    return lax.dot_general(a, b, (((1,), (1,)), ((), ())), preferred_element_type=F32)


def _rms(v, g):
    return v * lax.rsqrt(jnp.mean(v * v, axis=-1, keepdims=True) + NORM_EPS) * g


def _mod_kernel(c_ref, w_ref, b_ref, o_ref):
    a = _silu(c_ref[...]).astype(BF)
    o_ref[...] = _dot(a, w_ref[...].astype(BF)) + b_ref[...]


def _modulation(c8, w_ada, b_ada, tn=512):
    rows, d = c8.shape
    n = w_ada.shape[1]
    return pl.pallas_call(
        _mod_kernel,
        grid=(n // tn,),
        in_specs=[pl.BlockSpec((rows, d), lambda j: (0, 0)),
                  pl.BlockSpec((d, tn), lambda j: (0, j)),
                  pl.BlockSpec((1, tn), lambda j: (0, j))],
        out_specs=pl.BlockSpec((rows, tn), lambda j: (0, j)),
        out_shape=jax.ShapeDtypeStruct((rows, n), F32),
        compiler_params=_cparams("parallel"),
    )(c8, w_ada, b_ada)


def _prenorm_kernel(x_ref, g_ref, mod_ref, h_ref, *, shift_row, scale_row):
    y = _rms(x_ref[0], g_ref[...])
    h = y * (1.0 + mod_ref[0, scale_row:scale_row + 1, :]) + mod_ref[0, shift_row:shift_row + 1, :]
    h_ref[0] = h.astype(BF)


def _prenorm(x, g, mod3, shift_row, scale_row, tr=256):
    b, s, d = x.shape
    return pl.pallas_call(
        functools.partial(_prenorm_kernel, shift_row=shift_row, scale_row=scale_row),
        grid=(b, s // tr),
        in_specs=[pl.BlockSpec((1, tr, d), lambda i, r: (i, r, 0)),
                  pl.BlockSpec((1, d), lambda i, r: (0, 0)),
                  pl.BlockSpec((1, 6, d), lambda i, r: (i, 0, 0))],
        out_specs=pl.BlockSpec((1, tr, d), lambda i, r: (i, r, 0)),
        out_shape=jax.ShapeDtypeStruct((b, s, d), BF),
        compiler_params=_cparams("parallel", "parallel"),
    )(x, g, mod3)


def _post_mix_kernel(y_ref, x_ref, gp_ref, gn_ref, mod_ref, xo_ref, h_ref, *, gate_row, shift_row, scale_row):
    x1 = x_ref[0] + mod_ref[0, gate_row:gate_row + 1, :] * _rms(y_ref[0], gp_ref[...])
    xo_ref[0] = x1
    h = _rms(x1, gn_ref[...]) * (1.0 + mod_ref[0, scale_row:scale_row + 1, :]) \
        + mod_ref[0, shift_row:shift_row + 1, :]
    h_ref[0] = h.astype(BF)


def _post_mix(y, x, g_post, g_next, mod3, gate_row, shift_row, scale_row, tr=256):
    b, s, d = x.shape
    row = pl.BlockSpec((1, tr, d), lambda i, r: (i, r, 0))
    vec = pl.BlockSpec((1, d), lambda i, r: (0, 0))
    return pl.pallas_call(
        functools.partial(_post_mix_kernel, gate_row=gate_row, shift_row=shift_row, scale_row=scale_row),
        grid=(b, s // tr),
        in_specs=[row, row, vec, vec, pl.BlockSpec((1, 6, d), lambda i, r: (i, 0, 0))],
        out_specs=[row, row],
        out_shape=[jax.ShapeDtypeStruct((b, s, d), F32), jax.ShapeDtypeStruct((b, s, d), BF)],
        compiler_params=_cparams("parallel", "parallel"),
    )(y, x, g_post, g_next, mod3)


def _post_ffn_kernel(y_ref, x_ref, gp_ref, mod_ref, xo_ref, *, gate_row):
    xo_ref[0] = x_ref[0] + mod_ref[0, gate_row:gate_row + 1, :] * _rms(y_ref[0], gp_ref[...])


def _post_ffn(y, x, g_post, mod3, gate_row, tr=256):
    b, s, d = x.shape
    row = pl.BlockSpec((1, tr, d), lambda i, r: (i, r, 0))
    return pl.pallas_call(
        functools.partial(_post_ffn_kernel, gate_row=gate_row),
        grid=(b, s // tr),
        in_specs=[row, row, pl.BlockSpec((1, d), lambda i, r: (0, 0)),
                  pl.BlockSpec((1, 6, d), lambda i, r: (i, 0, 0))],
        out_specs=row,
        out_shape=jax.ShapeDtypeStruct((b, s, d), F32),
        compiler_params=_cparams("parallel", "parallel"),
    )(y, x, g_post, mod3)


def _mm_scale_kernel(a_ref, w_ref, s_ref, o_ref):
    o_ref[...] = (_dot(a_ref[...], w_ref[...]) * s_ref[...]).astype(o_ref.dtype)


def _mm_kernel(a_ref, w_ref, o_ref):
    o_ref[...] = _dot(a_ref[...], w_ref[...]).astype(o_ref.dtype)


def _matmul(a, w, out_dtype, tm, tn, col_scale=None):
    m, k = a.shape
    n = w.shape[1]
    in_specs = [pl.BlockSpec((tm, k), lambda i, j: (i, 0)),
                pl.BlockSpec((k, tn), lambda i, j: (0, j))]
    args = [a, w]
    body = _mm_kernel
    if col_scale is not None:
        in_specs.append(pl.BlockSpec((1, tn), lambda i, j: (0, j)))
        args.append(col_scale)
        body = _mm_scale_kernel
    return pl.pallas_call(
        body,
        grid=(m // tm, n // tn),
        in_specs=in_specs,
        out_specs=pl.BlockSpec((tm, tn), lambda i, j: (i, j)),
        out_shape=jax.ShapeDtypeStruct((m, n), out_dtype),
        compiler_params=_cparams("parallel", "arbitrary"),
    )(*args)


def _mm_tail_kernel(a_ref, w_ref, wt_ref, s_ref, o_ref, *, n_main):
    j = pl.program_id(1)

    @pl.when(j < n_main)
    def _main():
        o_ref[...] = (_dot(a_ref[...], w_ref[...]) * s_ref[...]).astype(o_ref.dtype)

    @pl.when(j >= n_main)
    def _tail():
        o_ref[...] = (_dot(a_ref[...], wt_ref[...]) * s_ref[...]).astype(o_ref.dtype)


def _matmul_with_tail(a, w, w_tail, col_scale, out_dtype, tm, tn):
    m, k = a.shape
    n_main = w.shape[1] // tn
    n_tail = w_tail.shape[1] // tn
    n = w.shape[1] + w_tail.shape[1]
    return pl.pallas_call(
        functools.partial(_mm_tail_kernel, n_main=n_main),
        grid=(m // tm, n_main + n_tail),
        in_specs=[pl.BlockSpec((tm, k), lambda i, j: (i, 0)),
                  pl.BlockSpec((k, tn), lambda i, j: (0, jnp.minimum(j, n_main - 1))),
                  pl.BlockSpec((k, tn), lambda i, j: (0, jnp.maximum(j - n_main, 0))),
                  pl.BlockSpec((1, tn), lambda i, j: (0, j))],
        out_specs=pl.BlockSpec((tm, tn), lambda i, j: (i, j)),
        out_shape=jax.ShapeDtypeStruct((m, n), out_dtype),
        compiler_params=_cparams("parallel", "arbitrary"),
    )(a, w, w_tail, col_scale)


def _mix_kernel(a1_ref, a2_ref, w1_ref, w2_ref, ga_ref, gb_ref, o_ref):
    ya = _dot(a1_ref[...], w1_ref[...])
    yb = _dot(a2_ref[...], w2_ref[...])
    o = jax.nn.sigmoid(ga_ref[...].astype(F32)) * ya + jax.nn.sigmoid(gb_ref[...].astype(F32)) * yb
    o_ref[...] = o.astype(o_ref.dtype)


def _gated_mix(u_conv, o_nsa, w_conv_out, w_nsa_out, gates, d_model, tm=1024, tn=512):
    m, ka = u_conv.shape
    kb = o_nsa.shape[1]
    gb_block = d_model // tn
    return pl.pallas_call(
        _mix_kernel,
        grid=(m // tm, d_model // tn),
        in_specs=[pl.BlockSpec((tm, ka), lambda i, j: (i, 0)),
                  pl.BlockSpec((tm, kb), lambda i, j: (i, 0)),
                  pl.BlockSpec((ka, tn), lambda i, j: (0, j)),
                  pl.BlockSpec((kb, tn), lambda i, j: (0, j)),
                  pl.BlockSpec((tm, tn), lambda i, j: (i, j)),
                  pl.BlockSpec((tm, tn), lambda i, j: (i, j + gb_block))],
        out_specs=pl.BlockSpec((tm, tn), lambda i, j: (i, j)),
        out_shape=jax.ShapeDtypeStruct((m, d_model), BF),
        compiler_params=_cparams("parallel", "arbitrary"),
    )(u_conv, o_nsa, w_conv_out, w_nsa_out, gates, gates)


def _ffn_up_kernel(a_ref, wg_ref, wu_ref, o_ref):
    a = a_ref[...]
    o_ref[...] = (_silu(_dot(a, wg_ref[...])) * _dot(a, wu_ref[...])).astype(o_ref.dtype)


def _ffn_up(h, w_gate, w_up, tm=1024, tn=256):
    m, k = h.shape
    n = w_gate.shape[1]
    wspec = pl.BlockSpec((k, tn), lambda i, j: (0, j))
    return pl.pallas_call(
        _ffn_up_kernel,
        grid=(m // tm, n // tn),
        in_specs=[pl.BlockSpec((tm, k), lambda i, j: (i, 0)), wspec, wspec],
        out_specs=pl.BlockSpec((tm, tn), lambda i, j: (i, j)),
        out_shape=jax.ShapeDtypeStruct((m, n), BF),
        compiler_params=_cparams("parallel", "arbitrary"),
    )(h, w_gate, w_up)


def _conv_kernel(a_ref, b_ref, ah_ref, bh_ref, w_ref, bdw_ref, g_ref, bb_ref, o_ref, u_ref, sh_ref, v_ref, *, ts):
    i = pl.program_id(1)
    u_ref[CONV_HALO:CONV_HALO + ts, :] = a_ref[0].astype(F32) * jax.nn.sigmoid(b_ref[0].astype(F32))
    uh = ah_ref[0].astype(F32) * jax.nn.sigmoid(bh_ref[0].astype(F32))
    u_ref[0:CONV_HALO, :] = jnp.where(i > 0, uh, 0.0)

    rc, cc = CONV_ROWS, CONV_LANES
    first = CONV_HALO - (CONV_WIDTH - 1)
    span = ts + CONV_HALO - SUBLANES
    for c0 in range(0, CONV_DIM, cc):
        for s in range(1, SUBLANES):
            sh_ref[s - 1] = u_ref[s:s + span, c0:c0 + cc]

        def row_chunk(rb, carry, c0=c0):
            r0 = pl.multiple_of(rb * rc, rc)
            acc = jnp.zeros((rc, cc), F32)
            for k in range(CONV_WIDTH):
                s = (first + k) % SUBLANES
                base = first + k - s
                if s == 0:
                    tap = u_ref[pl.ds(r0 + base, rc), c0:c0 + cc]
                else:
                    tap = sh_ref[s - 1, pl.ds(r0 + base, rc), :]
                acc = acc + tap * w_ref[k:k + 1, c0:c0 + cc]
            v_ref[pl.ds(r0, rc), c0:c0 + cc] = acc + bdw_ref[:, c0:c0 + cc]
            return carry

        lax.fori_loop(0, ts // rc, row_chunk, 0)

    v = v_ref[...]
    mu = jnp.mean(v, axis=-1, keepdims=True)
    vc = v - mu
    y = vc * lax.rsqrt(jnp.mean(vc * vc, axis=-1, keepdims=True) + NORM_EPS) * g_ref[...] + bb_ref[...]
    o_ref[0] = _silu(y).astype(o_ref.dtype)


def _conv_module(proj, w_dw, b_dw, ln_g, ln_b, ts=256):
    b, s, _ = proj.shape
    hb = ts // CONV_HALO
    cur_a = pl.BlockSpec((1, ts, CONV_DIM), lambda i, r: (i, r, 0))
    cur_b = pl.BlockSpec((1, ts, CONV_DIM), lambda i, r: (i, r, 1))
    halo_a = pl.BlockSpec((1, CONV_HALO, CONV_DIM), lambda i, r: (i, jnp.maximum(r * hb - 1, 0), 0))
    halo_b = pl.BlockSpec((1, CONV_HALO, CONV_DIM), lambda i, r: (i, jnp.maximum(r * hb - 1, 0), 1))
    vec = pl.BlockSpec((1, CONV_DIM), lambda i, r: (0, 0))
    return pl.pallas_call(
        functools.partial(_conv_kernel, ts=ts),
        grid=(b, s // ts),
        in_specs=[cur_a, cur_b, halo_a, halo_b,
                  pl.BlockSpec((CONV_WIDTH, CONV_DIM), lambda i, r: (0, 0)), vec, vec, vec],
        out_specs=pl.BlockSpec((1, ts, CONV_DIM), lambda i, r: (i, r, 0)),
        out_shape=jax.ShapeDtypeStruct((b, s, CONV_DIM), BF),
        scratch_shapes=[pltpu.VMEM((CONV_HALO + ts, CONV_DIM), F32),
                        pltpu.VMEM((SUBLANES - 1, ts + CONV_HALO - SUBLANES, CONV_LANES), F32),
                        pltpu.VMEM((ts, CONV_DIM), F32)],
        compiler_params=_cparams("parallel", "arbitrary"),
    )(proj, proj, proj, proj, w_dw, b_dw, ln_g, ln_b)


def _compress_kernel(k_ref, pe_ref, w1_ref, w2_ref, o_ref):
    k16 = k_ref[0]
    half = CMP_STRIDE * HEAD_DIM
    lo = _dot(k16, w1_ref[0:half, :])
    hi = _dot(k16, w1_ref[half:2 * half, :])
    pe = _dot(pe_ref[...], w1_ref[...])
    rows = k16.shape[0]
    hid = lo + pltpu.roll(hi, rows - 1, axis=0) + pe[0:1, :]
    act = 0.5 * hid * (1.0 + jnp.tanh(math.sqrt(2.0 / math.pi) * (hid + 0.044715 * hid * hid * hid)))
    o_ref[0] = _dot(act.astype(BF), w2_ref[...])


def _compress(k16, pe8, w1, w2):
    n, rows, width = k16.shape
    return pl.pallas_call(
        _compress_kernel,
        grid=(n,),
        in_specs=[pl.BlockSpec((1, rows, width), lambda i: (i, 0, 0)),
                  pl.BlockSpec(pe8.shape, lambda i: (0, 0)),
                  pl.BlockSpec(w1.shape, lambda i: (0, 0)),
                  pl.BlockSpec(w2.shape, lambda i: (0, 0))],
        out_specs=pl.BlockSpec((1, rows, HEAD_DIM), lambda i: (i, 0, 0)),
        out_shape=jax.ShapeDtypeStruct((n, rows, HEAD_DIM), F32),
        compiler_params=_cparams("parallel"),
    )(k16, pe8, w1, w2)


def _bias_tables_kernel(rb_ref, wb_ref, sb_ref, cf_ref):
    h = pl.program_id(0)
    last = rb_ref[REL_BUCKETS - 1, h]

    def rel(dist):
        n = jnp.maximum(dist, 0)
        bucket = jnp.full(n.shape, REL_BUCKETS // 2, jnp.int32)
        for t in T5_THRESHOLDS:
            bucket = bucket + (n >= t).astype(jnp.int32)
        bucket = jnp.where(n < REL_BUCKETS // 2, n, jnp.minimum(bucket, REL_BUCKETS - 1))
        out = jnp.zeros(n.shape, F32)
        for bk in range(REL_BUCKETS - 1):
            out = jnp.where(bucket == bk, rb_ref[bk, h] - last, out)
        return out * LOG2E

    def dist_of(shape, offset):
        qi = lax.broadcasted_iota(jnp.int32, shape, 0)
        kl = lax.broadcasted_iota(jnp.int32, shape, 1)
        return qi + offset - kl

    dw = dist_of((Q_TILE, WIN_KEYS), WIN_PAD)
    wb_ref[0] = jnp.where((dw >= 0) & (dw < WINDOW), rel(dw), NEG)

    ds_ = dist_of((Q_TILE, SLC_LOCAL_KEYS), SLC_PAD)
    sb_ref[0] = jnp.where(ds_ >= 0, rel(ds_), NEG)

    qi = lax.broadcasted_iota(jnp.int32, (Q_TILE, LANES), 0)
    col = lax.broadcasted_iota(jnp.int32, (Q_TILE, LANES), 1)
    mi = col & (CMP_BAND - 1)
    part = col >> (CMP_BAND.bit_length() - 1)
    dc = qi - CMP_STRIDE * (mi - CMP_BAND_LO) - (CMP_LEN - 1)
    val = jnp.where(dc >= 0, rel(dc), 0.0)
    p0 = val.astype(BF).astype(F32)
    r1 = val - p0
    p1 = r1.astype(BF).astype(F32)
    p2 = r1 - p1
    parts = jnp.where(part == 0, p0, jnp.where(part == 1, p1, jnp.where(part == 2, p2, 0.0)))
    cf_ref[0] = parts.astype(BF)


def _bias_tables(rel_bias):
    smem = pl.BlockSpec(memory_space=pltpu.SMEM)
    lk = SLC_LOCAL_KEYS
    return pl.pallas_call(
        _bias_tables_kernel,
        grid=(N_HEADS,),
        in_specs=[smem],
        out_specs=[pl.BlockSpec((1, Q_TILE, WIN_KEYS), lambda h: (h, 0, 0)),
                   pl.BlockSpec((1, Q_TILE, lk), lambda h: (h, 0, 0)),
                   pl.BlockSpec((1, Q_TILE, LANES), lambda h: (h, 0, 0))],
        out_shape=[jax.ShapeDtypeStruct((N_HEADS, Q_TILE, WIN_KEYS), F32),
                   jax.ShapeDtypeStruct((N_HEADS, Q_TILE, lk), F32),
                   jax.ShapeDtypeStruct((N_HEADS, Q_TILE, LANES), BF)],
        compiler_params=_cparams("arbitrary"),
    )(rel_bias)


def _softmax_parts(s):
    m = jnp.max(s, axis=-1, keepdims=True)
    e = jnp.exp2(s - m)
    return m, e, jnp.sum(e, axis=-1, keepdims=True)


def _lane_tiles(v, op):
    out = v[:, 0:LANES]
    for t in range(1, v.shape[1] // LANES):
        out = op(out, v[:, t * LANES:(t + 1) * LANES])
    return out


def _attn_kernel(q_ref, ks_ref, vs_ref, kw_ref, vw_ref, kc_ref, vc_ref, ng_ref,
                 wb_ref, sb_ref, cf_ref, ovt_ref, eye_ref, o_ref,
                 kaug_ref, vsp_ref, kwp_ref, vwp_ref, key_ref, sloc_ref, sfar_ref, *, seq):
    tile = pl.program_id(2)
    n_slc = seq // SLC_BLOCK
    ncp = seq // CMP_STRIDE
    rows = HPG * Q_TILE
    n_sel = min(N_SEL, n_slc)
    if n_slc != PAD_FLAG_COL or Q_TILE != LANES:
        raise NotImplementedError("selection layout expects 64 selection blocks and 128-query tiles")

    @pl.when(tile == 0)
    def _stage_keys():
        col = lax.broadcasted_iota(jnp.int32, (SLC_PAD, LANES), 1)
        kaug_ref[0:SLC_PAD, 0:HEAD_DIM] = jnp.zeros((SLC_PAD, HEAD_DIM), BF)
        kaug_ref[0:SLC_PAD, HEAD_DIM:] = jnp.where(col == PAD_FLAG_COL, NEG, 0.0).astype(BF)
        kaug_ref[SLC_PAD:, 0:HEAD_DIM] = ks_ref[0]
        blk = lax.broadcasted_iota(jnp.int32, (seq, LANES), 0) >> (SLC_BLOCK.bit_length() - 1)
        colk = lax.broadcasted_iota(jnp.int32, (seq, LANES), 1)
        kaug_ref[SLC_PAD:, HEAD_DIM:] = jnp.where(colk == blk, NEG, 0.0).astype(BF)
        ones_s = jnp.where(lax.broadcasted_iota(jnp.int32, (SLC_PAD + seq, LANES), 1) == 0, 1.0, 0.0)
        vsp_ref[:, HEAD_DIM:] = ones_s.astype(BF)
        vsp_ref[0:SLC_PAD, 0:HEAD_DIM] = jnp.zeros((SLC_PAD, HEAD_DIM), BF)
        vsp_ref[SLC_PAD:, 0:HEAD_DIM] = vs_ref[0]
        kwp_ref[0:WIN_PAD, :] = jnp.zeros((WIN_PAD, HEAD_DIM), BF)
        kwp_ref[WIN_PAD:, :] = kw_ref[0]
        ones_w = jnp.where(lax.broadcasted_iota(jnp.int32, (WIN_PAD + seq, LANES), 1) == 0, 1.0, 0.0)
        vwp_ref[:, HEAD_DIM:] = ones_w.astype(BF)
        vwp_ref[0:WIN_PAD, 0:HEAD_DIM] = jnp.zeros((WIN_PAD, HEAD_DIM), BF)
        vwp_ref[WIN_PAD:, 0:HEAD_DIM] = vw_ref[0]

    q4 = jnp.concatenate([q_ref[0, :, hh * HEAD_DIM:(hh + 1) * HEAD_DIM] for hh in range(HPG)], axis=0)
    t0 = tile * Q_TILE

    kc = kc_ref[0].astype(BF)
    vc = vc_ref[0].astype(BF)
    r_io = lax.broadcasted_iota(jnp.int32, (LANES, ncp), 0)
    n_io = lax.broadcasted_iota(jnp.int32, (LANES, ncp), 1)
    band = (r_io < 3 * CMP_BAND) & (n_io == CMP_PER_TILE * tile + (r_io & (CMP_BAND - 1)) - CMP_BAND_LO)
    shift = jnp.where(band, 1.0, 0.0).astype(BF)
    cf4 = cf_ref[...].reshape(rows, LANES)
    lc = _dot_nt(q4, kc) + _dot(cf4, shift)
    qi_c = lax.broadcasted_iota(jnp.int32, (Q_TILE, ncp), 0)
    n_c = lax.broadcasted_iota(jnp.int32, (Q_TILE, ncp), 1)
    vis = (CMP_STRIDE * n_c + (CMP_LEN - 1)) <= (t0 + qi_c)
    lc3 = jnp.where(vis[None], lc.reshape(HPG, Q_TILE, ncp), NEG)
    _, e_c, l_c = _softmax_parts(lc3)
    p_c = jnp.where(vis[None], e_c * (1.0 / l_c), 0.0)
    o_cmp = _dot(p_c.reshape(rows, ncp).astype(BF), vc)

    psum = p_c[0]
    for hh in range(1, HPG):
        psum = psum + p_c[hh]
    p0 = psum.astype(BF)
    r1 = psum - p0.astype(F32)
    p1 = r1.astype(BF)
    p2 = (r1 - p1.astype(F32)).astype(BF)
    ovt = ovt_ref[...]
    imp_t = _dot_nt(ovt, p0) + _dot_nt(ovt, p1) + _dot_nt(ovt, p2)

    blk_t = lax.broadcasted_iota(jnp.int32, (n_slc, Q_TILE), 0)
    q_t = lax.broadcasted_iota(jnp.int32, (n_slc, Q_TILE), 1)
    cur_t = (t0 + q_t) >> (SLC_BLOCK.bit_length() - 1)
    forced_t = (blk_t == 0) | (blk_t == cur_t) | (blk_t == cur_t - 1)
    causal_t = blk_t <= cur_t
    key = jnp.where(forced_t, KEY_FORCED, jnp.where(causal_t, lax.bitcast_convert_type(imp_t, jnp.int32), KEY_FUTURE))
    key_ref[...] = key
    groups = [key[8 * a:8 * a + 8, :] for a in range(n_slc // 8)]
    ranks = [jnp.zeros((8, Q_TILE), jnp.int32) for _ in groups]
    sub = lax.broadcasted_iota(jnp.int32, (8, Q_TILE), 0)
    for bb in range(n_slc):
        kb = key_ref[bb:bb + 1, :]
        kb1 = kb + 1
        for a in range(n_slc // 8):
            if 8 * a > bb:
                ahead = kb1
            elif 8 * a + 7 < bb:
                ahead = kb
            else:
                ahead = jnp.where(sub > bb - 8 * a, kb1, kb)
            ranks[a] = ranks[a] + (ahead > groups[a]).astype(jnp.int32)
    rank = jnp.concatenate(ranks, axis=0)
    sel_t = (rank < n_sel) & causal_t
    first_local = (t0 - SLC_PAD) >> (SLC_BLOCK.bit_length() - 1)
    far_t = blk_t < first_local
    flag_rows = jnp.where(blk_t == 0, 1.0, 0.0)
    stack = jnp.concatenate([jnp.where(sel_t, 0.0, 1.0), flag_rows,
                             jnp.where(sel_t & far_t, 0.0, 1.0), flag_rows], axis=0).astype(BF)
    qmask = _dot_nt(eye_ref[...], stack).astype(BF)
    qa_loc = jnp.concatenate([q4, qmask[:, 0:LANES]], axis=1)
    qa_far = jnp.concatenate([q4, qmask[:, LANES:]], axis=1)

    r0 = pl.multiple_of(t0, Q_TILE)
    key_w = lax.broadcasted_iota(jnp.int32, (1, WIN_KEYS), 1)
    before_start = jnp.where(key_w < WIN_PAD - t0, NEG, 0.0)
    s_w = _dot_nt(q4, kwp_ref[pl.ds(r0, WIN_KEYS), :]) + wb_ref[...].reshape(rows, WIN_KEYS) + before_start
    e_w = jnp.exp2(s_w - jnp.max(s_w, axis=-1, keepdims=True))
    pv_w = _dot(e_w.astype(BF), vwp_ref[pl.ds(r0, WIN_KEYS), :])
    o_win = pv_w[:, 0:HEAD_DIM] * (1.0 / pv_w[:, HEAD_DIM:HEAD_DIM + 1])

    lk = SLC_LOCAL_KEYS
    chunk = SLC_CHUNK_BLOCKS * SLC_BLOCK
    n_far = (first_local + SLC_CHUNK_BLOCKS - 1) // SLC_CHUNK_BLOCKS

    s_loc = _dot_nt(qa_loc, kaug_ref[pl.ds(r0, lk), :]) + sb_ref[...].reshape(rows, lk)
    sloc_ref[...] = s_loc

    def far_logits(c, m_part):
        rr = pl.multiple_of(SLC_PAD + c * chunk, SLC_BLOCK)
        s = _dot_nt(qa_far, kaug_ref[pl.ds(rr, chunk), :])
        sfar_ref[c] = s
        return jnp.maximum(m_part, _lane_tiles(s, jnp.maximum))

    m_part = lax.fori_loop(0, n_far, far_logits, _lane_tiles(s_loc, jnp.maximum))
    m_s = jnp.max(m_part, axis=-1, keepdims=True)

    e_loc = jnp.exp2(sloc_ref[...] - m_s)
    acc_s = _dot(e_loc.astype(BF), vsp_ref[pl.ds(r0, lk), :])

    def far_values(c, acc):
        rr = pl.multiple_of(SLC_PAD + c * chunk, SLC_BLOCK)
        e = jnp.exp2(sfar_ref[c] - m_s)
        return acc + _dot(e.astype(BF), vsp_ref[pl.ds(rr, chunk), :])

    acc_s = lax.fori_loop(0, n_far, far_values, acc_s)
    o_slc = acc_s[:, 0:HEAD_DIM] * (1.0 / acc_s[:, HEAD_DIM:HEAD_DIM + 1])

    gts = jax.nn.sigmoid(ng_ref[0].astype(F32))
    for hh in range(HPG):
        sl = slice(hh * Q_TILE, (hh + 1) * Q_TILE)
        o_h = (gts[:, 3 * hh:3 * hh + 1] * o_cmp[sl]
               + gts[:, 3 * hh + 1:3 * hh + 2] * o_slc[sl]
               + gts[:, 3 * hh + 2:3 * hh + 3] * o_win[sl])
        o_ref[0, :, hh * HEAD_DIM:(hh + 1) * HEAD_DIM] = o_h.astype(o_ref.dtype)


def _nsa_attention(proj, k_cmp, v_cmp, wb, sb, cf):
    b, s, _ = proj.shape
    ovt = _overlap_matrix(s)
    eye = jnp.asarray(np.tile(np.eye(Q_TILE), (HPG, 1)), BF)
    n_chunks = -(-(s // SLC_BLOCK) // SLC_CHUNK_BLOCKS)
    q_blk = Q_OFF // (HPG * HEAD_DIM)
    kv_blk = KC_OFF // HEAD_DIM
    ng_blk = NG_OFF // LANES
    lk = SLC_LOCAL_KEYS
    ncp = s // CMP_STRIDE

    def kv_spec(idx):
        return pl.BlockSpec((1, s, HEAD_DIM), lambda i, g, j: (i, 0, kv_blk + idx * N_KV + g))

    cmp_spec = pl.BlockSpec((1, ncp, HEAD_DIM), lambda i, g, j: (i * N_KV + g, 0, 0))
    return pl.pallas_call(
        functools.partial(_attn_kernel, seq=s),
        grid=(b, N_KV, s // Q_TILE),
        in_specs=[pl.BlockSpec((1, Q_TILE, HPG * HEAD_DIM), lambda i, g, j: (i, j, q_blk + g)),
                  kv_spec(2), kv_spec(3), kv_spec(4), kv_spec(5),
                  cmp_spec, cmp_spec,
                  pl.BlockSpec((1, Q_TILE, LANES), lambda i, g, j: (i, j, ng_blk + g)),
                  pl.BlockSpec((HPG, Q_TILE, WIN_KEYS), lambda i, g, j: (g, 0, 0)),
                  pl.BlockSpec((HPG, Q_TILE, lk), lambda i, g, j: (g, 0, 0)),
                  pl.BlockSpec((HPG, Q_TILE, LANES), lambda i, g, j: (g, 0, 0)),
                  pl.BlockSpec(ovt.shape, lambda i, g, j: (0, 0)),
                  pl.BlockSpec(eye.shape, lambda i, g, j: (0, 0))],
        out_specs=pl.BlockSpec((1, Q_TILE, HPG * HEAD_DIM), lambda i, g, j: (i, j, g)),
        out_shape=jax.ShapeDtypeStruct((b, s, Q_WIDTH), BF),
        scratch_shapes=[pltpu.VMEM((SLC_PAD + s, 2 * HEAD_DIM), BF),
                        pltpu.VMEM((SLC_PAD + s, HEAD_DIM + LANES), BF),
                        pltpu.VMEM((WIN_PAD + s, HEAD_DIM), BF),
                        pltpu.VMEM((WIN_PAD + s, HEAD_DIM + LANES), BF),
                        pltpu.VMEM((s // SLC_BLOCK, Q_TILE), jnp.int32),
                        pltpu.VMEM((HPG * Q_TILE, lk), F32),
                        pltpu.VMEM((n_chunks, HPG * Q_TILE, SLC_CHUNK_BLOCKS * SLC_BLOCK), F32)],
        compiler_params=_cparams("parallel", "parallel", "arbitrary"),
    )(proj, proj, proj, proj, proj, k_cmp, v_cmp, proj, wb, sb, cf, ovt, eye)


def _overlap_matrix(s):
    ncp = s // CMP_STRIDE
    n_slc = s // SLC_BLOCK
    nc = (s - CMP_LEN) // CMP_STRIDE + 1
    i = np.arange(ncp)[None, :]
    jj = np.arange(n_slc)[:, None]
    ov = ((i * CMP_STRIDE < (jj + 1) * SLC_BLOCK) & (i * CMP_STRIDE + CMP_LEN > jj * SLC_BLOCK) & (i < nc))
    return jnp.asarray(ov, BF)


def kernel(x, c, w_ada, b_ada, g_pre_mix, g_post_mix, g_pre_ffn, g_post_ffn, w_in, w_dw, b_dw, conv_ln_g, conv_ln_b, w_conv_out, cmp_pe_k, cmp_pe_v, w_cmp_k1, w_cmp_k2, w_cmp_v1, w_cmp_v2, rel_bias, w_nsa_out, w_out, w_ffn_gate, w_ffn_up, w_ffn_down):
    b, s, d = x.shape
    m = b * s
    depth = w_ada.shape[0]
    c8 = jnp.zeros((8, d), F32).at[:b].set(c)
    wb, sb, cf = _bias_tables(rel_bias)
    col_scale = jnp.ones((1, NG_OFF + N_KV * LANES), F32).at[:, Q_OFF:KC_OFF].set(HEAD_DIM ** -0.5 * LOG2E)

    for l in range(depth):
        mod = _modulation(c8, w_ada[l], b_ada[l][None, :])
        mod3 = mod[:b].reshape(b, 6, d)

        w_main = w_in[l][:, :NG_OFF].astype(BF)
        w_ng = w_in[l][:, NG_OFF:MG_OFF].reshape(d, N_KV, 3 * HPG)
        w_ng = jnp.pad(w_ng, ((0, 0), (0, 0), (0, LANES - 3 * HPG))).reshape(d, N_KV * LANES)
        w_ng = w_ng.astype(BF)
        w_gates = w_in[l][:, MG_OFF:].astype(BF)

        h = _prenorm(x, g_pre_mix[l][None, :], mod3, 0, 1).reshape(m, d)
        proj = _matmul_with_tail(h, w_main, w_ng, col_scale, BF, 1024, 512).reshape(b, s, -1)
        gates = _matmul(h, w_gates, BF, 1024, 512)

        u_conv = _conv_module(proj, w_dw[l], b_dw[l][None, :], conv_ln_g[l][None, :], conv_ln_b[l][None, :])

        def grouped(off):
            t = proj[:, :, off:off + KV_WIDTH].reshape(b, s // CMP_STRIDE, CMP_STRIDE, N_KV, HEAD_DIM)
            return t.transpose(0, 3, 1, 2, 4).reshape(b * N_KV, s // CMP_STRIDE, CMP_STRIDE * HEAD_DIM)

        pe_k = jnp.broadcast_to(cmp_pe_k[l].reshape(1, -1), (8, CMP_LEN * HEAD_DIM)).astype(BF)
        pe_v = jnp.broadcast_to(cmp_pe_v[l].reshape(1, -1), (8, CMP_LEN * HEAD_DIM)).astype(BF)
        k_cmp = _compress(grouped(KC_OFF), pe_k, w_cmp_k1[l].astype(BF), w_cmp_k2[l].astype(BF))
        v_cmp = _compress(grouped(KC_OFF + KV_WIDTH), pe_v, w_cmp_v1[l].astype(BF), w_cmp_v2[l].astype(BF))

        o_nsa = _nsa_attention(proj, k_cmp, v_cmp, wb, sb, cf)

        z = _gated_mix(u_conv.reshape(m, CONV_DIM), o_nsa.reshape(m, Q_WIDTH),
                       w_conv_out[l].astype(BF), w_nsa_out[l].astype(BF), gates, d)
        mix = _matmul(z, w_out[l].astype(BF), F32, 1024, 512).reshape(b, s, d)
        x, h2 = _post_mix(mix, x, g_post_mix[l][None, :], g_pre_ffn[l][None, :], mod3, 2, 3, 4)

        ff = _ffn_up(h2.reshape(m, d), w_ffn_gate[l].astype(BF), w_ffn_up[l].astype(BF))
        f = _matmul(ff, w_ffn_down[l].astype(BF), F32, 512, 256).reshape(b, s, d)
        x = _post_ffn(f, x, g_post_ffn[l][None, :], mod3, 5)
    return x
```

```python
import functools
import math

import numpy as np
import jax
import jax.numpy as jnp
from jax import lax
from jax.experimental import pallas as pl
from jax.experimental.pallas import tpu as pltpu

F32 = jnp.float32
BF = jnp.bfloat16

CONV_DIM = 2048
CONV_WIDTH = 31
N_HEADS = 16
N_KV = 4
HPG = N_HEADS // N_KV
HEAD_DIM = 128
CMP_LEN = 32
CMP_STRIDE = 16
CMP_HID = 512
SLC_BLOCK = 64
N_SEL = 16
WINDOW = 512
Q_BLOCK = 64
REL_BUCKETS = 32
REL_MAX_DIST = 128
NORM_EPS = 1e-6
NEG = -1e30
LOG2E = 1.4426950408889634

Q_WIDTH = N_HEADS * HEAD_DIM
KV_WIDTH = N_KV * HEAD_DIM
Q_OFF = 2 * CONV_DIM
KC_OFF = Q_OFF + Q_WIDTH
NG_OFF = KC_OFF + 6 * KV_WIDTH
MG_OFF = NG_OFF + 3 * N_HEADS

LANES = 128
SUBLANES = 8
VMEM_LIMIT = 56 * 1024 * 1024

CONV_HALO = 32
CONV_ROWS = 64
CONV_LANES = 256

Q_TILE = 2 * Q_BLOCK
SLC_PAD = 2 * SLC_BLOCK
SLC_LOCAL_KEYS = SLC_PAD + Q_TILE
SLC_CHUNK_BLOCKS = 16
WIN_KEYS = WINDOW + Q_TILE
WIN_PAD = WINDOW
CMP_PER_TILE = Q_TILE // CMP_STRIDE
CMP_BAND = 16
CMP_BAND_LO = 9
PAD_FLAG_COL = 64
KEY_FORCED = 0x7F000000
KEY_FUTURE = -2


def _t5_thresholds():
    max_exact = REL_BUCKETS // 2
    thr = []
    for k in range(1, REL_BUCKETS - max_exact):
        n = max_exact
        while max_exact + int(math.log(n / max_exact) / math.log(REL_MAX_DIST / max_exact)
                              * (REL_BUCKETS - max_exact)) < max_exact + k:
            n += 1
        thr.append(n)
    return tuple(thr)


T5_THRESHOLDS = _t5_thresholds()


def _cparams(*sem):
    return pltpu.CompilerParams(dimension_semantics=sem, vmem_limit_bytes=VMEM_LIMIT)


def _silu(v):
    return v * jax.nn.sigmoid(v)


def _dot(a, b):
    return jnp.dot(a, b, preferred_element_type=F32)


def _dot_nt(a, b):
    return lax.dot_general(a, b, (((1,), (1,)), ((), ())), preferred_element_type=F32)


def _rms(v, g):
    return v * lax.rsqrt(jnp.mean(v * v, axis=-1, keepdims=True) + NORM_EPS) * g


def _mod_kernel(c_ref, w_ref, b_ref, o_ref):
    a = _silu(c_ref[...]).astype(BF)
    o_ref[...] = _dot(a, w_ref[...].astype(BF)) + b_ref[...]


def _modulation(c8, w_ada, b_ada, tn=512):
    rows, d = c8.shape
    n = w_ada.shape[1]
    return pl.pallas_call(
        _mod_kernel,
        grid=(n // tn,),
        in_specs=[pl.BlockSpec((rows, d), lambda j: (0, 0)),
                  pl.BlockSpec((d, tn), lambda j: (0, j)),
                  pl.BlockSpec((1, tn), lambda j: (0, j))],
        out_specs=pl.BlockSpec((rows, tn), lambda j: (0, j)),
        out_shape=jax.ShapeDtypeStruct((rows, n), F32),
        compiler_params=_cparams("parallel"),
    )(c8, w_ada, b_ada)


def _prenorm_kernel(x_ref, g_ref, mod_ref, h_ref, *, shift_row, scale_row):
    y = _rms(x_ref[0], g_ref[...])
    h = y * (1.0 + mod_ref[0, scale_row:scale_row + 1, :]) + mod_ref[0, shift_row:shift_row + 1, :]
    h_ref[0] = h.astype(BF)


def _prenorm(x, g, mod3, shift_row, scale_row, tr=256):
    b, s, d = x.shape
    return pl.pallas_call(
        functools.partial(_prenorm_kernel, shift_row=shift_row, scale_row=scale_row),
        grid=(b, s // tr),
        in_specs=[pl.BlockSpec((1, tr, d), lambda i, r: (i, r, 0)),
                  pl.BlockSpec((1, d), lambda i, r: (0, 0)),
                  pl.BlockSpec((1, 6, d), lambda i, r: (i, 0, 0))],
        out_specs=pl.BlockSpec((1, tr, d), lambda i, r: (i, r, 0)),
        out_shape=jax.ShapeDtypeStruct((b, s, d), BF),
        compiler_params=_cparams("parallel", "parallel"),
    )(x, g, mod3)


def _post_mix_kernel(y_ref, x_ref, gp_ref, gn_ref, mod_ref, xo_ref, h_ref, *, gate_row, shift_row, scale_row):
    x1 = x_ref[0] + mod_ref[0, gate_row:gate_row + 1, :] * _rms(y_ref[0].astype(F32), gp_ref[...])
    xo_ref[0] = x1
    h = _rms(x1, gn_ref[...]) * (1.0 + mod_ref[0, scale_row:scale_row + 1, :]) \
        + mod_ref[0, shift_row:shift_row + 1, :]
    h_ref[0] = h.astype(BF)


def _post_mix(y, x, g_post, g_next, mod3, gate_row, shift_row, scale_row, tr=256):
    b, s, d = x.shape
    row = pl.BlockSpec((1, tr, d), lambda i, r: (i, r, 0))
    vec = pl.BlockSpec((1, d), lambda i, r: (0, 0))
    return pl.pallas_call(
        functools.partial(_post_mix_kernel, gate_row=gate_row, shift_row=shift_row, scale_row=scale_row),
        grid=(b, s // tr),
        in_specs=[row, row, vec, vec, pl.BlockSpec((1, 6, d), lambda i, r: (i, 0, 0))],
        out_specs=[row, row],
        out_shape=[jax.ShapeDtypeStruct((b, s, d), F32), jax.ShapeDtypeStruct((b, s, d), BF)],
        compiler_params=_cparams("parallel", "parallel"),
    )(y, x, g_post, g_next, mod3)


def _post_ffn_kernel(y_ref, x_ref, gp_ref, mod_ref, xo_ref, *, gate_row):
    xo_ref[0] = x_ref[0] + mod_ref[0, gate_row:gate_row + 1, :] * _rms(y_ref[0].astype(F32), gp_ref[...])


def _post_ffn(y, x, g_post, mod3, gate_row, tr=256):
    b, s, d = x.shape
    row = pl.BlockSpec((1, tr, d), lambda i, r: (i, r, 0))
    return pl.pallas_call(
        functools.partial(_post_ffn_kernel, gate_row=gate_row),
        grid=(b, s // tr),
        in_specs=[row, row, pl.BlockSpec((1, d), lambda i, r: (0, 0)),
                  pl.BlockSpec((1, 6, d), lambda i, r: (i, 0, 0))],
        out_specs=row,
        out_shape=jax.ShapeDtypeStruct((b, s, d), F32),
        compiler_params=_cparams("parallel", "parallel"),
    )(y, x, g_post, mod3)


def _mm_scale_kernel(a_ref, w_ref, s_ref, o_ref):
    o_ref[...] = (_dot(a_ref[...], w_ref[...]) * s_ref[...]).astype(o_ref.dtype)


def _mm_kernel(a_ref, w_ref, o_ref):
    o_ref[...] = _dot(a_ref[...], w_ref[...]).astype(o_ref.dtype)


def _matmul(a, w, out_dtype, tm, tn, col_scale=None):
    m, k = a.shape
    n = w.shape[1]
    in_specs = [pl.BlockSpec((tm, k), lambda i, j: (i, 0)),
                pl.BlockSpec((k, tn), lambda i, j: (0, j))]
    args = [a, w]
    body = _mm_kernel
    if col_scale is not None:
        in_specs.append(pl.BlockSpec((1, tn), lambda i, j: (0, j)))
        args.append(col_scale)
        body = _mm_scale_kernel
    return pl.pallas_call(
        body,
        grid=(m // tm, n // tn),
        in_specs=in_specs,
        out_specs=pl.BlockSpec((tm, tn), lambda i, j: (i, j)),
        out_shape=jax.ShapeDtypeStruct((m, n), out_dtype),
        compiler_params=_cparams("parallel", "arbitrary"),
    )(*args)


def _mm_tail_kernel(a_ref, w_ref, wt_ref, s_ref, o_ref, *, n_main):
    j = pl.program_id(1)

    @pl.when(j < n_main)
    def _main():
        o_ref[...] = (_dot(a_ref[...], w_ref[...]) * s_ref[...]).astype(o_ref.dtype)

    @pl.when(j >= n_main)
    def _tail():
        o_ref[...] = (_dot(a_ref[...], wt_ref[...]) * s_ref[...]).astype(o_ref.dtype)


def _matmul_with_tail(a, w, n_w, w_tail, col_scale, out_dtype, tm, tn):
    m, k = a.shape
    n_main = n_w // tn
    n_tail = w_tail.shape[1] // tn
    n = n_w + w_tail.shape[1]
    return pl.pallas_call(
        functools.partial(_mm_tail_kernel, n_main=n_main),
        grid=(m // tm, n_main + n_tail),
        in_specs=[pl.BlockSpec((tm, k), lambda i, j: (i, 0)),
                  pl.BlockSpec((k, tn), lambda i, j: (0, jnp.minimum(j, n_main - 1))),
                  pl.BlockSpec((k, tn), lambda i, j: (0, jnp.maximum(j - n_main, 0))),
                  pl.BlockSpec((1, tn), lambda i, j: (0, j))],
        out_specs=pl.BlockSpec((tm, tn), lambda i, j: (i, j)),
        out_shape=jax.ShapeDtypeStruct((m, n), out_dtype),
        compiler_params=_cparams("parallel", "arbitrary"),
    )(a, w, w_tail, col_scale)


def _mix_kernel(a1_ref, a2_ref, w1_ref, w2_ref, ga_ref, gb_ref, o_ref):
    ya = _dot(a1_ref[...], w1_ref[...])
    yb = _dot(a2_ref[...], w2_ref[...])
    o = jax.nn.sigmoid(ga_ref[...].astype(F32)) * ya + jax.nn.sigmoid(gb_ref[...].astype(F32)) * yb
    o_ref[...] = o.astype(o_ref.dtype)


def _gated_mix(u_conv, o_nsa, w_conv_out, w_nsa_out, gates, d_model, tm=1024, tn=512):
    m, ka = u_conv.shape
    kb = o_nsa.shape[1]
    gb_block = d_model // tn
    return pl.pallas_call(
        _mix_kernel,
        grid=(m // tm, d_model // tn),
        in_specs=[pl.BlockSpec((tm, ka), lambda i, j: (i, 0)),
                  pl.BlockSpec((tm, kb), lambda i, j: (i, 0)),
                  pl.BlockSpec((ka, tn), lambda i, j: (0, j)),
                  pl.BlockSpec((kb, tn), lambda i, j: (0, j)),
                  pl.BlockSpec((tm, tn), lambda i, j: (i, j)),
                  pl.BlockSpec((tm, tn), lambda i, j: (i, j + gb_block))],
        out_specs=pl.BlockSpec((tm, tn), lambda i, j: (i, j)),
        out_shape=jax.ShapeDtypeStruct((m, d_model), BF),
        compiler_params=_cparams("parallel", "arbitrary"),
    )(u_conv, o_nsa, w_conv_out, w_nsa_out, gates, gates)


def _ffn_up_kernel(a_ref, wg_ref, wu_ref, o_ref):
    a = a_ref[...]
    o_ref[...] = (_silu(_dot(a, wg_ref[...])) * _dot(a, wu_ref[...])).astype(o_ref.dtype)


def _ffn_up(h, w_gate, w_up, tm=1024, tn=256):
    m, k = h.shape
    n = w_gate.shape[1]
    wspec = pl.BlockSpec((k, tn), lambda i, j: (0, j))
    return pl.pallas_call(
        _ffn_up_kernel,
        grid=(m // tm, n // tn),
        in_specs=[pl.BlockSpec((tm, k), lambda i, j: (i, 0)), wspec, wspec],
        out_specs=pl.BlockSpec((tm, tn), lambda i, j: (i, j)),
        out_shape=jax.ShapeDtypeStruct((m, n), BF),
        compiler_params=_cparams("parallel", "arbitrary"),
    )(h, w_gate, w_up)


def _conv_kernel(a_ref, b_ref, ah_ref, bh_ref, w_ref, bdw_ref, g_ref, bb_ref, o_ref, u_ref, sh_ref, v_ref, *, ts):
    i = pl.program_id(1)
    u_ref[CONV_HALO:CONV_HALO + ts, :] = a_ref[0].astype(F32) * jax.nn.sigmoid(b_ref[0].astype(F32))
    uh = ah_ref[0].astype(F32) * jax.nn.sigmoid(bh_ref[0].astype(F32))
    u_ref[0:CONV_HALO, :] = jnp.where(i > 0, uh, 0.0)

    rc, cc = CONV_ROWS, CONV_LANES
    first = CONV_HALO - (CONV_WIDTH - 1)
    span = ts + CONV_HALO - SUBLANES
    for c0 in range(0, CONV_DIM, cc):
        for s in range(1, SUBLANES):
            sh_ref[s - 1] = u_ref[s:s + span, c0:c0 + cc]

        def row_chunk(rb, carry, c0=c0):
            r0 = pl.multiple_of(rb * rc, rc)
            acc = jnp.zeros((rc, cc), F32)
            for k in range(CONV_WIDTH):
                s = (first + k) % SUBLANES
                base = first + k - s
                if s == 0:
                    tap = u_ref[pl.ds(r0 + base, rc), c0:c0 + cc]
                else:
                    tap = sh_ref[s - 1, pl.ds(r0 + base, rc), :]
                acc = acc + tap * w_ref[k:k + 1, c0:c0 + cc]
            v_ref[pl.ds(r0, rc), c0:c0 + cc] = acc + bdw_ref[:, c0:c0 + cc]
            return carry

        lax.fori_loop(0, ts // rc, row_chunk, 0)

    v = v_ref[...]
    mu = jnp.mean(v, axis=-1, keepdims=True)
    vc = v - mu
    y = vc * lax.rsqrt(jnp.mean(vc * vc, axis=-1, keepdims=True) + NORM_EPS) * g_ref[...] + bb_ref[...]
    o_ref[0] = _silu(y).astype(o_ref.dtype)


def _conv_module(proj, w_dw, b_dw, ln_g, ln_b, ts=256):
    b, s, _ = proj.shape
    hb = ts // CONV_HALO
    cur_a = pl.BlockSpec((1, ts, CONV_DIM), lambda i, r: (i, r, 0))
    cur_b = pl.BlockSpec((1, ts, CONV_DIM), lambda i, r: (i, r, 1))
    halo_a = pl.BlockSpec((1, CONV_HALO, CONV_DIM), lambda i, r: (i, jnp.maximum(r * hb - 1, 0), 0))
    halo_b = pl.BlockSpec((1, CONV_HALO, CONV_DIM), lambda i, r: (i, jnp.maximum(r * hb - 1, 0), 1))
    vec = pl.BlockSpec((1, CONV_DIM), lambda i, r: (0, 0))
    return pl.pallas_call(
        functools.partial(_conv_kernel, ts=ts),
        grid=(b, s // ts),
        in_specs=[cur_a, cur_b, halo_a, halo_b,
                  pl.BlockSpec((CONV_WIDTH, CONV_DIM), lambda i, r: (0, 0)), vec, vec, vec],
        out_specs=pl.BlockSpec((1, ts, CONV_DIM), lambda i, r: (i, r, 0)),
        out_shape=jax.ShapeDtypeStruct((b, s, CONV_DIM), BF),
        scratch_shapes=[pltpu.VMEM((CONV_HALO + ts, CONV_DIM), F32),
                        pltpu.VMEM((SUBLANES - 1, ts + CONV_HALO - SUBLANES, CONV_LANES), F32),
                        pltpu.VMEM((ts, CONV_DIM), F32)],
        compiler_params=_cparams("parallel", "arbitrary"),
    )(proj, proj, proj, proj, w_dw, b_dw, ln_g, ln_b)


def _compress_kernel(k_ref, pe_ref, w1_ref, w2_ref, o_ref):
    k16 = k_ref[0]
    half = CMP_STRIDE * HEAD_DIM
    lo = _dot(k16, w1_ref[0:half, :])
    hi = _dot(k16, w1_ref[half:2 * half, :])
    pe = _dot(pe_ref[...], w1_ref[...])
    rows = k16.shape[0]
    hid = lo + pltpu.roll(hi, rows - 1, axis=0) + pe[0:1, :]
    act = 0.5 * hid * (1.0 + jnp.tanh(math.sqrt(2.0 / math.pi) * (hid + 0.044715 * hid * hid * hid)))
    o_ref[0] = _dot(act.astype(BF), w2_ref[...])


def _compress(k16, pe8, w1, w2):
    n, rows, width = k16.shape
    return pl.pallas_call(
        _compress_kernel,
        grid=(n,),
        in_specs=[pl.BlockSpec((1, rows, width), lambda i: (i, 0, 0)),
                  pl.BlockSpec(pe8.shape, lambda i: (0, 0)),
                  pl.BlockSpec(w1.shape, lambda i: (0, 0)),
                  pl.BlockSpec(w2.shape, lambda i: (0, 0))],
        out_specs=pl.BlockSpec((1, rows, HEAD_DIM), lambda i: (i, 0, 0)),
        out_shape=jax.ShapeDtypeStruct((n, rows, HEAD_DIM), F32),
        compiler_params=_cparams("parallel"),
    )(k16, pe8, w1, w2)


def _bias_tables_kernel(rb_ref, wb_ref, sb_ref, cf_ref):
    h = pl.program_id(0)
    last = rb_ref[REL_BUCKETS - 1, h]

    def rel(dist):
        n = jnp.maximum(dist, 0)
        bucket = jnp.full(n.shape, REL_BUCKETS // 2, jnp.int32)
        for t in T5_THRESHOLDS:
            bucket = bucket + (n >= t).astype(jnp.int32)
        bucket = jnp.where(n < REL_BUCKETS // 2, n, jnp.minimum(bucket, REL_BUCKETS - 1))
        out = jnp.zeros(n.shape, F32)
        for bk in range(REL_BUCKETS - 1):
            out = jnp.where(bucket == bk, rb_ref[bk, h] - last, out)
        return out * LOG2E

    def dist_of(shape, offset):
        qi = lax.broadcasted_iota(jnp.int32, shape, 0)
        kl = lax.broadcasted_iota(jnp.int32, shape, 1)
        return qi + offset - kl

    dw = dist_of((Q_TILE, WIN_KEYS), WIN_PAD)
    wb_ref[0] = jnp.where((dw >= 0) & (dw < WINDOW), rel(dw), NEG)

    ds_ = dist_of((Q_TILE, SLC_LOCAL_KEYS), SLC_PAD)
    sb_ref[0] = jnp.where(ds_ >= 0, rel(ds_), NEG)

    qi = lax.broadcasted_iota(jnp.int32, (Q_TILE, LANES), 0)
    col = lax.broadcasted_iota(jnp.int32, (Q_TILE, LANES), 1)
    mi = col & (CMP_BAND - 1)
    part = col >> (CMP_BAND.bit_length() - 1)
    dc = qi - CMP_STRIDE * (mi - CMP_BAND_LO) - (CMP_LEN - 1)
    val = jnp.where(dc >= 0, rel(dc), 0.0)
    p0 = val.astype(BF).astype(F32)
    r1 = val - p0
    p1 = r1.astype(BF).astype(F32)
    p2 = r1 - p1
    parts = jnp.where(part == 0, p0, jnp.where(part == 1, p1, jnp.where(part == 2, p2, 0.0)))
    cf_ref[0] = parts.astype(BF)


def _bias_tables(rel_bias):
    smem = pl.BlockSpec(memory_space=pltpu.SMEM)
    lk = SLC_LOCAL_KEYS
    return pl.pallas_call(
        _bias_tables_kernel,
        grid=(N_HEADS,),
        in_specs=[smem],
        out_specs=[pl.BlockSpec((1, Q_TILE, WIN_KEYS), lambda h: (h, 0, 0)),
                   pl.BlockSpec((1, Q_TILE, lk), lambda h: (h, 0, 0)),
                   pl.BlockSpec((1, Q_TILE, LANES), lambda h: (h, 0, 0))],
        out_shape=[jax.ShapeDtypeStruct((N_HEADS, Q_TILE, WIN_KEYS), F32),
                   jax.ShapeDtypeStruct((N_HEADS, Q_TILE, lk), F32),
                   jax.ShapeDtypeStruct((N_HEADS, Q_TILE, LANES), BF)],
        compiler_params=_cparams("arbitrary"),
    )(rel_bias)


def _softmax_parts(s):
    m = jnp.max(s, axis=-1, keepdims=True)
    e = jnp.exp2(s - m)
    return m, e, jnp.sum(e, axis=-1, keepdims=True)


def _lane_tiles(v, op):
    out = v[:, 0:LANES]
    for t in range(1, v.shape[1] // LANES):
        out = op(out, v[:, t * LANES:(t + 1) * LANES])
    return out


def _attn_kernel(q_ref, ks_ref, vs_ref, kw_ref, vw_ref, kc_ref, vc_ref, ng_ref,
                 wb_ref, sb_ref, cf_ref, ovt_ref, eye_ref, o_ref,
                 kaug_ref, vsp_ref, kwp_ref, vwp_ref, key_ref, sloc_ref, sfar_ref, *, seq):
    tile = pl.program_id(2)
    n_slc = seq // SLC_BLOCK
    ncp = seq // CMP_STRIDE
    rows = HPG * Q_TILE
    n_sel = min(N_SEL, n_slc)
    if n_slc != PAD_FLAG_COL or Q_TILE != LANES:
        raise NotImplementedError("selection layout expects 64 selection blocks and 128-query tiles")

    @pl.when(tile == 0)
    def _stage_keys():
        col = lax.broadcasted_iota(jnp.int32, (SLC_PAD, LANES), 1)
        kaug_ref[0:SLC_PAD, 0:HEAD_DIM] = jnp.zeros((SLC_PAD, HEAD_DIM), BF)
        kaug_ref[0:SLC_PAD, HEAD_DIM:] = jnp.where(col == PAD_FLAG_COL, NEG, 0.0).astype(BF)
        kaug_ref[SLC_PAD:, 0:HEAD_DIM] = ks_ref[0]
        blk = lax.broadcasted_iota(jnp.int32, (seq, LANES), 0) >> (SLC_BLOCK.bit_length() - 1)
        colk = lax.broadcasted_iota(jnp.int32, (seq, LANES), 1)
        kaug_ref[SLC_PAD:, HEAD_DIM:] = jnp.where(colk == blk, NEG, 0.0).astype(BF)
        ones_s = jnp.where(lax.broadcasted_iota(jnp.int32, (SLC_PAD + seq, LANES), 1) == 0, 1.0, 0.0)
        vsp_ref[:, HEAD_DIM:] = ones_s.astype(BF)
        vsp_ref[0:SLC_PAD, 0:HEAD_DIM] = jnp.zeros((SLC_PAD, HEAD_DIM), BF)
        vsp_ref[SLC_PAD:, 0:HEAD_DIM] = vs_ref[0]
        kwp_ref[0:WIN_PAD, :] = jnp.zeros((WIN_PAD, HEAD_DIM), BF)
        kwp_ref[WIN_PAD:, :] = kw_ref[0]
        ones_w = jnp.where(lax.broadcasted_iota(jnp.int32, (WIN_PAD + seq, LANES), 1) == 0, 1.0, 0.0)
        vwp_ref[:, HEAD_DIM:] = ones_w.astype(BF)
        vwp_ref[0:WIN_PAD, 0:HEAD_DIM] = jnp.zeros((WIN_PAD, HEAD_DIM), BF)
        vwp_ref[WIN_PAD:, 0:HEAD_DIM] = vw_ref[0]

    q4 = jnp.concatenate([q_ref[0, :, hh * HEAD_DIM:(hh + 1) * HEAD_DIM] for hh in range(HPG)], axis=0)
    t0 = tile * Q_TILE
    r0 = pl.multiple_of(t0, Q_TILE)

    key_w = lax.broadcasted_iota(jnp.int32, (1, WIN_KEYS), 1)
    before_start = jnp.where(key_w < WIN_PAD - t0, NEG, 0.0)
    s_w = _dot_nt(q4, kwp_ref[pl.ds(r0, WIN_KEYS), :]) + wb_ref[...].reshape(rows, WIN_KEYS) + before_start
    e_w = jnp.exp2(s_w - jnp.max(s_w, axis=-1, keepdims=True))
    pv_w = _dot(e_w.astype(BF), vwp_ref[pl.ds(r0, WIN_KEYS), :])
    o_win = pv_w[:, 0:HEAD_DIM] * (1.0 / pv_w[:, HEAD_DIM:HEAD_DIM + 1])

    kc = kc_ref[0].astype(BF)
    vc = vc_ref[0].astype(BF)
    r_io = lax.broadcasted_iota(jnp.int32, (LANES, ncp), 0)
    n_io = lax.broadcasted_iota(jnp.int32, (LANES, ncp), 1)
    band = (r_io < 3 * CMP_BAND) & (n_io == CMP_PER_TILE * tile + (r_io & (CMP_BAND - 1)) - CMP_BAND_LO)
    shift = jnp.where(band, 1.0, 0.0).astype(BF)
    cf4 = cf_ref[...].reshape(rows, LANES)
    lc = _dot_nt(q4, kc) + _dot(cf4, shift)
    qi_c = lax.broadcasted_iota(jnp.int32, (Q_TILE, ncp), 0)
    n_c = lax.broadcasted_iota(jnp.int32, (Q_TILE, ncp), 1)
    vis = (CMP_STRIDE * n_c + (CMP_LEN - 1)) <= (t0 + qi_c)
    lc3 = jnp.where(vis[None], lc.reshape(HPG, Q_TILE, ncp), NEG)
    _, e_c, l_c = _softmax_parts(lc3)
    p_c = jnp.where(vis[None], e_c * (1.0 / l_c), 0.0)
    o_cmp = _dot(p_c.reshape(rows, ncp).astype(BF), vc)

    psum = p_c[0]
    for hh in range(1, HPG):
        psum = psum + p_c[hh]
    p0 = psum.astype(BF)
    r1 = psum - p0.astype(F32)
    p1 = r1.astype(BF)
    p2 = (r1 - p1.astype(F32)).astype(BF)
    ovt = ovt_ref[...]
    imp_t = _dot_nt(ovt, p0) + _dot_nt(ovt, p1) + _dot_nt(ovt, p2)

    blk_t = lax.broadcasted_iota(jnp.int32, (n_slc, Q_TILE), 0)
    q_t = lax.broadcasted_iota(jnp.int32, (n_slc, Q_TILE), 1)
    cur_t = (t0 + q_t) >> (SLC_BLOCK.bit_length() - 1)
    forced_t = (blk_t == 0) | (blk_t == cur_t) | (blk_t == cur_t - 1)
    causal_t = blk_t <= cur_t
    key = jnp.where(forced_t, KEY_FORCED, jnp.where(causal_t, lax.bitcast_convert_type(imp_t, jnp.int32), KEY_FUTURE))
    key_ref[...] = key
    groups = [key[8 * a:8 * a + 8, :] for a in range(n_slc // 8)]
    ranks = [jnp.zeros((8, Q_TILE), jnp.int32) for _ in groups]
    sub = lax.broadcasted_iota(jnp.int32, (8, Q_TILE), 0)
    for bb in range(n_slc):
        kb = key_ref[bb:bb + 1, :]
        kb1 = kb + 1
        for a in range(n_slc // 8):
            if 8 * a > bb:
                ahead = kb1
            elif 8 * a + 7 < bb:
                ahead = kb
            else:
                ahead = jnp.where(sub > bb - 8 * a, kb1, kb)
            ranks[a] = ranks[a] + (ahead > groups[a]).astype(jnp.int32)
    rank = jnp.concatenate(ranks, axis=0)
    sel_t = (rank < n_sel) & causal_t
    first_local = (t0 - SLC_PAD) >> (SLC_BLOCK.bit_length() - 1)
    far_t = blk_t < first_local
    flag_rows = jnp.where(blk_t == 0, 1.0, 0.0)
    stack = jnp.concatenate([jnp.where(sel_t, 0.0, 1.0), flag_rows,
                             jnp.where(sel_t & far_t, 0.0, 1.0), flag_rows], axis=0).astype(BF)
    qmask = _dot_nt(eye_ref[...], stack).astype(BF)
    qa_loc = jnp.concatenate([q4, qmask[:, 0:LANES]], axis=1)
    qa_far = jnp.concatenate([q4, qmask[:, LANES:]], axis=1)

    lk = SLC_LOCAL_KEYS
    chunk = SLC_CHUNK_BLOCKS * SLC_BLOCK
    n_far = (first_local + SLC_CHUNK_BLOCKS - 1) // SLC_CHUNK_BLOCKS

    s_loc = _dot_nt(qa_loc, kaug_ref[pl.ds(r0, lk), :]) + sb_ref[...].reshape(rows, lk)
    sloc_ref[...] = s_loc

    def far_logits(c, m_part):
        rr = pl.multiple_of(SLC_PAD + c * chunk, SLC_BLOCK)
        s = _dot_nt(qa_far, kaug_ref[pl.ds(rr, chunk), :])
        sfar_ref[c] = s
        return jnp.maximum(m_part, _lane_tiles(s, jnp.maximum))

    m_part = lax.fori_loop(0, n_far, far_logits, _lane_tiles(s_loc, jnp.maximum))
    m_s = jnp.max(m_part, axis=-1, keepdims=True)

    e_loc = jnp.exp2(sloc_ref[...] - m_s)
    acc_s = _dot(e_loc.astype(BF), vsp_ref[pl.ds(r0, lk), :])

    def far_values(c, acc):
        rr = pl.multiple_of(SLC_PAD + c * chunk, SLC_BLOCK)
        e = jnp.exp2(sfar_ref[c] - m_s)
        return acc + _dot(e.astype(BF), vsp_ref[pl.ds(rr, chunk), :])

    acc_s = lax.fori_loop(0, n_far, far_values, acc_s)
    o_slc = acc_s[:, 0:HEAD_DIM] * (1.0 / acc_s[:, HEAD_DIM:HEAD_DIM + 1])

    gts = jax.nn.sigmoid(ng_ref[0].astype(F32))
    for hh in range(HPG):
        sl = slice(hh * Q_TILE, (hh + 1) * Q_TILE)
        o_h = (gts[:, 3 * hh:3 * hh + 1] * o_cmp[sl]
               + gts[:, 3 * hh + 1:3 * hh + 2] * o_slc[sl]
               + gts[:, 3 * hh + 2:3 * hh + 3] * o_win[sl])
        o_ref[0, :, hh * HEAD_DIM:(hh + 1) * HEAD_DIM] = o_h.astype(o_ref.dtype)


def _nsa_attention(proj, k_cmp, v_cmp, wb, sb, cf):
    b, s, _ = proj.shape
    ovt = _overlap_matrix(s)
    eye = jnp.asarray(np.tile(np.eye(Q_TILE), (HPG, 1)), BF)
    n_chunks = -(-(s // SLC_BLOCK) // SLC_CHUNK_BLOCKS)
    q_blk = Q_OFF // (HPG * HEAD_DIM)
    kv_blk = KC_OFF // HEAD_DIM
    ng_blk = NG_OFF // LANES
    lk = SLC_LOCAL_KEYS
    ncp = s // CMP_STRIDE

    def kv_spec(idx):
        return pl.BlockSpec((1, s, HEAD_DIM), lambda i, g, j: (i, 0, kv_blk + idx * N_KV + g))

    cmp_spec = pl.BlockSpec((1, ncp, HEAD_DIM), lambda i, g, j: (i * N_KV + g, 0, 0))
    return pl.pallas_call(
        functools.partial(_attn_kernel, seq=s),
        grid=(b, N_KV, s // Q_TILE),
        in_specs=[pl.BlockSpec((1, Q_TILE, HPG * HEAD_DIM), lambda i, g, j: (i, j, q_blk + g)),
                  kv_spec(2), kv_spec(3), kv_spec(4), kv_spec(5),
                  cmp_spec, cmp_spec,
                  pl.BlockSpec((1, Q_TILE, LANES), lambda i, g, j: (i, j, ng_blk + g)),
                  pl.BlockSpec((HPG, Q_TILE, WIN_KEYS), lambda i, g, j: (g, 0, 0)),
                  pl.BlockSpec((HPG, Q_TILE, lk), lambda i, g, j: (g, 0, 0)),
                  pl.BlockSpec((HPG, Q_TILE, LANES), lambda i, g, j: (g, 0, 0)),
                  pl.BlockSpec(ovt.shape, lambda i, g, j: (0, 0)),
                  pl.BlockSpec(eye.shape, lambda i, g, j: (0, 0))],
        out_specs=pl.BlockSpec((1, Q_TILE, HPG * HEAD_DIM), lambda i, g, j: (i, j, g)),
        out_shape=jax.ShapeDtypeStruct((b, s, Q_WIDTH), BF),
        scratch_shapes=[pltpu.VMEM((SLC_PAD + s, 2 * HEAD_DIM), BF),
                        pltpu.VMEM((SLC_PAD + s, HEAD_DIM + LANES), BF),
                        pltpu.VMEM((WIN_PAD + s, HEAD_DIM), BF),
                        pltpu.VMEM((WIN_PAD + s, HEAD_DIM + LANES), BF),
                        pltpu.VMEM((s // SLC_BLOCK, Q_TILE), jnp.int32),
                        pltpu.VMEM((HPG * Q_TILE, lk), F32),
                        pltpu.VMEM((n_chunks, HPG * Q_TILE, SLC_CHUNK_BLOCKS * SLC_BLOCK), F32)],
        compiler_params=_cparams("parallel", "parallel", "arbitrary"),
    )(proj, proj, proj, proj, proj, k_cmp, v_cmp, proj, wb, sb, cf, ovt, eye)


def _overlap_matrix(s):
    ncp = s // CMP_STRIDE
    n_slc = s // SLC_BLOCK
    nc = (s - CMP_LEN) // CMP_STRIDE + 1
    i = np.arange(ncp)[None, :]
    jj = np.arange(n_slc)[:, None]
    ov = ((i * CMP_STRIDE < (jj + 1) * SLC_BLOCK) & (i * CMP_STRIDE + CMP_LEN > jj * SLC_BLOCK) & (i < nc))
    return jnp.asarray(ov, BF)


def kernel(x, c, w_ada, b_ada, g_pre_mix, g_post_mix, g_pre_ffn, g_post_ffn, w_in, w_dw, b_dw, conv_ln_g, conv_ln_b, w_conv_out, cmp_pe_k, cmp_pe_v, w_cmp_k1, w_cmp_k2, w_cmp_v1, w_cmp_v2, rel_bias, w_nsa_out, w_out, w_ffn_gate, w_ffn_up, w_ffn_down):
    b, s, d = x.shape
    m = b * s
    depth = w_ada.shape[0]
    c8 = jnp.zeros((8, d), F32).at[:b].set(c)
    wb, sb, cf = _bias_tables(rel_bias)
    col_scale = jnp.ones((1, NG_OFF + N_KV * LANES), F32).at[:, Q_OFF:KC_OFF].set(HEAD_DIM ** -0.5 * LOG2E)

    for l in range(depth):
        mod = _modulation(c8, w_ada[l], b_ada[l][None, :])
        mod3 = mod[:b].reshape(b, 6, d)

        w_all = w_in[l].astype(BF)
        w_ng = w_in[l][:, NG_OFF:MG_OFF].reshape(d, N_KV, 3 * HPG)
        w_ng = jnp.pad(w_ng, ((0, 0), (0, 0), (0, LANES - 3 * HPG))).reshape(d, N_KV * LANES)
        w_ng = w_ng.astype(BF)
        w_gates = w_all[:, MG_OFF:]

        h = _prenorm(x, g_pre_mix[l][None, :], mod3, 0, 1).reshape(m, d)
        proj = _matmul_with_tail(h, w_all, NG_OFF, w_ng, col_scale, BF, 1024, 512).reshape(b, s, -1)
        gates = _matmul(h, w_gates, BF, 1024, 512)

        u_conv = _conv_module(proj, w_dw[l], b_dw[l][None, :], conv_ln_g[l][None, :], conv_ln_b[l][None, :])

        def grouped(off):
            t = proj[:, :, off:off + KV_WIDTH].reshape(b, s // CMP_STRIDE, CMP_STRIDE, N_KV, HEAD_DIM)
            return t.transpose(0, 3, 1, 2, 4).reshape(b * N_KV, s // CMP_STRIDE, CMP_STRIDE * HEAD_DIM)

        pe_k = jnp.broadcast_to(cmp_pe_k[l].reshape(1, -1), (8, CMP_LEN * HEAD_DIM)).astype(BF)
        pe_v = jnp.broadcast_to(cmp_pe_v[l].reshape(1, -1), (8, CMP_LEN * HEAD_DIM)).astype(BF)
        k_cmp = _compress(grouped(KC_OFF), pe_k, w_cmp_k1[l].astype(BF), w_cmp_k2[l].astype(BF))
        v_cmp = _compress(grouped(KC_OFF + KV_WIDTH), pe_v, w_cmp_v1[l].astype(BF), w_cmp_v2[l].astype(BF))

        o_nsa = _nsa_attention(proj, k_cmp, v_cmp, wb, sb, cf)

        z = _gated_mix(u_conv.reshape(m, CONV_DIM), o_nsa.reshape(m, Q_WIDTH),
                       w_conv_out[l].astype(BF), w_nsa_out[l].astype(BF), gates, d)
        mix = _matmul(z, w_out[l].astype(BF), BF, 1024, 512).reshape(b, s, d)
        x, h2 = _post_mix(mix, x, g_post_mix[l][None, :], g_pre_ffn[l][None, :], mod3, 2, 3, 4)

        ff = _ffn_up(h2.reshape(m, d), w_ffn_gate[l].astype(BF), w_ffn_up[l].astype(BF))
        f = _matmul(ff, w_ffn_down[l].astype(BF), BF, 512, 256).reshape(b, s, d)
        x = _post_ffn(f, x, g_post_ffn[l][None, :], mod3, 5)
    return x
```

```python
import functools
import math

import numpy as np
import jax
import jax.numpy as jnp
from jax import lax
from jax.experimental import pallas as pl
from jax.experimental.pallas import tpu as pltpu

F32 = jnp.float32
BF = jnp.bfloat16

CONV_DIM = 2048
CONV_WIDTH = 31
N_HEADS = 16
N_KV = 4
HPG = N_HEADS // N_KV
HEAD_DIM = 128
CMP_LEN = 32
CMP_STRIDE = 16
CMP_HID = 512
SLC_BLOCK = 64
N_SEL = 16
WINDOW = 512
Q_BLOCK = 64
REL_BUCKETS = 32
REL_MAX_DIST = 128
NORM_EPS = 1e-6
NEG = -1e30
LOG2E = 1.4426950408889634

Q_WIDTH = N_HEADS * HEAD_DIM
KV_WIDTH = N_KV * HEAD_DIM
Q_OFF = 2 * CONV_DIM
KC_OFF = Q_OFF + Q_WIDTH
NG_OFF = KC_OFF + 6 * KV_WIDTH
MG_OFF = NG_OFF + 3 * N_HEADS

LANES = 128
SUBLANES = 8
VMEM_LIMIT = 56 * 1024 * 1024

CONV_HALO = 32
CONV_ROWS = 64
CONV_LANES = 256

Q_TILE = 4 * Q_BLOCK
SLC_PAD = 2 * SLC_BLOCK
SLC_LOCAL_KEYS = SLC_PAD + Q_TILE
SLC_CHUNK_BLOCKS = 16
WIN_KEYS = WINDOW + Q_TILE
WIN_PAD = WINDOW
CMP_PER_TILE = Q_TILE // CMP_STRIDE
CMP_BAND = 32
CMP_BAND_LO = 9
PAD_FLAG_COL = 64
KEY_FORCED = 0x7F000000
KEY_FUTURE = -2


def _t5_thresholds():
    max_exact = REL_BUCKETS // 2
    thr = []
    for k in range(1, REL_BUCKETS - max_exact):
        n = max_exact
        while max_exact + int(math.log(n / max_exact) / math.log(REL_MAX_DIST / max_exact)
                              * (REL_BUCKETS - max_exact)) < max_exact + k:
            n += 1
        thr.append(n)
    return tuple(thr)


T5_THRESHOLDS = _t5_thresholds()


def _cparams(*sem):
    return pltpu.CompilerParams(dimension_semantics=sem, vmem_limit_bytes=VMEM_LIMIT)


def _silu(v):
    return v * jax.nn.sigmoid(v)


def _dot(a, b):
    return jnp.dot(a, b, preferred_element_type=F32)


def _dot_nt(a, b):
    return lax.dot_general(a, b, (((1,), (1,)), ((), ())), preferred_element_type=F32)


def _rms(v, g):
    return v * lax.rsqrt(jnp.mean(v * v, axis=-1, keepdims=True) + NORM_EPS) * g


def _mod_kernel(c_ref, w_ref, b_ref, o_ref):
    a = _silu(c_ref[...]).astype(BF)
    o_ref[...] = _dot(a, w_ref[...].astype(BF)) + b_ref[...]


def _modulation(c8, w_ada, b_ada, tn=512):
    rows, d = c8.shape
    n = w_ada.shape[1]
    return pl.pallas_call(
        _mod_kernel,
        grid=(n // tn,),
        in_specs=[pl.BlockSpec((rows, d), lambda j: (0, 0)),
                  pl.BlockSpec((d, tn), lambda j: (0, j)),
                  pl.BlockSpec((1, tn), lambda j: (0, j))],
        out_specs=pl.BlockSpec((rows, tn), lambda j: (0, j)),
        out_shape=jax.ShapeDtypeStruct((rows, n), F32),
        compiler_params=_cparams("parallel"),
    )(c8, w_ada, b_ada)


def _prenorm_kernel(x_ref, g_ref, mod_ref, h_ref, *, shift_row, scale_row):
    y = _rms(x_ref[0], g_ref[...])
    h = y * (1.0 + mod_ref[0, scale_row:scale_row + 1, :]) + mod_ref[0, shift_row:shift_row + 1, :]
    h_ref[0] = h.astype(BF)


def _prenorm(x, g, mod3, shift_row, scale_row, tr=256):
    b, s, d = x.shape
    return pl.pallas_call(
        functools.partial(_prenorm_kernel, shift_row=shift_row, scale_row=scale_row),
        grid=(b, s // tr),
        in_specs=[pl.BlockSpec((1, tr, d), lambda i, r: (i, r, 0)),
                  pl.BlockSpec((1, d), lambda i, r: (0, 0)),
                  pl.BlockSpec((1, 6, d), lambda i, r: (i, 0, 0))],
        out_specs=pl.BlockSpec((1, tr, d), lambda i, r: (i, r, 0)),
        out_shape=jax.ShapeDtypeStruct((b, s, d), BF),
        compiler_params=_cparams("parallel", "parallel"),
    )(x, g, mod3)


def _post_mix_kernel(y_ref, x_ref, gp_ref, gn_ref, mod_ref, xo_ref, h_ref, *, gate_row, shift_row, scale_row):
    x1 = x_ref[0] + mod_ref[0, gate_row:gate_row + 1, :] * _rms(y_ref[0].astype(F32), gp_ref[...])
    xo_ref[0] = x1
    h = _rms(x1, gn_ref[...]) * (1.0 + mod_ref[0, scale_row:scale_row + 1, :]) \
        + mod_ref[0, shift_row:shift_row + 1, :]
    h_ref[0] = h.astype(BF)


def _post_mix(y, x, g_post, g_next, mod3, gate_row, shift_row, scale_row, tr=256):
    b, s, d = x.shape
    row = pl.BlockSpec((1, tr, d), lambda i, r: (i, r, 0))
    vec = pl.BlockSpec((1, d), lambda i, r: (0, 0))
    return pl.pallas_call(
        functools.partial(_post_mix_kernel, gate_row=gate_row, shift_row=shift_row, scale_row=scale_row),
        grid=(b, s // tr),
        in_specs=[row, row, vec, vec, pl.BlockSpec((1, 6, d), lambda i, r: (i, 0, 0))],
        out_specs=[row, row],
        out_shape=[jax.ShapeDtypeStruct((b, s, d), F32), jax.ShapeDtypeStruct((b, s, d), BF)],
        compiler_params=_cparams("parallel", "parallel"),
    )(y, x, g_post, g_next, mod3)


def _post_ffn_kernel(y_ref, x_ref, gp_ref, mod_ref, xo_ref, *, gate_row):
    xo_ref[0] = x_ref[0] + mod_ref[0, gate_row:gate_row + 1, :] * _rms(y_ref[0].astype(F32), gp_ref[...])


def _post_ffn(y, x, g_post, mod3, gate_row, tr=256):
    b, s, d = x.shape
    row = pl.BlockSpec((1, tr, d), lambda i, r: (i, r, 0))
    return pl.pallas_call(
        functools.partial(_post_ffn_kernel, gate_row=gate_row),
        grid=(b, s // tr),
        in_specs=[row, row, pl.BlockSpec((1, d), lambda i, r: (0, 0)),
                  pl.BlockSpec((1, 6, d), lambda i, r: (i, 0, 0))],
        out_specs=row,
        out_shape=jax.ShapeDtypeStruct((b, s, d), F32),
        compiler_params=_cparams("parallel", "parallel"),
    )(y, x, g_post, mod3)


def _mm_scale_kernel(a_ref, w_ref, s_ref, o_ref):
    o_ref[...] = (_dot(a_ref[...], w_ref[...]) * s_ref[...]).astype(o_ref.dtype)


def _mm_kernel(a_ref, w_ref, o_ref):
    o_ref[...] = _dot(a_ref[...], w_ref[...]).astype(o_ref.dtype)


def _matmul(a, w, out_dtype, tm, tn, col_scale=None):
    m, k = a.shape
    n = w.shape[1]
    in_specs = [pl.BlockSpec((tm, k), lambda i, j: (i, 0)),
                pl.BlockSpec((k, tn), lambda i, j: (0, j))]
    args = [a, w]
    body = _mm_kernel
    if col_scale is not None:
        in_specs.append(pl.BlockSpec((1, tn), lambda i, j: (0, j)))
        args.append(col_scale)
        body = _mm_scale_kernel
    return pl.pallas_call(
        body,
        grid=(m // tm, n // tn),
        in_specs=in_specs,
        out_specs=pl.BlockSpec((tm, tn), lambda i, j: (i, j)),
        out_shape=jax.ShapeDtypeStruct((m, n), out_dtype),
        compiler_params=_cparams("parallel", "arbitrary"),
    )(*args)


def _mm_tail_kernel(a_ref, w_ref, wt_ref, s_ref, o_ref, *, n_main):
    j = pl.program_id(1)

    @pl.when(j < n_main)
    def _main():
        o_ref[...] = (_dot(a_ref[...], w_ref[...]) * s_ref[...]).astype(o_ref.dtype)

    @pl.when(j >= n_main)
    def _tail():
        o_ref[...] = (_dot(a_ref[...], wt_ref[...]) * s_ref[...]).astype(o_ref.dtype)


def _matmul_with_tail(a, w, n_w, w_tail, col_scale, out_dtype, tm, tn):
    m, k = a.shape
    n_main = n_w // tn
    n_tail = w_tail.shape[1] // tn
    n = n_w + w_tail.shape[1]
    return pl.pallas_call(
        functools.partial(_mm_tail_kernel, n_main=n_main),
        grid=(m // tm, n_main + n_tail),
        in_specs=[pl.BlockSpec((tm, k), lambda i, j: (i, 0)),
                  pl.BlockSpec((k, tn), lambda i, j: (0, jnp.minimum(j, n_main - 1))),
                  pl.BlockSpec((k, tn), lambda i, j: (0, jnp.maximum(j - n_main, 0))),
                  pl.BlockSpec((1, tn), lambda i, j: (0, j))],
        out_specs=pl.BlockSpec((tm, tn), lambda i, j: (i, j)),
        out_shape=jax.ShapeDtypeStruct((m, n), out_dtype),
        compiler_params=_cparams("parallel", "arbitrary"),
    )(a, w, w_tail, col_scale)


def _mix_kernel(a1_ref, a2_ref, w1_ref, w2_ref, ga_ref, gb_ref, o_ref):
    ya = _dot(a1_ref[...], w1_ref[...])
    yb = _dot(a2_ref[...], w2_ref[...])
    o = jax.nn.sigmoid(ga_ref[...].astype(F32)) * ya + jax.nn.sigmoid(gb_ref[...].astype(F32)) * yb
    o_ref[...] = o.astype(o_ref.dtype)


def _gated_mix(u_conv, o_nsa, w_conv_out, w_nsa_out, gates, d_model, tm=1024, tn=512):
    m, ka = u_conv.shape
    kb = o_nsa.shape[1]
    gb_block = d_model // tn
    return pl.pallas_call(
        _mix_kernel,
        grid=(m // tm, d_model // tn),
        in_specs=[pl.BlockSpec((tm, ka), lambda i, j: (i, 0)),
                  pl.BlockSpec((tm, kb), lambda i, j: (i, 0)),
                  pl.BlockSpec((ka, tn), lambda i, j: (0, j)),
                  pl.BlockSpec((kb, tn), lambda i, j: (0, j)),
                  pl.BlockSpec((tm, tn), lambda i, j: (i, j)),
                  pl.BlockSpec((tm, tn), lambda i, j: (i, j + gb_block))],
        out_specs=pl.BlockSpec((tm, tn), lambda i, j: (i, j)),
        out_shape=jax.ShapeDtypeStruct((m, d_model), BF),
        compiler_params=_cparams("parallel", "arbitrary"),
    )(u_conv, o_nsa, w_conv_out, w_nsa_out, gates, gates)


def _ffn_up_kernel(a_ref, wg_ref, wu_ref, o_ref):
    a = a_ref[...]
    o_ref[...] = (_silu(_dot(a, wg_ref[...])) * _dot(a, wu_ref[...])).astype(o_ref.dtype)


def _ffn_up(h, w_gate, w_up, tm=2048, tn=256):
    m, k = h.shape
    n = w_gate.shape[1]
    wspec = pl.BlockSpec((k, tn), lambda i, j: (0, j))
    return pl.pallas_call(
        _ffn_up_kernel,
        grid=(m // tm, n // tn),
        in_specs=[pl.BlockSpec((tm, k), lambda i, j: (i, 0)), wspec, wspec],
        out_specs=pl.BlockSpec((tm, tn), lambda i, j: (i, j)),
        out_shape=jax.ShapeDtypeStruct((m, n), BF),
        compiler_params=_cparams("parallel", "arbitrary"),
    )(h, w_gate, w_up)


def _conv_kernel(a_ref, b_ref, ah_ref, bh_ref, w_ref, bdw_ref, g_ref, bb_ref, o_ref, u_ref, sh_ref, v_ref, *, ts):
    i = pl.program_id(1)
    u_ref[CONV_HALO:CONV_HALO + ts, :] = a_ref[0].astype(F32) * jax.nn.sigmoid(b_ref[0].astype(F32))
    uh = ah_ref[0].astype(F32) * jax.nn.sigmoid(bh_ref[0].astype(F32))
    u_ref[0:CONV_HALO, :] = jnp.where(i > 0, uh, 0.0)

    rc, cc = CONV_ROWS, CONV_LANES
    first = CONV_HALO - (CONV_WIDTH - 1)
    span = ts + CONV_HALO - SUBLANES
    for c0 in range(0, CONV_DIM, cc):
        for s in range(1, SUBLANES):
            sh_ref[s - 1] = u_ref[s:s + span, c0:c0 + cc]

        def row_chunk(rb, carry, c0=c0):
            r0 = pl.multiple_of(rb * rc, rc)
            acc = jnp.zeros((rc, cc), F32)
            for k in range(CONV_WIDTH):
                s = (first + k) % SUBLANES
                base = first + k - s
                if s == 0:
                    tap = u_ref[pl.ds(r0 + base, rc), c0:c0 + cc]
                else:
                    tap = sh_ref[s - 1, pl.ds(r0 + base, rc), :]
                acc = acc + tap * w_ref[k:k + 1, c0:c0 + cc]
            v_ref[pl.ds(r0, rc), c0:c0 + cc] = acc + bdw_ref[:, c0:c0 + cc]
            return carry

        lax.fori_loop(0, ts // rc, row_chunk, 0)

    v = v_ref[...]
    mu = jnp.mean(v, axis=-1, keepdims=True)
    vc = v - mu
    y = vc * lax.rsqrt(jnp.mean(vc * vc, axis=-1, keepdims=True) + NORM_EPS) * g_ref[...] + bb_ref[...]
    o_ref[0] = _silu(y).astype(o_ref.dtype)


def _conv_module(proj, w_dw, b_dw, ln_g, ln_b, ts=256):
    b, s, _ = proj.shape
    hb = ts // CONV_HALO
    cur_a = pl.BlockSpec((1, ts, CONV_DIM), lambda i, r: (i, r, 0))
    cur_b = pl.BlockSpec((1, ts, CONV_DIM), lambda i, r: (i, r, 1))
    halo_a = pl.BlockSpec((1, CONV_HALO, CONV_DIM), lambda i, r: (i, jnp.maximum(r * hb - 1, 0), 0))
    halo_b = pl.BlockSpec((1, CONV_HALO, CONV_DIM), lambda i, r: (i, jnp.maximum(r * hb - 1, 0), 1))
    vec = pl.BlockSpec((1, CONV_DIM), lambda i, r: (0, 0))
    return pl.pallas_call(
        functools.partial(_conv_kernel, ts=ts),
        grid=(b, s // ts),
        in_specs=[cur_a, cur_b, halo_a, halo_b,
                  pl.BlockSpec((CONV_WIDTH, CONV_DIM), lambda i, r: (0, 0)), vec, vec, vec],
        out_specs=pl.BlockSpec((1, ts, CONV_DIM), lambda i, r: (i, r, 0)),
        out_shape=jax.ShapeDtypeStruct((b, s, CONV_DIM), BF),
        scratch_shapes=[pltpu.VMEM((CONV_HALO + ts, CONV_DIM), F32),
                        pltpu.VMEM((SUBLANES - 1, ts + CONV_HALO - SUBLANES, CONV_LANES), F32),
                        pltpu.VMEM((ts, CONV_DIM), F32)],
        compiler_params=_cparams("parallel", "arbitrary"),
    )(proj, proj, proj, proj, w_dw, b_dw, ln_g, ln_b)


def _compress_kernel(k_ref, pe_ref, w1_ref, w2_ref, o_ref):
    k16 = k_ref[0]
    half = CMP_STRIDE * HEAD_DIM
    lo = _dot(k16, w1_ref[0:half, :])
    hi = _dot(k16, w1_ref[half:2 * half, :])
    pe = _dot(pe_ref[...], w1_ref[...])
    rows = k16.shape[0]
    hid = lo + pltpu.roll(hi, rows - 1, axis=0) + pe[0:1, :]
    act = 0.5 * hid * (1.0 + jnp.tanh(math.sqrt(2.0 / math.pi) * (hid + 0.044715 * hid * hid * hid)))
    o_ref[0] = _dot(act.astype(BF), w2_ref[...])


def _compress(k16, pe8, w1, w2):
    n, rows, width = k16.shape
    return pl.pallas_call(
        _compress_kernel,
        grid=(n,),
        in_specs=[pl.BlockSpec((1, rows, width), lambda i: (i, 0, 0)),
                  pl.BlockSpec(pe8.shape, lambda i: (0, 0)),
                  pl.BlockSpec(w1.shape, lambda i: (0, 0)),
                  pl.BlockSpec(w2.shape, lambda i: (0, 0))],
        out_specs=pl.BlockSpec((1, rows, HEAD_DIM), lambda i: (i, 0, 0)),
        out_shape=jax.ShapeDtypeStruct((n, rows, HEAD_DIM), F32),
        compiler_params=_cparams("parallel"),
    )(k16, pe8, w1, w2)


def _bias_tables_kernel(rb_ref, wb_ref, sb_ref, cf_ref):
    h = pl.program_id(0)
    last = rb_ref[REL_BUCKETS - 1, h]

    def rel(dist):
        n = jnp.maximum(dist, 0)
        bucket = jnp.full(n.shape, REL_BUCKETS // 2, jnp.int32)
        for t in T5_THRESHOLDS:
            bucket = bucket + (n >= t).astype(jnp.int32)
        bucket = jnp.where(n < REL_BUCKETS // 2, n, jnp.minimum(bucket, REL_BUCKETS - 1))
        out = jnp.zeros(n.shape, F32)
        for bk in range(REL_BUCKETS - 1):
            out = jnp.where(bucket == bk, rb_ref[bk, h] - last, out)
        return out * LOG2E

    def dist_of(shape, offset):
        qi = lax.broadcasted_iota(jnp.int32, shape, 0)
        kl = lax.broadcasted_iota(jnp.int32, shape, 1)
        return qi + offset - kl

    dw = dist_of((Q_TILE, WIN_KEYS), WIN_PAD)
    wb_ref[0] = jnp.where((dw >= 0) & (dw < WINDOW), rel(dw), NEG)

    ds_ = dist_of((Q_TILE, SLC_LOCAL_KEYS), SLC_PAD)
    sb_ref[0] = jnp.where(ds_ >= 0, rel(ds_), NEG)

    qi = lax.broadcasted_iota(jnp.int32, (Q_TILE, LANES), 0)
    col = lax.broadcasted_iota(jnp.int32, (Q_TILE, LANES), 1)
    mi = col & (CMP_BAND - 1)
    part = col >> (CMP_BAND.bit_length() - 1)
    dc = qi - CMP_STRIDE * (mi - CMP_BAND_LO) - (CMP_LEN - 1)
    val = jnp.where(dc >= 0, rel(dc), 0.0)
    p0 = val.astype(BF).astype(F32)
    r1 = val - p0
    p1 = r1.astype(BF).astype(F32)
    p2 = r1 - p1
    parts = jnp.where(part == 0, p0, jnp.where(part == 1, p1, jnp.where(part == 2, p2, 0.0)))
    cf_ref[0] = parts.astype(BF)


def _bias_tables(rel_bias):
    smem = pl.BlockSpec(memory_space=pltpu.SMEM)
    lk = SLC_LOCAL_KEYS
    return pl.pallas_call(
        _bias_tables_kernel,
        grid=(N_HEADS,),
        in_specs=[smem],
        out_specs=[pl.BlockSpec((1, Q_TILE, WIN_KEYS), lambda h: (h, 0, 0)),
                   pl.BlockSpec((1, Q_TILE, lk), lambda h: (h, 0, 0)),
                   pl.BlockSpec((1, Q_TILE, LANES), lambda h: (h, 0, 0))],
        out_shape=[jax.ShapeDtypeStruct((N_HEADS, Q_TILE, WIN_KEYS), F32),
                   jax.ShapeDtypeStruct((N_HEADS, Q_TILE, lk), F32),
                   jax.ShapeDtypeStruct((N_HEADS, Q_TILE, LANES), BF)],
        compiler_params=_cparams("arbitrary"),
    )(rel_bias)


def _softmax_parts(s):
    m = jnp.max(s, axis=-1, keepdims=True)
    e = jnp.exp2(s - m)
    return m, e, jnp.sum(e, axis=-1, keepdims=True)


def _lane_tiles(v, op):
    out = v[:, 0:LANES]
    for t in range(1, v.shape[1] // LANES):
        out = op(out, v[:, t * LANES:(t + 1) * LANES])
    return out


def _attn_kernel(q_ref, ks_ref, vs_ref, kw_ref, vw_ref, kc_ref, vc_ref, ng_ref,
                 wb_ref, sb_ref, cf_ref, ovt_ref, eye_ref, spread_ref, o_ref,
                 kaug_ref, vsp_ref, kwp_ref, vwp_ref, key_ref, sloc_ref, sfar_ref, *, seq):
    tile = pl.program_id(2)
    n_slc = seq // SLC_BLOCK
    ncp = seq // CMP_STRIDE
    rows = HPG * Q_TILE
    n_sel = min(N_SEL, n_slc)
    if n_slc != PAD_FLAG_COL or Q_TILE % LANES or 3 * CMP_BAND > LANES:
        raise NotImplementedError("selection layout expects 64 selection blocks and whole-lane query tiles")
    assert CMP_STRIDE * (CMP_BAND_LO + 1) - (CMP_LEN - 1) >= T5_THRESHOLDS[-1]
    assert CMP_STRIDE * (CMP_BAND - CMP_BAND_LO) + CMP_LEN - 1 >= Q_TILE

    @pl.when(tile == 0)
    def _stage_keys():
        col = lax.broadcasted_iota(jnp.int32, (SLC_PAD, LANES), 1)
        kaug_ref[0:SLC_PAD, 0:HEAD_DIM] = jnp.zeros((SLC_PAD, HEAD_DIM), BF)
        kaug_ref[0:SLC_PAD, HEAD_DIM:] = jnp.where(col == PAD_FLAG_COL, NEG, 0.0).astype(BF)
        kaug_ref[SLC_PAD:, 0:HEAD_DIM] = ks_ref[0]
        blk = lax.broadcasted_iota(jnp.int32, (seq, LANES), 0) >> (SLC_BLOCK.bit_length() - 1)
        colk = lax.broadcasted_iota(jnp.int32, (seq, LANES), 1)
        kaug_ref[SLC_PAD:, HEAD_DIM:] = jnp.where(colk == blk, NEG, 0.0).astype(BF)
        vsp_ref[:, HEAD_DIM:] = jnp.ones((SLC_PAD + seq, LANES), BF)
        vsp_ref[0:SLC_PAD, 0:HEAD_DIM] = jnp.zeros((SLC_PAD, HEAD_DIM), BF)
        vsp_ref[SLC_PAD:, 0:HEAD_DIM] = vs_ref[0]
        kwp_ref[0:WIN_PAD, :] = jnp.zeros((WIN_PAD, HEAD_DIM), BF)
        kwp_ref[WIN_PAD:, :] = kw_ref[0]
        vwp_ref[:, HEAD_DIM:] = jnp.ones((WIN_PAD + seq, LANES), BF)
        vwp_ref[0:WIN_PAD, 0:HEAD_DIM] = jnp.zeros((WIN_PAD, HEAD_DIM), BF)
        vwp_ref[WIN_PAD:, 0:HEAD_DIM] = vw_ref[0]

    q4 = jnp.concatenate([q_ref[0, :, hh * HEAD_DIM:(hh + 1) * HEAD_DIM] for hh in range(HPG)], axis=0)
    t0 = tile * Q_TILE
    r0 = pl.multiple_of(t0, Q_TILE)

    gts = jax.nn.sigmoid(ng_ref[0].astype(F32))
    g_hi = gts.astype(BF)
    g_lo = (gts - g_hi.astype(F32)).astype(BF)
    spread = spread_ref[...]
    gsp = _dot(g_hi, spread) + _dot(g_lo, spread)

    key_w = lax.broadcasted_iota(jnp.int32, (1, WIN_KEYS), 1)
    before_start = jnp.where(key_w < WIN_PAD - t0, NEG, 0.0)
    s_w = _dot_nt(q4, kwp_ref[pl.ds(r0, WIN_KEYS), :]) + wb_ref[...].reshape(rows, WIN_KEYS) + before_start
    e_w = jnp.exp2(s_w - jnp.max(s_w, axis=-1, keepdims=True))
    pv_w = _dot(e_w.astype(BF), vwp_ref[pl.ds(r0, WIN_KEYS), :])
    o_win = pv_w[:, 0:HEAD_DIM] * (1.0 / pv_w[:, HEAD_DIM:])

    kc = kc_ref[0].astype(BF)
    vc = vc_ref[0].astype(BF)
    r_io = lax.broadcasted_iota(jnp.int32, (LANES, ncp), 0)
    n_io = lax.broadcasted_iota(jnp.int32, (LANES, ncp), 1)
    band = (r_io < 3 * CMP_BAND) & (n_io == CMP_PER_TILE * tile + (r_io & (CMP_BAND - 1)) - CMP_BAND_LO)
    shift = jnp.where(band, 1.0, 0.0).astype(BF)
    cf4 = cf_ref[...].reshape(rows, LANES)
    lc = _dot_nt(q4, kc) + _dot(cf4, shift)
    qi_c = lax.broadcasted_iota(jnp.int32, (Q_TILE, ncp), 0)
    n_c = lax.broadcasted_iota(jnp.int32, (Q_TILE, ncp), 1)
    vis = (CMP_STRIDE * n_c + (CMP_LEN - 1)) <= (t0 + qi_c)
    lc3 = jnp.where(vis[None], lc.reshape(HPG, Q_TILE, ncp), NEG)
    _, e_c, l_c = _softmax_parts(lc3)
    p_c = jnp.where(vis[None], e_c * (1.0 / l_c), 0.0)
    o_cmp = _dot(p_c.reshape(rows, ncp).astype(BF), vc)

    psum = p_c[0]
    for hh in range(1, HPG):
        psum = psum + p_c[hh]
    p0 = psum.astype(BF)
    r1 = psum - p0.astype(F32)
    p1 = r1.astype(BF)
    p2 = (r1 - p1.astype(F32)).astype(BF)
    ovt = ovt_ref[...]
    imp_t = _dot_nt(ovt, p0) + _dot_nt(ovt, p1) + _dot_nt(ovt, p2)

    def gate(hh, branch):
        c = 3 * hh + branch
        return gsp[:, c * LANES:(c + 1) * LANES]

    gated_cw = [gate(hh, 0) * o_cmp[hh * Q_TILE:(hh + 1) * Q_TILE] + gate(hh, 2) * o_win[hh * Q_TILE:(hh + 1) * Q_TILE]
                for hh in range(HPG)]

    blk_t = lax.broadcasted_iota(jnp.int32, (n_slc, Q_TILE), 0)
    q_t = lax.broadcasted_iota(jnp.int32, (n_slc, Q_TILE), 1)
    cur_t = (t0 + q_t) >> (SLC_BLOCK.bit_length() - 1)
    forced_t = (blk_t == 0) | (blk_t == cur_t) | (blk_t == cur_t - 1)
    causal_t = blk_t <= cur_t
    key = jnp.where(forced_t, KEY_FORCED, jnp.where(causal_t, lax.bitcast_convert_type(imp_t, jnp.int32), KEY_FUTURE))
    key_ref[...] = key
    groups = [key[8 * a:8 * a + 8, :] for a in range(n_slc // 8)]
    ranks = [jnp.zeros((8, Q_TILE), jnp.int32) for _ in groups]
    sub = lax.broadcasted_iota(jnp.int32, (8, Q_TILE), 0)
    for bb in range(n_slc):
        kb = key_ref[bb:bb + 1, :]
        kb1 = kb + 1
        for a in range(n_slc // 8):
            if 8 * a > bb:
                ahead = kb1
            elif 8 * a + 7 < bb:
                ahead = kb
            else:
                ahead = jnp.where(sub > bb - 8 * a, kb1, kb)
            ranks[a] = ranks[a] + (ahead > groups[a]).astype(jnp.int32)
    rank = jnp.concatenate(ranks, axis=0)
    sel_t = (rank < n_sel) & causal_t
    first_local = (t0 - SLC_PAD) >> (SLC_BLOCK.bit_length() - 1)
    far_t = blk_t < first_local
    flag_rows = jnp.where(blk_t == 0, 1.0, 0.0)
    stack = jnp.concatenate([jnp.where(sel_t, 0.0, 1.0), flag_rows,
                             jnp.where(sel_t & far_t, 0.0, 1.0), flag_rows], axis=0).astype(BF)
    qmask = _dot_nt(eye_ref[...], stack).astype(BF)
    qa_loc = jnp.concatenate([q4, qmask[:, 0:LANES]], axis=1)
    qa_far = jnp.concatenate([q4, qmask[:, LANES:]], axis=1)

    lk = SLC_LOCAL_KEYS
    chunk = SLC_CHUNK_BLOCKS * SLC_BLOCK
    n_far = (first_local + SLC_CHUNK_BLOCKS - 1) // SLC_CHUNK_BLOCKS

    s_loc = _dot_nt(qa_loc, kaug_ref[pl.ds(r0, lk), :]) + sb_ref[...].reshape(rows, lk)
    sloc_ref[...] = s_loc

    def far_logits(c, m_part):
        rr = pl.multiple_of(SLC_PAD + c * chunk, SLC_BLOCK)
        s = _dot_nt(qa_far, kaug_ref[pl.ds(rr, chunk), :])
        sfar_ref[c] = s
        return jnp.maximum(m_part, _lane_tiles(s, jnp.maximum))

    m_part = lax.fori_loop(0, n_far, far_logits, _lane_tiles(s_loc, jnp.maximum))
    m_s = jnp.max(m_part, axis=-1, keepdims=True)

    e_loc = jnp.exp2(sloc_ref[...] - m_s)
    acc_s = _dot(e_loc.astype(BF), vsp_ref[pl.ds(r0, lk), :])

    def far_values(c, acc):
        rr = pl.multiple_of(SLC_PAD + c * chunk, SLC_BLOCK)
        e = jnp.exp2(sfar_ref[c] - m_s)
        return acc + _dot(e.astype(BF), vsp_ref[pl.ds(rr, chunk), :])

    acc_s = lax.fori_loop(0, n_far, far_values, acc_s)
    o_slc = acc_s[:, 0:HEAD_DIM] * (1.0 / acc_s[:, HEAD_DIM:])

    for hh in range(HPG):
        o_h = gated_cw[hh] + gate(hh, 1) * o_slc[hh * Q_TILE:(hh + 1) * Q_TILE]
        o_ref[0, :, hh * HEAD_DIM:(hh + 1) * HEAD_DIM] = o_h.astype(o_ref.dtype)


def _nsa_attention(proj, k_cmp, v_cmp, wb, sb, cf):
    b, s, _ = proj.shape
    ovt = _overlap_matrix(s)
    eye = jnp.asarray(np.tile(np.eye(Q_TILE), (HPG, 1)), BF)
    spread = jnp.asarray(np.repeat(np.eye(LANES)[:, :3 * HPG], LANES, axis=1), BF)
    n_chunks = -(-(s // SLC_BLOCK) // SLC_CHUNK_BLOCKS)
    q_blk = Q_OFF // (HPG * HEAD_DIM)
    kv_blk = KC_OFF // HEAD_DIM
    ng_blk = NG_OFF // LANES
    lk = SLC_LOCAL_KEYS
    ncp = s // CMP_STRIDE

    def kv_spec(idx):
        return pl.BlockSpec((1, s, HEAD_DIM), lambda i, g, j: (i, 0, kv_blk + idx * N_KV + g))

    cmp_spec = pl.BlockSpec((1, ncp, HEAD_DIM), lambda i, g, j: (i * N_KV + g, 0, 0))
    return pl.pallas_call(
        functools.partial(_attn_kernel, seq=s),
        grid=(b, N_KV, s // Q_TILE),
        in_specs=[pl.BlockSpec((1, Q_TILE, HPG * HEAD_DIM), lambda i, g, j: (i, j, q_blk + g)),
                  kv_spec(2), kv_spec(3), kv_spec(4), kv_spec(5),
                  cmp_spec, cmp_spec,
                  pl.BlockSpec((1, Q_TILE, LANES), lambda i, g, j: (i, j, ng_blk + g)),
                  pl.BlockSpec((HPG, Q_TILE, WIN_KEYS), lambda i, g, j: (g, 0, 0)),
                  pl.BlockSpec((HPG, Q_TILE, lk), lambda i, g, j: (g, 0, 0)),
                  pl.BlockSpec((HPG, Q_TILE, LANES), lambda i, g, j: (g, 0, 0)),
                  pl.BlockSpec(ovt.shape, lambda i, g, j: (0, 0)),
                  pl.BlockSpec(eye.shape, lambda i, g, j: (0, 0)),
                  pl.BlockSpec(spread.shape, lambda i, g, j: (0, 0))],
        out_specs=pl.BlockSpec((1, Q_TILE, HPG * HEAD_DIM), lambda i, g, j: (i, j, g)),
        out_shape=jax.ShapeDtypeStruct((b, s, Q_WIDTH), BF),
        scratch_shapes=[pltpu.VMEM((SLC_PAD + s, 2 * HEAD_DIM), BF),
                        pltpu.VMEM((SLC_PAD + s, HEAD_DIM + LANES), BF),
                        pltpu.VMEM((WIN_PAD + s, HEAD_DIM), BF),
                        pltpu.VMEM((WIN_PAD + s, HEAD_DIM + LANES), BF),
                        pltpu.VMEM((s // SLC_BLOCK, Q_TILE), jnp.int32),
                        pltpu.VMEM((HPG * Q_TILE, lk), F32),
                        pltpu.VMEM((n_chunks, HPG * Q_TILE, SLC_CHUNK_BLOCKS * SLC_BLOCK), F32)],
        compiler_params=_cparams("parallel", "parallel", "arbitrary"),
    )(proj, proj, proj, proj, proj, k_cmp, v_cmp, proj, wb, sb, cf, ovt, eye, spread)


def _overlap_matrix(s):
    ncp = s // CMP_STRIDE
    n_slc = s // SLC_BLOCK
    nc = (s - CMP_LEN) // CMP_STRIDE + 1
    i = np.arange(ncp)[None, :]
    jj = np.arange(n_slc)[:, None]
    ov = ((i * CMP_STRIDE < (jj + 1) * SLC_BLOCK) & (i * CMP_STRIDE + CMP_LEN > jj * SLC_BLOCK) & (i < nc))
    return jnp.asarray(ov, BF)


def kernel(x, c, w_ada, b_ada, g_pre_mix, g_post_mix, g_pre_ffn, g_post_ffn, w_in, w_dw, b_dw, conv_ln_g, conv_ln_b, w_conv_out, cmp_pe_k, cmp_pe_v, w_cmp_k1, w_cmp_k2, w_cmp_v1, w_cmp_v2, rel_bias, w_nsa_out, w_out, w_ffn_gate, w_ffn_up, w_ffn_down):
    b, s, d = x.shape
    m = b * s
    depth = w_ada.shape[0]
    c8 = jnp.zeros((8, d), F32).at[:b].set(c)
    wb, sb, cf = _bias_tables(rel_bias)
    col_scale = jnp.ones((1, NG_OFF + N_KV * LANES), F32).at[:, Q_OFF:KC_OFF].set(HEAD_DIM ** -0.5 * LOG2E)

    for l in range(depth):
        mod = _modulation(c8, w_ada[l], b_ada[l][None, :])
        mod3 = mod[:b].reshape(b, 6, d)

        w_all = w_in[l].astype(BF)
        w_ng = w_in[l][:, NG_OFF:MG_OFF].reshape(d, N_KV, 3 * HPG)
        w_ng = jnp.pad(w_ng, ((0, 0), (0, 0), (0, LANES - 3 * HPG))).reshape(d, N_KV * LANES)
        w_ng = w_ng.astype(BF)
        w_gates = w_all[:, MG_OFF:]

        h = _prenorm(x, g_pre_mix[l][None, :], mod3, 0, 1).reshape(m, d)
        proj = _matmul_with_tail(h, w_all, NG_OFF, w_ng, col_scale, BF, 1024, 512).reshape(b, s, -1)
        gates = _matmul(h, w_gates, BF, 1024, 1024)

        u_conv = _conv_module(proj, w_dw[l], b_dw[l][None, :], conv_ln_g[l][None, :], conv_ln_b[l][None, :])

        def grouped(off):
            t = proj[:, :, off:off + KV_WIDTH].reshape(b, s // CMP_STRIDE, CMP_STRIDE, N_KV, HEAD_DIM)
            return t.transpose(0, 3, 1, 2, 4).reshape(b * N_KV, s // CMP_STRIDE, CMP_STRIDE * HEAD_DIM)

        pe_k = jnp.broadcast_to(cmp_pe_k[l].reshape(1, -1), (8, CMP_LEN * HEAD_DIM)).astype(BF)
        pe_v = jnp.broadcast_to(cmp_pe_v[l].reshape(1, -1), (8, CMP_LEN * HEAD_DIM)).astype(BF)
        k_cmp = _compress(grouped(KC_OFF), pe_k, w_cmp_k1[l].astype(BF), w_cmp_k2[l].astype(BF))
        v_cmp = _compress(grouped(KC_OFF + KV_WIDTH), pe_v, w_cmp_v1[l].astype(BF), w_cmp_v2[l].astype(BF))

        o_nsa = _nsa_attention(proj, k_cmp, v_cmp, wb, sb, cf)

        z = _gated_mix(u_conv.reshape(m, CONV_DIM), o_nsa.reshape(m, Q_WIDTH),
                       w_conv_out[l].astype(BF), w_nsa_out[l].astype(BF), gates, d)
        mix = _matmul(z, w_out[l].astype(BF), BF, 1024, 1024).reshape(b, s, d)
        x, h2 = _post_mix(mix, x, g_post_mix[l][None, :], g_pre_ffn[l][None, :], mod3, 2, 3, 4)

        ff = _ffn_up(h2.reshape(m, d), w_ffn_gate[l].astype(BF), w_ffn_up[l].astype(BF))
        f = _matmul(ff, w_ffn_down[l].astype(BF), BF, 512, 512).reshape(b, s, d)
        x = _post_ffn(f, x, g_post_ffn[l][None, :], mod3, 5)
    return x
```

```python
import functools
import math

import numpy as np
import jax
import jax.numpy as jnp
from jax import lax
from jax.experimental import pallas as pl
from jax.experimental.pallas import tpu as pltpu

F32 = jnp.float32
BF = jnp.bfloat16

CONV_DIM = 2048
CONV_WIDTH = 31
N_HEADS = 16
N_KV = 4
HPG = N_HEADS // N_KV
HEAD_DIM = 128
CMP_LEN = 32
CMP_STRIDE = 16
CMP_HID = 512
SLC_BLOCK = 64
N_SEL = 16
WINDOW = 512
Q_BLOCK = 64
REL_BUCKETS = 32
REL_MAX_DIST = 128
NORM_EPS = 1e-6
NEG = -1e30
LOG2E = 1.4426950408889634

Q_WIDTH = N_HEADS * HEAD_DIM
KV_WIDTH = N_KV * HEAD_DIM
Q_OFF = 2 * CONV_DIM
KC_OFF = Q_OFF + Q_WIDTH
NG_OFF = KC_OFF + 6 * KV_WIDTH
MG_OFF = NG_OFF + 3 * N_HEADS

LANES = 128
SUBLANES = 8
BF16_SUBLANES = 16
VMEM_LIMIT = 56 * 1024 * 1024
ATTN_VMEM_LIMIT = 61 * 1024 * 1024

CONV_HALO = 32
CONV_ROWS = 64
CONV_LANES = 256

Q_TILE = 4 * Q_BLOCK
SLC_PAD = 2 * SLC_BLOCK
SLC_LOCAL_KEYS = SLC_PAD + Q_TILE
SLC_CHUNK_BLOCKS = 16
WIN_KEYS = WINDOW + Q_TILE
WIN_PAD = WINDOW
CMP_PER_TILE = Q_TILE // CMP_STRIDE
CMP_BAND = 32
CMP_BAND_LO = 9
PAD_FLAG_COL = 64
KEY_FORCED = 0x7F000000
KEY_FUTURE = -2


def _t5_thresholds():
    max_exact = REL_BUCKETS // 2
    thr = []
    for k in range(1, REL_BUCKETS - max_exact):
        n = max_exact
        while max_exact + int(math.log(n / max_exact) / math.log(REL_MAX_DIST / max_exact)
                              * (REL_BUCKETS - max_exact)) < max_exact + k:
            n += 1
        thr.append(n)
    return tuple(thr)


T5_THRESHOLDS = _t5_thresholds()


def _cparams(*sem):
    return pltpu.CompilerParams(dimension_semantics=sem, vmem_limit_bytes=VMEM_LIMIT)


def _silu(v):
    return v * jax.nn.sigmoid(v)


def _dot(a, b):
    return jnp.dot(a, b, preferred_element_type=F32)


def _dot_nt(a, b):
    return lax.dot_general(a, b, (((1,), (1,)), ((), ())), preferred_element_type=F32)


def _rms(v, g):
    return v * lax.rsqrt(jnp.mean(v * v, axis=-1, keepdims=True) + NORM_EPS) * g


def _mod_kernel(c_ref, w_ref, b_ref, o_ref):
    a = _silu(c_ref[...]).astype(BF)
    o_ref[...] = _dot(a, w_ref[...].astype(BF)) + b_ref[...]


def _modulation(c8, w_ada, b_ada, tn=512):
    rows, d = c8.shape
    n = w_ada.shape[1]
    return pl.pallas_call(
        _mod_kernel,
        grid=(n // tn,),
        in_specs=[pl.BlockSpec((rows, d), lambda j: (0, 0)),
                  pl.BlockSpec((d, tn), lambda j: (0, j)),
                  pl.BlockSpec((1, tn), lambda j: (0, j))],
        out_specs=pl.BlockSpec((rows, tn), lambda j: (0, j)),
        out_shape=jax.ShapeDtypeStruct((rows, n), F32),
        compiler_params=_cparams("parallel"),
    )(c8, w_ada, b_ada)


def _prenorm_kernel(x_ref, g_ref, mod_ref, h_ref, *, shift_row, scale_row):
    y = _rms(x_ref[0], g_ref[...])
    h = y * (1.0 + mod_ref[0, scale_row:scale_row + 1, :]) + mod_ref[0, shift_row:shift_row + 1, :]
    h_ref[0] = h.astype(BF)


def _prenorm(x, g, mod3, shift_row, scale_row, tr=256):
    b, s, d = x.shape
    return pl.pallas_call(
        functools.partial(_prenorm_kernel, shift_row=shift_row, scale_row=scale_row),
        grid=(b, s // tr),
        in_specs=[pl.BlockSpec((1, tr, d), lambda i, r: (i, r, 0)),
                  pl.BlockSpec((1, d), lambda i, r: (0, 0)),
                  pl.BlockSpec((1, 6, d), lambda i, r: (i, 0, 0))],
        out_specs=pl.BlockSpec((1, tr, d), lambda i, r: (i, r, 0)),
        out_shape=jax.ShapeDtypeStruct((b, s, d), BF),
        compiler_params=_cparams("parallel", "parallel"),
    )(x, g, mod3)


def _post_mix_kernel(y_ref, x_ref, gp_ref, gn_ref, mod_ref, xo_ref, h_ref, *, gate_row, shift_row, scale_row):
    x1 = x_ref[0] + mod_ref[0, gate_row:gate_row + 1, :] * _rms(y_ref[0].astype(F32), gp_ref[...])
    xo_ref[0] = x1
    h = _rms(x1, gn_ref[...]) * (1.0 + mod_ref[0, scale_row:scale_row + 1, :]) \
        + mod_ref[0, shift_row:shift_row + 1, :]
    h_ref[0] = h.astype(BF)


def _post_mix(y, x, g_post, g_next, mod3, gate_row, shift_row, scale_row, tr=256):
    b, s, d = x.shape
    row = pl.BlockSpec((1, tr, d), lambda i, r: (i, r, 0))
    vec = pl.BlockSpec((1, d), lambda i, r: (0, 0))
    return pl.pallas_call(
        functools.partial(_post_mix_kernel, gate_row=gate_row, shift_row=shift_row, scale_row=scale_row),
        grid=(b, s // tr),
        in_specs=[row, row, vec, vec, pl.BlockSpec((1, 6, d), lambda i, r: (i, 0, 0))],
        out_specs=[row, row],
        out_shape=[jax.ShapeDtypeStruct((b, s, d), F32), jax.ShapeDtypeStruct((b, s, d), BF)],
        compiler_params=_cparams("parallel", "parallel"),
    )(y, x, g_post, g_next, mod3)


def _post_ffn_kernel(y_ref, x_ref, gp_ref, mod_ref, xo_ref, *, gate_row):
    xo_ref[0] = x_ref[0] + mod_ref[0, gate_row:gate_row + 1, :] * _rms(y_ref[0].astype(F32), gp_ref[...])


def _post_ffn(y, x, g_post, mod3, gate_row, tr=256):
    b, s, d = x.shape
    row = pl.BlockSpec((1, tr, d), lambda i, r: (i, r, 0))
    return pl.pallas_call(
        functools.partial(_post_ffn_kernel, gate_row=gate_row),
        grid=(b, s // tr),
        in_specs=[row, row, pl.BlockSpec((1, d), lambda i, r: (0, 0)),
                  pl.BlockSpec((1, 6, d), lambda i, r: (i, 0, 0))],
        out_specs=row,
        out_shape=jax.ShapeDtypeStruct((b, s, d), F32),
        compiler_params=_cparams("parallel", "parallel"),
    )(y, x, g_post, mod3)


def _mm_scale_kernel(a_ref, w_ref, s_ref, o_ref):
    o_ref[...] = (_dot(a_ref[...], w_ref[...]) * s_ref[...]).astype(o_ref.dtype)


def _mm_kernel(a_ref, w_ref, o_ref):
    o_ref[...] = _dot(a_ref[...], w_ref[...]).astype(o_ref.dtype)


def _matmul(a, w, out_dtype, tm, tn, col_scale=None):
    m, k = a.shape
    n = w.shape[1]
    in_specs = [pl.BlockSpec((tm, k), lambda i, j: (i, 0)),
                pl.BlockSpec((k, tn), lambda i, j: (0, j))]
    args = [a, w]
    body = _mm_kernel
    if col_scale is not None:
        in_specs.append(pl.BlockSpec((1, tn), lambda i, j: (0, j)))
        args.append(col_scale)
        body = _mm_scale_kernel
    return pl.pallas_call(
        body,
        grid=(m // tm, n // tn),
        in_specs=in_specs,
        out_specs=pl.BlockSpec((tm, tn), lambda i, j: (i, j)),
        out_shape=jax.ShapeDtypeStruct((m, n), out_dtype),
        compiler_params=_cparams("parallel", "arbitrary"),
    )(*args)


def _mm_tail_kernel(a_ref, w_ref, wt_ref, s_ref, o_ref, *, n_main):
    j = pl.program_id(1)

    @pl.when(j < n_main)
    def _main():
        o_ref[...] = (_dot(a_ref[...], w_ref[...]) * s_ref[...]).astype(o_ref.dtype)

    @pl.when(j >= n_main)
    def _tail():
        o_ref[...] = (_dot(a_ref[...], wt_ref[...]) * s_ref[...]).astype(o_ref.dtype)


def _matmul_with_tail(a, w, n_w, w_tail, col_scale, out_dtype, tm, tn):
    m, k = a.shape
    n_main = n_w // tn
    n_tail = w_tail.shape[1] // tn
    n = n_w + w_tail.shape[1]
    return pl.pallas_call(
        functools.partial(_mm_tail_kernel, n_main=n_main),
        grid=(m // tm, n_main + n_tail),
        in_specs=[pl.BlockSpec((tm, k), lambda i, j: (i, 0)),
                  pl.BlockSpec((k, tn), lambda i, j: (0, jnp.minimum(j, n_main - 1))),
                  pl.BlockSpec((k, tn), lambda i, j: (0, jnp.maximum(j - n_main, 0))),
                  pl.BlockSpec((1, tn), lambda i, j: (0, j))],
        out_specs=pl.BlockSpec((tm, tn), lambda i, j: (i, j)),
        out_shape=jax.ShapeDtypeStruct((m, n), out_dtype),
        compiler_params=_cparams("parallel", "arbitrary"),
    )(a, w, w_tail, col_scale)


def _mix_kernel(a1_ref, a2_ref, w1_ref, w2_ref, ga_ref, gb_ref, o_ref):
    ya = _dot(a1_ref[...], w1_ref[...])
    yb = _dot(a2_ref[...], w2_ref[...])
    o = jax.nn.sigmoid(ga_ref[...].astype(F32)) * ya + jax.nn.sigmoid(gb_ref[...].astype(F32)) * yb
    o_ref[...] = o.astype(o_ref.dtype)


def _gated_mix(u_conv, o_nsa, w_conv_out, w_nsa_out, gates, d_model, tm=1024, tn=512):
    m, ka = u_conv.shape
    kb = o_nsa.shape[1]
    gb_block = d_model // tn
    return pl.pallas_call(
        _mix_kernel,
        grid=(m // tm, d_model // tn),
        in_specs=[pl.BlockSpec((tm, ka), lambda i, j: (i, 0)),
                  pl.BlockSpec((tm, kb), lambda i, j: (i, 0)),
                  pl.BlockSpec((ka, tn), lambda i, j: (0, j)),
                  pl.BlockSpec((kb, tn), lambda i, j: (0, j)),
                  pl.BlockSpec((tm, tn), lambda i, j: (i, j)),
                  pl.BlockSpec((tm, tn), lambda i, j: (i, j + gb_block))],
        out_specs=pl.BlockSpec((tm, tn), lambda i, j: (i, j)),
        out_shape=jax.ShapeDtypeStruct((m, d_model), BF),
        compiler_params=_cparams("parallel", "arbitrary"),
    )(u_conv, o_nsa, w_conv_out, w_nsa_out, gates, gates)


def _ffn_up_kernel(a_ref, wg_ref, wu_ref, o_ref):
    a = a_ref[...]
    o_ref[...] = (_silu(_dot(a, wg_ref[...])) * _dot(a, wu_ref[...])).astype(o_ref.dtype)


def _ffn_up(h, w_gate, w_up, tm=2048, tn=256):
    m, k = h.shape
    n = w_gate.shape[1]
    wspec = pl.BlockSpec((k, tn), lambda i, j: (0, j))
    return pl.pallas_call(
        _ffn_up_kernel,
        grid=(m // tm, n // tn),
        in_specs=[pl.BlockSpec((tm, k), lambda i, j: (i, 0)), wspec, wspec],
        out_specs=pl.BlockSpec((tm, tn), lambda i, j: (i, j)),
        out_shape=jax.ShapeDtypeStruct((m, n), BF),
        compiler_params=_cparams("parallel", "arbitrary"),
    )(h, w_gate, w_up)


def _conv_kernel(a_ref, b_ref, ah_ref, bh_ref, w_ref, bdw_ref, g_ref, bb_ref, o_ref, u_ref, sh_ref, v_ref, *, ts):
    i = pl.program_id(1)
    u_ref[CONV_HALO:CONV_HALO + ts, :] = a_ref[0].astype(F32) * jax.nn.sigmoid(b_ref[0].astype(F32))
    uh = ah_ref[0].astype(F32) * jax.nn.sigmoid(bh_ref[0].astype(F32))
    u_ref[0:CONV_HALO, :] = jnp.where(i > 0, uh, 0.0)

    rc, cc = CONV_ROWS, CONV_LANES
    first = CONV_HALO - (CONV_WIDTH - 1)
    span = ts + CONV_HALO - SUBLANES
    for c0 in range(0, CONV_DIM, cc):
        for s in range(1, SUBLANES):
            sh_ref[s - 1] = u_ref[s:s + span, c0:c0 + cc]

        def row_chunk(rb, carry, c0=c0):
            r0 = pl.multiple_of(rb * rc, rc)
            acc = jnp.zeros((rc, cc), F32)
            for k in range(CONV_WIDTH):
                s = (first + k) % SUBLANES
                base = first + k - s
                if s == 0:
                    tap = u_ref[pl.ds(r0 + base, rc), c0:c0 + cc]
                else:
                    tap = sh_ref[s - 1, pl.ds(r0 + base, rc), :]
                acc = acc + tap * w_ref[k:k + 1, c0:c0 + cc]
            v_ref[pl.ds(r0, rc), c0:c0 + cc] = acc + bdw_ref[:, c0:c0 + cc]
            return carry

        lax.fori_loop(0, ts // rc, row_chunk, 0)

    v = v_ref[...]
    mu = jnp.mean(v, axis=-1, keepdims=True)
    vc = v - mu
    y = vc * lax.rsqrt(jnp.mean(vc * vc, axis=-1, keepdims=True) + NORM_EPS) * g_ref[...] + bb_ref[...]
    o_ref[0] = _silu(y).astype(o_ref.dtype)


def _conv_module(proj, w_dw, b_dw, ln_g, ln_b, ts=256):
    b, s, _ = proj.shape
    hb = ts // CONV_HALO
    cur_a = pl.BlockSpec((1, ts, CONV_DIM), lambda i, r: (i, r, 0))
    cur_b = pl.BlockSpec((1, ts, CONV_DIM), lambda i, r: (i, r, 1))
    halo_a = pl.BlockSpec((1, CONV_HALO, CONV_DIM), lambda i, r: (i, jnp.maximum(r * hb - 1, 0), 0))
    halo_b = pl.BlockSpec((1, CONV_HALO, CONV_DIM), lambda i, r: (i, jnp.maximum(r * hb - 1, 0), 1))
    vec = pl.BlockSpec((1, CONV_DIM), lambda i, r: (0, 0))
    return pl.pallas_call(
        functools.partial(_conv_kernel, ts=ts),
        grid=(b, s // ts),
        in_specs=[cur_a, cur_b, halo_a, halo_b,
                  pl.BlockSpec((CONV_WIDTH, CONV_DIM), lambda i, r: (0, 0)), vec, vec, vec],
        out_specs=pl.BlockSpec((1, ts, CONV_DIM), lambda i, r: (i, r, 0)),
        out_shape=jax.ShapeDtypeStruct((b, s, CONV_DIM), BF),
        scratch_shapes=[pltpu.VMEM((CONV_HALO + ts, CONV_DIM), F32),
                        pltpu.VMEM((SUBLANES - 1, ts + CONV_HALO - SUBLANES, CONV_LANES), F32),
                        pltpu.VMEM((ts, CONV_DIM), F32)],
        compiler_params=_cparams("parallel", "arbitrary"),
    )(proj, proj, proj, proj, w_dw, b_dw, ln_g, ln_b)


def _compress_kernel(k_ref, pe_ref, w1_ref, w2_ref, o_ref):
    k16 = k_ref[0]
    half = CMP_STRIDE * HEAD_DIM
    lo = _dot(k16, w1_ref[0:half, :])
    hi = _dot(k16, w1_ref[half:2 * half, :])
    pe = _dot(pe_ref[...], w1_ref[...])
    rows = k16.shape[0]
    hid = lo + pltpu.roll(hi, rows - 1, axis=0) + pe[0:1, :]
    act = 0.5 * hid * (1.0 + jnp.tanh(math.sqrt(2.0 / math.pi) * (hid + 0.044715 * hid * hid * hid)))
    o_ref[0] = _dot(act.astype(BF), w2_ref[...])


def _compress(k16, pe8, w1, w2):
    n, rows, width = k16.shape
    return pl.pallas_call(
        _compress_kernel,
        grid=(n,),
        in_specs=[pl.BlockSpec((1, rows, width), lambda i: (i, 0, 0)),
                  pl.BlockSpec(pe8.shape, lambda i: (0, 0)),
                  pl.BlockSpec(w1.shape, lambda i: (0, 0)),
                  pl.BlockSpec(w2.shape, lambda i: (0, 0))],
        out_specs=pl.BlockSpec((1, rows, HEAD_DIM), lambda i: (i, 0, 0)),
        out_shape=jax.ShapeDtypeStruct((n, rows, HEAD_DIM), F32),
        compiler_params=_cparams("parallel"),
    )(k16, pe8, w1, w2)


def _bias_tables_kernel(rb_ref, wb_ref, sb_ref, cf_ref):
    h = pl.program_id(0)
    last = rb_ref[REL_BUCKETS - 1, h]

    def rel(dist):
        n = jnp.maximum(dist, 0)
        bucket = jnp.full(n.shape, REL_BUCKETS // 2, jnp.int32)
        for t in T5_THRESHOLDS:
            bucket = bucket + (n >= t).astype(jnp.int32)
        bucket = jnp.where(n < REL_BUCKETS // 2, n, jnp.minimum(bucket, REL_BUCKETS - 1))
        out = jnp.zeros(n.shape, F32)
        for bk in range(REL_BUCKETS - 1):
            out = jnp.where(bucket == bk, rb_ref[bk, h] - last, out)
        return out * LOG2E

    def dist_of(shape, offset):
        qi = lax.broadcasted_iota(jnp.int32, shape, 0)
        kl = lax.broadcasted_iota(jnp.int32, shape, 1)
        return qi + offset - kl

    dw = dist_of((Q_TILE, WIN_KEYS), WIN_PAD)
    wb_ref[0] = jnp.where((dw >= 0) & (dw < WINDOW), rel(dw), NEG)

    ds_ = dist_of((Q_TILE, SLC_LOCAL_KEYS), SLC_PAD)
    sb_ref[0] = jnp.where(ds_ >= 0, rel(ds_), NEG)

    qi = lax.broadcasted_iota(jnp.int32, (Q_TILE, LANES), 0)
    col = lax.broadcasted_iota(jnp.int32, (Q_TILE, LANES), 1)
    mi = col & (CMP_BAND - 1)
    part = col >> (CMP_BAND.bit_length() - 1)
    dc = qi - CMP_STRIDE * (mi - CMP_BAND_LO) - (CMP_LEN - 1)
    val = jnp.where(dc >= 0, rel(dc), 0.0)
    p0 = val.astype(BF).astype(F32)
    r1 = val - p0
    p1 = r1.astype(BF).astype(F32)
    p2 = r1 - p1
    parts = jnp.where(part == 0, p0, jnp.where(part == 1, p1, jnp.where(part == 2, p2, 0.0)))
    cf_ref[0] = parts.astype(BF)


def _bias_tables(rel_bias):
    smem = pl.BlockSpec(memory_space=pltpu.SMEM)
    lk = SLC_LOCAL_KEYS
    return pl.pallas_call(
        _bias_tables_kernel,
        grid=(N_HEADS,),
        in_specs=[smem],
        out_specs=[pl.BlockSpec((1, Q_TILE, WIN_KEYS), lambda h: (h, 0, 0)),
                   pl.BlockSpec((1, Q_TILE, lk), lambda h: (h, 0, 0)),
                   pl.BlockSpec((1, Q_TILE, LANES), lambda h: (h, 0, 0))],
        out_shape=[jax.ShapeDtypeStruct((N_HEADS, Q_TILE, WIN_KEYS), F32),
                   jax.ShapeDtypeStruct((N_HEADS, Q_TILE, lk), F32),
                   jax.ShapeDtypeStruct((N_HEADS, Q_TILE, LANES), BF)],
        compiler_params=_cparams("arbitrary"),
    )(rel_bias)


def _softmax_parts(s):
    m = jnp.max(s, axis=-1, keepdims=True)
    e = jnp.exp2(s - m)
    return m, e, jnp.sum(e, axis=-1, keepdims=True)


def _lane_tiles(v, op):
    out = v[:, 0:LANES]
    for t in range(1, v.shape[1] // LANES):
        out = op(out, v[:, t * LANES:(t + 1) * LANES])
    return out


def _attn_kernel(*refs, seq, n_cast):
    (q_ref, ks_ref, vs_ref, kw_ref, vw_ref, kc_ref, vc_ref, ng_ref,
     wb_ref, sb_ref, cf_ref, ovt_ref, eye_ref, spread_ref) = refs[:N_ATTN_IN]
    cast_in = refs[N_ATTN_IN:N_ATTN_IN + n_cast]
    o_ref = refs[N_ATTN_IN + n_cast]
    cast_out = refs[N_ATTN_IN + n_cast + 1:N_ATTN_IN + 2 * n_cast + 1]
    kaug_ref, vsp_ref, kwp_ref, vwp_ref, key_ref, sfar_ref = refs[N_ATTN_IN + 2 * n_cast + 1:]

    for w_ref, wo_ref in zip(cast_in, cast_out):
        wo_ref[...] = w_ref[...].astype(BF)

    tile = pl.program_id(2)
    n_slc = seq // SLC_BLOCK
    ncp = seq // CMP_STRIDE
    rows = HPG * Q_TILE
    n_sel = min(N_SEL, n_slc)
    if n_slc != PAD_FLAG_COL or Q_TILE % LANES or 3 * CMP_BAND > LANES:
        raise NotImplementedError("selection layout expects 64 selection blocks and whole-lane query tiles")
    assert CMP_STRIDE * (CMP_BAND_LO + 1) - (CMP_LEN - 1) >= T5_THRESHOLDS[-1]
    assert CMP_STRIDE * (CMP_BAND - CMP_BAND_LO) + CMP_LEN - 1 >= Q_TILE

    @pl.when(tile == 0)
    def _stage_keys():
        col = lax.broadcasted_iota(jnp.int32, (SLC_PAD, LANES), 1)
        kaug_ref[0:SLC_PAD, 0:HEAD_DIM] = jnp.zeros((SLC_PAD, HEAD_DIM), BF)
        kaug_ref[0:SLC_PAD, HEAD_DIM:] = jnp.where(col == PAD_FLAG_COL, NEG, 0.0).astype(BF)
        kaug_ref[SLC_PAD:, 0:HEAD_DIM] = ks_ref[0]
        blk = lax.broadcasted_iota(jnp.int32, (seq, LANES), 0) >> (SLC_BLOCK.bit_length() - 1)
        colk = lax.broadcasted_iota(jnp.int32, (seq, LANES), 1)
        kaug_ref[SLC_PAD:, HEAD_DIM:] = jnp.where(colk == blk, NEG, 0.0).astype(BF)
        vsp_ref[:, HEAD_DIM:] = jnp.ones((SLC_PAD + seq, LANES), BF)
        vsp_ref[0:SLC_PAD, 0:HEAD_DIM] = jnp.zeros((SLC_PAD, HEAD_DIM), BF)
        vsp_ref[SLC_PAD:, 0:HEAD_DIM] = vs_ref[0]
        kwp_ref[0:WIN_PAD, :] = jnp.zeros((WIN_PAD, HEAD_DIM), BF)
        kwp_ref[WIN_PAD:, :] = kw_ref[0]
        vwp_ref[:, HEAD_DIM:] = jnp.ones((WIN_PAD + seq, LANES), BF)
        vwp_ref[0:WIN_PAD, 0:HEAD_DIM] = jnp.zeros((WIN_PAD, HEAD_DIM), BF)
        vwp_ref[WIN_PAD:, 0:HEAD_DIM] = vw_ref[0]

    q4 = jnp.concatenate([q_ref[0, :, hh * HEAD_DIM:(hh + 1) * HEAD_DIM] for hh in range(HPG)], axis=0)
    t0 = tile * Q_TILE
    r0 = pl.multiple_of(t0, Q_TILE)

    gts = jax.nn.sigmoid(ng_ref[0].astype(F32))
    g_hi = gts.astype(BF)
    g_lo = (gts - g_hi.astype(F32)).astype(BF)
    spread = spread_ref[...]
    gsp = _dot(g_hi, spread) + _dot(g_lo, spread)

    key_w = lax.broadcasted_iota(jnp.int32, (1, WIN_KEYS), 1)
    before_start = jnp.where(key_w < WIN_PAD - t0, NEG, 0.0)
    s_w = _dot_nt(q4, kwp_ref[pl.ds(r0, WIN_KEYS), :]) + wb_ref[...].reshape(rows, WIN_KEYS) + before_start
    e_w = jnp.exp2(s_w - jnp.max(s_w, axis=-1, keepdims=True))
    pv_w = _dot(e_w.astype(BF), vwp_ref[pl.ds(r0, WIN_KEYS), :])
    o_win = pv_w[:, 0:HEAD_DIM] * (1.0 / pv_w[:, HEAD_DIM:])

    kc = kc_ref[0].astype(BF)
    vc = vc_ref[0].astype(BF)
    r_io = lax.broadcasted_iota(jnp.int32, (LANES, ncp), 0)
    n_io = lax.broadcasted_iota(jnp.int32, (LANES, ncp), 1)
    band = (r_io < 3 * CMP_BAND) & (n_io == CMP_PER_TILE * tile + (r_io & (CMP_BAND - 1)) - CMP_BAND_LO)
    shift = jnp.where(band, 1.0, 0.0).astype(BF)
    cf4 = cf_ref[...].reshape(rows, LANES)
    lc = _dot_nt(q4, kc) + _dot(cf4, shift)
    qi_c = lax.broadcasted_iota(jnp.int32, (Q_TILE, ncp), 0)
    n_c = lax.broadcasted_iota(jnp.int32, (Q_TILE, ncp), 1)
    vis = (CMP_STRIDE * n_c + (CMP_LEN - 1)) <= (t0 + qi_c)
    lc3 = jnp.where(vis[None], lc.reshape(HPG, Q_TILE, ncp), NEG)
    _, e_c, l_c = _softmax_parts(lc3)
    p_c = jnp.where(vis[None], e_c * (1.0 / l_c), 0.0)
    o_cmp = _dot(p_c.reshape(rows, ncp).astype(BF), vc)

    psum = p_c[0]
    for hh in range(1, HPG):
        psum = psum + p_c[hh]
    p0 = psum.astype(BF)
    r1 = psum - p0.astype(F32)
    p1 = r1.astype(BF)
    p2 = (r1 - p1.astype(F32)).astype(BF)
    ovt = ovt_ref[...]
    imp_t = _dot_nt(ovt, p0) + _dot_nt(ovt, p1) + _dot_nt(ovt, p2)

    def gate(hh, branch):
        c = 3 * hh + branch
        return gsp[:, c * LANES:(c + 1) * LANES]

    gated_cw = [gate(hh, 0) * o_cmp[hh * Q_TILE:(hh + 1) * Q_TILE] + gate(hh, 2) * o_win[hh * Q_TILE:(hh + 1) * Q_TILE]
                for hh in range(HPG)]

    blk_t = lax.broadcasted_iota(jnp.int32, (n_slc, Q_TILE), 0)
    q_t = lax.broadcasted_iota(jnp.int32, (n_slc, Q_TILE), 1)
    cur_t = (t0 + q_t) >> (SLC_BLOCK.bit_length() - 1)
    forced_t = (blk_t == 0) | (blk_t == cur_t) | (blk_t == cur_t - 1)
    causal_t = blk_t <= cur_t
    key = jnp.where(forced_t, KEY_FORCED, jnp.where(causal_t, lax.bitcast_convert_type(imp_t, jnp.int32), KEY_FUTURE))
    key_ref[...] = key
    groups = [key[8 * a:8 * a + 8, :] for a in range(n_slc // 8)]
    ranks = [jnp.zeros((8, Q_TILE), jnp.int32) for _ in groups]
    sub = lax.broadcasted_iota(jnp.int32, (8, Q_TILE), 0)
    for bb in range(n_slc):
        kb = key_ref[bb:bb + 1, :]
        kb1 = kb + 1
        for a in range(n_slc // 8):
            if 8 * a > bb:
                ahead = kb1
            elif 8 * a + 7 < bb:
                ahead = kb
            else:
                ahead = jnp.where(sub > bb - 8 * a, kb1, kb)
            ranks[a] = ranks[a] + (ahead > groups[a]).astype(jnp.int32)
    rank = jnp.concatenate(ranks, axis=0)
    sel_t = (rank < n_sel) & causal_t
    first_local = (t0 - SLC_PAD) >> (SLC_BLOCK.bit_length() - 1)
    far_t = blk_t < first_local
    flag_rows = jnp.where(blk_t == 0, 1.0, 0.0)
    stack = jnp.concatenate([jnp.where(sel_t, 0.0, 1.0), flag_rows,
                             jnp.where(sel_t & far_t, 0.0, 1.0), flag_rows], axis=0).astype(BF)
    qmask = _dot_nt(eye_ref[...], stack).astype(BF)
    qa_loc = jnp.concatenate([q4, qmask[:, 0:LANES]], axis=1)
    qa_far = jnp.concatenate([q4, qmask[:, LANES:]], axis=1)

    lk = SLC_LOCAL_KEYS
    chunk = SLC_CHUNK_BLOCKS * SLC_BLOCK
    n_far = (first_local + SLC_CHUNK_BLOCKS - 1) // SLC_CHUNK_BLOCKS

    s_loc = _dot_nt(qa_loc, kaug_ref[pl.ds(r0, lk), :]) + sb_ref[...].reshape(rows, lk)

    def far_logits(c, m_part):
        rr = pl.multiple_of(SLC_PAD + c * chunk, SLC_BLOCK)
        s = _dot_nt(qa_far, kaug_ref[pl.ds(rr, chunk), :])
        sfar_ref[c] = s
        return jnp.maximum(m_part, _lane_tiles(s, jnp.maximum))

    m_part = lax.fori_loop(0, n_far, far_logits, _lane_tiles(s_loc, jnp.maximum))
    m_s = jnp.max(m_part, axis=-1, keepdims=True)

    e_loc = jnp.exp2(s_loc - m_s)
    acc_s = _dot(e_loc.astype(BF), vsp_ref[pl.ds(r0, lk), :])

    def far_values(c, acc):
        rr = pl.multiple_of(SLC_PAD + c * chunk, SLC_BLOCK)
        e = jnp.exp2(sfar_ref[c] - m_s)
        return acc + _dot(e.astype(BF), vsp_ref[pl.ds(rr, chunk), :])

    acc_s = lax.fori_loop(0, n_far, far_values, acc_s)
    o_slc = acc_s[:, 0:HEAD_DIM] * (1.0 / acc_s[:, HEAD_DIM:])

    for hh in range(HPG):
        o_h = gated_cw[hh] + gate(hh, 1) * o_slc[hh * Q_TILE:(hh + 1) * Q_TILE]
        o_ref[0, :, hh * HEAD_DIM:(hh + 1) * HEAD_DIM] = o_h.astype(o_ref.dtype)


N_ATTN_IN = 14


def _nsa_attention(proj, k_cmp, v_cmp, wb, sb, cf, to_cast):
    b, s, _ = proj.shape
    n_steps = b * N_KV * (s // Q_TILE)
    cast_specs = []
    for w in to_cast:
        slab = next(r for r in range(BF16_SUBLANES, w.shape[0] + 1, BF16_SUBLANES)
                    if w.shape[0] % r == 0 and w.shape[0] // r <= n_steps)
        last = w.shape[0] // slab - 1
        cast_specs.append(pl.BlockSpec(
            (slab, w.shape[1]),
            lambda i, g, j, last=last: (jnp.minimum((i * N_KV + g) * (s // Q_TILE) + j, last), 0)))
    ovt = _overlap_matrix(s)
    eye = jnp.asarray(np.tile(np.eye(Q_TILE), (HPG, 1)), BF)
    spread = jnp.asarray(np.repeat(np.eye(LANES)[:, :3 * HPG], LANES, axis=1), BF)
    n_chunks = -(-(s // SLC_BLOCK) // SLC_CHUNK_BLOCKS)
    q_blk = Q_OFF // (HPG * HEAD_DIM)
    kv_blk = KC_OFF // HEAD_DIM
    ng_blk = NG_OFF // LANES
    lk = SLC_LOCAL_KEYS
    ncp = s // CMP_STRIDE

    def kv_spec(idx):
        return pl.BlockSpec((1, s, HEAD_DIM), lambda i, g, j: (i, 0, kv_blk + idx * N_KV + g))

    cmp_spec = pl.BlockSpec((1, ncp, HEAD_DIM), lambda i, g, j: (i * N_KV + g, 0, 0))
    outs = pl.pallas_call(
        functools.partial(_attn_kernel, seq=s, n_cast=len(to_cast)),
        grid=(b, N_KV, s // Q_TILE),
        in_specs=[pl.BlockSpec((1, Q_TILE, HPG * HEAD_DIM), lambda i, g, j: (i, j, q_blk + g)),
                  kv_spec(2), kv_spec(3), kv_spec(4), kv_spec(5),
                  cmp_spec, cmp_spec,
                  pl.BlockSpec((1, Q_TILE, LANES), lambda i, g, j: (i, j, ng_blk + g)),
                  pl.BlockSpec((HPG, Q_TILE, WIN_KEYS), lambda i, g, j: (g, 0, 0)),
                  pl.BlockSpec((HPG, Q_TILE, lk), lambda i, g, j: (g, 0, 0)),
                  pl.BlockSpec((HPG, Q_TILE, LANES), lambda i, g, j: (g, 0, 0)),
                  pl.BlockSpec(ovt.shape, lambda i, g, j: (0, 0)),
                  pl.BlockSpec(eye.shape, lambda i, g, j: (0, 0)),
                  pl.BlockSpec(spread.shape, lambda i, g, j: (0, 0))] + cast_specs,
        out_specs=[pl.BlockSpec((1, Q_TILE, HPG * HEAD_DIM), lambda i, g, j: (i, j, g))] + cast_specs,
        out_shape=[jax.ShapeDtypeStruct((b, s, Q_WIDTH), BF)]
        + [jax.ShapeDtypeStruct(w.shape, BF) for w in to_cast],
        scratch_shapes=[pltpu.VMEM((SLC_PAD + s, 2 * HEAD_DIM), BF),
                        pltpu.VMEM((SLC_PAD + s, HEAD_DIM + LANES), BF),
                        pltpu.VMEM((WIN_PAD + s, HEAD_DIM), BF),
                        pltpu.VMEM((WIN_PAD + s, HEAD_DIM + LANES), BF),
                        pltpu.VMEM((s // SLC_BLOCK, Q_TILE), jnp.int32),
                        pltpu.VMEM((n_chunks, HPG * Q_TILE, SLC_CHUNK_BLOCKS * SLC_BLOCK), F32)],
        compiler_params=pltpu.CompilerParams(dimension_semantics=("arbitrary",) * 3,
                                             vmem_limit_bytes=ATTN_VMEM_LIMIT),
    )(proj, proj, proj, proj, proj, k_cmp, v_cmp, proj, wb, sb, cf, ovt, eye, spread, *to_cast)
    return outs[0], outs[1:]


def _overlap_matrix(s):
    ncp = s // CMP_STRIDE
    n_slc = s // SLC_BLOCK
    nc = (s - CMP_LEN) // CMP_STRIDE + 1
    i = np.arange(ncp)[None, :]
    jj = np.arange(n_slc)[:, None]
    ov = ((i * CMP_STRIDE < (jj + 1) * SLC_BLOCK) & (i * CMP_STRIDE + CMP_LEN > jj * SLC_BLOCK) & (i < nc))
    return jnp.asarray(ov, BF)


def kernel(x, c, w_ada, b_ada, g_pre_mix, g_post_mix, g_pre_ffn, g_post_ffn, w_in, w_dw, b_dw, conv_ln_g, conv_ln_b, w_conv_out, cmp_pe_k, cmp_pe_v, w_cmp_k1, w_cmp_k2, w_cmp_v1, w_cmp_v2, rel_bias, w_nsa_out, w_out, w_ffn_gate, w_ffn_up, w_ffn_down):
    b, s, d = x.shape
    m = b * s
    depth = w_ada.shape[0]
    c8 = jnp.zeros((8, d), F32).at[:b].set(c)
    wb, sb, cf = _bias_tables(rel_bias)
    col_scale = jnp.ones((1, NG_OFF + N_KV * LANES), F32).at[:, Q_OFF:KC_OFF].set(HEAD_DIM ** -0.5 * LOG2E)

    for l in range(depth):
        mod = _modulation(c8, w_ada[l], b_ada[l][None, :])
        mod3 = mod[:b].reshape(b, 6, d)

        w_all = w_in[l].astype(BF)
        w_ng = w_in[l][:, NG_OFF:MG_OFF].reshape(d, N_KV, 3 * HPG)
        w_ng = jnp.pad(w_ng, ((0, 0), (0, 0), (0, LANES - 3 * HPG))).reshape(d, N_KV * LANES)
        w_ng = w_ng.astype(BF)
        w_gates = w_all[:, MG_OFF:]

        h = _prenorm(x, g_pre_mix[l][None, :], mod3, 0, 1).reshape(m, d)
        proj = _matmul_with_tail(h, w_all, NG_OFF, w_ng, col_scale, BF, 1024, 512).reshape(b, s, -1)
        gates = _matmul(h, w_gates, BF, 1024, 1024)

        u_conv = _conv_module(proj, w_dw[l], b_dw[l][None, :], conv_ln_g[l][None, :], conv_ln_b[l][None, :])

        def grouped(off):
            t = proj[:, :, off:off + KV_WIDTH].reshape(b, s // CMP_STRIDE, CMP_STRIDE, N_KV, HEAD_DIM)
            return t.transpose(0, 3, 1, 2, 4).reshape(b * N_KV, s // CMP_STRIDE, CMP_STRIDE * HEAD_DIM)

        pe_k = jnp.broadcast_to(cmp_pe_k[l].reshape(1, -1), (8, CMP_LEN * HEAD_DIM)).astype(BF)
        pe_v = jnp.broadcast_to(cmp_pe_v[l].reshape(1, -1), (8, CMP_LEN * HEAD_DIM)).astype(BF)
        k_cmp = _compress(grouped(KC_OFF), pe_k, w_cmp_k1[l].astype(BF), w_cmp_k2[l].astype(BF))
        v_cmp = _compress(grouped(KC_OFF + KV_WIDTH), pe_v, w_cmp_v1[l].astype(BF), w_cmp_v2[l].astype(BF))

        later = [w_conv_out[l], w_nsa_out[l], w_out[l], w_ffn_gate[l], w_ffn_up[l], w_ffn_down[l]]
        o_nsa, (wb_conv, wb_nsa, wb_out, wb_gate, wb_up, wb_down) = _nsa_attention(proj, k_cmp, v_cmp, wb, sb, cf, later)

        z = _gated_mix(u_conv.reshape(m, CONV_DIM), o_nsa.reshape(m, Q_WIDTH), wb_conv, wb_nsa, gates, d)
        mix = _matmul(z, wb_out, BF, 1024, 1024).reshape(b, s, d)
        x, h2 = _post_mix(mix, x, g_post_mix[l][None, :], g_pre_ffn[l][None, :], mod3, 2, 3, 4)

        ff = _ffn_up(h2.reshape(m, d), wb_gate, wb_up)
        f = _matmul(ff, wb_down, BF, 512, 512).reshape(b, s, d)
        x = _post_ffn(f, x, g_post_ffn[l][None, :], mod3, 5)
    return x
```

```python
import functools
import math

import numpy as np
import jax
import jax.numpy as jnp
from jax import lax
from jax.experimental import pallas as pl
from jax.experimental.pallas import tpu as pltpu

F32 = jnp.float32
BF = jnp.bfloat16

CONV_DIM = 2048
CONV_WIDTH = 31
N_HEADS = 16
N_KV = 4
HPG = N_HEADS // N_KV
HEAD_DIM = 128
CMP_LEN = 32
CMP_STRIDE = 16
CMP_HID = 512
SLC_BLOCK = 64
N_SEL = 16
WINDOW = 512
Q_BLOCK = 64
REL_BUCKETS = 32
REL_MAX_DIST = 128
NORM_EPS = 1e-6
NEG = -1e30
LOG2E = 1.4426950408889634

Q_WIDTH = N_HEADS * HEAD_DIM
KV_WIDTH = N_KV * HEAD_DIM
Q_OFF = 2 * CONV_DIM
KC_OFF = Q_OFF + Q_WIDTH
NG_OFF = KC_OFF + 6 * KV_WIDTH
MG_OFF = NG_OFF + 3 * N_HEADS

LANES = 128
SUBLANES = 8
BF16_SUBLANES = 16
VMEM_LIMIT = 56 * 1024 * 1024
ATTN_VMEM_LIMIT = 61 * 1024 * 1024

CONV_HALO = 32
CONV_ROWS = 64
CONV_LANES = 256

Q_TILE = 4 * Q_BLOCK
SLC_PAD = 2 * SLC_BLOCK
SLC_LOCAL_KEYS = SLC_PAD + Q_TILE
SLC_CHUNK_BLOCKS = 16
WIN_KEYS = WINDOW + Q_TILE
WIN_PAD = WINDOW
CMP_PER_TILE = Q_TILE // CMP_STRIDE
CMP_BAND = 32
CMP_BAND_LO = 9
PAD_FLAG_COL = 64
KEY_FORCED = 0x7F000000
KEY_FUTURE = -2


def _t5_thresholds():
    max_exact = REL_BUCKETS // 2
    thr = []
    for k in range(1, REL_BUCKETS - max_exact):
        n = max_exact
        while max_exact + int(math.log(n / max_exact) / math.log(REL_MAX_DIST / max_exact)
                              * (REL_BUCKETS - max_exact)) < max_exact + k:
            n += 1
        thr.append(n)
    return tuple(thr)


T5_THRESHOLDS = _t5_thresholds()


def _cparams(*sem):
    return pltpu.CompilerParams(dimension_semantics=sem, vmem_limit_bytes=VMEM_LIMIT)


def _silu(v):
    return v * jax.nn.sigmoid(v)


def _dot(a, b):
    return jnp.dot(a, b, preferred_element_type=F32)


def _dot_nt(a, b):
    return lax.dot_general(a, b, (((1,), (1,)), ((), ())), preferred_element_type=F32)


def _rms(v, g):
    return v * lax.rsqrt(jnp.mean(v * v, axis=-1, keepdims=True) + NORM_EPS) * g


def _mod_kernel(c_ref, w_ref, b_ref, o_ref):
    a = _silu(c_ref[...]).astype(BF)
    o_ref[...] = _dot(a, w_ref[...].astype(BF)) + b_ref[...]


def _modulation(c8, w_ada, b_ada, tn=512):
    rows, d = c8.shape
    n = w_ada.shape[1]
    return pl.pallas_call(
        _mod_kernel,
        grid=(n // tn,),
        in_specs=[pl.BlockSpec((rows, d), lambda j: (0, 0)),
                  pl.BlockSpec((d, tn), lambda j: (0, j)),
                  pl.BlockSpec((1, tn), lambda j: (0, j))],
        out_specs=pl.BlockSpec((rows, tn), lambda j: (0, j)),
        out_shape=jax.ShapeDtypeStruct((rows, n), F32),
        compiler_params=_cparams("parallel"),
    )(c8, w_ada, b_ada)


def _prenorm_kernel(x_ref, g_ref, mod_ref, h_ref, *, shift_row, scale_row):
    y = _rms(x_ref[0], g_ref[...])
    h = y * (1.0 + mod_ref[0, scale_row:scale_row + 1, :]) + mod_ref[0, shift_row:shift_row + 1, :]
    h_ref[0] = h.astype(BF)


def _prenorm(x, g, mod3, shift_row, scale_row, tr=256):
    b, s, d = x.shape
    return pl.pallas_call(
        functools.partial(_prenorm_kernel, shift_row=shift_row, scale_row=scale_row),
        grid=(b, s // tr),
        in_specs=[pl.BlockSpec((1, tr, d), lambda i, r: (i, r, 0)),
                  pl.BlockSpec((1, d), lambda i, r: (0, 0)),
                  pl.BlockSpec((1, 6, d), lambda i, r: (i, 0, 0))],
        out_specs=pl.BlockSpec((1, tr, d), lambda i, r: (i, r, 0)),
        out_shape=jax.ShapeDtypeStruct((b, s, d), BF),
        compiler_params=_cparams("parallel", "parallel"),
    )(x, g, mod3)


def _post_mix_kernel(y_ref, x_ref, gp_ref, gn_ref, mod_ref, xo_ref, h_ref, *, gate_row, shift_row, scale_row):
    x1 = x_ref[0] + mod_ref[0, gate_row:gate_row + 1, :] * _rms(y_ref[0].astype(F32), gp_ref[...])
    xo_ref[0] = x1
    h = _rms(x1, gn_ref[...]) * (1.0 + mod_ref[0, scale_row:scale_row + 1, :]) \
        + mod_ref[0, shift_row:shift_row + 1, :]
    h_ref[0] = h.astype(BF)


def _post_mix(y, x, g_post, g_next, mod3, gate_row, shift_row, scale_row, tr=256):
    b, s, d = x.shape
    row = pl.BlockSpec((1, tr, d), lambda i, r: (i, r, 0))
    vec = pl.BlockSpec((1, d), lambda i, r: (0, 0))
    return pl.pallas_call(
        functools.partial(_post_mix_kernel, gate_row=gate_row, shift_row=shift_row, scale_row=scale_row),
        grid=(b, s // tr),
        in_specs=[row, row, vec, vec, pl.BlockSpec((1, 6, d), lambda i, r: (i, 0, 0))],
        out_specs=[row, row],
        out_shape=[jax.ShapeDtypeStruct((b, s, d), F32), jax.ShapeDtypeStruct((b, s, d), BF)],
        compiler_params=_cparams("parallel", "parallel"),
    )(y, x, g_post, g_next, mod3)


def _post_ffn_kernel(y_ref, x_ref, gp_ref, mod_ref, xo_ref, *, gate_row):
    xo_ref[0] = x_ref[0] + mod_ref[0, gate_row:gate_row + 1, :] * _rms(y_ref[0].astype(F32), gp_ref[...])


def _post_ffn(y, x, g_post, mod3, gate_row, tr=256):
    b, s, d = x.shape
    row = pl.BlockSpec((1, tr, d), lambda i, r: (i, r, 0))
    return pl.pallas_call(
        functools.partial(_post_ffn_kernel, gate_row=gate_row),
        grid=(b, s // tr),
        in_specs=[row, row, pl.BlockSpec((1, d), lambda i, r: (0, 0)),
                  pl.BlockSpec((1, 6, d), lambda i, r: (i, 0, 0))],
        out_specs=row,
        out_shape=jax.ShapeDtypeStruct((b, s, d), F32),
        compiler_params=_cparams("parallel", "parallel"),
    )(y, x, g_post, mod3)


def _mm_scale_kernel(a_ref, w_ref, s_ref, o_ref):
    o_ref[...] = (_dot(a_ref[...], w_ref[...]) * s_ref[...]).astype(o_ref.dtype)


def _mm_kernel(a_ref, w_ref, o_ref):
    o_ref[...] = _dot(a_ref[...], w_ref[...]).astype(o_ref.dtype)


def _matmul(a, w, out_dtype, tm, tn, col_scale=None):
    m, k = a.shape
    n = w.shape[1]
    in_specs = [pl.BlockSpec((tm, k), lambda i, j: (i, 0)),
                pl.BlockSpec((k, tn), lambda i, j: (0, j))]
    args = [a, w]
    body = _mm_kernel
    if col_scale is not None:
        in_specs.append(pl.BlockSpec((1, tn), lambda i, j: (0, j)))
        args.append(col_scale)
        body = _mm_scale_kernel
    return pl.pallas_call(
        body,
        grid=(m // tm, n // tn),
        in_specs=in_specs,
        out_specs=pl.BlockSpec((tm, tn), lambda i, j: (i, j)),
        out_shape=jax.ShapeDtypeStruct((m, n), out_dtype),
        compiler_params=_cparams("parallel", "arbitrary"),
    )(*args)


def _mm_tail_kernel(a_ref, w_ref, wt_ref, s_ref, o_ref, *, n_main):
    j = pl.program_id(1)

    @pl.when(j < n_main)
    def _main():
        o_ref[...] = (_dot(a_ref[...], w_ref[...]) * s_ref[...]).astype(o_ref.dtype)

    @pl.when(j >= n_main)
    def _tail():
        o_ref[...] = (_dot(a_ref[...], wt_ref[...]) * s_ref[...]).astype(o_ref.dtype)


def _matmul_with_tail(a, w, n_w, w_tail, col_scale, out_dtype, tm, tn):
    m, k = a.shape
    n_main = n_w // tn
    n_tail = w_tail.shape[1] // tn
    n = n_w + w_tail.shape[1]
    return pl.pallas_call(
        functools.partial(_mm_tail_kernel, n_main=n_main),
        grid=(m // tm, n_main + n_tail),
        in_specs=[pl.BlockSpec((tm, k), lambda i, j: (i, 0)),
                  pl.BlockSpec((k, tn), lambda i, j: (0, jnp.minimum(j, n_main - 1))),
                  pl.BlockSpec((k, tn), lambda i, j: (0, jnp.maximum(j - n_main, 0))),
                  pl.BlockSpec((1, tn), lambda i, j: (0, j))],
        out_specs=pl.BlockSpec((tm, tn), lambda i, j: (i, j)),
        out_shape=jax.ShapeDtypeStruct((m, n), out_dtype),
        compiler_params=_cparams("parallel", "arbitrary"),
    )(a, w, w_tail, col_scale)


def _mix_kernel(a1_ref, a2_ref, w1_ref, w2_ref, ga_ref, gb_ref, o_ref):
    ya = _dot(a1_ref[...], w1_ref[...])
    yb = _dot(a2_ref[...], w2_ref[...])
    o = jax.nn.sigmoid(ga_ref[...].astype(F32)) * ya + jax.nn.sigmoid(gb_ref[...].astype(F32)) * yb
    o_ref[...] = o.astype(o_ref.dtype)


def _gated_mix(u_conv, o_nsa, w_conv_out, w_nsa_out, gates, d_model, tm=1024, tn=512):
    m, ka = u_conv.shape
    kb = o_nsa.shape[1]
    gb_block = d_model // tn
    return pl.pallas_call(
        _mix_kernel,
        grid=(m // tm, d_model // tn),
        in_specs=[pl.BlockSpec((tm, ka), lambda i, j: (i, 0)),
                  pl.BlockSpec((tm, kb), lambda i, j: (i, 0)),
                  pl.BlockSpec((ka, tn), lambda i, j: (0, j)),
                  pl.BlockSpec((kb, tn), lambda i, j: (0, j)),
                  pl.BlockSpec((tm, tn), lambda i, j: (i, j)),
                  pl.BlockSpec((tm, tn), lambda i, j: (i, j + gb_block))],
        out_specs=pl.BlockSpec((tm, tn), lambda i, j: (i, j)),
        out_shape=jax.ShapeDtypeStruct((m, d_model), BF),
        compiler_params=_cparams("parallel", "arbitrary"),
    )(u_conv, o_nsa, w_conv_out, w_nsa_out, gates, gates)


def _ffn_up_kernel(a_ref, wg_ref, wu_ref, o_ref):
    a = a_ref[...]
    o_ref[...] = (_silu(_dot(a, wg_ref[...])) * _dot(a, wu_ref[...])).astype(o_ref.dtype)


def _ffn_up(h, w_gate, w_up, tm=2048, tn=256):
    m, k = h.shape
    n = w_gate.shape[1]
    wspec = pl.BlockSpec((k, tn), lambda i, j: (0, j))
    return pl.pallas_call(
        _ffn_up_kernel,
        grid=(m // tm, n // tn),
        in_specs=[pl.BlockSpec((tm, k), lambda i, j: (i, 0)), wspec, wspec],
        out_specs=pl.BlockSpec((tm, tn), lambda i, j: (i, j)),
        out_shape=jax.ShapeDtypeStruct((m, n), BF),
        compiler_params=_cparams("parallel", "arbitrary"),
    )(h, w_gate, w_up)


def _conv_kernel(a_ref, b_ref, ah_ref, bh_ref, w_ref, bdw_ref, g_ref, bb_ref, o_ref, u_ref, sh_ref, v_ref, *, ts):
    i = pl.program_id(1)
    u_ref[CONV_HALO:CONV_HALO + ts, :] = a_ref[0].astype(F32) * jax.nn.sigmoid(b_ref[0].astype(F32))
    uh = ah_ref[0].astype(F32) * jax.nn.sigmoid(bh_ref[0].astype(F32))
    u_ref[0:CONV_HALO, :] = jnp.where(i > 0, uh, 0.0)

    rc, cc = CONV_ROWS, CONV_LANES
    first = CONV_HALO - (CONV_WIDTH - 1)
    span = ts + CONV_HALO - SUBLANES
    for c0 in range(0, CONV_DIM, cc):
        for s in range(1, SUBLANES):
            sh_ref[s - 1] = u_ref[s:s + span, c0:c0 + cc]

        def row_chunk(rb, carry, c0=c0):
            r0 = pl.multiple_of(rb * rc, rc)
            acc = jnp.zeros((rc, cc), F32)
            for k in range(CONV_WIDTH):
                s = (first + k) % SUBLANES
                base = first + k - s
                if s == 0:
                    tap = u_ref[pl.ds(r0 + base, rc), c0:c0 + cc]
                else:
                    tap = sh_ref[s - 1, pl.ds(r0 + base, rc), :]
                acc = acc + tap * w_ref[k:k + 1, c0:c0 + cc]
            v_ref[pl.ds(r0, rc), c0:c0 + cc] = acc + bdw_ref[:, c0:c0 + cc]
            return carry

        lax.fori_loop(0, ts // rc, row_chunk, 0)

    v = v_ref[...]
    mu = jnp.mean(v, axis=-1, keepdims=True)
    vc = v - mu
    y = vc * lax.rsqrt(jnp.mean(vc * vc, axis=-1, keepdims=True) + NORM_EPS) * g_ref[...] + bb_ref[...]
    o_ref[0] = _silu(y).astype(o_ref.dtype)


def _conv_module(proj, w_dw, b_dw, ln_g, ln_b, ts=256):
    b, s, _ = proj.shape
    hb = ts // CONV_HALO
    cur_a = pl.BlockSpec((1, ts, CONV_DIM), lambda i, r: (i, r, 0))
    cur_b = pl.BlockSpec((1, ts, CONV_DIM), lambda i, r: (i, r, 1))
    halo_a = pl.BlockSpec((1, CONV_HALO, CONV_DIM), lambda i, r: (i, jnp.maximum(r * hb - 1, 0), 0))
    halo_b = pl.BlockSpec((1, CONV_HALO, CONV_DIM), lambda i, r: (i, jnp.maximum(r * hb - 1, 0), 1))
    vec = pl.BlockSpec((1, CONV_DIM), lambda i, r: (0, 0))
    return pl.pallas_call(
        functools.partial(_conv_kernel, ts=ts),
        grid=(b, s // ts),
        in_specs=[cur_a, cur_b, halo_a, halo_b,
                  pl.BlockSpec((CONV_WIDTH, CONV_DIM), lambda i, r: (0, 0)), vec, vec, vec],
        out_specs=pl.BlockSpec((1, ts, CONV_DIM), lambda i, r: (i, r, 0)),
        out_shape=jax.ShapeDtypeStruct((b, s, CONV_DIM), BF),
        scratch_shapes=[pltpu.VMEM((CONV_HALO + ts, CONV_DIM), F32),
                        pltpu.VMEM((SUBLANES - 1, ts + CONV_HALO - SUBLANES, CONV_LANES), F32),
                        pltpu.VMEM((ts, CONV_DIM), F32)],
        compiler_params=_cparams("parallel", "arbitrary"),
    )(proj, proj, proj, proj, w_dw, b_dw, ln_g, ln_b)


def _compress_kernel(k_ref, pe_ref, w1_ref, w2_ref, o_ref):
    k16 = k_ref[0]
    half = CMP_STRIDE * HEAD_DIM
    lo = _dot(k16, w1_ref[0:half, :])
    hi = _dot(k16, w1_ref[half:2 * half, :])
    pe = _dot(pe_ref[...], w1_ref[...])
    rows = k16.shape[0]
    hid = lo + pltpu.roll(hi, rows - 1, axis=0) + pe[0:1, :]
    act = 0.5 * hid * (1.0 + jnp.tanh(math.sqrt(2.0 / math.pi) * (hid + 0.044715 * hid * hid * hid)))
    o_ref[0] = _dot(act.astype(BF), w2_ref[...])


def _compress(k16, pe8, w1, w2):
    n, rows, width = k16.shape
    return pl.pallas_call(
        _compress_kernel,
        grid=(n,),
        in_specs=[pl.BlockSpec((1, rows, width), lambda i: (i, 0, 0)),
                  pl.BlockSpec(pe8.shape, lambda i: (0, 0)),
                  pl.BlockSpec(w1.shape, lambda i: (0, 0)),
                  pl.BlockSpec(w2.shape, lambda i: (0, 0))],
        out_specs=pl.BlockSpec((1, rows, HEAD_DIM), lambda i: (i, 0, 0)),
        out_shape=jax.ShapeDtypeStruct((n, rows, HEAD_DIM), F32),
        compiler_params=_cparams("parallel"),
    )(k16, pe8, w1, w2)


def _bias_tables_kernel(rb_ref, wb_ref, sb_ref, cf_ref):
    h = pl.program_id(0)
    last = rb_ref[REL_BUCKETS - 1, h]

    def rel(dist):
        n = jnp.maximum(dist, 0)
        bucket = jnp.full(n.shape, REL_BUCKETS // 2, jnp.int32)
        for t in T5_THRESHOLDS:
            bucket = bucket + (n >= t).astype(jnp.int32)
        bucket = jnp.where(n < REL_BUCKETS // 2, n, jnp.minimum(bucket, REL_BUCKETS - 1))
        out = jnp.zeros(n.shape, F32)
        for bk in range(REL_BUCKETS - 1):
            out = jnp.where(bucket == bk, rb_ref[bk, h] - last, out)
        return out * LOG2E

    def dist_of(shape, offset):
        qi = lax.broadcasted_iota(jnp.int32, shape, 0)
        kl = lax.broadcasted_iota(jnp.int32, shape, 1)
        return qi + offset - kl

    dw = dist_of((Q_TILE, WIN_KEYS), WIN_PAD)
    wb_ref[0] = jnp.where((dw >= 0) & (dw < WINDOW), rel(dw), NEG)

    ds_ = dist_of((Q_TILE, SLC_LOCAL_KEYS), SLC_PAD)
    sb_ref[0] = jnp.where(ds_ >= 0, rel(ds_), NEG)

    qi = lax.broadcasted_iota(jnp.int32, (Q_TILE, LANES), 0)
    col = lax.broadcasted_iota(jnp.int32, (Q_TILE, LANES), 1)
    mi = col & (CMP_BAND - 1)
    part = col >> (CMP_BAND.bit_length() - 1)
    dc = qi - CMP_STRIDE * (mi - CMP_BAND_LO) - (CMP_LEN - 1)
    val = jnp.where(dc >= 0, rel(dc), 0.0)
    p0 = val.astype(BF).astype(F32)
    r1 = val - p0
    p1 = r1.astype(BF).astype(F32)
    p2 = r1 - p1
    parts = jnp.where(part == 0, p0, jnp.where(part == 1, p1, jnp.where(part == 2, p2, 0.0)))
    cf_ref[0] = parts.astype(BF)


def _bias_tables(rel_bias):
    smem = pl.BlockSpec(memory_space=pltpu.SMEM)
    lk = SLC_LOCAL_KEYS
    return pl.pallas_call(
        _bias_tables_kernel,
        grid=(N_HEADS,),
        in_specs=[smem],
        out_specs=[pl.BlockSpec((1, Q_TILE, WIN_KEYS), lambda h: (h, 0, 0)),
                   pl.BlockSpec((1, Q_TILE, lk), lambda h: (h, 0, 0)),
                   pl.BlockSpec((1, Q_TILE, LANES), lambda h: (h, 0, 0))],
        out_shape=[jax.ShapeDtypeStruct((N_HEADS, Q_TILE, WIN_KEYS), F32),
                   jax.ShapeDtypeStruct((N_HEADS, Q_TILE, lk), F32),
                   jax.ShapeDtypeStruct((N_HEADS, Q_TILE, LANES), BF)],
        compiler_params=_cparams("arbitrary"),
    )(rel_bias)


def _softmax_parts(s):
    m = jnp.max(s, axis=-1, keepdims=True)
    e = jnp.exp2(s - m)
    return m, e, jnp.sum(e, axis=-1, keepdims=True)


def _lane_tiles(v, op):
    out = v[:, 0:LANES]
    for t in range(1, v.shape[1] // LANES):
        out = op(out, v[:, t * LANES:(t + 1) * LANES])
    return out


def _attn_kernel(*refs, seq, cast_cols):
    n_cast = len(cast_cols)
    (q_ref, ks_ref, vs_ref, kw_ref, vw_ref, kc_ref, vc_ref, ng_ref,
     wb_ref, sb_ref, cf_ref, ovt_ref, eye_ref, spread_ref) = refs[:N_ATTN_IN]
    cast_in = refs[N_ATTN_IN:N_ATTN_IN + n_cast]
    o_ref = refs[N_ATTN_IN + n_cast]
    cast_out = refs[N_ATTN_IN + n_cast + 1:N_ATTN_IN + 2 * n_cast + 1]
    kaug_ref, vsp_ref, kwp_ref, vwp_ref, key_ref, sfar_ref = refs[N_ATTN_IN + 2 * n_cast + 1:]

    for w_ref, wo_ref, c0 in zip(cast_in, cast_out, cast_cols):
        wo_ref[...] = w_ref[...][:, c0:].astype(BF)

    tile = pl.program_id(2)
    n_slc = seq // SLC_BLOCK
    ncp = seq // CMP_STRIDE
    rows = HPG * Q_TILE
    n_sel = min(N_SEL, n_slc)
    if n_slc != PAD_FLAG_COL or Q_TILE % LANES or 3 * CMP_BAND > LANES:
        raise NotImplementedError("selection layout expects 64 selection blocks and whole-lane query tiles")
    assert CMP_STRIDE * (CMP_BAND_LO + 1) - (CMP_LEN - 1) >= T5_THRESHOLDS[-1]
    assert CMP_STRIDE * (CMP_BAND - CMP_BAND_LO) + CMP_LEN - 1 >= Q_TILE

    @pl.when(tile == 0)
    def _stage_keys():
        col = lax.broadcasted_iota(jnp.int32, (SLC_PAD, LANES), 1)
        kaug_ref[0:SLC_PAD, 0:HEAD_DIM] = jnp.zeros((SLC_PAD, HEAD_DIM), BF)
        kaug_ref[0:SLC_PAD, HEAD_DIM:] = jnp.where(col == PAD_FLAG_COL, NEG, 0.0).astype(BF)
        kaug_ref[SLC_PAD:, 0:HEAD_DIM] = ks_ref[0]
        blk = lax.broadcasted_iota(jnp.int32, (seq, LANES), 0) >> (SLC_BLOCK.bit_length() - 1)
        colk = lax.broadcasted_iota(jnp.int32, (seq, LANES), 1)
        kaug_ref[SLC_PAD:, HEAD_DIM:] = jnp.where(colk == blk, NEG, 0.0).astype(BF)
        vsp_ref[:, HEAD_DIM:] = jnp.ones((SLC_PAD + seq, LANES), BF)
        vsp_ref[0:SLC_PAD, 0:HEAD_DIM] = jnp.zeros((SLC_PAD, HEAD_DIM), BF)
        vsp_ref[SLC_PAD:, 0:HEAD_DIM] = vs_ref[0]
        kwp_ref[0:WIN_PAD, :] = jnp.zeros((WIN_PAD, HEAD_DIM), BF)
        kwp_ref[WIN_PAD:, :] = kw_ref[0]
        vwp_ref[:, HEAD_DIM:] = jnp.ones((WIN_PAD + seq, LANES), BF)
        vwp_ref[0:WIN_PAD, 0:HEAD_DIM] = jnp.zeros((WIN_PAD, HEAD_DIM), BF)
        vwp_ref[WIN_PAD:, 0:HEAD_DIM] = vw_ref[0]

    q4 = jnp.concatenate([q_ref[0, :, hh * HEAD_DIM:(hh + 1) * HEAD_DIM] for hh in range(HPG)], axis=0)
    t0 = tile * Q_TILE
    r0 = pl.multiple_of(t0, Q_TILE)

    gts = jax.nn.sigmoid(ng_ref[0].astype(F32))
    g_hi = gts.astype(BF)
    g_lo = (gts - g_hi.astype(F32)).astype(BF)
    spread = spread_ref[...]
    gsp = _dot(g_hi, spread) + _dot(g_lo, spread)

    key_w = lax.broadcasted_iota(jnp.int32, (1, WIN_KEYS), 1)
    before_start = jnp.where(key_w < WIN_PAD - t0, NEG, 0.0)
    s_w = _dot_nt(q4, kwp_ref[pl.ds(r0, WIN_KEYS), :]) + wb_ref[...].reshape(rows, WIN_KEYS) + before_start
    e_w = jnp.exp2(s_w - jnp.max(s_w, axis=-1, keepdims=True))
    pv_w = _dot(e_w.astype(BF), vwp_ref[pl.ds(r0, WIN_KEYS), :])
    o_win = pv_w[:, 0:HEAD_DIM] * (1.0 / pv_w[:, HEAD_DIM:])

    kc = kc_ref[0].astype(BF)
    vc = vc_ref[0].astype(BF)
    r_io = lax.broadcasted_iota(jnp.int32, (LANES, ncp), 0)
    n_io = lax.broadcasted_iota(jnp.int32, (LANES, ncp), 1)
    band = (r_io < 3 * CMP_BAND) & (n_io == CMP_PER_TILE * tile + (r_io & (CMP_BAND - 1)) - CMP_BAND_LO)
    shift = jnp.where(band, 1.0, 0.0).astype(BF)
    cf4 = cf_ref[...].reshape(rows, LANES)
    lc = _dot_nt(q4, kc) + _dot(cf4, shift)
    qi_c = lax.broadcasted_iota(jnp.int32, (Q_TILE, ncp), 0)
    n_c = lax.broadcasted_iota(jnp.int32, (Q_TILE, ncp), 1)
    vis = (CMP_STRIDE * n_c + (CMP_LEN - 1)) <= (t0 + qi_c)
    lc3 = jnp.where(vis[None], lc.reshape(HPG, Q_TILE, ncp), NEG)
    _, e_c, l_c = _softmax_parts(lc3)
    p_c = jnp.where(vis[None], e_c * (1.0 / l_c), 0.0)
    o_cmp = _dot(p_c.reshape(rows, ncp).astype(BF), vc)

    psum = p_c[0]
    for hh in range(1, HPG):
        psum = psum + p_c[hh]
    p0 = psum.astype(BF)
    r1 = psum - p0.astype(F32)
    p1 = r1.astype(BF)
    p2 = (r1 - p1.astype(F32)).astype(BF)
    ovt = ovt_ref[...]
    imp_t = _dot_nt(ovt, p0) + _dot_nt(ovt, p1) + _dot_nt(ovt, p2)

    def gate(hh, branch):
        c = 3 * hh + branch
        return gsp[:, c * LANES:(c + 1) * LANES]

    gated_cw = [gate(hh, 0) * o_cmp[hh * Q_TILE:(hh + 1) * Q_TILE] + gate(hh, 2) * o_win[hh * Q_TILE:(hh + 1) * Q_TILE]
                for hh in range(HPG)]

    blk_t = lax.broadcasted_iota(jnp.int32, (n_slc, Q_TILE), 0)
    q_t = lax.broadcasted_iota(jnp.int32, (n_slc, Q_TILE), 1)
    cur_t = (t0 + q_t) >> (SLC_BLOCK.bit_length() - 1)
    forced_t = (blk_t == 0) | (blk_t == cur_t) | (blk_t == cur_t - 1)
    causal_t = blk_t <= cur_t
    key = jnp.where(forced_t, KEY_FORCED, jnp.where(causal_t, lax.bitcast_convert_type(imp_t, jnp.int32), KEY_FUTURE))
    key_ref[...] = key
    groups = [key[8 * a:8 * a + 8, :] for a in range(n_slc // 8)]
    ranks = [jnp.zeros((8, Q_TILE), jnp.int32) for _ in groups]
    sub = lax.broadcasted_iota(jnp.int32, (8, Q_TILE), 0)
    for bb in range(n_slc):
        kb = key_ref[bb:bb + 1, :]
        kb1 = kb + 1
        for a in range(n_slc // 8):
            if 8 * a > bb:
                ahead = kb1
            elif 8 * a + 7 < bb:
                ahead = kb
            else:
                ahead = jnp.where(sub > bb - 8 * a, kb1, kb)
            ranks[a] = ranks[a] + (ahead > groups[a]).astype(jnp.int32)
    rank = jnp.concatenate(ranks, axis=0)
    sel_t = (rank < n_sel) & causal_t
    first_local = (t0 - SLC_PAD) >> (SLC_BLOCK.bit_length() - 1)
    far_t = blk_t < first_local
    flag_rows = jnp.where(blk_t == 0, 1.0, 0.0)
    stack = jnp.concatenate([jnp.where(sel_t, 0.0, 1.0), flag_rows,
                             jnp.where(sel_t & far_t, 0.0, 1.0), flag_rows], axis=0).astype(BF)
    qmask = _dot_nt(eye_ref[...], stack).astype(BF)
    qa_loc = jnp.concatenate([q4, qmask[:, 0:LANES]], axis=1)
    qa_far = jnp.concatenate([q4, qmask[:, LANES:]], axis=1)

    lk = SLC_LOCAL_KEYS
    chunk = SLC_CHUNK_BLOCKS * SLC_BLOCK
    n_far = (first_local + SLC_CHUNK_BLOCKS - 1) // SLC_CHUNK_BLOCKS

    s_loc = _dot_nt(qa_loc, kaug_ref[pl.ds(r0, lk), :]) + sb_ref[...].reshape(rows, lk)

    def far_logits(c, m_part):
        rr = pl.multiple_of(SLC_PAD + c * chunk, SLC_BLOCK)
        s = _dot_nt(qa_far, kaug_ref[pl.ds(rr, chunk), :])
        sfar_ref[c] = s
        return jnp.maximum(m_part, _lane_tiles(s, jnp.maximum))

    m_part = lax.fori_loop(0, n_far, far_logits, _lane_tiles(s_loc, jnp.maximum))
    m_s = jnp.max(m_part, axis=-1, keepdims=True)

    e_loc = jnp.exp2(s_loc - m_s)
    acc_s = _dot(e_loc.astype(BF), vsp_ref[pl.ds(r0, lk), :])

    def far_values(c, acc):
        rr = pl.multiple_of(SLC_PAD + c * chunk, SLC_BLOCK)
        e = jnp.exp2(sfar_ref[c] - m_s)
        return acc + _dot(e.astype(BF), vsp_ref[pl.ds(rr, chunk), :])

    acc_s = lax.fori_loop(0, n_far, far_values, acc_s)
    o_slc = acc_s[:, 0:HEAD_DIM] * (1.0 / acc_s[:, HEAD_DIM:])

    for hh in range(HPG):
        o_h = gated_cw[hh] + gate(hh, 1) * o_slc[hh * Q_TILE:(hh + 1) * Q_TILE]
        o_ref[0, :, hh * HEAD_DIM:(hh + 1) * HEAD_DIM] = o_h.astype(o_ref.dtype)


N_ATTN_IN = 14


def _nsa_attention(proj, k_cmp, v_cmp, wb, sb, cf, to_cast):
    b, s, _ = proj.shape
    n_steps = b * N_KV * (s // Q_TILE)
    cast_in_specs, cast_out_specs, cast_shapes = [], [], []
    for w, c0 in to_cast:
        slab = next(r for r in range(BF16_SUBLANES, w.shape[0] + 1, BF16_SUBLANES)
                    if w.shape[0] % r == 0 and w.shape[0] // r <= n_steps)
        last = w.shape[0] // slab - 1

        def slab_index(i, g, j, last=last):
            return jnp.minimum((i * N_KV + g) * (s // Q_TILE) + j, last), 0

        cast_in_specs.append(pl.BlockSpec((slab, w.shape[1]), slab_index))
        cast_out_specs.append(pl.BlockSpec((slab, w.shape[1] - c0), slab_index))
        cast_shapes.append(jax.ShapeDtypeStruct((w.shape[0], w.shape[1] - c0), BF))
    once = pl.Buffered(1)
    ovt = _overlap_matrix(s)
    eye = jnp.asarray(np.tile(np.eye(Q_TILE), (HPG, 1)), BF)
    spread = jnp.asarray(np.repeat(np.eye(LANES)[:, :3 * HPG], LANES, axis=1), BF)
    n_chunks = -(-(s // SLC_BLOCK) // SLC_CHUNK_BLOCKS)
    q_blk = Q_OFF // (HPG * HEAD_DIM)
    kv_blk = KC_OFF // HEAD_DIM
    ng_blk = NG_OFF // LANES
    lk = SLC_LOCAL_KEYS
    ncp = s // CMP_STRIDE

    def kv_spec(idx):
        return pl.BlockSpec((1, s, HEAD_DIM), lambda i, g, j: (i, 0, kv_blk + idx * N_KV + g), pipeline_mode=once)

    cmp_spec = pl.BlockSpec((1, ncp, HEAD_DIM), lambda i, g, j: (i * N_KV + g, 0, 0), pipeline_mode=once)
    outs = pl.pallas_call(
        functools.partial(_attn_kernel, seq=s, cast_cols=tuple(c0 for _, c0 in to_cast)),
        grid=(b, N_KV, s // Q_TILE),
        in_specs=[pl.BlockSpec((1, Q_TILE, HPG * HEAD_DIM), lambda i, g, j: (i, j, q_blk + g)),
                  kv_spec(2), kv_spec(3), kv_spec(4), kv_spec(5),
                  cmp_spec, cmp_spec,
                  pl.BlockSpec((1, Q_TILE, LANES), lambda i, g, j: (i, j, ng_blk + g)),
                  pl.BlockSpec((HPG, Q_TILE, WIN_KEYS), lambda i, g, j: (g, 0, 0), pipeline_mode=once),
                  pl.BlockSpec((HPG, Q_TILE, lk), lambda i, g, j: (g, 0, 0), pipeline_mode=once),
                  pl.BlockSpec((HPG, Q_TILE, LANES), lambda i, g, j: (g, 0, 0), pipeline_mode=once),
                  pl.BlockSpec(ovt.shape, lambda i, g, j: (0, 0), pipeline_mode=once),
                  pl.BlockSpec(eye.shape, lambda i, g, j: (0, 0), pipeline_mode=once),
                  pl.BlockSpec(spread.shape, lambda i, g, j: (0, 0), pipeline_mode=once)] + cast_in_specs,
        out_specs=[pl.BlockSpec((1, Q_TILE, HPG * HEAD_DIM), lambda i, g, j: (i, j, g))] + cast_out_specs,
        out_shape=[jax.ShapeDtypeStruct((b, s, Q_WIDTH), BF)] + cast_shapes,
        scratch_shapes=[pltpu.VMEM((SLC_PAD + s, 2 * HEAD_DIM), BF),
                        pltpu.VMEM((SLC_PAD + s, HEAD_DIM + LANES), BF),
                        pltpu.VMEM((WIN_PAD + s, HEAD_DIM), BF),
                        pltpu.VMEM((WIN_PAD + s, HEAD_DIM + LANES), BF),
                        pltpu.VMEM((s // SLC_BLOCK, Q_TILE), jnp.int32),
                        pltpu.VMEM((n_chunks, HPG * Q_TILE, SLC_CHUNK_BLOCKS * SLC_BLOCK), F32)],
        compiler_params=pltpu.CompilerParams(dimension_semantics=("arbitrary",) * 3,
                                             vmem_limit_bytes=ATTN_VMEM_LIMIT),
    )(proj, proj, proj, proj, proj, k_cmp, v_cmp, proj, wb, sb, cf, ovt, eye, spread, *[w for w, _ in to_cast])
    return outs[0], outs[1:]


def _overlap_matrix(s):
    ncp = s // CMP_STRIDE
    n_slc = s // SLC_BLOCK
    nc = (s - CMP_LEN) // CMP_STRIDE + 1
    i = np.arange(ncp)[None, :]
    jj = np.arange(n_slc)[:, None]
    ov = ((i * CMP_STRIDE < (jj + 1) * SLC_BLOCK) & (i * CMP_STRIDE + CMP_LEN > jj * SLC_BLOCK) & (i < nc))
    return jnp.asarray(ov, BF)


def kernel(x, c, w_ada, b_ada, g_pre_mix, g_post_mix, g_pre_ffn, g_post_ffn, w_in, w_dw, b_dw, conv_ln_g, conv_ln_b, w_conv_out, cmp_pe_k, cmp_pe_v, w_cmp_k1, w_cmp_k2, w_cmp_v1, w_cmp_v2, rel_bias, w_nsa_out, w_out, w_ffn_gate, w_ffn_up, w_ffn_down):
    b, s, d = x.shape
    m = b * s
    depth = w_ada.shape[0]
    c8 = jnp.zeros((8, d), F32).at[:b].set(c)
    wb, sb, cf = _bias_tables(rel_bias)
    col_scale = jnp.ones((1, NG_OFF + N_KV * LANES), F32).at[:, Q_OFF:KC_OFF].set(HEAD_DIM ** -0.5 * LOG2E)

    for l in range(depth):
        mod = _modulation(c8, w_ada[l], b_ada[l][None, :])
        mod3 = mod[:b].reshape(b, 6, d)

        w_main = w_in[l][:, :NG_OFF].astype(BF)
        w_ng = w_in[l][:, NG_OFF:MG_OFF].reshape(d, N_KV, 3 * HPG)
        w_ng = jnp.pad(w_ng, ((0, 0), (0, 0), (0, LANES - 3 * HPG))).reshape(d, N_KV * LANES)
        w_ng = w_ng.astype(BF)

        h = _prenorm(x, g_pre_mix[l][None, :], mod3, 0, 1).reshape(m, d)
        proj = _matmul_with_tail(h, w_main, NG_OFF, w_ng, col_scale, BF, 1024, 512).reshape(b, s, -1)

        u_conv = _conv_module(proj, w_dw[l], b_dw[l][None, :], conv_ln_g[l][None, :], conv_ln_b[l][None, :])

        def grouped(off):
            t = proj[:, :, off:off + KV_WIDTH].reshape(b, s // CMP_STRIDE, CMP_STRIDE, N_KV, HEAD_DIM)
            return t.transpose(0, 3, 1, 2, 4).reshape(b * N_KV, s // CMP_STRIDE, CMP_STRIDE * HEAD_DIM)

        pe_k = jnp.broadcast_to(cmp_pe_k[l].reshape(1, -1), (8, CMP_LEN * HEAD_DIM)).astype(BF)
        pe_v = jnp.broadcast_to(cmp_pe_v[l].reshape(1, -1), (8, CMP_LEN * HEAD_DIM)).astype(BF)
        k_cmp = _compress(grouped(KC_OFF), pe_k, w_cmp_k1[l].astype(BF), w_cmp_k2[l].astype(BF))
        v_cmp = _compress(grouped(KC_OFF + KV_WIDTH), pe_v, w_cmp_v1[l].astype(BF), w_cmp_v2[l].astype(BF))

        later = [(w_in[l], MG_OFF), (w_conv_out[l], 0), (w_nsa_out[l], 0), (w_out[l], 0),
                 (w_ffn_gate[l], 0), (w_ffn_up[l], 0), (w_ffn_down[l], 0)]
        o_nsa, (wb_mg, wb_conv, wb_nsa, wb_out, wb_gate, wb_up, wb_down) = _nsa_attention(
            proj, k_cmp, v_cmp, wb, sb, cf, later)
        gates = _matmul(h, wb_mg, BF, 1024, 1024)

        z = _gated_mix(u_conv.reshape(m, CONV_DIM), o_nsa.reshape(m, Q_WIDTH), wb_conv, wb_nsa, gates, d)
        mix = _matmul(z, wb_out, BF, 1024, 1024).reshape(b, s, d)
        x, h2 = _post_mix(mix, x, g_post_mix[l][None, :], g_pre_ffn[l][None, :], mod3, 2, 3, 4)

        ff = _ffn_up(h2.reshape(m, d), wb_gate, wb_up)
        f = _matmul(ff, wb_down, BF, 512, 512).reshape(b, s, d)
        x = _post_ffn(f, x, g_post_ffn[l][None, :], mod3, 5)
    return x
```

```python
import functools
import math

import numpy as np
import jax
import jax.numpy as jnp
from jax import lax
from jax.experimental import pallas as pl
from jax.experimental.pallas import tpu as pltpu

F32 = jnp.float32
BF = jnp.bfloat16

CONV_DIM = 2048
CONV_WIDTH = 31
N_HEADS = 16
N_KV = 4
HPG = N_HEADS // N_KV
HEAD_DIM = 128
CMP_LEN = 32
CMP_STRIDE = 16
CMP_HID = 512
SLC_BLOCK = 64
N_SEL = 16
WINDOW = 512
Q_BLOCK = 64
REL_BUCKETS = 32
REL_MAX_DIST = 128
NORM_EPS = 1e-6
NEG = -1e30
LOG2E = 1.4426950408889634

Q_WIDTH = N_HEADS * HEAD_DIM
KV_WIDTH = N_KV * HEAD_DIM
Q_OFF = 2 * CONV_DIM
KC_OFF = Q_OFF + Q_WIDTH
NG_OFF = KC_OFF + 6 * KV_WIDTH
MG_OFF = NG_OFF + 3 * N_HEADS

LANES = 128
SUBLANES = 8
BF16_SUBLANES = 16
VMEM_LIMIT = 56 * 1024 * 1024
ATTN_VMEM_LIMIT = 61 * 1024 * 1024

CONV_HALO = 32
CONV_ROWS = 64
CONV_LANES = 256

Q_TILE = 4 * Q_BLOCK
SLC_PAD = 2 * SLC_BLOCK
SLC_LOCAL_KEYS = SLC_PAD + Q_TILE
SLC_CHUNK_BLOCKS = 16
WIN_KEYS = WINDOW + Q_TILE
WIN_PAD = WINDOW
CMP_PER_TILE = Q_TILE // CMP_STRIDE
CMP_BAND = 32
CMP_BAND_LO = 9
PAD_FLAG_COL = 64
KEY_FORCED = 0x7F000000
KEY_FUTURE = -2


def _t5_thresholds():
    max_exact = REL_BUCKETS // 2
    thr = []
    for k in range(1, REL_BUCKETS - max_exact):
        n = max_exact
        while max_exact + int(math.log(n / max_exact) / math.log(REL_MAX_DIST / max_exact)
                              * (REL_BUCKETS - max_exact)) < max_exact + k:
            n += 1
        thr.append(n)
    return tuple(thr)


T5_THRESHOLDS = _t5_thresholds()


def _cparams(*sem):
    return pltpu.CompilerParams(dimension_semantics=sem, vmem_limit_bytes=VMEM_LIMIT)


def _silu(v):
    return v * jax.nn.sigmoid(v)


def _dot(a, b):
    return jnp.dot(a, b, preferred_element_type=F32)


def _dot_nt(a, b):
    return lax.dot_general(a, b, (((1,), (1,)), ((), ())), preferred_element_type=F32)


def _rms(v, g):
    return v * lax.rsqrt(jnp.mean(v * v, axis=-1, keepdims=True) + NORM_EPS) * g


def _mod_kernel(c_ref, w_ref, b_ref, o_ref):
    a = _silu(c_ref[...]).astype(BF)
    o_ref[...] = _dot(a, w_ref[...].astype(BF)) + b_ref[...]


def _modulation(c8, w_ada, b_ada, tn=512):
    rows, d = c8.shape
    n = w_ada.shape[1]
    return pl.pallas_call(
        _mod_kernel,
        grid=(n // tn,),
        in_specs=[pl.BlockSpec((rows, d), lambda j: (0, 0)),
                  pl.BlockSpec((d, tn), lambda j: (0, j)),
                  pl.BlockSpec((1, tn), lambda j: (0, j))],
        out_specs=pl.BlockSpec((rows, tn), lambda j: (0, j)),
        out_shape=jax.ShapeDtypeStruct((rows, n), F32),
        compiler_params=_cparams("parallel"),
    )(c8, w_ada, b_ada)


def _prenorm_kernel(x_ref, g_ref, mod_ref, h_ref, *, shift_row, scale_row):
    y = _rms(x_ref[0], g_ref[...])
    h = y * (1.0 + mod_ref[0, scale_row:scale_row + 1, :]) + mod_ref[0, shift_row:shift_row + 1, :]
    h_ref[0] = h.astype(BF)


def _prenorm(x, g, mod3, shift_row, scale_row, tr=512):
    b, s, d = x.shape
    return pl.pallas_call(
        functools.partial(_prenorm_kernel, shift_row=shift_row, scale_row=scale_row),
        grid=(b, s // tr),
        in_specs=[pl.BlockSpec((1, tr, d), lambda i, r: (i, r, 0)),
                  pl.BlockSpec((1, d), lambda i, r: (0, 0)),
                  pl.BlockSpec((1, 6, d), lambda i, r: (i, 0, 0))],
        out_specs=pl.BlockSpec((1, tr, d), lambda i, r: (i, r, 0)),
        out_shape=jax.ShapeDtypeStruct((b, s, d), BF),
        compiler_params=_cparams("parallel", "parallel"),
    )(x, g, mod3)


def _post_mix_kernel(y_ref, x_ref, gp_ref, gn_ref, mod_ref, xo_ref, h_ref, *, gate_row, shift_row, scale_row):
    x1 = x_ref[0] + mod_ref[0, gate_row:gate_row + 1, :] * _rms(y_ref[0].astype(F32), gp_ref[...])
    xo_ref[0] = x1
    h = _rms(x1, gn_ref[...]) * (1.0 + mod_ref[0, scale_row:scale_row + 1, :]) \
        + mod_ref[0, shift_row:shift_row + 1, :]
    h_ref[0] = h.astype(BF)


def _post_mix(y, x, g_post, g_next, mod3, gate_row, shift_row, scale_row, tr=256):
    b, s, d = x.shape
    row = pl.BlockSpec((1, tr, d), lambda i, r: (i, r, 0))
    vec = pl.BlockSpec((1, d), lambda i, r: (0, 0))
    return pl.pallas_call(
        functools.partial(_post_mix_kernel, gate_row=gate_row, shift_row=shift_row, scale_row=scale_row),
        grid=(b, s // tr),
        in_specs=[row, row, vec, vec, pl.BlockSpec((1, 6, d), lambda i, r: (i, 0, 0))],
        out_specs=[row, row],
        out_shape=[jax.ShapeDtypeStruct((b, s, d), F32), jax.ShapeDtypeStruct((b, s, d), BF)],
        compiler_params=_cparams("parallel", "parallel"),
    )(y, x, g_post, g_next, mod3)


def _post_ffn_kernel(y_ref, x_ref, gp_ref, mod_ref, xo_ref, *, gate_row):
    xo_ref[0] = x_ref[0] + mod_ref[0, gate_row:gate_row + 1, :] * _rms(y_ref[0].astype(F32), gp_ref[...])


def _post_ffn(y, x, g_post, mod3, gate_row, tr=256):
    b, s, d = x.shape
    row = pl.BlockSpec((1, tr, d), lambda i, r: (i, r, 0))
    return pl.pallas_call(
        functools.partial(_post_ffn_kernel, gate_row=gate_row),
        grid=(b, s // tr),
        in_specs=[row, row, pl.BlockSpec((1, d), lambda i, r: (0, 0)),
                  pl.BlockSpec((1, 6, d), lambda i, r: (i, 0, 0))],
        out_specs=row,
        out_shape=jax.ShapeDtypeStruct((b, s, d), F32),
        compiler_params=_cparams("parallel", "parallel"),
    )(y, x, g_post, mod3)


def _mm_scale_kernel(a_ref, w_ref, s_ref, o_ref):
    o_ref[...] = (_dot(a_ref[...], w_ref[...]) * s_ref[...]).astype(o_ref.dtype)


def _mm_kernel(a_ref, w_ref, o_ref):
    o_ref[...] = _dot(a_ref[...], w_ref[...]).astype(o_ref.dtype)


def _matmul(a, w, out_dtype, tm, tn, col_scale=None):
    m, k = a.shape
    n = w.shape[1]
    in_specs = [pl.BlockSpec((tm, k), lambda i, j: (i, 0)),
                pl.BlockSpec((k, tn), lambda i, j: (0, j))]
    args = [a, w]
    body = _mm_kernel
    if col_scale is not None:
        in_specs.append(pl.BlockSpec((1, tn), lambda i, j: (0, j)))
        args.append(col_scale)
        body = _mm_scale_kernel
    return pl.pallas_call(
        body,
        grid=(m // tm, n // tn),
        in_specs=in_specs,
        out_specs=pl.BlockSpec((tm, tn), lambda i, j: (i, j)),
        out_shape=jax.ShapeDtypeStruct((m, n), out_dtype),
        compiler_params=_cparams("parallel", "arbitrary"),
    )(*args)


def _mm_tail_kernel(a_ref, w_ref, wt_ref, s_ref, o_ref, *, n_main):
    j = pl.program_id(1)

    @pl.when(j < n_main)
    def _main():
        o_ref[...] = (_dot(a_ref[...], w_ref[...]) * s_ref[...]).astype(o_ref.dtype)

    @pl.when(j >= n_main)
    def _tail():
        o_ref[...] = (_dot(a_ref[...], wt_ref[...]) * s_ref[...]).astype(o_ref.dtype)


def _matmul_with_tail(a, w, n_w, w_tail, col_scale, out_dtype, tm, tn):
    m, k = a.shape
    n_main = n_w // tn
    n_tail = w_tail.shape[1] // tn
    n = n_w + w_tail.shape[1]
    return pl.pallas_call(
        functools.partial(_mm_tail_kernel, n_main=n_main),
        grid=(m // tm, n_main + n_tail),
        in_specs=[pl.BlockSpec((tm, k), lambda i, j: (i, 0)),
                  pl.BlockSpec((k, tn), lambda i, j: (0, jnp.minimum(j, n_main - 1))),
                  pl.BlockSpec((k, tn), lambda i, j: (0, jnp.maximum(j - n_main, 0))),
                  pl.BlockSpec((1, tn), lambda i, j: (0, j))],
        out_specs=pl.BlockSpec((tm, tn), lambda i, j: (i, j)),
        out_shape=jax.ShapeDtypeStruct((m, n), out_dtype),
        compiler_params=_cparams("parallel", "arbitrary"),
    )(a, w, w_tail, col_scale)


def _mix_kernel(h_ref, a1_ref, a2_ref, w1_ref, w2_ref, wga_ref, wgb_ref, o_ref):
    h = h_ref[...]
    ya = _dot(a1_ref[...], w1_ref[...])
    yb = _dot(a2_ref[...], w2_ref[...])
    o = jax.nn.sigmoid(_dot(h, wga_ref[...])) * ya + jax.nn.sigmoid(_dot(h, wgb_ref[...])) * yb
    o_ref[...] = o.astype(o_ref.dtype)


def _gated_mix(h, u_conv, o_nsa, w_conv_out, w_nsa_out, w_gates, tm=1024, tn=256):
    m, k = h.shape
    ka = u_conv.shape[1]
    kb = o_nsa.shape[1]
    d_model = w_conv_out.shape[1]
    gb_block = d_model // tn
    return pl.pallas_call(
        _mix_kernel,
        grid=(m // tm, d_model // tn),
        in_specs=[pl.BlockSpec((tm, k), lambda i, j: (i, 0)),
                  pl.BlockSpec((tm, ka), lambda i, j: (i, 0)),
                  pl.BlockSpec((tm, kb), lambda i, j: (i, 0)),
                  pl.BlockSpec((ka, tn), lambda i, j: (0, j)),
                  pl.BlockSpec((kb, tn), lambda i, j: (0, j)),
                  pl.BlockSpec((k, tn), lambda i, j: (0, j)),
                  pl.BlockSpec((k, tn), lambda i, j: (0, j + gb_block))],
        out_specs=pl.BlockSpec((tm, tn), lambda i, j: (i, j)),
        out_shape=jax.ShapeDtypeStruct((m, d_model), BF),
        compiler_params=_cparams("parallel", "arbitrary"),
    )(h, u_conv, o_nsa, w_conv_out, w_nsa_out, w_gates, w_gates)


def _ffn_up_kernel(a_ref, wg_ref, wu_ref, o_ref):
    a = a_ref[...]
    o_ref[...] = (_silu(_dot(a, wg_ref[...])) * _dot(a, wu_ref[...])).astype(o_ref.dtype)


def _ffn_up(h, w_gate, w_up, tm=2048, tn=256):
    m, k = h.shape
    n = w_gate.shape[1]
    wspec = pl.BlockSpec((k, tn), lambda i, j: (0, j))
    return pl.pallas_call(
        _ffn_up_kernel,
        grid=(m // tm, n // tn),
        in_specs=[pl.BlockSpec((tm, k), lambda i, j: (i, 0)), wspec, wspec],
        out_specs=pl.BlockSpec((tm, tn), lambda i, j: (i, j)),
        out_shape=jax.ShapeDtypeStruct((m, n), BF),
        compiler_params=_cparams("parallel", "arbitrary"),
    )(h, w_gate, w_up)


def _conv_kernel(a_ref, b_ref, ah_ref, bh_ref, w_ref, bdw_ref, g_ref, bb_ref, o_ref, u_ref, sh_ref, v_ref, *, ts):
    i = pl.program_id(1)
    u_ref[CONV_HALO:CONV_HALO + ts, :] = a_ref[0].astype(F32) * jax.nn.sigmoid(b_ref[0].astype(F32))
    uh = ah_ref[0].astype(F32) * jax.nn.sigmoid(bh_ref[0].astype(F32))
    u_ref[0:CONV_HALO, :] = jnp.where(i > 0, uh, 0.0)

    rc, cc = CONV_ROWS, CONV_LANES
    first = CONV_HALO - (CONV_WIDTH - 1)
    span = ts + CONV_HALO - SUBLANES
    for c0 in range(0, CONV_DIM, cc):
        for s in range(1, SUBLANES):
            sh_ref[s - 1] = u_ref[s:s + span, c0:c0 + cc]

        def row_chunk(rb, carry, c0=c0):
            r0 = pl.multiple_of(rb * rc, rc)
            acc = jnp.zeros((rc, cc), F32)
            for k in range(CONV_WIDTH):
                s = (first + k) % SUBLANES
                base = first + k - s
                if s == 0:
                    tap = u_ref[pl.ds(r0 + base, rc), c0:c0 + cc]
                else:
                    tap = sh_ref[s - 1, pl.ds(r0 + base, rc), :]
                acc = acc + tap * w_ref[k:k + 1, c0:c0 + cc]
            v_ref[pl.ds(r0, rc), c0:c0 + cc] = acc + bdw_ref[:, c0:c0 + cc]
            return carry

        lax.fori_loop(0, ts // rc, row_chunk, 0)

    v = v_ref[...]
    mu = jnp.mean(v, axis=-1, keepdims=True)
    vc = v - mu
    y = vc * lax.rsqrt(jnp.mean(vc * vc, axis=-1, keepdims=True) + NORM_EPS) * g_ref[...] + bb_ref[...]
    o_ref[0] = _silu(y).astype(o_ref.dtype)


def _conv_module(proj, w_dw, b_dw, ln_g, ln_b, ts=256):
    b, s, _ = proj.shape
    hb = ts // CONV_HALO
    cur_a = pl.BlockSpec((1, ts, CONV_DIM), lambda i, r: (i, r, 0))
    cur_b = pl.BlockSpec((1, ts, CONV_DIM), lambda i, r: (i, r, 1))
    halo_a = pl.BlockSpec((1, CONV_HALO, CONV_DIM), lambda i, r: (i, jnp.maximum(r * hb - 1, 0), 0))
    halo_b = pl.BlockSpec((1, CONV_HALO, CONV_DIM), lambda i, r: (i, jnp.maximum(r * hb - 1, 0), 1))
    vec = pl.BlockSpec((1, CONV_DIM), lambda i, r: (0, 0))
    return pl.pallas_call(
        functools.partial(_conv_kernel, ts=ts),
        grid=(b, s // ts),
        in_specs=[cur_a, cur_b, halo_a, halo_b,
                  pl.BlockSpec((CONV_WIDTH, CONV_DIM), lambda i, r: (0, 0)), vec, vec, vec],
        out_specs=pl.BlockSpec((1, ts, CONV_DIM), lambda i, r: (i, r, 0)),
        out_shape=jax.ShapeDtypeStruct((b, s, CONV_DIM), BF),
        scratch_shapes=[pltpu.VMEM((CONV_HALO + ts, CONV_DIM), F32),
                        pltpu.VMEM((SUBLANES - 1, ts + CONV_HALO - SUBLANES, CONV_LANES), F32),
                        pltpu.VMEM((ts, CONV_DIM), F32)],
        compiler_params=_cparams("parallel", "arbitrary"),
    )(proj, proj, proj, proj, w_dw, b_dw, ln_g, ln_b)


def _compress_kernel(k_ref, pe_ref, w1_ref, w2_ref, o_ref):
    k16 = k_ref[0]
    half = CMP_STRIDE * HEAD_DIM
    lo = _dot(k16, w1_ref[0:half, :])
    hi = _dot(k16, w1_ref[half:2 * half, :])
    pe = _dot(pe_ref[...], w1_ref[...])
    rows = k16.shape[0]
    hid = lo + pltpu.roll(hi, rows - 1, axis=0) + pe[0:1, :]
    act = 0.5 * hid * (1.0 + jnp.tanh(math.sqrt(2.0 / math.pi) * (hid + 0.044715 * hid * hid * hid)))
    o_ref[0] = _dot(act.astype(BF), w2_ref[...])


def _compress(k16, pe8, w1, w2):
    n, rows, width = k16.shape
    return pl.pallas_call(
        _compress_kernel,
        grid=(n,),
        in_specs=[pl.BlockSpec((1, rows, width), lambda i: (i, 0, 0)),
                  pl.BlockSpec(pe8.shape, lambda i: (0, 0)),
                  pl.BlockSpec(w1.shape, lambda i: (0, 0)),
                  pl.BlockSpec(w2.shape, lambda i: (0, 0))],
        out_specs=pl.BlockSpec((1, rows, HEAD_DIM), lambda i: (i, 0, 0)),
        out_shape=jax.ShapeDtypeStruct((n, rows, HEAD_DIM), F32),
        compiler_params=_cparams("parallel"),
    )(k16, pe8, w1, w2)


def _bias_tables_kernel(rb_ref, wb_ref, sb_ref, cf_ref):
    h = pl.program_id(0)
    last = rb_ref[REL_BUCKETS - 1, h]

    def rel(dist):
        n = jnp.maximum(dist, 0)
        bucket = jnp.full(n.shape, REL_BUCKETS // 2, jnp.int32)
        for t in T5_THRESHOLDS:
            bucket = bucket + (n >= t).astype(jnp.int32)
        bucket = jnp.where(n < REL_BUCKETS // 2, n, jnp.minimum(bucket, REL_BUCKETS - 1))
        out = jnp.zeros(n.shape, F32)
        for bk in range(REL_BUCKETS - 1):
            out = jnp.where(bucket == bk, rb_ref[bk, h] - last, out)
        return out * LOG2E

    def dist_of(shape, offset):
        qi = lax.broadcasted_iota(jnp.int32, shape, 0)
        kl = lax.broadcasted_iota(jnp.int32, shape, 1)
        return qi + offset - kl

    dw = dist_of((Q_TILE, WIN_KEYS), WIN_PAD)
    wb_ref[0] = jnp.where((dw >= 0) & (dw < WINDOW), rel(dw), NEG)

    ds_ = dist_of((Q_TILE, SLC_LOCAL_KEYS), SLC_PAD)
    sb_ref[0] = jnp.where(ds_ >= 0, rel(ds_), NEG)

    qi = lax.broadcasted_iota(jnp.int32, (Q_TILE, LANES), 0)
    col = lax.broadcasted_iota(jnp.int32, (Q_TILE, LANES), 1)
    mi = col & (CMP_BAND - 1)
    part = col >> (CMP_BAND.bit_length() - 1)
    dc = qi - CMP_STRIDE * (mi - CMP_BAND_LO) - (CMP_LEN - 1)
    val = jnp.where(dc >= 0, rel(dc), 0.0)
    p0 = val.astype(BF).astype(F32)
    r1 = val - p0
    p1 = r1.astype(BF).astype(F32)
    p2 = r1 - p1
    parts = jnp.where(part == 0, p0, jnp.where(part == 1, p1, jnp.where(part == 2, p2, 0.0)))
    cf_ref[0] = parts.astype(BF)


def _bias_tables(rel_bias):
    smem = pl.BlockSpec(memory_space=pltpu.SMEM)
    lk = SLC_LOCAL_KEYS
    return pl.pallas_call(
        _bias_tables_kernel,
        grid=(N_HEADS,),
        in_specs=[smem],
        out_specs=[pl.BlockSpec((1, Q_TILE, WIN_KEYS), lambda h: (h, 0, 0)),
                   pl.BlockSpec((1, Q_TILE, lk), lambda h: (h, 0, 0)),
                   pl.BlockSpec((1, Q_TILE, LANES), lambda h: (h, 0, 0))],
        out_shape=[jax.ShapeDtypeStruct((N_HEADS, Q_TILE, WIN_KEYS), F32),
                   jax.ShapeDtypeStruct((N_HEADS, Q_TILE, lk), F32),
                   jax.ShapeDtypeStruct((N_HEADS, Q_TILE, LANES), BF)],
        compiler_params=_cparams("arbitrary"),
    )(rel_bias)


def _softmax_parts(s):
    m = jnp.max(s, axis=-1, keepdims=True)
    e = jnp.exp2(s - m)
    return m, e, jnp.sum(e, axis=-1, keepdims=True)


def _lane_tiles(v, op):
    out = v[:, 0:LANES]
    for t in range(1, v.shape[1] // LANES):
        out = op(out, v[:, t * LANES:(t + 1) * LANES])
    return out


def _attn_kernel(*refs, seq, n_cast):
    (q_ref, ks_ref, vs_ref, kw_ref, vw_ref, kc_ref, vc_ref, ng_ref,
     wb_ref, sb_ref, cf_ref, ovt_ref, eye_ref, spread_ref) = refs[:N_ATTN_IN]
    cast_in = refs[N_ATTN_IN:N_ATTN_IN + n_cast]
    o_ref = refs[N_ATTN_IN + n_cast]
    cast_out = refs[N_ATTN_IN + n_cast + 1:N_ATTN_IN + 2 * n_cast + 1]
    kaug_ref, vsp_ref, kwp_ref, vwp_ref, key_ref, sfar_ref = refs[N_ATTN_IN + 2 * n_cast + 1:]

    for w_ref, wo_ref in zip(cast_in, cast_out):
        wo_ref[...] = w_ref[...].astype(BF)

    tile = pl.program_id(2)
    n_slc = seq // SLC_BLOCK
    ncp = seq // CMP_STRIDE
    rows = HPG * Q_TILE
    n_sel = min(N_SEL, n_slc)
    if n_slc != PAD_FLAG_COL or Q_TILE % LANES or 3 * CMP_BAND > LANES:
        raise NotImplementedError("selection layout expects 64 selection blocks and whole-lane query tiles")
    assert CMP_STRIDE * (CMP_BAND_LO + 1) - (CMP_LEN - 1) >= T5_THRESHOLDS[-1]
    assert CMP_STRIDE * (CMP_BAND - CMP_BAND_LO) + CMP_LEN - 1 >= Q_TILE

    @pl.when(tile == 0)
    def _stage_keys():
        col = lax.broadcasted_iota(jnp.int32, (SLC_PAD, LANES), 1)
        kaug_ref[0:SLC_PAD, 0:HEAD_DIM] = jnp.zeros((SLC_PAD, HEAD_DIM), BF)
        kaug_ref[0:SLC_PAD, HEAD_DIM:] = jnp.where(col == PAD_FLAG_COL, NEG, 0.0).astype(BF)
        kaug_ref[SLC_PAD:, 0:HEAD_DIM] = ks_ref[0]
        blk = lax.broadcasted_iota(jnp.int32, (seq, LANES), 0) >> (SLC_BLOCK.bit_length() - 1)
        colk = lax.broadcasted_iota(jnp.int32, (seq, LANES), 1)
        kaug_ref[SLC_PAD:, HEAD_DIM:] = jnp.where(colk == blk, NEG, 0.0).astype(BF)
        vsp_ref[:, HEAD_DIM:] = jnp.ones((SLC_PAD + seq, LANES), BF)
        vsp_ref[0:SLC_PAD, 0:HEAD_DIM] = jnp.zeros((SLC_PAD, HEAD_DIM), BF)
        vsp_ref[SLC_PAD:, 0:HEAD_DIM] = vs_ref[0]
        kwp_ref[0:WIN_PAD, :] = jnp.zeros((WIN_PAD, HEAD_DIM), BF)
        kwp_ref[WIN_PAD:, :] = kw_ref[0]
        vwp_ref[:, HEAD_DIM:] = jnp.ones((WIN_PAD + seq, LANES), BF)
        vwp_ref[0:WIN_PAD, 0:HEAD_DIM] = jnp.zeros((WIN_PAD, HEAD_DIM), BF)
        vwp_ref[WIN_PAD:, 0:HEAD_DIM] = vw_ref[0]

    q4 = jnp.concatenate([q_ref[0, :, hh * HEAD_DIM:(hh + 1) * HEAD_DIM] for hh in range(HPG)], axis=0)
    t0 = tile * Q_TILE
    r0 = pl.multiple_of(t0, Q_TILE)

    gts = jax.nn.sigmoid(ng_ref[0].astype(F32))
    g_hi = gts.astype(BF)
    g_lo = (gts - g_hi.astype(F32)).astype(BF)
    spread = spread_ref[...]
    gsp = _dot(g_hi, spread) + _dot(g_lo, spread)

    key_w = lax.broadcasted_iota(jnp.int32, (1, WIN_KEYS), 1)
    before_start = jnp.where(key_w < WIN_PAD - t0, NEG, 0.0)
    s_w = _dot_nt(q4, kwp_ref[pl.ds(r0, WIN_KEYS), :]) + wb_ref[...].reshape(rows, WIN_KEYS) + before_start
    e_w = jnp.exp2(s_w - jnp.max(s_w, axis=-1, keepdims=True))
    pv_w = _dot(e_w.astype(BF), vwp_ref[pl.ds(r0, WIN_KEYS), :])
    o_win = pv_w[:, 0:HEAD_DIM] * (1.0 / pv_w[:, HEAD_DIM:])

    kc = kc_ref[0].astype(BF)
    vc = vc_ref[0].astype(BF)
    r_io = lax.broadcasted_iota(jnp.int32, (LANES, ncp), 0)
    n_io = lax.broadcasted_iota(jnp.int32, (LANES, ncp), 1)
    band = (r_io < 3 * CMP_BAND) & (n_io == CMP_PER_TILE * tile + (r_io & (CMP_BAND - 1)) - CMP_BAND_LO)
    shift = jnp.where(band, 1.0, 0.0).astype(BF)
    cf4 = cf_ref[...].reshape(rows, LANES)
    lc = _dot_nt(q4, kc) + _dot(cf4, shift)
    qi_c = lax.broadcasted_iota(jnp.int32, (Q_TILE, ncp), 0)
    n_c = lax.broadcasted_iota(jnp.int32, (Q_TILE, ncp), 1)
    vis = (CMP_STRIDE * n_c + (CMP_LEN - 1)) <= (t0 + qi_c)
    lc3 = jnp.where(vis[None], lc.reshape(HPG, Q_TILE, ncp), NEG)
    _, e_c, l_c = _softmax_parts(lc3)
    p_c = jnp.where(vis[None], e_c * (1.0 / l_c), 0.0)
    o_cmp = _dot(p_c.reshape(rows, ncp).astype(BF), vc)

    psum = p_c[0]
    for hh in range(1, HPG):
        psum = psum + p_c[hh]
    p0 = psum.astype(BF)
    r1 = psum - p0.astype(F32)
    p1 = r1.astype(BF)
    p2 = (r1 - p1.astype(F32)).astype(BF)
    ovt = ovt_ref[...]
    imp_t = _dot_nt(ovt, p0) + _dot_nt(ovt, p1) + _dot_nt(ovt, p2)

    def gate(hh, branch):
        c = 3 * hh + branch
        return gsp[:, c * LANES:(c + 1) * LANES]

    gated_cw = [gate(hh, 0) * o_cmp[hh * Q_TILE:(hh + 1) * Q_TILE] + gate(hh, 2) * o_win[hh * Q_TILE:(hh + 1) * Q_TILE]
                for hh in range(HPG)]

    blk_t = lax.broadcasted_iota(jnp.int32, (n_slc, Q_TILE), 0)
    q_t = lax.broadcasted_iota(jnp.int32, (n_slc, Q_TILE), 1)
    cur_t = (t0 + q_t) >> (SLC_BLOCK.bit_length() - 1)
    forced_t = (blk_t == 0) | (blk_t == cur_t) | (blk_t == cur_t - 1)
    causal_t = blk_t <= cur_t
    key = jnp.where(forced_t, KEY_FORCED, jnp.where(causal_t, lax.bitcast_convert_type(imp_t, jnp.int32), KEY_FUTURE))
    key_ref[...] = key
    groups = [key[8 * a:8 * a + 8, :] for a in range(n_slc // 8)]
    ranks = [jnp.zeros((8, Q_TILE), jnp.int32) for _ in groups]
    sub = lax.broadcasted_iota(jnp.int32, (8, Q_TILE), 0)
    for bb in range(n_slc):
        kb = key_ref[bb:bb + 1, :]
        kb1 = kb + 1
        for a in range(n_slc // 8):
            if 8 * a > bb:
                ahead = kb1
            elif 8 * a + 7 < bb:
                ahead = kb
            else:
                ahead = jnp.where(sub > bb - 8 * a, kb1, kb)
            ranks[a] = ranks[a] + (ahead > groups[a]).astype(jnp.int32)
    rank = jnp.concatenate(ranks, axis=0)
    sel_t = (rank < n_sel) & causal_t
    first_local = (t0 - SLC_PAD) >> (SLC_BLOCK.bit_length() - 1)
    far_t = blk_t < first_local
    flag_rows = jnp.where(blk_t == 0, 1.0, 0.0)
    stack = jnp.concatenate([jnp.where(sel_t, 0.0, 1.0), flag_rows,
                             jnp.where(sel_t & far_t, 0.0, 1.0), flag_rows], axis=0).astype(BF)
    qmask = _dot_nt(eye_ref[...], stack).astype(BF)
    qa_loc = jnp.concatenate([q4, qmask[:, 0:LANES]], axis=1)
    qa_far = jnp.concatenate([q4, qmask[:, LANES:]], axis=1)

    lk = SLC_LOCAL_KEYS
    chunk = SLC_CHUNK_BLOCKS * SLC_BLOCK
    n_far = (first_local + SLC_CHUNK_BLOCKS - 1) // SLC_CHUNK_BLOCKS

    s_loc = _dot_nt(qa_loc, kaug_ref[pl.ds(r0, lk), :]) + sb_ref[...].reshape(rows, lk)

    def far_logits(c, m_part):
        rr = pl.multiple_of(SLC_PAD + c * chunk, SLC_BLOCK)
        s = _dot_nt(qa_far, kaug_ref[pl.ds(rr, chunk), :])
        sfar_ref[c] = s
        return jnp.maximum(m_part, _lane_tiles(s, jnp.maximum))

    m_part = lax.fori_loop(0, n_far, far_logits, _lane_tiles(s_loc, jnp.maximum))
    m_s = jnp.max(m_part, axis=-1, keepdims=True)

    e_loc = jnp.exp2(s_loc - m_s)
    acc_s = _dot(e_loc.astype(BF), vsp_ref[pl.ds(r0, lk), :])

    def far_values(c, acc):
        rr = pl.multiple_of(SLC_PAD + c * chunk, SLC_BLOCK)
        e = jnp.exp2(sfar_ref[c] - m_s)
        return acc + _dot(e.astype(BF), vsp_ref[pl.ds(rr, chunk), :])

    acc_s = lax.fori_loop(0, n_far, far_values, acc_s)
    o_slc = acc_s[:, 0:HEAD_DIM] * (1.0 / acc_s[:, HEAD_DIM:])

    for hh in range(HPG):
        o_h = gated_cw[hh] + gate(hh, 1) * o_slc[hh * Q_TILE:(hh + 1) * Q_TILE]
        o_ref[0, :, hh * HEAD_DIM:(hh + 1) * HEAD_DIM] = o_h.astype(o_ref.dtype)


N_ATTN_IN = 14


def _nsa_attention(proj, k_cmp, v_cmp, wb, sb, cf, to_cast):
    b, s, _ = proj.shape
    n_steps = b * N_KV * (s // Q_TILE)
    cast_specs = []
    for w in to_cast:
        slab = next(r for r in range(BF16_SUBLANES, w.shape[0] + 1, BF16_SUBLANES)
                    if w.shape[0] % r == 0 and w.shape[0] // r <= n_steps)
        last = w.shape[0] // slab - 1
        cast_specs.append(pl.BlockSpec(
            (slab, w.shape[1]),
            lambda i, g, j, last=last: (jnp.minimum((i * N_KV + g) * (s // Q_TILE) + j, last), 0)))
    ovt = _overlap_matrix(s)
    eye = jnp.asarray(np.tile(np.eye(Q_TILE), (HPG, 1)), BF)
    spread = jnp.asarray(np.repeat(np.eye(LANES)[:, :3 * HPG], LANES, axis=1), BF)
    n_chunks = -(-(s // SLC_BLOCK) // SLC_CHUNK_BLOCKS)
    q_blk = Q_OFF // (HPG * HEAD_DIM)
    kv_blk = KC_OFF // HEAD_DIM
    ng_blk = NG_OFF // LANES
    lk = SLC_LOCAL_KEYS
    ncp = s // CMP_STRIDE

    def kv_spec(idx):
        return pl.BlockSpec((1, s, HEAD_DIM), lambda i, g, j: (i, 0, kv_blk + idx * N_KV + g))

    cmp_spec = pl.BlockSpec((1, ncp, HEAD_DIM), lambda i, g, j: (i * N_KV + g, 0, 0))
    outs = pl.pallas_call(
        functools.partial(_attn_kernel, seq=s, n_cast=len(to_cast)),
        grid=(b, N_KV, s // Q_TILE),
        in_specs=[pl.BlockSpec((1, Q_TILE, HPG * HEAD_DIM), lambda i, g, j: (i, j, q_blk + g)),
                  kv_spec(2), kv_spec(3), kv_spec(4), kv_spec(5),
                  cmp_spec, cmp_spec,
                  pl.BlockSpec((1, Q_TILE, LANES), lambda i, g, j: (i, j, ng_blk + g)),
                  pl.BlockSpec((HPG, Q_TILE, WIN_KEYS), lambda i, g, j: (g, 0, 0)),
                  pl.BlockSpec((HPG, Q_TILE, lk), lambda i, g, j: (g, 0, 0)),
                  pl.BlockSpec((HPG, Q_TILE, LANES), lambda i, g, j: (g, 0, 0)),
                  pl.BlockSpec(ovt.shape, lambda i, g, j: (0, 0)),
                  pl.BlockSpec(eye.shape, lambda i, g, j: (0, 0)),
                  pl.BlockSpec(spread.shape, lambda i, g, j: (0, 0))] + cast_specs,
        out_specs=[pl.BlockSpec((1, Q_TILE, HPG * HEAD_DIM), lambda i, g, j: (i, j, g))] + cast_specs,
        out_shape=[jax.ShapeDtypeStruct((b, s, Q_WIDTH), BF)]
        + [jax.ShapeDtypeStruct(w.shape, BF) for w in to_cast],
        scratch_shapes=[pltpu.VMEM((SLC_PAD + s, 2 * HEAD_DIM), BF),
                        pltpu.VMEM((SLC_PAD + s, HEAD_DIM + LANES), BF),
                        pltpu.VMEM((WIN_PAD + s, HEAD_DIM), BF),
                        pltpu.VMEM((WIN_PAD + s, HEAD_DIM + LANES), BF),
                        pltpu.VMEM((s // SLC_BLOCK, Q_TILE), jnp.int32),
                        pltpu.VMEM((n_chunks, HPG * Q_TILE, SLC_CHUNK_BLOCKS * SLC_BLOCK), F32)],
        compiler_params=pltpu.CompilerParams(dimension_semantics=("arbitrary",) * 3,
                                             vmem_limit_bytes=ATTN_VMEM_LIMIT),
    )(proj, proj, proj, proj, proj, k_cmp, v_cmp, proj, wb, sb, cf, ovt, eye, spread, *to_cast)
    return outs[0], outs[1:]


def _overlap_matrix(s):
    ncp = s // CMP_STRIDE
    n_slc = s // SLC_BLOCK
    nc = (s - CMP_LEN) // CMP_STRIDE + 1
    i = np.arange(ncp)[None, :]
    jj = np.arange(n_slc)[:, None]
    ov = ((i * CMP_STRIDE < (jj + 1) * SLC_BLOCK) & (i * CMP_STRIDE + CMP_LEN > jj * SLC_BLOCK) & (i < nc))
    return jnp.asarray(ov, BF)


def kernel(x, c, w_ada, b_ada, g_pre_mix, g_post_mix, g_pre_ffn, g_post_ffn, w_in, w_dw, b_dw, conv_ln_g, conv_ln_b, w_conv_out, cmp_pe_k, cmp_pe_v, w_cmp_k1, w_cmp_k2, w_cmp_v1, w_cmp_v2, rel_bias, w_nsa_out, w_out, w_ffn_gate, w_ffn_up, w_ffn_down):
    b, s, d = x.shape
    m = b * s
    depth = w_ada.shape[0]
    c8 = jnp.zeros((8, d), F32).at[:b].set(c)
    wb, sb, cf = _bias_tables(rel_bias)
    col_scale = jnp.ones((1, NG_OFF + N_KV * LANES), F32).at[:, Q_OFF:KC_OFF].set(HEAD_DIM ** -0.5 * LOG2E)

    for l in range(depth):
        mod = _modulation(c8, w_ada[l], b_ada[l][None, :])
        mod3 = mod[:b].reshape(b, 6, d)

        w_all = w_in[l].astype(BF)
        w_ng = w_in[l][:, NG_OFF:MG_OFF].reshape(d, N_KV, 3 * HPG)
        w_ng = jnp.pad(w_ng, ((0, 0), (0, 0), (0, LANES - 3 * HPG))).reshape(d, N_KV * LANES)
        w_ng = w_ng.astype(BF)
        w_gates = w_all[:, MG_OFF:]

        h = _prenorm(x, g_pre_mix[l][None, :], mod3, 0, 1).reshape(m, d)
        proj = _matmul_with_tail(h, w_all, NG_OFF, w_ng, col_scale, BF, 1024, 512).reshape(b, s, -1)

        u_conv = _conv_module(proj, w_dw[l], b_dw[l][None, :], conv_ln_g[l][None, :], conv_ln_b[l][None, :])

        def grouped(off):
            t = proj[:, :, off:off + KV_WIDTH].reshape(b, s // CMP_STRIDE, CMP_STRIDE, N_KV, HEAD_DIM)
            return t.transpose(0, 3, 1, 2, 4).reshape(b * N_KV, s // CMP_STRIDE, CMP_STRIDE * HEAD_DIM)

        pe_k = jnp.broadcast_to(cmp_pe_k[l].reshape(1, -1), (8, CMP_LEN * HEAD_DIM)).astype(BF)
        pe_v = jnp.broadcast_to(cmp_pe_v[l].reshape(1, -1), (8, CMP_LEN * HEAD_DIM)).astype(BF)
        k_cmp = _compress(grouped(KC_OFF), pe_k, w_cmp_k1[l].astype(BF), w_cmp_k2[l].astype(BF))
        v_cmp = _compress(grouped(KC_OFF + KV_WIDTH), pe_v, w_cmp_v1[l].astype(BF), w_cmp_v2[l].astype(BF))

        later = [w_conv_out[l], w_nsa_out[l], w_out[l], w_ffn_gate[l], w_ffn_up[l], w_ffn_down[l]]
        o_nsa, (wb_conv, wb_nsa, wb_out, wb_gate, wb_up, wb_down) = _nsa_attention(proj, k_cmp, v_cmp, wb, sb, cf, later)

        z = _gated_mix(h, u_conv.reshape(m, CONV_DIM), o_nsa.reshape(m, Q_WIDTH), wb_conv, wb_nsa, w_gates)
        mix = _matmul(z, wb_out, BF, 1024, 1024).reshape(b, s, d)
        x, h2 = _post_mix(mix, x, g_post_mix[l][None, :], g_pre_ffn[l][None, :], mod3, 2, 3, 4)

        ff = _ffn_up(h2.reshape(m, d), wb_gate, wb_up)
        f = _matmul(ff, wb_down, BF, 512, 512).reshape(b, s, d)
        x = _post_ffn(f, x, g_post_ffn[l][None, :], mod3, 5)
    return x
```

```python
import functools
import math

import numpy as np
import jax
import jax.numpy as jnp
from jax import lax
from jax.experimental import pallas as pl
from jax.experimental.pallas import tpu as pltpu

F32 = jnp.float32
BF = jnp.bfloat16

CONV_DIM = 2048
CONV_WIDTH = 31
N_HEADS = 16
N_KV = 4
HPG = N_HEADS // N_KV
HEAD_DIM = 128
CMP_LEN = 32
CMP_STRIDE = 16
CMP_HID = 512
SLC_BLOCK = 64
N_SEL = 16
WINDOW = 512
Q_BLOCK = 64
REL_BUCKETS = 32
REL_MAX_DIST = 128
NORM_EPS = 1e-6
NEG = -1e30
LOG2E = 1.4426950408889634

Q_WIDTH = N_HEADS * HEAD_DIM
KV_WIDTH = N_KV * HEAD_DIM
Q_OFF = 2 * CONV_DIM
KC_OFF = Q_OFF + Q_WIDTH
NG_OFF = KC_OFF + 6 * KV_WIDTH
MG_OFF = NG_OFF + 3 * N_HEADS

LANES = 128
SUBLANES = 8
BF16_SUBLANES = 16
VMEM_LIMIT = 56 * 1024 * 1024
ATTN_VMEM_LIMIT = 61 * 1024 * 1024

CONV_HALO = 32
CONV_ROWS = 64
CONV_LANES = 256

Q_TILE = 4 * Q_BLOCK
SLC_PAD = 2 * SLC_BLOCK
SLC_LOCAL_KEYS = SLC_PAD + Q_TILE
SLC_CHUNK_BLOCKS = 16
WIN_KEYS = WINDOW + Q_TILE
WIN_PAD = WINDOW
CMP_PER_TILE = Q_TILE // CMP_STRIDE
CMP_BAND = 32
CMP_BAND_LO = 9
PAD_FLAG_COL = 64
KEY_FORCED = 0x7F000000
KEY_FUTURE = -2


def _t5_thresholds():
    max_exact = REL_BUCKETS // 2
    thr = []
    for k in range(1, REL_BUCKETS - max_exact):
        n = max_exact
        while max_exact + int(math.log(n / max_exact) / math.log(REL_MAX_DIST / max_exact)
                              * (REL_BUCKETS - max_exact)) < max_exact + k:
            n += 1
        thr.append(n)
    return tuple(thr)


T5_THRESHOLDS = _t5_thresholds()


def _cparams(*sem):
    return pltpu.CompilerParams(dimension_semantics=sem, vmem_limit_bytes=VMEM_LIMIT)


def _silu(v):
    return v * jax.nn.sigmoid(v)


def _dot(a, b):
    return jnp.dot(a, b, preferred_element_type=F32)


def _dot_nt(a, b):
    return lax.dot_general(a, b, (((1,), (1,)), ((), ())), preferred_element_type=F32)


def _rms(v, g):
    return v * lax.rsqrt(jnp.mean(v * v, axis=-1, keepdims=True) + NORM_EPS) * g


def _mod_kernel(c_ref, w_ref, b_ref, o_ref):
    a = _silu(c_ref[...]).astype(BF)
    o_ref[...] = _dot(a, w_ref[...].astype(BF)) + b_ref[...]


def _modulation(c8, w_ada, b_ada, tn=512):
    rows, d = c8.shape
    n = w_ada.shape[1]
    return pl.pallas_call(
        _mod_kernel,
        grid=(n // tn,),
        in_specs=[pl.BlockSpec((rows, d), lambda j: (0, 0)),
                  pl.BlockSpec((d, tn), lambda j: (0, j)),
                  pl.BlockSpec((1, tn), lambda j: (0, j))],
        out_specs=pl.BlockSpec((rows, tn), lambda j: (0, j)),
        out_shape=jax.ShapeDtypeStruct((rows, n), F32),
        compiler_params=_cparams("parallel"),
    )(c8, w_ada, b_ada)


def _prenorm_kernel(x_ref, g_ref, mod_ref, h_ref, *, shift_row, scale_row):
    y = _rms(x_ref[0], g_ref[...])
    h = y * (1.0 + mod_ref[0, scale_row:scale_row + 1, :]) + mod_ref[0, shift_row:shift_row + 1, :]
    h_ref[0] = h.astype(BF)


def _prenorm(x, g, mod3, shift_row, scale_row, tr=512):
    b, s, d = x.shape
    return pl.pallas_call(
        functools.partial(_prenorm_kernel, shift_row=shift_row, scale_row=scale_row),
        grid=(b, s // tr),
        in_specs=[pl.BlockSpec((1, tr, d), lambda i, r: (i, r, 0)),
                  pl.BlockSpec((1, d), lambda i, r: (0, 0)),
                  pl.BlockSpec((1, 6, d), lambda i, r: (i, 0, 0))],
        out_specs=pl.BlockSpec((1, tr, d), lambda i, r: (i, r, 0)),
        out_shape=jax.ShapeDtypeStruct((b, s, d), BF),
        compiler_params=_cparams("parallel", "parallel"),
    )(x, g, mod3)


def _post_mix_kernel(y_ref, x_ref, gp_ref, gn_ref, mod_ref, xo_ref, h_ref, *, gate_row, shift_row, scale_row):
    x1 = x_ref[0] + mod_ref[0, gate_row:gate_row + 1, :] * _rms(y_ref[0].astype(F32), gp_ref[...])
    xo_ref[0] = x1
    h = _rms(x1, gn_ref[...]) * (1.0 + mod_ref[0, scale_row:scale_row + 1, :]) \
        + mod_ref[0, shift_row:shift_row + 1, :]
    h_ref[0] = h.astype(BF)


def _post_mix(y, x, g_post, g_next, mod3, gate_row, shift_row, scale_row, tr=256):
    b, s, d = x.shape
    row = pl.BlockSpec((1, tr, d), lambda i, r: (i, r, 0))
    vec = pl.BlockSpec((1, d), lambda i, r: (0, 0))
    return pl.pallas_call(
        functools.partial(_post_mix_kernel, gate_row=gate_row, shift_row=shift_row, scale_row=scale_row),
        grid=(b, s // tr),
        in_specs=[row, row, vec, vec, pl.BlockSpec((1, 6, d), lambda i, r: (i, 0, 0))],
        out_specs=[row, row],
        out_shape=[jax.ShapeDtypeStruct((b, s, d), F32), jax.ShapeDtypeStruct((b, s, d), BF)],
        compiler_params=_cparams("parallel", "parallel"),
    )(y, x, g_post, g_next, mod3)


def _post_ffn_kernel(y_ref, x_ref, gp_ref, mod_ref, xo_ref, *, gate_row):
    xo_ref[0] = x_ref[0] + mod_ref[0, gate_row:gate_row + 1, :] * _rms(y_ref[0].astype(F32), gp_ref[...])


def _post_ffn(y, x, g_post, mod3, gate_row, tr=256):
    b, s, d = x.shape
    row = pl.BlockSpec((1, tr, d), lambda i, r: (i, r, 0))
    return pl.pallas_call(
        functools.partial(_post_ffn_kernel, gate_row=gate_row),
        grid=(b, s // tr),
        in_specs=[row, row, pl.BlockSpec((1, d), lambda i, r: (0, 0)),
                  pl.BlockSpec((1, 6, d), lambda i, r: (i, 0, 0))],
        out_specs=row,
        out_shape=jax.ShapeDtypeStruct((b, s, d), F32),
        compiler_params=_cparams("parallel", "parallel"),
    )(y, x, g_post, mod3)


def _mm_scale_kernel(a_ref, w_ref, s_ref, o_ref):
    o_ref[...] = (_dot(a_ref[...], w_ref[...]) * s_ref[...]).astype(o_ref.dtype)


def _mm_kernel(a_ref, w_ref, o_ref):
    o_ref[...] = _dot(a_ref[...], w_ref[...]).astype(o_ref.dtype)


def _matmul(a, w, out_dtype, tm, tn, col_scale=None):
    m, k = a.shape
    n = w.shape[1]
    in_specs = [pl.BlockSpec((tm, k), lambda i, j: (i, 0)),
                pl.BlockSpec((k, tn), lambda i, j: (0, j))]
    args = [a, w]
    body = _mm_kernel
    if col_scale is not None:
        in_specs.append(pl.BlockSpec((1, tn), lambda i, j: (0, j)))
        args.append(col_scale)
        body = _mm_scale_kernel
    return pl.pallas_call(
        body,
        grid=(m // tm, n // tn),
        in_specs=in_specs,
        out_specs=pl.BlockSpec((tm, tn), lambda i, j: (i, j)),
        out_shape=jax.ShapeDtypeStruct((m, n), out_dtype),
        compiler_params=_cparams("parallel", "arbitrary"),
    )(*args)


def _mm_tail_kernel(a_ref, w_ref, wt_ref, s_ref, o_ref, *, n_main):
    j = pl.program_id(1)

    @pl.when(j < n_main)
    def _main():
        o_ref[...] = (_dot(a_ref[...], w_ref[...]) * s_ref[...]).astype(o_ref.dtype)

    @pl.when(j >= n_main)
    def _tail():
        o_ref[...] = (_dot(a_ref[...], wt_ref[...]) * s_ref[...]).astype(o_ref.dtype)


def _matmul_with_tail(a, w, n_w, w_tail, col_scale, out_dtype, tm, tn):
    m, k = a.shape
    n_main = n_w // tn
    n_tail = w_tail.shape[1] // tn
    n = n_w + w_tail.shape[1]
    return pl.pallas_call(
        functools.partial(_mm_tail_kernel, n_main=n_main),
        grid=(m // tm, n_main + n_tail),
        in_specs=[pl.BlockSpec((tm, k), lambda i, j: (i, 0)),
                  pl.BlockSpec((k, tn), lambda i, j: (0, jnp.minimum(j, n_main - 1))),
                  pl.BlockSpec((k, tn), lambda i, j: (0, jnp.maximum(j - n_main, 0))),
                  pl.BlockSpec((1, tn), lambda i, j: (0, j))],
        out_specs=pl.BlockSpec((tm, tn), lambda i, j: (i, j)),
        out_shape=jax.ShapeDtypeStruct((m, n), out_dtype),
        compiler_params=_cparams("parallel", "arbitrary"),
    )(a, w, w_tail, col_scale)


def _mix_kernel(h_ref, a1_ref, a2_ref, w1_ref, w2_ref, wga_ref, wgb_ref, o_ref):
    h = h_ref[...]
    ya = _dot(a1_ref[...], w1_ref[...])
    yb = _dot(a2_ref[...], w2_ref[...])
    o = jax.nn.sigmoid(_dot(h, wga_ref[...])) * ya + jax.nn.sigmoid(_dot(h, wgb_ref[...])) * yb
    o_ref[...] = o.astype(o_ref.dtype)


def _gated_mix(h, u_conv, o_nsa, w_conv_out, w_nsa_out, w_gates, tm=1024, tn=256):
    m, k = h.shape
    ka = u_conv.shape[1]
    kb = o_nsa.shape[1]
    d_model = w_conv_out.shape[1]
    gb_block = d_model // tn
    return pl.pallas_call(
        _mix_kernel,
        grid=(m // tm, d_model // tn),
        in_specs=[pl.BlockSpec((tm, k), lambda i, j: (i, 0)),
                  pl.BlockSpec((tm, ka), lambda i, j: (i, 0)),
                  pl.BlockSpec((tm, kb), lambda i, j: (i, 0)),
                  pl.BlockSpec((ka, tn), lambda i, j: (0, j)),
                  pl.BlockSpec((kb, tn), lambda i, j: (0, j)),
                  pl.BlockSpec((k, tn), lambda i, j: (0, j)),
                  pl.BlockSpec((k, tn), lambda i, j: (0, j + gb_block))],
        out_specs=pl.BlockSpec((tm, tn), lambda i, j: (i, j)),
        out_shape=jax.ShapeDtypeStruct((m, d_model), BF),
        compiler_params=_cparams("parallel", "arbitrary"),
    )(h, u_conv, o_nsa, w_conv_out, w_nsa_out, w_gates, w_gates)


def _ffn_up_kernel(a_ref, wg_ref, wu_ref, o_ref):
    a = a_ref[...]
    o_ref[...] = (_silu(_dot(a, wg_ref[...])) * _dot(a, wu_ref[...])).astype(o_ref.dtype)


def _ffn_up(h, w_gate, w_up, tm=2048, tn=256):
    m, k = h.shape
    n = w_gate.shape[1]
    wspec = pl.BlockSpec((k, tn), lambda i, j: (0, j))
    return pl.pallas_call(
        _ffn_up_kernel,
        grid=(m // tm, n // tn),
        in_specs=[pl.BlockSpec((tm, k), lambda i, j: (i, 0)), wspec, wspec],
        out_specs=pl.BlockSpec((tm, tn), lambda i, j: (i, j)),
        out_shape=jax.ShapeDtypeStruct((m, n), BF),
        compiler_params=_cparams("parallel", "arbitrary"),
    )(h, w_gate, w_up)


def _conv_kernel(a_ref, b_ref, ah_ref, bh_ref, w_ref, bdw_ref, g_ref, bb_ref, o_ref, u_ref, sh_ref, v_ref, *, ts):
    i = pl.program_id(1)
    u_ref[CONV_HALO:CONV_HALO + ts, :] = a_ref[0].astype(F32) * jax.nn.sigmoid(b_ref[0].astype(F32))
    uh = ah_ref[0].astype(F32) * jax.nn.sigmoid(bh_ref[0].astype(F32))
    u_ref[0:CONV_HALO, :] = jnp.where(i > 0, uh, 0.0)

    rc, cc = CONV_ROWS, CONV_LANES
    first = CONV_HALO - (CONV_WIDTH - 1)
    span = ts + CONV_HALO - SUBLANES
    for c0 in range(0, CONV_DIM, cc):
        for s in range(1, SUBLANES):
            sh_ref[s - 1] = u_ref[s:s + span, c0:c0 + cc]

        def row_chunk(rb, carry, c0=c0):
            r0 = pl.multiple_of(rb * rc, rc)
            acc = jnp.zeros((rc, cc), F32)
            for k in range(CONV_WIDTH):
                s = (first + k) % SUBLANES
                base = first + k - s
                if s == 0:
                    tap = u_ref[pl.ds(r0 + base, rc), c0:c0 + cc]
                else:
                    tap = sh_ref[s - 1, pl.ds(r0 + base, rc), :]
                acc = acc + tap * w_ref[k:k + 1, c0:c0 + cc]
            v_ref[pl.ds(r0, rc), c0:c0 + cc] = acc + bdw_ref[:, c0:c0 + cc]
            return carry

        lax.fori_loop(0, ts // rc, row_chunk, 0)

    v = v_ref[...]
    mu = jnp.mean(v, axis=-1, keepdims=True)
    vc = v - mu
    y = vc * lax.rsqrt(jnp.mean(vc * vc, axis=-1, keepdims=True) + NORM_EPS) * g_ref[...] + bb_ref[...]
    o_ref[0] = _silu(y).astype(o_ref.dtype)


def _conv_module(proj, w_dw, b_dw, ln_g, ln_b, ts=256):
    b, s, _ = proj.shape
    hb = ts // CONV_HALO
    cur_a = pl.BlockSpec((1, ts, CONV_DIM), lambda i, r: (i, r, 0))
    cur_b = pl.BlockSpec((1, ts, CONV_DIM), lambda i, r: (i, r, 1))
    halo_a = pl.BlockSpec((1, CONV_HALO, CONV_DIM), lambda i, r: (i, jnp.maximum(r * hb - 1, 0), 0))
    halo_b = pl.BlockSpec((1, CONV_HALO, CONV_DIM), lambda i, r: (i, jnp.maximum(r * hb - 1, 0), 1))
    vec = pl.BlockSpec((1, CONV_DIM), lambda i, r: (0, 0))
    return pl.pallas_call(
        functools.partial(_conv_kernel, ts=ts),
        grid=(b, s // ts),
        in_specs=[cur_a, cur_b, halo_a, halo_b,
                  pl.BlockSpec((CONV_WIDTH, CONV_DIM), lambda i, r: (0, 0)), vec, vec, vec],
        out_specs=pl.BlockSpec((1, ts, CONV_DIM), lambda i, r: (i, r, 0)),
        out_shape=jax.ShapeDtypeStruct((b, s, CONV_DIM), BF),
        scratch_shapes=[pltpu.VMEM((CONV_HALO + ts, CONV_DIM), F32),
                        pltpu.VMEM((SUBLANES - 1, ts + CONV_HALO - SUBLANES, CONV_LANES), F32),
                        pltpu.VMEM((ts, CONV_DIM), F32)],
        compiler_params=_cparams("parallel", "arbitrary"),
    )(proj, proj, proj, proj, w_dw, b_dw, ln_g, ln_b)


def _compress_kernel(k_ref, pe_ref, w1_ref, w2_ref, o_ref, kf_ref):
    kf_ref[...] = k_ref[0].astype(F32)
    rows = kf_ref.shape[0] // CMP_STRIDE
    lo = jnp.zeros((rows, CMP_HID), F32)
    hi = jnp.zeros((rows, CMP_HID), F32)
    for l in range(CMP_STRIDE):
        tok = kf_ref[pl.ds(l, rows, stride=CMP_STRIDE), :].astype(BF)
        lo = lo + _dot(tok, w1_ref[l * HEAD_DIM:(l + 1) * HEAD_DIM, :])
        hi = hi + _dot(tok, w1_ref[(CMP_STRIDE + l) * HEAD_DIM:(CMP_STRIDE + l + 1) * HEAD_DIM, :])
    pe = _dot(pe_ref[...], w1_ref[...])
    hid = lo + pltpu.roll(hi, rows - 1, axis=0) + pe[0:1, :]
    act = 0.5 * hid * (1.0 + jnp.tanh(math.sqrt(2.0 / math.pi) * (hid + 0.044715 * hid * hid * hid)))
    o_ref[0] = _dot(act.astype(BF), w2_ref[...])


def _compress(proj, col_block, pe8, w1, w2):
    b, s, _ = proj.shape
    rows = s // CMP_STRIDE
    return pl.pallas_call(
        _compress_kernel,
        grid=(b, N_KV),
        in_specs=[pl.BlockSpec((1, s, HEAD_DIM), lambda i, g: (i, 0, col_block + g)),
                  pl.BlockSpec(pe8.shape, lambda i, g: (0, 0)),
                  pl.BlockSpec(w1.shape, lambda i, g: (0, 0)),
                  pl.BlockSpec(w2.shape, lambda i, g: (0, 0))],
        out_specs=pl.BlockSpec((1, rows, HEAD_DIM), lambda i, g: (i * N_KV + g, 0, 0)),
        out_shape=jax.ShapeDtypeStruct((b * N_KV, rows, HEAD_DIM), F32),
        scratch_shapes=[pltpu.VMEM((s, HEAD_DIM), F32)],
        compiler_params=_cparams("parallel", "parallel"),
    )(proj, pe8, w1, w2)


def _bias_tables_kernel(rb_ref, wb_ref, sb_ref, cf_ref):
    h = pl.program_id(0)
    last = rb_ref[REL_BUCKETS - 1, h]

    def rel(dist):
        n = jnp.maximum(dist, 0)
        bucket = jnp.full(n.shape, REL_BUCKETS // 2, jnp.int32)
        for t in T5_THRESHOLDS:
            bucket = bucket + (n >= t).astype(jnp.int32)
        bucket = jnp.where(n < REL_BUCKETS // 2, n, jnp.minimum(bucket, REL_BUCKETS - 1))
        out = jnp.zeros(n.shape, F32)
        for bk in range(REL_BUCKETS - 1):
            out = jnp.where(bucket == bk, rb_ref[bk, h] - last, out)
        return out * LOG2E

    def dist_of(shape, offset):
        qi = lax.broadcasted_iota(jnp.int32, shape, 0)
        kl = lax.broadcasted_iota(jnp.int32, shape, 1)
        return qi + offset - kl

    dw = dist_of((Q_TILE, WIN_KEYS), WIN_PAD)
    wb_ref[0] = jnp.where((dw >= 0) & (dw < WINDOW), rel(dw), NEG)

    ds_ = dist_of((Q_TILE, SLC_LOCAL_KEYS), SLC_PAD)
    sb_ref[0] = jnp.where(ds_ >= 0, rel(ds_), NEG)

    qi = lax.broadcasted_iota(jnp.int32, (Q_TILE, LANES), 0)
    col = lax.broadcasted_iota(jnp.int32, (Q_TILE, LANES), 1)
    mi = col & (CMP_BAND - 1)
    part = col >> (CMP_BAND.bit_length() - 1)
    dc = qi - CMP_STRIDE * (mi - CMP_BAND_LO) - (CMP_LEN - 1)
    val = jnp.where(dc >= 0, rel(dc), 0.0)
    p0 = val.astype(BF).astype(F32)
    r1 = val - p0
    p1 = r1.astype(BF).astype(F32)
    p2 = r1 - p1
    parts = jnp.where(part == 0, p0, jnp.where(part == 1, p1, jnp.where(part == 2, p2, 0.0)))
    cf_ref[0] = parts.astype(BF)


def _bias_tables(rel_bias):
    smem = pl.BlockSpec(memory_space=pltpu.SMEM)
    lk = SLC_LOCAL_KEYS
    return pl.pallas_call(
        _bias_tables_kernel,
        grid=(N_HEADS,),
        in_specs=[smem],
        out_specs=[pl.BlockSpec((1, Q_TILE, WIN_KEYS), lambda h: (h, 0, 0)),
                   pl.BlockSpec((1, Q_TILE, lk), lambda h: (h, 0, 0)),
                   pl.BlockSpec((1, Q_TILE, LANES), lambda h: (h, 0, 0))],
        out_shape=[jax.ShapeDtypeStruct((N_HEADS, Q_TILE, WIN_KEYS), F32),
                   jax.ShapeDtypeStruct((N_HEADS, Q_TILE, lk), F32),
                   jax.ShapeDtypeStruct((N_HEADS, Q_TILE, LANES), BF)],
        compiler_params=_cparams("arbitrary"),
    )(rel_bias)


def _softmax_parts(s):
    m = jnp.max(s, axis=-1, keepdims=True)
    e = jnp.exp2(s - m)
    return m, e, jnp.sum(e, axis=-1, keepdims=True)


def _lane_tiles(v, op):
    out = v[:, 0:LANES]
    for t in range(1, v.shape[1] // LANES):
        out = op(out, v[:, t * LANES:(t + 1) * LANES])
    return out


def _attn_kernel(*refs, seq, n_cast):
    (q_ref, ks_ref, vs_ref, kw_ref, vw_ref, kc_ref, vc_ref, ng_ref,
     wb_ref, sb_ref, cf_ref, ovt_ref, eye_ref, spread_ref) = refs[:N_ATTN_IN]
    cast_in = refs[N_ATTN_IN:N_ATTN_IN + n_cast]
    o_ref = refs[N_ATTN_IN + n_cast]
    cast_out = refs[N_ATTN_IN + n_cast + 1:N_ATTN_IN + 2 * n_cast + 1]
    kaug_ref, vsp_ref, kwp_ref, vwp_ref, key_ref, sfar_ref = refs[N_ATTN_IN + 2 * n_cast + 1:]

    for w_ref, wo_ref in zip(cast_in, cast_out):
        wo_ref[...] = w_ref[...].astype(BF)

    tile = pl.program_id(2)
    n_slc = seq // SLC_BLOCK
    ncp = seq // CMP_STRIDE
    rows = HPG * Q_TILE
    n_sel = min(N_SEL, n_slc)
    if n_slc != PAD_FLAG_COL or Q_TILE % LANES or 3 * CMP_BAND > LANES:
        raise NotImplementedError("selection layout expects 64 selection blocks and whole-lane query tiles")
    assert CMP_STRIDE * (CMP_BAND_LO + 1) - (CMP_LEN - 1) >= T5_THRESHOLDS[-1]
    assert CMP_STRIDE * (CMP_BAND - CMP_BAND_LO) + CMP_LEN - 1 >= Q_TILE

    @pl.when(tile == 0)
    def _stage_keys():
        col = lax.broadcasted_iota(jnp.int32, (SLC_PAD, LANES), 1)
        kaug_ref[0:SLC_PAD, 0:HEAD_DIM] = jnp.zeros((SLC_PAD, HEAD_DIM), BF)
        kaug_ref[0:SLC_PAD, HEAD_DIM:] = jnp.where(col == PAD_FLAG_COL, NEG, 0.0).astype(BF)
        kaug_ref[SLC_PAD:, 0:HEAD_DIM] = ks_ref[0]
        blk = lax.broadcasted_iota(jnp.int32, (seq, LANES), 0) >> (SLC_BLOCK.bit_length() - 1)
        colk = lax.broadcasted_iota(jnp.int32, (seq, LANES), 1)
        kaug_ref[SLC_PAD:, HEAD_DIM:] = jnp.where(colk == blk, NEG, 0.0).astype(BF)
        vsp_ref[:, HEAD_DIM:] = jnp.ones((SLC_PAD + seq, LANES), BF)
        vsp_ref[0:SLC_PAD, 0:HEAD_DIM] = jnp.zeros((SLC_PAD, HEAD_DIM), BF)
        vsp_ref[SLC_PAD:, 0:HEAD_DIM] = vs_ref[0]
        kwp_ref[0:WIN_PAD, :] = jnp.zeros((WIN_PAD, HEAD_DIM), BF)
        kwp_ref[WIN_PAD:, :] = kw_ref[0]
        vwp_ref[:, HEAD_DIM:] = jnp.ones((WIN_PAD + seq, LANES), BF)
        vwp_ref[0:WIN_PAD, 0:HEAD_DIM] = jnp.zeros((WIN_PAD, HEAD_DIM), BF)
        vwp_ref[WIN_PAD:, 0:HEAD_DIM] = vw_ref[0]

    q4 = jnp.concatenate([q_ref[0, :, hh * HEAD_DIM:(hh + 1) * HEAD_DIM] for hh in range(HPG)], axis=0)
    t0 = tile * Q_TILE
    r0 = pl.multiple_of(t0, Q_TILE)

    gts = jax.nn.sigmoid(ng_ref[0].astype(F32))
    g_hi = gts.astype(BF)
    g_lo = (gts - g_hi.astype(F32)).astype(BF)
    spread = spread_ref[...]
    gsp = _dot(g_hi, spread) + _dot(g_lo, spread)

    key_w = lax.broadcasted_iota(jnp.int32, (1, WIN_KEYS), 1)
    before_start = jnp.where(key_w < WIN_PAD - t0, NEG, 0.0)
    s_w = _dot_nt(q4, kwp_ref[pl.ds(r0, WIN_KEYS), :]) + wb_ref[...].reshape(rows, WIN_KEYS) + before_start
    e_w = jnp.exp2(s_w - jnp.max(s_w, axis=-1, keepdims=True))
    pv_w = _dot(e_w.astype(BF), vwp_ref[pl.ds(r0, WIN_KEYS), :])
    o_win = pv_w[:, 0:HEAD_DIM] * (1.0 / pv_w[:, HEAD_DIM:])

    kc = kc_ref[0].astype(BF)
    vc = vc_ref[0].astype(BF)
    r_io = lax.broadcasted_iota(jnp.int32, (LANES, ncp), 0)
    n_io = lax.broadcasted_iota(jnp.int32, (LANES, ncp), 1)
    band = (r_io < 3 * CMP_BAND) & (n_io == CMP_PER_TILE * tile + (r_io & (CMP_BAND - 1)) - CMP_BAND_LO)
    shift = jnp.where(band, 1.0, 0.0).astype(BF)
    cf4 = cf_ref[...].reshape(rows, LANES)
    lc = _dot_nt(q4, kc) + _dot(cf4, shift)
    qi_c = lax.broadcasted_iota(jnp.int32, (Q_TILE, ncp), 0)
    n_c = lax.broadcasted_iota(jnp.int32, (Q_TILE, ncp), 1)
    vis = (CMP_STRIDE * n_c + (CMP_LEN - 1)) <= (t0 + qi_c)
    lc3 = jnp.where(vis[None], lc.reshape(HPG, Q_TILE, ncp), NEG)
    _, e_c, l_c = _softmax_parts(lc3)
    p_c = jnp.where(vis[None], e_c * (1.0 / l_c), 0.0)
    o_cmp = _dot(p_c.reshape(rows, ncp).astype(BF), vc)

    psum = p_c[0]
    for hh in range(1, HPG):
        psum = psum + p_c[hh]
    p0 = psum.astype(BF)
    r1 = psum - p0.astype(F32)
    p1 = r1.astype(BF)
    p2 = (r1 - p1.astype(F32)).astype(BF)
    ovt = ovt_ref[...]
    imp_t = _dot_nt(ovt, p0) + _dot_nt(ovt, p1) + _dot_nt(ovt, p2)

    def gate(hh, branch):
        c = 3 * hh + branch
        return gsp[:, c * LANES:(c + 1) * LANES]

    gated_cw = [gate(hh, 0) * o_cmp[hh * Q_TILE:(hh + 1) * Q_TILE] + gate(hh, 2) * o_win[hh * Q_TILE:(hh + 1) * Q_TILE]
                for hh in range(HPG)]

    blk_t = lax.broadcasted_iota(jnp.int32, (n_slc, Q_TILE), 0)
    q_t = lax.broadcasted_iota(jnp.int32, (n_slc, Q_TILE), 1)
    cur_t = (t0 + q_t) >> (SLC_BLOCK.bit_length() - 1)
    forced_t = (blk_t == 0) | (blk_t == cur_t) | (blk_t == cur_t - 1)
    causal_t = blk_t <= cur_t
    key = jnp.where(forced_t, KEY_FORCED, jnp.where(causal_t, lax.bitcast_convert_type(imp_t, jnp.int32), KEY_FUTURE))
    key_ref[...] = key
    groups = [key[8 * a:8 * a + 8, :] for a in range(n_slc // 8)]
    ranks = [jnp.zeros((8, Q_TILE), jnp.int32) for _ in groups]
    sub = lax.broadcasted_iota(jnp.int32, (8, Q_TILE), 0)
    for bb in range(n_slc):
        kb = key_ref[bb:bb + 1, :]
        kb1 = kb + 1
        for a in range(n_slc // 8):
            if 8 * a > bb:
                ahead = kb1
            elif 8 * a + 7 < bb:
                ahead = kb
            else:
                ahead = jnp.where(sub > bb - 8 * a, kb1, kb)
            ranks[a] = ranks[a] + (ahead > groups[a]).astype(jnp.int32)
    rank = jnp.concatenate(ranks, axis=0)
    sel_t = (rank < n_sel) & causal_t
    first_local = (t0 - SLC_PAD) >> (SLC_BLOCK.bit_length() - 1)
    far_t = blk_t < first_local
    flag_rows = jnp.where(blk_t == 0, 1.0, 0.0)
    stack = jnp.concatenate([jnp.where(sel_t, 0.0, 1.0), flag_rows,
                             jnp.where(sel_t & far_t, 0.0, 1.0), flag_rows], axis=0).astype(BF)
    qmask = _dot_nt(eye_ref[...], stack).astype(BF)
    qa_loc = jnp.concatenate([q4, qmask[:, 0:LANES]], axis=1)
    qa_far = jnp.concatenate([q4, qmask[:, LANES:]], axis=1)

    lk = SLC_LOCAL_KEYS
    chunk = SLC_CHUNK_BLOCKS * SLC_BLOCK
    n_far = (first_local + SLC_CHUNK_BLOCKS - 1) // SLC_CHUNK_BLOCKS

    s_loc = _dot_nt(qa_loc, kaug_ref[pl.ds(r0, lk), :]) + sb_ref[...].reshape(rows, lk)

    def far_logits(c, m_part):
        rr = pl.multiple_of(SLC_PAD + c * chunk, SLC_BLOCK)
        s = _dot_nt(qa_far, kaug_ref[pl.ds(rr, chunk), :])
        sfar_ref[c] = s
        return jnp.maximum(m_part, _lane_tiles(s, jnp.maximum))

    m_part = lax.fori_loop(0, n_far, far_logits, _lane_tiles(s_loc, jnp.maximum))
    m_s = jnp.max(m_part, axis=-1, keepdims=True)

    e_loc = jnp.exp2(s_loc - m_s)
    acc_s = _dot(e_loc.astype(BF), vsp_ref[pl.ds(r0, lk), :])

    def far_values(c, acc):
        rr = pl.multiple_of(SLC_PAD + c * chunk, SLC_BLOCK)
        e = jnp.exp2(sfar_ref[c] - m_s)
        return acc + _dot(e.astype(BF), vsp_ref[pl.ds(rr, chunk), :])

    acc_s = lax.fori_loop(0, n_far, far_values, acc_s)
    o_slc = acc_s[:, 0:HEAD_DIM] * (1.0 / acc_s[:, HEAD_DIM:])

    for hh in range(HPG):
        o_h = gated_cw[hh] + gate(hh, 1) * o_slc[hh * Q_TILE:(hh + 1) * Q_TILE]
        o_ref[0, :, hh * HEAD_DIM:(hh + 1) * HEAD_DIM] = o_h.astype(o_ref.dtype)


N_ATTN_IN = 14


def _nsa_attention(proj, k_cmp, v_cmp, wb, sb, cf, to_cast):
    b, s, _ = proj.shape
    n_steps = b * N_KV * (s // Q_TILE)
    cast_specs = []
    for w in to_cast:
        slab = next(r for r in range(BF16_SUBLANES, w.shape[0] + 1, BF16_SUBLANES)
                    if w.shape[0] % r == 0 and w.shape[0] // r <= n_steps)
        last = w.shape[0] // slab - 1
        cast_specs.append(pl.BlockSpec(
            (slab, w.shape[1]),
            lambda i, g, j, last=last: (jnp.minimum((i * N_KV + g) * (s // Q_TILE) + j, last), 0)))
    ovt = _overlap_matrix(s)
    eye = jnp.asarray(np.tile(np.eye(Q_TILE), (HPG, 1)), BF)
    spread = jnp.asarray(np.repeat(np.eye(LANES)[:, :3 * HPG], LANES, axis=1), BF)
    n_chunks = -(-(s // SLC_BLOCK) // SLC_CHUNK_BLOCKS)
    q_blk = Q_OFF // (HPG * HEAD_DIM)
    kv_blk = KC_OFF // HEAD_DIM
    ng_blk = NG_OFF // LANES
    lk = SLC_LOCAL_KEYS
    ncp = s // CMP_STRIDE

    def kv_spec(idx):
        return pl.BlockSpec((1, s, HEAD_DIM), lambda i, g, j: (i, 0, kv_blk + idx * N_KV + g))

    cmp_spec = pl.BlockSpec((1, ncp, HEAD_DIM), lambda i, g, j: (i * N_KV + g, 0, 0))
    outs = pl.pallas_call(
        functools.partial(_attn_kernel, seq=s, n_cast=len(to_cast)),
        grid=(b, N_KV, s // Q_TILE),
        in_specs=[pl.BlockSpec((1, Q_TILE, HPG * HEAD_DIM), lambda i, g, j: (i, j, q_blk + g)),
                  kv_spec(2), kv_spec(3), kv_spec(4), kv_spec(5),
                  cmp_spec, cmp_spec,
                  pl.BlockSpec((1, Q_TILE, LANES), lambda i, g, j: (i, j, ng_blk + g)),
                  pl.BlockSpec((HPG, Q_TILE, WIN_KEYS), lambda i, g, j: (g, 0, 0)),
                  pl.BlockSpec((HPG, Q_TILE, lk), lambda i, g, j: (g, 0, 0)),
                  pl.BlockSpec((HPG, Q_TILE, LANES), lambda i, g, j: (g, 0, 0)),
                  pl.BlockSpec(ovt.shape, lambda i, g, j: (0, 0)),
                  pl.BlockSpec(eye.shape, lambda i, g, j: (0, 0)),
                  pl.BlockSpec(spread.shape, lambda i, g, j: (0, 0))] + cast_specs,
        out_specs=[pl.BlockSpec((1, Q_TILE, HPG * HEAD_DIM), lambda i, g, j: (i, j, g))] + cast_specs,
        out_shape=[jax.ShapeDtypeStruct((b, s, Q_WIDTH), BF)]
        + [jax.ShapeDtypeStruct(w.shape, BF) for w in to_cast],
        scratch_shapes=[pltpu.VMEM((SLC_PAD + s, 2 * HEAD_DIM), BF),
                        pltpu.VMEM((SLC_PAD + s, HEAD_DIM + LANES), BF),
                        pltpu.VMEM((WIN_PAD + s, HEAD_DIM), BF),
                        pltpu.VMEM((WIN_PAD + s, HEAD_DIM + LANES), BF),
                        pltpu.VMEM((s // SLC_BLOCK, Q_TILE), jnp.int32),
                        pltpu.VMEM((n_chunks, HPG * Q_TILE, SLC_CHUNK_BLOCKS * SLC_BLOCK), F32)],
        compiler_params=pltpu.CompilerParams(dimension_semantics=("arbitrary",) * 3,
                                             vmem_limit_bytes=ATTN_VMEM_LIMIT),
    )(proj, proj, proj, proj, proj, k_cmp, v_cmp, proj, wb, sb, cf, ovt, eye, spread, *to_cast)
    return outs[0], outs[1:]


def _overlap_matrix(s):
    ncp = s // CMP_STRIDE
    n_slc = s // SLC_BLOCK
    nc = (s - CMP_LEN) // CMP_STRIDE + 1
    i = np.arange(ncp)[None, :]
    jj = np.arange(n_slc)[:, None]
    ov = ((i * CMP_STRIDE < (jj + 1) * SLC_BLOCK) & (i * CMP_STRIDE + CMP_LEN > jj * SLC_BLOCK) & (i < nc))
    return jnp.asarray(ov, BF)


def kernel(x, c, w_ada, b_ada, g_pre_mix, g_post_mix, g_pre_ffn, g_post_ffn, w_in, w_dw, b_dw, conv_ln_g, conv_ln_b, w_conv_out, cmp_pe_k, cmp_pe_v, w_cmp_k1, w_cmp_k2, w_cmp_v1, w_cmp_v2, rel_bias, w_nsa_out, w_out, w_ffn_gate, w_ffn_up, w_ffn_down):
    b, s, d = x.shape
    m = b * s
    depth = w_ada.shape[0]
    c8 = jnp.zeros((8, d), F32).at[:b].set(c)
    wb, sb, cf = _bias_tables(rel_bias)
    col_scale = jnp.ones((1, NG_OFF + N_KV * LANES), F32).at[:, Q_OFF:KC_OFF].set(HEAD_DIM ** -0.5 * LOG2E)

    for l in range(depth):
        mod = _modulation(c8, w_ada[l], b_ada[l][None, :])
        mod3 = mod[:b].reshape(b, 6, d)

        w_all = w_in[l].astype(BF)
        w_ng = w_in[l][:, NG_OFF:MG_OFF].reshape(d, N_KV, 3 * HPG)
        w_ng = jnp.pad(w_ng, ((0, 0), (0, 0), (0, LANES - 3 * HPG))).reshape(d, N_KV * LANES)
        w_ng = w_ng.astype(BF)
        w_gates = w_all[:, MG_OFF:]

        h = _prenorm(x, g_pre_mix[l][None, :], mod3, 0, 1).reshape(m, d)
        proj = _matmul_with_tail(h, w_all, NG_OFF, w_ng, col_scale, BF, 1024, 512).reshape(b, s, -1)

        u_conv = _conv_module(proj, w_dw[l], b_dw[l][None, :], conv_ln_g[l][None, :], conv_ln_b[l][None, :])

        pe_k = jnp.broadcast_to(cmp_pe_k[l].reshape(1, -1), (8, CMP_LEN * HEAD_DIM)).astype(BF)
        pe_v = jnp.broadcast_to(cmp_pe_v[l].reshape(1, -1), (8, CMP_LEN * HEAD_DIM)).astype(BF)
        k_cmp = _compress(proj, KC_OFF // HEAD_DIM, pe_k, w_cmp_k1[l].astype(BF), w_cmp_k2[l].astype(BF))
        v_cmp = _compress(proj, (KC_OFF + KV_WIDTH) // HEAD_DIM, pe_v,
                          w_cmp_v1[l].astype(BF), w_cmp_v2[l].astype(BF))

        later = [w_conv_out[l], w_nsa_out[l], w_out[l], w_ffn_gate[l], w_ffn_up[l], w_ffn_down[l]]
        o_nsa, (wb_conv, wb_nsa, wb_out, wb_gate, wb_up, wb_down) = _nsa_attention(proj, k_cmp, v_cmp, wb, sb, cf, later)

        z = _gated_mix(h, u_conv.reshape(m, CONV_DIM), o_nsa.reshape(m, Q_WIDTH), wb_conv, wb_nsa, w_gates)
        mix = _matmul(z, wb_out, BF, 1024, 1024).reshape(b, s, d)
        x, h2 = _post_mix(mix, x, g_post_mix[l][None, :], g_pre_ffn[l][None, :], mod3, 2, 3, 4)

        ff = _ffn_up(h2.reshape(m, d), wb_gate, wb_up)
        f = _matmul(ff, wb_down, BF, 512, 512).reshape(b, s, d)
        x = _post_ffn(f, x, g_post_ffn[l][None, :], mod3, 5)
    return x
```

```python
import functools
import math

import numpy as np
import jax
import jax.numpy as jnp
from jax import lax
from jax.experimental import pallas as pl
from jax.experimental.pallas import tpu as pltpu

F32 = jnp.float32
BF = jnp.bfloat16

CONV_DIM = 2048
CONV_WIDTH = 31
N_HEADS = 16
N_KV = 4
HPG = N_HEADS // N_KV
HEAD_DIM = 128
CMP_LEN = 32
CMP_STRIDE = 16
CMP_HID = 512
SLC_BLOCK = 64
N_SEL = 16
WINDOW = 512
Q_BLOCK = 64
REL_BUCKETS = 32
REL_MAX_DIST = 128
NORM_EPS = 1e-6
NEG = -1e30
LOG2E = 1.4426950408889634

Q_WIDTH = N_HEADS * HEAD_DIM
KV_WIDTH = N_KV * HEAD_DIM
Q_OFF = 2 * CONV_DIM
KC_OFF = Q_OFF + Q_WIDTH
NG_OFF = KC_OFF + 6 * KV_WIDTH
MG_OFF = NG_OFF + 3 * N_HEADS

LANES = 128
SUBLANES = 8
BF16_SUBLANES = 16
VMEM_LIMIT = 56 * 1024 * 1024
ATTN_VMEM_LIMIT = 61 * 1024 * 1024

CONV_HALO = 32
CONV_ROWS = 64
CONV_LANES = 256

Q_TILE = 4 * Q_BLOCK
SLC_PAD = 2 * SLC_BLOCK
SLC_LOCAL_KEYS = SLC_PAD + Q_TILE
SLC_CHUNK_BLOCKS = 16
WIN_KEYS = WINDOW + Q_TILE
WIN_PAD = WINDOW
CMP_PER_TILE = Q_TILE // CMP_STRIDE
CMP_BAND = 32
CMP_BAND_LO = 9
PAD_FLAG_COL = 64
KEY_FORCED = 0x7F000000
KEY_FUTURE = -2


def _t5_thresholds():
    max_exact = REL_BUCKETS // 2
    thr = []
    for k in range(1, REL_BUCKETS - max_exact):
        n = max_exact
        while max_exact + int(math.log(n / max_exact) / math.log(REL_MAX_DIST / max_exact)
                              * (REL_BUCKETS - max_exact)) < max_exact + k:
            n += 1
        thr.append(n)
    return tuple(thr)


T5_THRESHOLDS = _t5_thresholds()


def _cparams(*sem):
    return pltpu.CompilerParams(dimension_semantics=sem, vmem_limit_bytes=VMEM_LIMIT)


def _silu(v):
    return v * jax.nn.sigmoid(v)


def _dot(a, b):
    return jnp.dot(a, b, preferred_element_type=F32)


def _dot_nt(a, b):
    return lax.dot_general(a, b, (((1,), (1,)), ((), ())), preferred_element_type=F32)


def _rms(v, g):
    return v * lax.rsqrt(jnp.mean(v * v, axis=-1, keepdims=True) + NORM_EPS) * g


def _mod_kernel(c_ref, w_ref, b_ref, o_ref):
    a = _silu(c_ref[...]).astype(BF)
    o_ref[...] = _dot(a, w_ref[...].astype(BF)) + b_ref[...]


def _modulation(c8, w_ada, b_ada, tn=512):
    rows, d = c8.shape
    n = w_ada.shape[1]
    return pl.pallas_call(
        _mod_kernel,
        grid=(n // tn,),
        in_specs=[pl.BlockSpec((rows, d), lambda j: (0, 0)),
                  pl.BlockSpec((d, tn), lambda j: (0, j)),
                  pl.BlockSpec((1, tn), lambda j: (0, j))],
        out_specs=pl.BlockSpec((rows, tn), lambda j: (0, j)),
        out_shape=jax.ShapeDtypeStruct((rows, n), F32),
        compiler_params=_cparams("parallel"),
    )(c8, w_ada, b_ada)


def _prenorm_kernel(x_ref, g_ref, mod_ref, h_ref, *, shift_row, scale_row):
    y = _rms(x_ref[0], g_ref[...])
    h = y * (1.0 + mod_ref[0, scale_row:scale_row + 1, :]) + mod_ref[0, shift_row:shift_row + 1, :]
    h_ref[0] = h.astype(BF)


def _prenorm(x, g, mod3, shift_row, scale_row, tr=512):
    b, s, d = x.shape
    return pl.pallas_call(
        functools.partial(_prenorm_kernel, shift_row=shift_row, scale_row=scale_row),
        grid=(b, s // tr),
        in_specs=[pl.BlockSpec((1, tr, d), lambda i, r: (i, r, 0)),
                  pl.BlockSpec((1, d), lambda i, r: (0, 0)),
                  pl.BlockSpec((1, 6, d), lambda i, r: (i, 0, 0))],
        out_specs=pl.BlockSpec((1, tr, d), lambda i, r: (i, r, 0)),
        out_shape=jax.ShapeDtypeStruct((b, s, d), BF),
        compiler_params=_cparams("parallel", "parallel"),
    )(x, g, mod3)


def _post_mix_kernel(y_ref, x_ref, gp_ref, gn_ref, mod_ref, xo_ref, h_ref, *, gate_row, shift_row, scale_row):
    x1 = x_ref[0] + mod_ref[0, gate_row:gate_row + 1, :] * _rms(y_ref[0].astype(F32), gp_ref[...])
    xo_ref[0] = x1
    h = _rms(x1, gn_ref[...]) * (1.0 + mod_ref[0, scale_row:scale_row + 1, :]) \
        + mod_ref[0, shift_row:shift_row + 1, :]
    h_ref[0] = h.astype(BF)


def _post_mix(y, x, g_post, g_next, mod3, gate_row, shift_row, scale_row, tr=256):
    b, s, d = x.shape
    row = pl.BlockSpec((1, tr, d), lambda i, r: (i, r, 0))
    vec = pl.BlockSpec((1, d), lambda i, r: (0, 0))
    return pl.pallas_call(
        functools.partial(_post_mix_kernel, gate_row=gate_row, shift_row=shift_row, scale_row=scale_row),
        grid=(b, s // tr),
        in_specs=[row, row, vec, vec, pl.BlockSpec((1, 6, d), lambda i, r: (i, 0, 0))],
        out_specs=[row, row],
        out_shape=[jax.ShapeDtypeStruct((b, s, d), F32), jax.ShapeDtypeStruct((b, s, d), BF)],
        compiler_params=_cparams("parallel", "parallel"),
    )(y, x, g_post, g_next, mod3)


def _post_ffn_kernel(y_ref, x_ref, gp_ref, mod_ref, xo_ref, *, gate_row):
    xo_ref[0] = x_ref[0] + mod_ref[0, gate_row:gate_row + 1, :] * _rms(y_ref[0].astype(F32), gp_ref[...])


def _post_ffn(y, x, g_post, mod3, gate_row, tr=256):
    b, s, d = x.shape
    row = pl.BlockSpec((1, tr, d), lambda i, r: (i, r, 0))
    return pl.pallas_call(
        functools.partial(_post_ffn_kernel, gate_row=gate_row),
        grid=(b, s // tr),
        in_specs=[row, row, pl.BlockSpec((1, d), lambda i, r: (0, 0)),
                  pl.BlockSpec((1, 6, d), lambda i, r: (i, 0, 0))],
        out_specs=row,
        out_shape=jax.ShapeDtypeStruct((b, s, d), F32),
        compiler_params=_cparams("parallel", "parallel"),
    )(y, x, g_post, mod3)


def _mm_scale_kernel(a_ref, w_ref, s_ref, o_ref):
    o_ref[...] = (_dot(a_ref[...], w_ref[...]) * s_ref[...]).astype(o_ref.dtype)


def _mm_kernel(a_ref, w_ref, o_ref):
    o_ref[...] = _dot(a_ref[...], w_ref[...]).astype(o_ref.dtype)


def _matmul(a, w, out_dtype, tm, tn, col_scale=None):
    m, k = a.shape
    n = w.shape[1]
    in_specs = [pl.BlockSpec((tm, k), lambda i, j: (i, 0)),
                pl.BlockSpec((k, tn), lambda i, j: (0, j))]
    args = [a, w]
    body = _mm_kernel
    if col_scale is not None:
        in_specs.append(pl.BlockSpec((1, tn), lambda i, j: (0, j)))
        args.append(col_scale)
        body = _mm_scale_kernel
    return pl.pallas_call(
        body,
        grid=(m // tm, n // tn),
        in_specs=in_specs,
        out_specs=pl.BlockSpec((tm, tn), lambda i, j: (i, j)),
        out_shape=jax.ShapeDtypeStruct((m, n), out_dtype),
        compiler_params=_cparams("parallel", "arbitrary"),
    )(*args)


def _mm_tail_kernel(a_ref, w_ref, wt_ref, s_ref, o_ref, *, n_main):
    j = pl.program_id(1)

    @pl.when(j < n_main)
    def _main():
        o_ref[...] = (_dot(a_ref[...], w_ref[...]) * s_ref[...]).astype(o_ref.dtype)

    @pl.when(j >= n_main)
    def _tail():
        o_ref[...] = (_dot(a_ref[...], wt_ref[...]) * s_ref[...]).astype(o_ref.dtype)


def _matmul_with_tail(a, w, n_w, w_tail, col_scale, out_dtype, tm, tn):
    m, k = a.shape
    n_main = n_w // tn
    n_tail = w_tail.shape[1] // tn
    n = n_w + w_tail.shape[1]
    return pl.pallas_call(
        functools.partial(_mm_tail_kernel, n_main=n_main),
        grid=(m // tm, n_main + n_tail),
        in_specs=[pl.BlockSpec((tm, k), lambda i, j: (i, 0)),
                  pl.BlockSpec((k, tn), lambda i, j: (0, jnp.minimum(j, n_main - 1))),
                  pl.BlockSpec((k, tn), lambda i, j: (0, jnp.maximum(j - n_main, 0)), pipeline_mode=pl.Buffered(1)),
                  pl.BlockSpec((1, tn), lambda i, j: (0, j))],
        out_specs=pl.BlockSpec((tm, tn), lambda i, j: (i, j)),
        out_shape=jax.ShapeDtypeStruct((m, n), out_dtype),
        compiler_params=_cparams("parallel", "arbitrary"),
    )(a, w, w_tail, col_scale)


def _mix_kernel(h_ref, a1_ref, a2_ref, w1_ref, w2_ref, wga_ref, wgb_ref, o_ref):
    h = h_ref[...]
    ya = _dot(a1_ref[...], w1_ref[...])
    yb = _dot(a2_ref[...], w2_ref[...])
    o = jax.nn.sigmoid(_dot(h, wga_ref[...])) * ya + jax.nn.sigmoid(_dot(h, wgb_ref[...])) * yb
    o_ref[...] = o.astype(o_ref.dtype)


def _gated_mix(h, u_conv, o_nsa, w_conv_out, w_nsa_out, w_gates, tm=1024, tn=256):
    m, k = h.shape
    ka = u_conv.shape[1]
    kb = o_nsa.shape[1]
    d_model = w_conv_out.shape[1]
    gb_block = d_model // tn
    return pl.pallas_call(
        _mix_kernel,
        grid=(m // tm, d_model // tn),
        in_specs=[pl.BlockSpec((tm, k), lambda i, j: (i, 0)),
                  pl.BlockSpec((tm, ka), lambda i, j: (i, 0)),
                  pl.BlockSpec((tm, kb), lambda i, j: (i, 0)),
                  pl.BlockSpec((ka, tn), lambda i, j: (0, j)),
                  pl.BlockSpec((kb, tn), lambda i, j: (0, j)),
                  pl.BlockSpec((k, tn), lambda i, j: (0, j)),
                  pl.BlockSpec((k, tn), lambda i, j: (0, j + gb_block))],
        out_specs=pl.BlockSpec((tm, tn), lambda i, j: (i, j)),
        out_shape=jax.ShapeDtypeStruct((m, d_model), BF),
        compiler_params=_cparams("parallel", "arbitrary"),
    )(h, u_conv, o_nsa, w_conv_out, w_nsa_out, w_gates, w_gates)


def _ffn_up_kernel(a_ref, wg_ref, wu_ref, o_ref):
    a = a_ref[...]
    o_ref[...] = (_silu(_dot(a, wg_ref[...])) * _dot(a, wu_ref[...])).astype(o_ref.dtype)


def _ffn_up(h, w_gate, w_up, tm=2048, tn=256):
    m, k = h.shape
    n = w_gate.shape[1]
    wspec = pl.BlockSpec((k, tn), lambda i, j: (0, j))
    return pl.pallas_call(
        _ffn_up_kernel,
        grid=(m // tm, n // tn),
        in_specs=[pl.BlockSpec((tm, k), lambda i, j: (i, 0)), wspec, wspec],
        out_specs=pl.BlockSpec((tm, tn), lambda i, j: (i, j)),
        out_shape=jax.ShapeDtypeStruct((m, n), BF),
        compiler_params=_cparams("parallel", "arbitrary"),
    )(h, w_gate, w_up)


def _conv_kernel(a_ref, b_ref, ah_ref, bh_ref, w_ref, bdw_ref, g_ref, bb_ref, o_ref, u_ref, sh_ref, v_ref, *, ts):
    i = pl.program_id(1)
    u_ref[CONV_HALO:CONV_HALO + ts, :] = a_ref[0].astype(F32) * jax.nn.sigmoid(b_ref[0].astype(F32))
    uh = ah_ref[0].astype(F32) * jax.nn.sigmoid(bh_ref[0].astype(F32))
    u_ref[0:CONV_HALO, :] = jnp.where(i > 0, uh, 0.0)

    rc, cc = CONV_ROWS, CONV_LANES
    first = CONV_HALO - (CONV_WIDTH - 1)
    span = ts + CONV_HALO - SUBLANES
    for c0 in range(0, CONV_DIM, cc):
        for s in range(1, SUBLANES):
            sh_ref[s - 1] = u_ref[s:s + span, c0:c0 + cc]

        def row_chunk(rb, carry, c0=c0):
            r0 = pl.multiple_of(rb * rc, rc)
            acc = jnp.zeros((rc, cc), F32)
            for k in range(CONV_WIDTH):
                s = (first + k) % SUBLANES
                base = first + k - s
                if s == 0:
                    tap = u_ref[pl.ds(r0 + base, rc), c0:c0 + cc]
                else:
                    tap = sh_ref[s - 1, pl.ds(r0 + base, rc), :]
                acc = acc + tap * w_ref[k:k + 1, c0:c0 + cc]
            v_ref[pl.ds(r0, rc), c0:c0 + cc] = acc + bdw_ref[:, c0:c0 + cc]
            return carry

        lax.fori_loop(0, ts // rc, row_chunk, 0)

    v = v_ref[...]
    mu = jnp.mean(v, axis=-1, keepdims=True)
    vc = v - mu
    y = vc * lax.rsqrt(jnp.mean(vc * vc, axis=-1, keepdims=True) + NORM_EPS) * g_ref[...] + bb_ref[...]
    o_ref[0] = _silu(y).astype(o_ref.dtype)


def _conv_module(proj, w_dw, b_dw, ln_g, ln_b, ts=256):
    b, s, _ = proj.shape
    hb = ts // CONV_HALO
    cur_a = pl.BlockSpec((1, ts, CONV_DIM), lambda i, r: (i, r, 0))
    cur_b = pl.BlockSpec((1, ts, CONV_DIM), lambda i, r: (i, r, 1))
    halo_a = pl.BlockSpec((1, CONV_HALO, CONV_DIM), lambda i, r: (i, jnp.maximum(r * hb - 1, 0), 0))
    halo_b = pl.BlockSpec((1, CONV_HALO, CONV_DIM), lambda i, r: (i, jnp.maximum(r * hb - 1, 0), 1))
    vec = pl.BlockSpec((1, CONV_DIM), lambda i, r: (0, 0))
    return pl.pallas_call(
        functools.partial(_conv_kernel, ts=ts),
        grid=(b, s // ts),
        in_specs=[cur_a, cur_b, halo_a, halo_b,
                  pl.BlockSpec((CONV_WIDTH, CONV_DIM), lambda i, r: (0, 0)), vec, vec, vec],
        out_specs=pl.BlockSpec((1, ts, CONV_DIM), lambda i, r: (i, r, 0)),
        out_shape=jax.ShapeDtypeStruct((b, s, CONV_DIM), BF),
        scratch_shapes=[pltpu.VMEM((CONV_HALO + ts, CONV_DIM), F32),
                        pltpu.VMEM((SUBLANES - 1, ts + CONV_HALO - SUBLANES, CONV_LANES), F32),
                        pltpu.VMEM((ts, CONV_DIM), F32)],
        compiler_params=_cparams("parallel", "arbitrary"),
    )(proj, proj, proj, proj, w_dw, b_dw, ln_g, ln_b)


def _compress_kernel(k_ref, pe_ref, w1_ref, w2_ref, o_ref, kf_ref):
    kf_ref[...] = k_ref[0].astype(F32)
    rows = kf_ref.shape[0] // CMP_STRIDE
    lo = jnp.zeros((rows, CMP_HID), F32)
    hi = jnp.zeros((rows, CMP_HID), F32)
    for l in range(CMP_STRIDE):
        tok = kf_ref[pl.ds(l, rows, stride=CMP_STRIDE), :].astype(BF)
        lo = lo + _dot(tok, w1_ref[l * HEAD_DIM:(l + 1) * HEAD_DIM, :])
        hi = hi + _dot(tok, w1_ref[(CMP_STRIDE + l) * HEAD_DIM:(CMP_STRIDE + l + 1) * HEAD_DIM, :])
    pe = _dot(pe_ref[...], w1_ref[...])
    hid = lo + pltpu.roll(hi, rows - 1, axis=0) + pe[0:1, :]
    act = 0.5 * hid * (1.0 + jnp.tanh(math.sqrt(2.0 / math.pi) * (hid + 0.044715 * hid * hid * hid)))
    o_ref[0] = _dot(act.astype(BF), w2_ref[...])


def _compress(proj, col_block, pe8, w1, w2):
    b, s, _ = proj.shape
    rows = s // CMP_STRIDE
    return pl.pallas_call(
        _compress_kernel,
        grid=(b, N_KV),
        in_specs=[pl.BlockSpec((1, s, HEAD_DIM), lambda i, g: (i, 0, col_block + g)),
                  pl.BlockSpec(pe8.shape, lambda i, g: (0, 0)),
                  pl.BlockSpec(w1.shape, lambda i, g: (0, 0)),
                  pl.BlockSpec(w2.shape, lambda i, g: (0, 0))],
        out_specs=pl.BlockSpec((1, rows, HEAD_DIM), lambda i, g: (i * N_KV + g, 0, 0)),
        out_shape=jax.ShapeDtypeStruct((b * N_KV, rows, HEAD_DIM), F32),
        scratch_shapes=[pltpu.VMEM((s, HEAD_DIM), F32)],
        compiler_params=_cparams("parallel", "parallel"),
    )(proj, pe8, w1, w2)


def _bias_tables_kernel(rb_ref, wb_ref, sb_ref, cf_ref):
    h = pl.program_id(0)
    last = rb_ref[REL_BUCKETS - 1, h]

    def rel(dist):
        n = jnp.maximum(dist, 0)
        bucket = jnp.full(n.shape, REL_BUCKETS // 2, jnp.int32)
        for t in T5_THRESHOLDS:
            bucket = bucket + (n >= t).astype(jnp.int32)
        bucket = jnp.where(n < REL_BUCKETS // 2, n, jnp.minimum(bucket, REL_BUCKETS - 1))
        out = jnp.zeros(n.shape, F32)
        for bk in range(REL_BUCKETS - 1):
            out = jnp.where(bucket == bk, rb_ref[bk, h] - last, out)
        return out * LOG2E

    def dist_of(shape, offset):
        qi = lax.broadcasted_iota(jnp.int32, shape, 0)
        kl = lax.broadcasted_iota(jnp.int32, shape, 1)
        return qi + offset - kl

    dw = dist_of((Q_TILE, WIN_KEYS), WIN_PAD)
    wb_ref[0] = jnp.where((dw >= 0) & (dw < WINDOW), rel(dw), NEG)

    ds_ = dist_of((Q_TILE, SLC_LOCAL_KEYS), SLC_PAD)
    sb_ref[0] = jnp.where(ds_ >= 0, rel(ds_), NEG)

    qi = lax.broadcasted_iota(jnp.int32, (Q_TILE, LANES), 0)
    col = lax.broadcasted_iota(jnp.int32, (Q_TILE, LANES), 1)
    mi = col & (CMP_BAND - 1)
    part = col >> (CMP_BAND.bit_length() - 1)
    dc = qi - CMP_STRIDE * (mi - CMP_BAND_LO) - (CMP_LEN - 1)
    val = jnp.where(dc >= 0, rel(dc), 0.0)
    p0 = val.astype(BF).astype(F32)
    r1 = val - p0
    p1 = r1.astype(BF).astype(F32)
    p2 = r1 - p1
    parts = jnp.where(part == 0, p0, jnp.where(part == 1, p1, jnp.where(part == 2, p2, 0.0)))
    cf_ref[0] = parts.astype(BF)


def _bias_tables(rel_bias):
    smem = pl.BlockSpec(memory_space=pltpu.SMEM)
    lk = SLC_LOCAL_KEYS
    return pl.pallas_call(
        _bias_tables_kernel,
        grid=(N_HEADS,),
        in_specs=[smem],
        out_specs=[pl.BlockSpec((1, Q_TILE, WIN_KEYS), lambda h: (h, 0, 0)),
                   pl.BlockSpec((1, Q_TILE, lk), lambda h: (h, 0, 0)),
                   pl.BlockSpec((1, Q_TILE, LANES), lambda h: (h, 0, 0))],
        out_shape=[jax.ShapeDtypeStruct((N_HEADS, Q_TILE, WIN_KEYS), F32),
                   jax.ShapeDtypeStruct((N_HEADS, Q_TILE, lk), F32),
                   jax.ShapeDtypeStruct((N_HEADS, Q_TILE, LANES), BF)],
        compiler_params=_cparams("arbitrary"),
    )(rel_bias)


def _softmax_parts(s):
    m = jnp.max(s, axis=-1, keepdims=True)
    e = jnp.exp2(s - m)
    return m, e, jnp.sum(e, axis=-1, keepdims=True)


def _lane_tiles(v, op):
    out = v[:, 0:LANES]
    for t in range(1, v.shape[1] // LANES):
        out = op(out, v[:, t * LANES:(t + 1) * LANES])
    return out


def _attn_kernel(*refs, seq, n_cast):
    (q_ref, ks_ref, vs_ref, kw_ref, vw_ref, kc_ref, vc_ref, ng_ref,
     wb_ref, sb_ref, cf_ref, ovt_ref, eye_ref, spread_ref) = refs[:N_ATTN_IN]
    cast_in = refs[N_ATTN_IN:N_ATTN_IN + n_cast]
    o_ref = refs[N_ATTN_IN + n_cast]
    cast_out = refs[N_ATTN_IN + n_cast + 1:N_ATTN_IN + 2 * n_cast + 1]
    kaug_ref, vsp_ref, kwp_ref, vwp_ref, key_ref, sfar_ref = refs[N_ATTN_IN + 2 * n_cast + 1:]

    for w_ref, wo_ref in zip(cast_in, cast_out):
        wo_ref[...] = w_ref[...].astype(BF)

    tile = pl.program_id(2)
    n_slc = seq // SLC_BLOCK
    ncp = seq // CMP_STRIDE
    rows = HPG * Q_TILE
    n_sel = min(N_SEL, n_slc)
    if n_slc != PAD_FLAG_COL or Q_TILE % LANES or 3 * CMP_BAND > LANES:
        raise NotImplementedError("selection layout expects 64 selection blocks and whole-lane query tiles")
    assert CMP_STRIDE * (CMP_BAND_LO + 1) - (CMP_LEN - 1) >= T5_THRESHOLDS[-1]
    assert CMP_STRIDE * (CMP_BAND - CMP_BAND_LO) + CMP_LEN - 1 >= Q_TILE

    @pl.when(tile == 0)
    def _stage_keys():
        col = lax.broadcasted_iota(jnp.int32, (SLC_PAD, LANES), 1)
        kaug_ref[0:SLC_PAD, 0:HEAD_DIM] = jnp.zeros((SLC_PAD, HEAD_DIM), BF)
        kaug_ref[0:SLC_PAD, HEAD_DIM:] = jnp.where(col == PAD_FLAG_COL, NEG, 0.0).astype(BF)
        kaug_ref[SLC_PAD:, 0:HEAD_DIM] = ks_ref[0]
        blk = lax.broadcasted_iota(jnp.int32, (seq, LANES), 0) >> (SLC_BLOCK.bit_length() - 1)
        colk = lax.broadcasted_iota(jnp.int32, (seq, LANES), 1)
        kaug_ref[SLC_PAD:, HEAD_DIM:] = jnp.where(colk == blk, NEG, 0.0).astype(BF)
        vsp_ref[:, HEAD_DIM:] = jnp.ones((SLC_PAD + seq, LANES), BF)
        vsp_ref[0:SLC_PAD, 0:HEAD_DIM] = jnp.zeros((SLC_PAD, HEAD_DIM), BF)
        vsp_ref[SLC_PAD:, 0:HEAD_DIM] = vs_ref[0]
        kwp_ref[0:WIN_PAD, :] = jnp.zeros((WIN_PAD, HEAD_DIM), BF)
        kwp_ref[WIN_PAD:, :] = kw_ref[0]
        vwp_ref[:, HEAD_DIM:] = jnp.ones((WIN_PAD + seq, LANES), BF)
        vwp_ref[0:WIN_PAD, 0:HEAD_DIM] = jnp.zeros((WIN_PAD, HEAD_DIM), BF)
        vwp_ref[WIN_PAD:, 0:HEAD_DIM] = vw_ref[0]

    q4 = jnp.concatenate([q_ref[0, :, hh * HEAD_DIM:(hh + 1) * HEAD_DIM] for hh in range(HPG)], axis=0)
    t0 = tile * Q_TILE
    r0 = pl.multiple_of(t0, Q_TILE)

    gts = jax.nn.sigmoid(ng_ref[0].astype(F32))
    g_hi = gts.astype(BF)
    g_lo = (gts - g_hi.astype(F32)).astype(BF)
    spread = spread_ref[...]
    gsp = _dot(g_hi, spread) + _dot(g_lo, spread)

    key_w = lax.broadcasted_iota(jnp.int32, (1, WIN_KEYS), 1)
    before_start = jnp.where(key_w < WIN_PAD - t0, NEG, 0.0)
    s_w = _dot_nt(q4, kwp_ref[pl.ds(r0, WIN_KEYS), :]) + wb_ref[...].reshape(rows, WIN_KEYS) + before_start
    e_w = jnp.exp2(s_w - jnp.max(s_w, axis=-1, keepdims=True))
    pv_w = _dot(e_w.astype(BF), vwp_ref[pl.ds(r0, WIN_KEYS), :])
    o_win = pv_w[:, 0:HEAD_DIM] * (1.0 / pv_w[:, HEAD_DIM:])

    kc = kc_ref[0].astype(BF)
    vc = vc_ref[0].astype(BF)
    r_io = lax.broadcasted_iota(jnp.int32, (LANES, ncp), 0)
    n_io = lax.broadcasted_iota(jnp.int32, (LANES, ncp), 1)
    band = (r_io < 3 * CMP_BAND) & (n_io == CMP_PER_TILE * tile + (r_io & (CMP_BAND - 1)) - CMP_BAND_LO)
    shift = jnp.where(band, 1.0, 0.0).astype(BF)
    cf4 = cf_ref[...].reshape(rows, LANES)
    lc = _dot_nt(q4, kc) + _dot(cf4, shift)
    qi_c = lax.broadcasted_iota(jnp.int32, (Q_TILE, ncp), 0)
    n_c = lax.broadcasted_iota(jnp.int32, (Q_TILE, ncp), 1)
    vis = (CMP_STRIDE * n_c + (CMP_LEN - 1)) <= (t0 + qi_c)
    lc3 = jnp.where(vis[None], lc.reshape(HPG, Q_TILE, ncp), NEG)
    _, e_c, l_c = _softmax_parts(lc3)
    p_c = jnp.where(vis[None], e_c * (1.0 / l_c), 0.0)
    o_cmp = _dot(p_c.reshape(rows, ncp).astype(BF), vc)

    psum = p_c[0]
    for hh in range(1, HPG):
        psum = psum + p_c[hh]
    p0 = psum.astype(BF)
    r1 = psum - p0.astype(F32)
    p1 = r1.astype(BF)
    p2 = (r1 - p1.astype(F32)).astype(BF)
    ovt = ovt_ref[...]
    imp_t = _dot_nt(ovt, p0) + _dot_nt(ovt, p1) + _dot_nt(ovt, p2)

    def gate(hh, branch):
        c = 3 * hh + branch
        return gsp[:, c * LANES:(c + 1) * LANES]

    gated_cw = [gate(hh, 0) * o_cmp[hh * Q_TILE:(hh + 1) * Q_TILE] + gate(hh, 2) * o_win[hh * Q_TILE:(hh + 1) * Q_TILE]
                for hh in range(HPG)]

    blk_t = lax.broadcasted_iota(jnp.int32, (n_slc, Q_TILE), 0)
    q_t = lax.broadcasted_iota(jnp.int32, (n_slc, Q_TILE), 1)
    cur_t = (t0 + q_t) >> (SLC_BLOCK.bit_length() - 1)
    forced_t = (blk_t == 0) | (blk_t == cur_t) | (blk_t == cur_t - 1)
    causal_t = blk_t <= cur_t
    key = jnp.where(forced_t, KEY_FORCED, jnp.where(causal_t, lax.bitcast_convert_type(imp_t, jnp.int32), KEY_FUTURE))
    key_ref[...] = key
    groups = [key[8 * a:8 * a + 8, :] for a in range(n_slc // 8)]
    ranks = [jnp.zeros((8, Q_TILE), jnp.int32) for _ in groups]
    sub = lax.broadcasted_iota(jnp.int32, (8, Q_TILE), 0)
    for bb in range(n_slc):
        kb = key_ref[bb:bb + 1, :]
        kb1 = kb + 1
        for a in range(n_slc // 8):
            if 8 * a > bb:
                ahead = kb1
            elif 8 * a + 7 < bb:
                ahead = kb
            else:
                ahead = jnp.where(sub > bb - 8 * a, kb1, kb)
            ranks[a] = ranks[a] + (ahead > groups[a]).astype(jnp.int32)
    rank = jnp.concatenate(ranks, axis=0)
    sel_t = (rank < n_sel) & causal_t
    first_local = (t0 - SLC_PAD) >> (SLC_BLOCK.bit_length() - 1)
    far_t = blk_t < first_local
    flag_rows = jnp.where(blk_t == 0, 1.0, 0.0)
    stack = jnp.concatenate([jnp.where(sel_t, 0.0, 1.0), flag_rows,
                             jnp.where(sel_t & far_t, 0.0, 1.0), flag_rows], axis=0).astype(BF)
    qmask = _dot_nt(eye_ref[...], stack).astype(BF)
    qa_loc = jnp.concatenate([q4, qmask[:, 0:LANES]], axis=1)
    qa_far = jnp.concatenate([q4, qmask[:, LANES:]], axis=1)

    lk = SLC_LOCAL_KEYS
    chunk = SLC_CHUNK_BLOCKS * SLC_BLOCK
    n_far = (first_local + SLC_CHUNK_BLOCKS - 1) // SLC_CHUNK_BLOCKS

    s_loc = _dot_nt(qa_loc, kaug_ref[pl.ds(r0, lk), :]) + sb_ref[...].reshape(rows, lk)

    def far_logits(c, m_part):
        rr = pl.multiple_of(SLC_PAD + c * chunk, SLC_BLOCK)
        s = _dot_nt(qa_far, kaug_ref[pl.ds(rr, chunk), :])
        sfar_ref[c] = s
        return jnp.maximum(m_part, _lane_tiles(s, jnp.maximum))

    m_part = lax.fori_loop(0, n_far, far_logits, _lane_tiles(s_loc, jnp.maximum))
    m_s = jnp.max(m_part, axis=-1, keepdims=True)

    e_loc = jnp.exp2(s_loc - m_s)
    acc_s = _dot(e_loc.astype(BF), vsp_ref[pl.ds(r0, lk), :])

    def far_values(c, acc):
        rr = pl.multiple_of(SLC_PAD + c * chunk, SLC_BLOCK)
        e = jnp.exp2(sfar_ref[c] - m_s)
        return acc + _dot(e.astype(BF), vsp_ref[pl.ds(rr, chunk), :])

    acc_s = lax.fori_loop(0, n_far, far_values, acc_s)
    o_slc = acc_s[:, 0:HEAD_DIM] * (1.0 / acc_s[:, HEAD_DIM:])

    for hh in range(HPG):
        o_h = gated_cw[hh] + gate(hh, 1) * o_slc[hh * Q_TILE:(hh + 1) * Q_TILE]
        o_ref[0, :, hh * HEAD_DIM:(hh + 1) * HEAD_DIM] = o_h.astype(o_ref.dtype)


N_ATTN_IN = 14


def _nsa_attention(proj, k_cmp, v_cmp, wb, sb, cf, to_cast):
    b, s, _ = proj.shape
    n_steps = b * N_KV * (s // Q_TILE)
    cast_specs = []
    for w in to_cast:
        slab = next(r for r in range(BF16_SUBLANES, w.shape[0] + 1, BF16_SUBLANES)
                    if w.shape[0] % r == 0 and w.shape[0] // r <= n_steps)
        last = w.shape[0] // slab - 1
        cast_specs.append(pl.BlockSpec(
            (slab, w.shape[1]),
            lambda i, g, j, last=last: (jnp.minimum((i * N_KV + g) * (s // Q_TILE) + j, last), 0)))
    ovt = _overlap_matrix(s)
    eye = jnp.asarray(np.tile(np.eye(Q_TILE), (HPG, 1)), BF)
    spread = jnp.asarray(np.repeat(np.eye(LANES)[:, :3 * HPG], LANES, axis=1), BF)
    n_chunks = -(-(s // SLC_BLOCK) // SLC_CHUNK_BLOCKS)
    q_blk = Q_OFF // (HPG * HEAD_DIM)
    kv_blk = KC_OFF // HEAD_DIM
    ng_blk = NG_OFF // LANES
    lk = SLC_LOCAL_KEYS
    ncp = s // CMP_STRIDE

    def kv_spec(idx):
        return pl.BlockSpec((1, s, HEAD_DIM), lambda i, g, j: (i, 0, kv_blk + idx * N_KV + g))

    cmp_spec = pl.BlockSpec((1, ncp, HEAD_DIM), lambda i, g, j: (i * N_KV + g, 0, 0))
    outs = pl.pallas_call(
        functools.partial(_attn_kernel, seq=s, n_cast=len(to_cast)),
        grid=(b, N_KV, s // Q_TILE),
        in_specs=[pl.BlockSpec((1, Q_TILE, HPG * HEAD_DIM), lambda i, g, j: (i, j, q_blk + g)),
                  kv_spec(2), kv_spec(3), kv_spec(4), kv_spec(5),
                  cmp_spec, cmp_spec,
                  pl.BlockSpec((1, Q_TILE, LANES), lambda i, g, j: (i, j, ng_blk + g)),
                  pl.BlockSpec((HPG, Q_TILE, WIN_KEYS), lambda i, g, j: (g, 0, 0)),
                  pl.BlockSpec((HPG, Q_TILE, lk), lambda i, g, j: (g, 0, 0)),
                  pl.BlockSpec((HPG, Q_TILE, LANES), lambda i, g, j: (g, 0, 0)),
                  pl.BlockSpec(ovt.shape, lambda i, g, j: (0, 0)),
                  pl.BlockSpec(eye.shape, lambda i, g, j: (0, 0)),
                  pl.BlockSpec(spread.shape, lambda i, g, j: (0, 0))] + cast_specs,
        out_specs=[pl.BlockSpec((1, Q_TILE, HPG * HEAD_DIM), lambda i, g, j: (i, j, g))] + cast_specs,
        out_shape=[jax.ShapeDtypeStruct((b, s, Q_WIDTH), BF)]
        + [jax.ShapeDtypeStruct(w.shape, BF) for w in to_cast],
        scratch_shapes=[pltpu.VMEM((SLC_PAD + s, 2 * HEAD_DIM), BF),
                        pltpu.VMEM((SLC_PAD + s, HEAD_DIM + LANES), BF),
                        pltpu.VMEM((WIN_PAD + s, HEAD_DIM), BF),
                        pltpu.VMEM((WIN_PAD + s, HEAD_DIM + LANES), BF),
                        pltpu.VMEM((s // SLC_BLOCK, Q_TILE), jnp.int32),
                        pltpu.VMEM((n_chunks, HPG * Q_TILE, SLC_CHUNK_BLOCKS * SLC_BLOCK), F32)],
        compiler_params=pltpu.CompilerParams(dimension_semantics=("arbitrary",) * 3,
                                             vmem_limit_bytes=ATTN_VMEM_LIMIT),
    )(proj, proj, proj, proj, proj, k_cmp, v_cmp, proj, wb, sb, cf, ovt, eye, spread, *to_cast)
    return outs[0], outs[1:]


def _overlap_matrix(s):
    ncp = s // CMP_STRIDE
    n_slc = s // SLC_BLOCK
    nc = (s - CMP_LEN) // CMP_STRIDE + 1
    i = np.arange(ncp)[None, :]
    jj = np.arange(n_slc)[:, None]
    ov = ((i * CMP_STRIDE < (jj + 1) * SLC_BLOCK) & (i * CMP_STRIDE + CMP_LEN > jj * SLC_BLOCK) & (i < nc))
    return jnp.asarray(ov, BF)


def kernel(x, c, w_ada, b_ada, g_pre_mix, g_post_mix, g_pre_ffn, g_post_ffn, w_in, w_dw, b_dw, conv_ln_g, conv_ln_b, w_conv_out, cmp_pe_k, cmp_pe_v, w_cmp_k1, w_cmp_k2, w_cmp_v1, w_cmp_v2, rel_bias, w_nsa_out, w_out, w_ffn_gate, w_ffn_up, w_ffn_down):
    b, s, d = x.shape
    m = b * s
    depth = w_ada.shape[0]
    c8 = jnp.zeros((8, d), F32).at[:b].set(c)
    wb, sb, cf = _bias_tables(rel_bias)
    col_scale = jnp.ones((1, NG_OFF + N_KV * LANES), F32).at[:, Q_OFF:KC_OFF].set(HEAD_DIM ** -0.5 * LOG2E)

    for l in range(depth):
        mod = _modulation(c8, w_ada[l], b_ada[l][None, :])
        mod3 = mod[:b].reshape(b, 6, d)

        w_all = w_in[l].astype(BF)
        w_ng = w_in[l][:, NG_OFF:MG_OFF].reshape(d, N_KV, 3 * HPG)
        w_ng = jnp.pad(w_ng, ((0, 0), (0, 0), (0, LANES - 3 * HPG))).reshape(d, N_KV * LANES)
        w_ng = w_ng.astype(BF)
        w_gates = w_all[:, MG_OFF:]

        h = _prenorm(x, g_pre_mix[l][None, :], mod3, 0, 1).reshape(m, d)
        proj = _matmul_with_tail(h, w_all, NG_OFF, w_ng, col_scale, BF, 2048, 512).reshape(b, s, -1)

        u_conv = _conv_module(proj, w_dw[l], b_dw[l][None, :], conv_ln_g[l][None, :], conv_ln_b[l][None, :])

        pe_k = jnp.broadcast_to(cmp_pe_k[l].reshape(1, -1), (8, CMP_LEN * HEAD_DIM)).astype(BF)
        pe_v = jnp.broadcast_to(cmp_pe_v[l].reshape(1, -1), (8, CMP_LEN * HEAD_DIM)).astype(BF)
        k_cmp = _compress(proj, KC_OFF // HEAD_DIM, pe_k, w_cmp_k1[l].astype(BF), w_cmp_k2[l].astype(BF))
        v_cmp = _compress(proj, (KC_OFF + KV_WIDTH) // HEAD_DIM, pe_v,
                          w_cmp_v1[l].astype(BF), w_cmp_v2[l].astype(BF))

        later = [w_conv_out[l], w_nsa_out[l], w_out[l], w_ffn_gate[l], w_ffn_up[l], w_ffn_down[l]]
        o_nsa, (wb_conv, wb_nsa, wb_out, wb_gate, wb_up, wb_down) = _nsa_attention(proj, k_cmp, v_cmp, wb, sb, cf, later)

        z = _gated_mix(h, u_conv.reshape(m, CONV_DIM), o_nsa.reshape(m, Q_WIDTH), wb_conv, wb_nsa, w_gates)
        mix = _matmul(z, wb_out, BF, 1024, 1024).reshape(b, s, d)
        x, h2 = _post_mix(mix, x, g_post_mix[l][None, :], g_pre_ffn[l][None, :], mod3, 2, 3, 4)

        ff = _ffn_up(h2.reshape(m, d), wb_gate, wb_up)
        f = _matmul(ff, wb_down, BF, 512, 512).reshape(b, s, d)
        x = _post_ffn(f, x, g_post_ffn[l][None, :], mod3, 5)
    return x
```

```python
import functools
import math

import numpy as np
import jax
import jax.numpy as jnp
from jax import lax
from jax.experimental import pallas as pl
from jax.experimental.pallas import tpu as pltpu

F32 = jnp.float32
BF = jnp.bfloat16

CONV_DIM = 2048
CONV_WIDTH = 31
N_HEADS = 16
N_KV = 4
HPG = N_HEADS // N_KV
HEAD_DIM = 128
CMP_LEN = 32
CMP_STRIDE = 16
CMP_HID = 512
SLC_BLOCK = 64
N_SEL = 16
WINDOW = 512
Q_BLOCK = 64
REL_BUCKETS = 32
REL_MAX_DIST = 128
NORM_EPS = 1e-6
NEG = -1e30
LOG2E = 1.4426950408889634

Q_WIDTH = N_HEADS * HEAD_DIM
KV_WIDTH = N_KV * HEAD_DIM
Q_OFF = 2 * CONV_DIM
KC_OFF = Q_OFF + Q_WIDTH
NG_OFF = KC_OFF + 6 * KV_WIDTH
MG_OFF = NG_OFF + 3 * N_HEADS

LANES = 128
SUBLANES = 8
BF16_SUBLANES = 16
PAIR_TILE = 256
VMEM_LIMIT = 56 * 1024 * 1024
ATTN_VMEM_LIMIT = 61 * 1024 * 1024

CONV_HALO = 32
CONV_ROWS = 64
CONV_LANES = 256

Q_TILE = 4 * Q_BLOCK
SLC_PAD = 2 * SLC_BLOCK
SLC_LOCAL_KEYS = SLC_PAD + Q_TILE
SLC_CHUNK_BLOCKS = 16
WIN_KEYS = WINDOW + Q_TILE
WIN_PAD = WINDOW
CMP_PER_TILE = Q_TILE // CMP_STRIDE
CMP_BAND = 32
CMP_BAND_LO = 9
PAD_FLAG_COL = 64
KEY_FORCED = 0x7F000000
KEY_FUTURE = -2


def _t5_thresholds():
    max_exact = REL_BUCKETS // 2
    thr = []
    for k in range(1, REL_BUCKETS - max_exact):
        n = max_exact
        while max_exact + int(math.log(n / max_exact) / math.log(REL_MAX_DIST / max_exact)
                              * (REL_BUCKETS - max_exact)) < max_exact + k:
            n += 1
        thr.append(n)
    return tuple(thr)


T5_THRESHOLDS = _t5_thresholds()


def _cparams(*sem):
    return pltpu.CompilerParams(dimension_semantics=sem, vmem_limit_bytes=VMEM_LIMIT)


def _silu(v):
    return v * jax.nn.sigmoid(v)


def _dot(a, b):
    return jnp.dot(a, b, preferred_element_type=F32)


def _dot_nt(a, b):
    return lax.dot_general(a, b, (((1,), (1,)), ((), ())), preferred_element_type=F32)


def _rms(v, g):
    return v * lax.rsqrt(jnp.mean(v * v, axis=-1, keepdims=True) + NORM_EPS) * g


def _mod_kernel(c_ref, w_ref, b_ref, o_ref):
    a = _silu(c_ref[...]).astype(BF)
    o_ref[...] = _dot(a, w_ref[...].astype(BF)) + b_ref[...]


def _modulation(c8, w_ada, b_ada, tn=512):
    rows, d = c8.shape
    n = w_ada.shape[1]
    return pl.pallas_call(
        _mod_kernel,
        grid=(n // tn,),
        in_specs=[pl.BlockSpec((rows, d), lambda j: (0, 0)),
                  pl.BlockSpec((d, tn), lambda j: (0, j)),
                  pl.BlockSpec((1, tn), lambda j: (0, j))],
        out_specs=pl.BlockSpec((rows, tn), lambda j: (0, j)),
        out_shape=jax.ShapeDtypeStruct((rows, n), F32),
        compiler_params=_cparams("parallel"),
    )(c8, w_ada, b_ada)


def _prenorm_kernel(x_ref, g_ref, mod_ref, h_ref, *, shift_row, scale_row):
    y = _rms(x_ref[0], g_ref[...])
    h = y * (1.0 + mod_ref[0, scale_row:scale_row + 1, :]) + mod_ref[0, shift_row:shift_row + 1, :]
    h_ref[0] = h.astype(BF)


def _prenorm(x, g, mod3, shift_row, scale_row, tr=512):
    b, s, d = x.shape
    return pl.pallas_call(
        functools.partial(_prenorm_kernel, shift_row=shift_row, scale_row=scale_row),
        grid=(b, s // tr),
        in_specs=[pl.BlockSpec((1, tr, d), lambda i, r: (i, r, 0)),
                  pl.BlockSpec((1, d), lambda i, r: (0, 0)),
                  pl.BlockSpec((1, 6, d), lambda i, r: (i, 0, 0))],
        out_specs=pl.BlockSpec((1, tr, d), lambda i, r: (i, r, 0)),
        out_shape=jax.ShapeDtypeStruct((b, s, d), BF),
        compiler_params=_cparams("parallel", "parallel"),
    )(x, g, mod3)


def _post_mix_kernel(y_ref, x_ref, gp_ref, gn_ref, mod_ref, xo_ref, h_ref, *, gate_row, shift_row, scale_row):
    x1 = x_ref[0] + mod_ref[0, gate_row:gate_row + 1, :] * _rms(y_ref[0].astype(F32), gp_ref[...])
    xo_ref[0] = x1
    h = _rms(x1, gn_ref[...]) * (1.0 + mod_ref[0, scale_row:scale_row + 1, :]) \
        + mod_ref[0, shift_row:shift_row + 1, :]
    h_ref[0] = h.astype(BF)


def _post_mix(y, x, g_post, g_next, mod3, gate_row, shift_row, scale_row, tr=256):
    b, s, d = x.shape
    row = pl.BlockSpec((1, tr, d), lambda i, r: (i, r, 0))
    vec = pl.BlockSpec((1, d), lambda i, r: (0, 0))
    return pl.pallas_call(
        functools.partial(_post_mix_kernel, gate_row=gate_row, shift_row=shift_row, scale_row=scale_row),
        grid=(b, s // tr),
        in_specs=[row, row, vec, vec, pl.BlockSpec((1, 6, d), lambda i, r: (i, 0, 0))],
        out_specs=[row, row],
        out_shape=[jax.ShapeDtypeStruct((b, s, d), F32), jax.ShapeDtypeStruct((b, s, d), BF)],
        compiler_params=_cparams("parallel", "parallel"),
    )(y, x, g_post, g_next, mod3)


def _post_ffn_kernel(y_ref, x_ref, gp_ref, mod_ref, xo_ref, *, gate_row):
    xo_ref[0] = x_ref[0] + mod_ref[0, gate_row:gate_row + 1, :] * _rms(y_ref[0].astype(F32), gp_ref[...])


def _post_ffn(y, x, g_post, mod3, gate_row, tr=256):
    b, s, d = x.shape
    row = pl.BlockSpec((1, tr, d), lambda i, r: (i, r, 0))
    return pl.pallas_call(
        functools.partial(_post_ffn_kernel, gate_row=gate_row),
        grid=(b, s // tr),
        in_specs=[row, row, pl.BlockSpec((1, d), lambda i, r: (0, 0)),
                  pl.BlockSpec((1, 6, d), lambda i, r: (i, 0, 0))],
        out_specs=row,
        out_shape=jax.ShapeDtypeStruct((b, s, d), F32),
        compiler_params=_cparams("parallel", "parallel"),
    )(y, x, g_post, mod3)


def _mm_scale_kernel(a_ref, w_ref, s_ref, o_ref):
    o_ref[...] = (_dot(a_ref[...], w_ref[...]) * s_ref[...]).astype(o_ref.dtype)


def _mm_kernel(a_ref, w_ref, o_ref):
    o_ref[...] = _dot(a_ref[...], w_ref[...]).astype(o_ref.dtype)


def _matmul(a, w, out_dtype, tm, tn, col_scale=None):
    m, k = a.shape
    n = w.shape[1]
    in_specs = [pl.BlockSpec((tm, k), lambda i, j: (i, 0)),
                pl.BlockSpec((k, tn), lambda i, j: (0, j))]
    args = [a, w]
    body = _mm_kernel
    if col_scale is not None:
        in_specs.append(pl.BlockSpec((1, tn), lambda i, j: (0, j)))
        args.append(col_scale)
        body = _mm_scale_kernel
    return pl.pallas_call(
        body,
        grid=(m // tm, n // tn),
        in_specs=in_specs,
        out_specs=pl.BlockSpec((tm, tn), lambda i, j: (i, j)),
        out_shape=jax.ShapeDtypeStruct((m, n), out_dtype),
        compiler_params=_cparams("parallel", "arbitrary"),
    )(*args)


def _mm_tail_kernel(a_ref, w_ref, wt_ref, s_ref, o_ref, *, n_main):
    j = pl.program_id(1)

    @pl.when(j < n_main)
    def _main():
        o_ref[...] = (_dot(a_ref[...], w_ref[...]) * s_ref[...]).astype(o_ref.dtype)

    @pl.when(j >= n_main)
    def _tail():
        o_ref[...] = (_dot(a_ref[...], wt_ref[...]) * s_ref[...]).astype(o_ref.dtype)


def _matmul_with_tail(a, w, n_w, w_tail, col_scale, out_dtype, tm, tn):
    m, k = a.shape
    n_main = n_w // tn
    n_tail = w_tail.shape[1] // tn
    n = n_w + w_tail.shape[1]
    return pl.pallas_call(
        functools.partial(_mm_tail_kernel, n_main=n_main),
        grid=(m // tm, n_main + n_tail),
        in_specs=[pl.BlockSpec((tm, k), lambda i, j: (i, 0)),
                  pl.BlockSpec((k, tn), lambda i, j: (0, jnp.minimum(j, n_main - 1))),
                  pl.BlockSpec((k, tn), lambda i, j: (0, jnp.maximum(j - n_main, 0)), pipeline_mode=pl.Buffered(1)),
                  pl.BlockSpec((1, tn), lambda i, j: (0, j))],
        out_specs=pl.BlockSpec((tm, tn), lambda i, j: (i, j)),
        out_shape=jax.ShapeDtypeStruct((m, n), out_dtype),
        compiler_params=_cparams("parallel", "arbitrary"),
    )(a, w, w_tail, col_scale)


def _mix_kernel(h_ref, a1_ref, a2_ref, w1_ref, w2_ref, wg_ref, o_ref):
    tn = o_ref.shape[1]
    ya = _dot(a1_ref[...], w1_ref[...])
    yb = _dot(a2_ref[...], w2_ref[...])
    g = _dot(h_ref[...], wg_ref[...])
    o = jax.nn.sigmoid(g[:, 0:tn]) * ya + jax.nn.sigmoid(g[:, tn:]) * yb
    o_ref[...] = o.astype(o_ref.dtype)


def _gated_mix(h, u_conv, o_nsa, w_conv_out, w_nsa_out, w_gates, tm=1024, tn=PAIR_TILE):
    m, k = h.shape
    ka = u_conv.shape[1]
    kb = o_nsa.shape[1]
    d_model = w_conv_out.shape[1]
    return pl.pallas_call(
        _mix_kernel,
        grid=(m // tm, d_model // tn),
        in_specs=[pl.BlockSpec((tm, k), lambda i, j: (i, 0)),
                  pl.BlockSpec((tm, ka), lambda i, j: (i, 0)),
                  pl.BlockSpec((tm, kb), lambda i, j: (i, 0)),
                  pl.BlockSpec((ka, tn), lambda i, j: (0, j)),
                  pl.BlockSpec((kb, tn), lambda i, j: (0, j)),
                  pl.BlockSpec((k, 2 * tn), lambda i, j: (0, j))],
        out_specs=pl.BlockSpec((tm, tn), lambda i, j: (i, j)),
        out_shape=jax.ShapeDtypeStruct((m, d_model), BF),
        compiler_params=_cparams("parallel", "arbitrary"),
    )(h, u_conv, o_nsa, w_conv_out, w_nsa_out, w_gates)


def _ffn_up_kernel(a_ref, w_ref, o_ref):
    tn = o_ref.shape[1]
    gu = _dot(a_ref[...], w_ref[...])
    o_ref[...] = (_silu(gu[:, 0:tn]) * gu[:, tn:]).astype(o_ref.dtype)


def _ffn_up(h, w_gate_up, tm=2048, tn=PAIR_TILE):
    m, k = h.shape
    n = w_gate_up.shape[1] // 2
    return pl.pallas_call(
        _ffn_up_kernel,
        grid=(m // tm, n // tn),
        in_specs=[pl.BlockSpec((tm, k), lambda i, j: (i, 0)),
                  pl.BlockSpec((k, 2 * tn), lambda i, j: (0, j))],
        out_specs=pl.BlockSpec((tm, tn), lambda i, j: (i, j)),
        out_shape=jax.ShapeDtypeStruct((m, n), BF),
        compiler_params=_cparams("parallel", "arbitrary"),
    )(h, w_gate_up)


def _conv_kernel(a_ref, b_ref, ah_ref, bh_ref, w_ref, bdw_ref, g_ref, bb_ref, o_ref, u_ref, sh_ref, v_ref, *, ts):
    i = pl.program_id(1)
    u_ref[CONV_HALO:CONV_HALO + ts, :] = a_ref[0].astype(F32) * jax.nn.sigmoid(b_ref[0].astype(F32))
    uh = ah_ref[0].astype(F32) * jax.nn.sigmoid(bh_ref[0].astype(F32))
    u_ref[0:CONV_HALO, :] = jnp.where(i > 0, uh, 0.0)

    rc, cc = CONV_ROWS, CONV_LANES
    first = CONV_HALO - (CONV_WIDTH - 1)
    span = ts + CONV_HALO - SUBLANES
    for c0 in range(0, CONV_DIM, cc):
        for s in range(1, SUBLANES):
            sh_ref[s - 1] = u_ref[s:s + span, c0:c0 + cc]

        def row_chunk(rb, carry, c0=c0):
            r0 = pl.multiple_of(rb * rc, rc)
            acc = jnp.zeros((rc, cc), F32)
            for k in range(CONV_WIDTH):
                s = (first + k) % SUBLANES
                base = first + k - s
                if s == 0:
                    tap = u_ref[pl.ds(r0 + base, rc), c0:c0 + cc]
                else:
                    tap = sh_ref[s - 1, pl.ds(r0 + base, rc), :]
                acc = acc + tap * w_ref[k:k + 1, c0:c0 + cc]
            v_ref[pl.ds(r0, rc), c0:c0 + cc] = acc + bdw_ref[:, c0:c0 + cc]
            return carry

        lax.fori_loop(0, ts // rc, row_chunk, 0)

    v = v_ref[...]
    mu = jnp.mean(v, axis=-1, keepdims=True)
    vc = v - mu
    y = vc * lax.rsqrt(jnp.mean(vc * vc, axis=-1, keepdims=True) + NORM_EPS) * g_ref[...] + bb_ref[...]
    o_ref[0] = _silu(y).astype(o_ref.dtype)


def _conv_module(proj, w_dw, b_dw, ln_g, ln_b, ts=256):
    b, s, _ = proj.shape
    hb = ts // CONV_HALO
    cur_a = pl.BlockSpec((1, ts, CONV_DIM), lambda i, r: (i, r, 0))
    cur_b = pl.BlockSpec((1, ts, CONV_DIM), lambda i, r: (i, r, 1))
    halo_a = pl.BlockSpec((1, CONV_HALO, CONV_DIM), lambda i, r: (i, jnp.maximum(r * hb - 1, 0), 0))
    halo_b = pl.BlockSpec((1, CONV_HALO, CONV_DIM), lambda i, r: (i, jnp.maximum(r * hb - 1, 0), 1))
    vec = pl.BlockSpec((1, CONV_DIM), lambda i, r: (0, 0))
    return pl.pallas_call(
        functools.partial(_conv_kernel, ts=ts),
        grid=(b, s // ts),
        in_specs=[cur_a, cur_b, halo_a, halo_b,
                  pl.BlockSpec((CONV_WIDTH, CONV_DIM), lambda i, r: (0, 0)), vec, vec, vec],
        out_specs=pl.BlockSpec((1, ts, CONV_DIM), lambda i, r: (i, r, 0)),
        out_shape=jax.ShapeDtypeStruct((b, s, CONV_DIM), BF),
        scratch_shapes=[pltpu.VMEM((CONV_HALO + ts, CONV_DIM), F32),
                        pltpu.VMEM((SUBLANES - 1, ts + CONV_HALO - SUBLANES, CONV_LANES), F32),
                        pltpu.VMEM((ts, CONV_DIM), F32)],
        compiler_params=_cparams("parallel", "arbitrary"),
    )(proj, proj, proj, proj, w_dw, b_dw, ln_g, ln_b)


def _compress_kernel(k_ref, pe_ref, w1_ref, w2_ref, o_ref, kf_ref):
    kf_ref[...] = k_ref[0].astype(F32)
    rows = kf_ref.shape[0] // CMP_STRIDE
    lo = jnp.zeros((rows, CMP_HID), F32)
    hi = jnp.zeros((rows, CMP_HID), F32)
    for l in range(CMP_STRIDE):
        tok = kf_ref[pl.ds(l, rows, stride=CMP_STRIDE), :].astype(BF)
        lo = lo + _dot(tok, w1_ref[l * HEAD_DIM:(l + 1) * HEAD_DIM, :])
        hi = hi + _dot(tok, w1_ref[(CMP_STRIDE + l) * HEAD_DIM:(CMP_STRIDE + l + 1) * HEAD_DIM, :])
    pe = _dot(pe_ref[...], w1_ref[...])
    hid = lo + pltpu.roll(hi, rows - 1, axis=0) + pe[0:1, :]
    act = 0.5 * hid * (1.0 + jnp.tanh(math.sqrt(2.0 / math.pi) * (hid + 0.044715 * hid * hid * hid)))
    o_ref[0] = _dot(act.astype(BF), w2_ref[...])


def _compress(proj, col_block, pe8, w1, w2):
    b, s, _ = proj.shape
    rows = s // CMP_STRIDE
    return pl.pallas_call(
        _compress_kernel,
        grid=(b, N_KV),
        in_specs=[pl.BlockSpec((1, s, HEAD_DIM), lambda i, g: (i, 0, col_block + g)),
                  pl.BlockSpec(pe8.shape, lambda i, g: (0, 0)),
                  pl.BlockSpec(w1.shape, lambda i, g: (0, 0)),
                  pl.BlockSpec(w2.shape, lambda i, g: (0, 0))],
        out_specs=pl.BlockSpec((1, rows, HEAD_DIM), lambda i, g: (i * N_KV + g, 0, 0)),
        out_shape=jax.ShapeDtypeStruct((b * N_KV, rows, HEAD_DIM), F32),
        scratch_shapes=[pltpu.VMEM((s, HEAD_DIM), F32)],
        compiler_params=_cparams("parallel", "parallel"),
    )(proj, pe8, w1, w2)


def _bias_tables_kernel(rb_ref, wb_ref, sb_ref, cf_ref):
    h = pl.program_id(0)
    last = rb_ref[REL_BUCKETS - 1, h]

    def rel(dist):
        n = jnp.maximum(dist, 0)
        bucket = jnp.full(n.shape, REL_BUCKETS // 2, jnp.int32)
        for t in T5_THRESHOLDS:
            bucket = bucket + (n >= t).astype(jnp.int32)
        bucket = jnp.where(n < REL_BUCKETS // 2, n, jnp.minimum(bucket, REL_BUCKETS - 1))
        out = jnp.zeros(n.shape, F32)
        for bk in range(REL_BUCKETS - 1):
            out = jnp.where(bucket == bk, rb_ref[bk, h] - last, out)
        return out * LOG2E

    def dist_of(shape, offset):
        qi = lax.broadcasted_iota(jnp.int32, shape, 0)
        kl = lax.broadcasted_iota(jnp.int32, shape, 1)
        return qi + offset - kl

    dw = dist_of((Q_TILE, WIN_KEYS), WIN_PAD)
    wb_ref[0] = jnp.where((dw >= 0) & (dw < WINDOW), rel(dw), NEG)

    ds_ = dist_of((Q_TILE, SLC_LOCAL_KEYS), SLC_PAD)
    sb_ref[0] = jnp.where(ds_ >= 0, rel(ds_), NEG)

    qi = lax.broadcasted_iota(jnp.int32, (Q_TILE, LANES), 0)
    col = lax.broadcasted_iota(jnp.int32, (Q_TILE, LANES), 1)
    mi = col & (CMP_BAND - 1)
    part = col >> (CMP_BAND.bit_length() - 1)
    dc = qi - CMP_STRIDE * (mi - CMP_BAND_LO) - (CMP_LEN - 1)
    val = jnp.where(dc >= 0, rel(dc), 0.0)
    p0 = val.astype(BF).astype(F32)
    r1 = val - p0
    p1 = r1.astype(BF).astype(F32)
    p2 = r1 - p1
    parts = jnp.where(part == 0, p0, jnp.where(part == 1, p1, jnp.where(part == 2, p2, 0.0)))
    cf_ref[0] = parts.astype(BF)


def _bias_tables(rel_bias):
    smem = pl.BlockSpec(memory_space=pltpu.SMEM)
    lk = SLC_LOCAL_KEYS
    return pl.pallas_call(
        _bias_tables_kernel,
        grid=(N_HEADS,),
        in_specs=[smem],
        out_specs=[pl.BlockSpec((1, Q_TILE, WIN_KEYS), lambda h: (h, 0, 0)),
                   pl.BlockSpec((1, Q_TILE, lk), lambda h: (h, 0, 0)),
                   pl.BlockSpec((1, Q_TILE, LANES), lambda h: (h, 0, 0))],
        out_shape=[jax.ShapeDtypeStruct((N_HEADS, Q_TILE, WIN_KEYS), F32),
                   jax.ShapeDtypeStruct((N_HEADS, Q_TILE, lk), F32),
                   jax.ShapeDtypeStruct((N_HEADS, Q_TILE, LANES), BF)],
        compiler_params=_cparams("arbitrary"),
    )(rel_bias)


def _softmax_parts(s):
    m = jnp.max(s, axis=-1, keepdims=True)
    e = jnp.exp2(s - m)
    return m, e, jnp.sum(e, axis=-1, keepdims=True)


def _lane_tiles(v, op):
    out = v[:, 0:LANES]
    for t in range(1, v.shape[1] // LANES):
        out = op(out, v[:, t * LANES:(t + 1) * LANES])
    return out


def _attn_kernel(*refs, seq, cast_groups):
    (q_ref, ks_ref, vs_ref, kw_ref, vw_ref, kc_ref, vc_ref, ng_ref,
     wb_ref, sb_ref, cf_ref, ovt_ref, eye_ref, spread_ref) = refs[:N_ATTN_IN]
    n_in, n_out = sum(cast_groups), len(cast_groups)
    cast_in = refs[N_ATTN_IN:N_ATTN_IN + n_in]
    o_ref = refs[N_ATTN_IN + n_in]
    cast_out = refs[N_ATTN_IN + n_in + 1:N_ATTN_IN + n_in + 1 + n_out]
    kaug_ref, vsp_ref, kwp_ref, vwp_ref, key_ref, sfar_ref = refs[N_ATTN_IN + n_in + 1 + n_out:]

    first = 0
    for size, wo_ref in zip(cast_groups, cast_out):
        srcs = cast_in[first:first + size]
        first += size
        if size == 1:
            wo_ref[...] = srcs[0][...].astype(BF)
            continue
        for t in range(srcs[0].shape[1] // PAIR_TILE):
            for u, w_ref in enumerate(srcs):
                wo_ref[:, (size * t + u) * PAIR_TILE:(size * t + u + 1) * PAIR_TILE] = (
                    w_ref[:, t * PAIR_TILE:(t + 1) * PAIR_TILE].astype(BF))

    tile = pl.program_id(2)
    n_slc = seq // SLC_BLOCK
    ncp = seq // CMP_STRIDE
    rows = HPG * Q_TILE
    n_sel = min(N_SEL, n_slc)
    if n_slc != PAD_FLAG_COL or Q_TILE % LANES or 3 * CMP_BAND > LANES:
        raise NotImplementedError("selection layout expects 64 selection blocks and whole-lane query tiles")
    assert CMP_STRIDE * (CMP_BAND_LO + 1) - (CMP_LEN - 1) >= T5_THRESHOLDS[-1]
    assert CMP_STRIDE * (CMP_BAND - CMP_BAND_LO) + CMP_LEN - 1 >= Q_TILE

    @pl.when(tile == 0)
    def _stage_keys():
        col = lax.broadcasted_iota(jnp.int32, (SLC_PAD, LANES), 1)
        kaug_ref[0:SLC_PAD, 0:HEAD_DIM] = jnp.zeros((SLC_PAD, HEAD_DIM), BF)
        kaug_ref[0:SLC_PAD, HEAD_DIM:] = jnp.where(col == PAD_FLAG_COL, NEG, 0.0).astype(BF)
        kaug_ref[SLC_PAD:, 0:HEAD_DIM] = ks_ref[0]
        blk = lax.broadcasted_iota(jnp.int32, (seq, LANES), 0) >> (SLC_BLOCK.bit_length() - 1)
        colk = lax.broadcasted_iota(jnp.int32, (seq, LANES), 1)
        kaug_ref[SLC_PAD:, HEAD_DIM:] = jnp.where(colk == blk, NEG, 0.0).astype(BF)
        vsp_ref[:, HEAD_DIM:] = jnp.ones((SLC_PAD + seq, LANES), BF)
        vsp_ref[0:SLC_PAD, 0:HEAD_DIM] = jnp.zeros((SLC_PAD, HEAD_DIM), BF)
        vsp_ref[SLC_PAD:, 0:HEAD_DIM] = vs_ref[0]
        kwp_ref[0:WIN_PAD, :] = jnp.zeros((WIN_PAD, HEAD_DIM), BF)
        kwp_ref[WIN_PAD:, :] = kw_ref[0]
        vwp_ref[:, HEAD_DIM:] = jnp.ones((WIN_PAD + seq, LANES), BF)
        vwp_ref[0:WIN_PAD, 0:HEAD_DIM] = jnp.zeros((WIN_PAD, HEAD_DIM), BF)
        vwp_ref[WIN_PAD:, 0:HEAD_DIM] = vw_ref[0]

    q4 = jnp.concatenate([q_ref[0, :, hh * HEAD_DIM:(hh + 1) * HEAD_DIM] for hh in range(HPG)], axis=0)
    t0 = tile * Q_TILE
    r0 = pl.multiple_of(t0, Q_TILE)

    gts = jax.nn.sigmoid(ng_ref[0].astype(F32))
    g_hi = gts.astype(BF)
    g_lo = (gts - g_hi.astype(F32)).astype(BF)
    spread = spread_ref[...]
    gsp = _dot(g_hi, spread) + _dot(g_lo, spread)

    key_w = lax.broadcasted_iota(jnp.int32, (1, WIN_KEYS), 1)
    before_start = jnp.where(key_w < WIN_PAD - t0, NEG, 0.0)
    s_w = _dot_nt(q4, kwp_ref[pl.ds(r0, WIN_KEYS), :]) + wb_ref[...].reshape(rows, WIN_KEYS) + before_start
    e_w = jnp.exp2(s_w - jnp.max(s_w, axis=-1, keepdims=True))
    pv_w = _dot(e_w.astype(BF), vwp_ref[pl.ds(r0, WIN_KEYS), :])
    o_win = pv_w[:, 0:HEAD_DIM] * (1.0 / pv_w[:, HEAD_DIM:])

    kc = kc_ref[0].astype(BF)
    vc = vc_ref[0].astype(BF)
    r_io = lax.broadcasted_iota(jnp.int32, (LANES, ncp), 0)
    n_io = lax.broadcasted_iota(jnp.int32, (LANES, ncp), 1)
    band = (r_io < 3 * CMP_BAND) & (n_io == CMP_PER_TILE * tile + (r_io & (CMP_BAND - 1)) - CMP_BAND_LO)
    shift = jnp.where(band, 1.0, 0.0).astype(BF)
    cf4 = cf_ref[...].reshape(rows, LANES)
    lc = _dot_nt(q4, kc) + _dot(cf4, shift)
    qi_c = lax.broadcasted_iota(jnp.int32, (Q_TILE, ncp), 0)
    n_c = lax.broadcasted_iota(jnp.int32, (Q_TILE, ncp), 1)
    vis = (CMP_STRIDE * n_c + (CMP_LEN - 1)) <= (t0 + qi_c)
    lc3 = jnp.where(vis[None], lc.reshape(HPG, Q_TILE, ncp), NEG)
    _, e_c, l_c = _softmax_parts(lc3)
    p_c = jnp.where(vis[None], e_c * (1.0 / l_c), 0.0)
    o_cmp = _dot(p_c.reshape(rows, ncp).astype(BF), vc)

    psum = p_c[0]
    for hh in range(1, HPG):
        psum = psum + p_c[hh]
    p0 = psum.astype(BF)
    r1 = psum - p0.astype(F32)
    p1 = r1.astype(BF)
    p2 = (r1 - p1.astype(F32)).astype(BF)
    ovt = ovt_ref[...]
    imp_t = _dot_nt(ovt, p0) + _dot_nt(ovt, p1) + _dot_nt(ovt, p2)

    def gate(hh, branch):
        c = 3 * hh + branch
        return gsp[:, c * LANES:(c + 1) * LANES]

    gated_cw = [gate(hh, 0) * o_cmp[hh * Q_TILE:(hh + 1) * Q_TILE] + gate(hh, 2) * o_win[hh * Q_TILE:(hh + 1) * Q_TILE]
                for hh in range(HPG)]

    blk_t = lax.broadcasted_iota(jnp.int32, (n_slc, Q_TILE), 0)
    q_t = lax.broadcasted_iota(jnp.int32, (n_slc, Q_TILE), 1)
    cur_t = (t0 + q_t) >> (SLC_BLOCK.bit_length() - 1)
    forced_t = (blk_t == 0) | (blk_t == cur_t) | (blk_t == cur_t - 1)
    causal_t = blk_t <= cur_t
    key = jnp.where(forced_t, KEY_FORCED, jnp.where(causal_t, lax.bitcast_convert_type(imp_t, jnp.int32), KEY_FUTURE))
    key_ref[...] = key
    groups = [key[8 * a:8 * a + 8, :] for a in range(n_slc // 8)]
    ranks = [jnp.zeros((8, Q_TILE), jnp.int32) for _ in groups]
    sub = lax.broadcasted_iota(jnp.int32, (8, Q_TILE), 0)
    for bb in range(n_slc):
        kb = key_ref[bb:bb + 1, :]
        kb1 = kb + 1
        for a in range(n_slc // 8):
            if 8 * a > bb:
                ahead = kb1
            elif 8 * a + 7 < bb:
                ahead = kb
            else:
                ahead = jnp.where(sub > bb - 8 * a, kb1, kb)
            ranks[a] = ranks[a] + (ahead > groups[a]).astype(jnp.int32)
    rank = jnp.concatenate(ranks, axis=0)
    sel_t = (rank < n_sel) & causal_t
    first_local = (t0 - SLC_PAD) >> (SLC_BLOCK.bit_length() - 1)
    far_t = blk_t < first_local
    flag_rows = jnp.where(blk_t == 0, 1.0, 0.0)
    stack = jnp.concatenate([jnp.where(sel_t, 0.0, 1.0), flag_rows,
                             jnp.where(sel_t & far_t, 0.0, 1.0), flag_rows], axis=0).astype(BF)
    qmask = _dot_nt(eye_ref[...], stack).astype(BF)
    qa_loc = jnp.concatenate([q4, qmask[:, 0:LANES]], axis=1)
    qa_far = jnp.concatenate([q4, qmask[:, LANES:]], axis=1)

    lk = SLC_LOCAL_KEYS
    chunk = SLC_CHUNK_BLOCKS * SLC_BLOCK
    n_far = (first_local + SLC_CHUNK_BLOCKS - 1) // SLC_CHUNK_BLOCKS

    s_loc = _dot_nt(qa_loc, kaug_ref[pl.ds(r0, lk), :]) + sb_ref[...].reshape(rows, lk)

    def far_logits(c, m_part):
        rr = pl.multiple_of(SLC_PAD + c * chunk, SLC_BLOCK)
        s = _dot_nt(qa_far, kaug_ref[pl.ds(rr, chunk), :])
        sfar_ref[c] = s
        return jnp.maximum(m_part, _lane_tiles(s, jnp.maximum))

    m_part = lax.fori_loop(0, n_far, far_logits, _lane_tiles(s_loc, jnp.maximum))
    m_s = jnp.max(m_part, axis=-1, keepdims=True)

    e_loc = jnp.exp2(s_loc - m_s)
    acc_s = _dot(e_loc.astype(BF), vsp_ref[pl.ds(r0, lk), :])

    def far_values(c, acc):
        rr = pl.multiple_of(SLC_PAD + c * chunk, SLC_BLOCK)
        e = jnp.exp2(sfar_ref[c] - m_s)
        return acc + _dot(e.astype(BF), vsp_ref[pl.ds(rr, chunk), :])

    acc_s = lax.fori_loop(0, n_far, far_values, acc_s)
    o_slc = acc_s[:, 0:HEAD_DIM] * (1.0 / acc_s[:, HEAD_DIM:])

    for hh in range(HPG):
        o_h = gated_cw[hh] + gate(hh, 1) * o_slc[hh * Q_TILE:(hh + 1) * Q_TILE]
        o_ref[0, :, hh * HEAD_DIM:(hh + 1) * HEAD_DIM] = o_h.astype(o_ref.dtype)


N_ATTN_IN = 14


def _nsa_attention(proj, k_cmp, v_cmp, wb, sb, cf, to_cast):
    b, s, _ = proj.shape
    n_steps = b * N_KV * (s // Q_TILE)
    cast_in_specs, cast_out_specs, cast_shapes = [], [], []
    for group in to_cast:
        rows_w, cols_w = group[0].shape
        slab = next(r for r in range(BF16_SUBLANES, rows_w + 1, BF16_SUBLANES)
                    if rows_w % r == 0 and rows_w // r <= n_steps)
        last = rows_w // slab - 1

        def slab_index(i, g, j, last=last):
            return jnp.minimum((i * N_KV + g) * (s // Q_TILE) + j, last), 0

        cast_in_specs += [pl.BlockSpec((slab, cols_w), slab_index) for _ in group]
        cast_out_specs.append(pl.BlockSpec((slab, len(group) * cols_w), slab_index))
        cast_shapes.append(jax.ShapeDtypeStruct((rows_w, len(group) * cols_w), BF))
    ovt = _overlap_matrix(s)
    eye = jnp.asarray(np.tile(np.eye(Q_TILE), (HPG, 1)), BF)
    spread = jnp.asarray(np.repeat(np.eye(LANES)[:, :3 * HPG], LANES, axis=1), BF)
    n_chunks = -(-(s // SLC_BLOCK) // SLC_CHUNK_BLOCKS)
    q_blk = Q_OFF // (HPG * HEAD_DIM)
    kv_blk = KC_OFF // HEAD_DIM
    ng_blk = NG_OFF // LANES
    lk = SLC_LOCAL_KEYS
    ncp = s // CMP_STRIDE

    def kv_spec(idx):
        return pl.BlockSpec((1, s, HEAD_DIM), lambda i, g, j: (i, 0, kv_blk + idx * N_KV + g))

    cmp_spec = pl.BlockSpec((1, ncp, HEAD_DIM), lambda i, g, j: (i * N_KV + g, 0, 0))
    outs = pl.pallas_call(
        functools.partial(_attn_kernel, seq=s, cast_groups=tuple(len(group) for group in to_cast)),
        grid=(b, N_KV, s // Q_TILE),
        in_specs=[pl.BlockSpec((1, Q_TILE, HPG * HEAD_DIM), lambda i, g, j: (i, j, q_blk + g)),
                  kv_spec(2), kv_spec(3), kv_spec(4), kv_spec(5),
                  cmp_spec, cmp_spec,
                  pl.BlockSpec((1, Q_TILE, LANES), lambda i, g, j: (i, j, ng_blk + g)),
                  pl.BlockSpec((HPG, Q_TILE, WIN_KEYS), lambda i, g, j: (g, 0, 0)),
                  pl.BlockSpec((HPG, Q_TILE, lk), lambda i, g, j: (g, 0, 0)),
                  pl.BlockSpec((HPG, Q_TILE, LANES), lambda i, g, j: (g, 0, 0)),
                  pl.BlockSpec(ovt.shape, lambda i, g, j: (0, 0)),
                  pl.BlockSpec(eye.shape, lambda i, g, j: (0, 0)),
                  pl.BlockSpec(spread.shape, lambda i, g, j: (0, 0))] + cast_in_specs,
        out_specs=[pl.BlockSpec((1, Q_TILE, HPG * HEAD_DIM), lambda i, g, j: (i, j, g))] + cast_out_specs,
        out_shape=[jax.ShapeDtypeStruct((b, s, Q_WIDTH), BF)]
        + cast_shapes,
        scratch_shapes=[pltpu.VMEM((SLC_PAD + s, 2 * HEAD_DIM), BF),
                        pltpu.VMEM((SLC_PAD + s, HEAD_DIM + LANES), BF),
                        pltpu.VMEM((WIN_PAD + s, HEAD_DIM), BF),
                        pltpu.VMEM((WIN_PAD + s, HEAD_DIM + LANES), BF),
                        pltpu.VMEM((s // SLC_BLOCK, Q_TILE), jnp.int32),
                        pltpu.VMEM((n_chunks, HPG * Q_TILE, SLC_CHUNK_BLOCKS * SLC_BLOCK), F32)],
        compiler_params=pltpu.CompilerParams(dimension_semantics=("arbitrary",) * 3,
                                             vmem_limit_bytes=ATTN_VMEM_LIMIT),
    )(proj, proj, proj, proj, proj, k_cmp, v_cmp, proj, wb, sb, cf, ovt, eye, spread, *[w for group in to_cast for w in group])
    return outs[0], outs[1:]


def _overlap_matrix(s):
    ncp = s // CMP_STRIDE
    n_slc = s // SLC_BLOCK
    nc = (s - CMP_LEN) // CMP_STRIDE + 1
    i = np.arange(ncp)[None, :]
    jj = np.arange(n_slc)[:, None]
    ov = ((i * CMP_STRIDE < (jj + 1) * SLC_BLOCK) & (i * CMP_STRIDE + CMP_LEN > jj * SLC_BLOCK) & (i < nc))
    return jnp.asarray(ov, BF)


def kernel(x, c, w_ada, b_ada, g_pre_mix, g_post_mix, g_pre_ffn, g_post_ffn, w_in, w_dw, b_dw, conv_ln_g, conv_ln_b, w_conv_out, cmp_pe_k, cmp_pe_v, w_cmp_k1, w_cmp_k2, w_cmp_v1, w_cmp_v2, rel_bias, w_nsa_out, w_out, w_ffn_gate, w_ffn_up, w_ffn_down):
    b, s, d = x.shape
    m = b * s
    depth = w_ada.shape[0]
    c8 = jnp.zeros((8, d), F32).at[:b].set(c)
    wb, sb, cf = _bias_tables(rel_bias)
    col_scale = jnp.ones((1, NG_OFF + N_KV * LANES), F32).at[:, Q_OFF:KC_OFF].set(HEAD_DIM ** -0.5 * LOG2E)

    for l in range(depth):
        mod = _modulation(c8, w_ada[l], b_ada[l][None, :])
        mod3 = mod[:b].reshape(b, 6, d)

        w_all = w_in[l].astype(BF)
        w_ng = w_in[l][:, NG_OFF:MG_OFF].reshape(d, N_KV, 3 * HPG)
        w_ng = jnp.pad(w_ng, ((0, 0), (0, 0), (0, LANES - 3 * HPG))).reshape(d, N_KV * LANES)
        w_ng = w_ng.astype(BF)
        w_gates = (w_all[:, MG_OFF:].reshape(d, 2, d // PAIR_TILE, PAIR_TILE)
                   .transpose(0, 2, 1, 3).reshape(d, 2 * d))

        h = _prenorm(x, g_pre_mix[l][None, :], mod3, 0, 1).reshape(m, d)
        proj = _matmul_with_tail(h, w_all, NG_OFF, w_ng, col_scale, BF, 2048, 512).reshape(b, s, -1)

        u_conv = _conv_module(proj, w_dw[l], b_dw[l][None, :], conv_ln_g[l][None, :], conv_ln_b[l][None, :])

        pe_k = jnp.broadcast_to(cmp_pe_k[l].reshape(1, -1), (8, CMP_LEN * HEAD_DIM)).astype(BF)
        pe_v = jnp.broadcast_to(cmp_pe_v[l].reshape(1, -1), (8, CMP_LEN * HEAD_DIM)).astype(BF)
        k_cmp = _compress(proj, KC_OFF // HEAD_DIM, pe_k, w_cmp_k1[l].astype(BF), w_cmp_k2[l].astype(BF))
        v_cmp = _compress(proj, (KC_OFF + KV_WIDTH) // HEAD_DIM, pe_v,
                          w_cmp_v1[l].astype(BF), w_cmp_v2[l].astype(BF))

        later = [(w_conv_out[l],), (w_nsa_out[l],), (w_out[l],), (w_ffn_gate[l], w_ffn_up[l]), (w_ffn_down[l],)]
        o_nsa, (wb_conv, wb_nsa, wb_out, wb_gate_up, wb_down) = _nsa_attention(proj, k_cmp, v_cmp, wb, sb, cf, later)

        z = _gated_mix(h, u_conv.reshape(m, CONV_DIM), o_nsa.reshape(m, Q_WIDTH), wb_conv, wb_nsa, w_gates)
        mix = _matmul(z, wb_out, BF, 1024, 1024).reshape(b, s, d)
        x, h2 = _post_mix(mix, x, g_post_mix[l][None, :], g_pre_ffn[l][None, :], mod3, 2, 3, 4)

        ff = _ffn_up(h2.reshape(m, d), wb_gate_up)
        f = _matmul(ff, wb_down, BF, 512, 512).reshape(b, s, d)
        x = _post_ffn(f, x, g_post_ffn[l][None, :], mod3, 5)
    return x
```

```python
import functools
import math

import numpy as np
import jax
import jax.numpy as jnp
from jax import lax
from jax.experimental import pallas as pl
from jax.experimental.pallas import tpu as pltpu

F32 = jnp.float32
BF = jnp.bfloat16

CONV_DIM = 2048
CONV_WIDTH = 31
N_HEADS = 16
N_KV = 4
HPG = N_HEADS // N_KV
HEAD_DIM = 128
CMP_LEN = 32
CMP_STRIDE = 16
CMP_HID = 512
SLC_BLOCK = 64
N_SEL = 16
WINDOW = 512
Q_BLOCK = 64
REL_BUCKETS = 32
REL_MAX_DIST = 128
NORM_EPS = 1e-6
NEG = -1e30
LOG2E = 1.4426950408889634

Q_WIDTH = N_HEADS * HEAD_DIM
KV_WIDTH = N_KV * HEAD_DIM
Q_OFF = 2 * CONV_DIM
KC_OFF = Q_OFF + Q_WIDTH
NG_OFF = KC_OFF + 6 * KV_WIDTH
MG_OFF = NG_OFF + 3 * N_HEADS

LANES = 128
SUBLANES = 8
BF16_SUBLANES = 16
VMEM_LIMIT = 56 * 1024 * 1024
ATTN_VMEM_LIMIT = 61 * 1024 * 1024

CONV_HALO = 32
CONV_ROWS = 64
CONV_LANES = 256

Q_TILE = 4 * Q_BLOCK
SLC_PAD = 2 * SLC_BLOCK
SLC_LOCAL_KEYS = SLC_PAD + Q_TILE
SLC_CHUNK_BLOCKS = 16
WIN_KEYS = WINDOW + Q_TILE
WIN_PAD = WINDOW
CMP_PER_TILE = Q_TILE // CMP_STRIDE
CMP_BAND = 32
CMP_BAND_LO = 9
PAD_FLAG_COL = 64
KEY_FORCED = 0x7F000000
KEY_FUTURE = -2


def _t5_thresholds():
    max_exact = REL_BUCKETS // 2
    thr = []
    for k in range(1, REL_BUCKETS - max_exact):
        n = max_exact
        while max_exact + int(math.log(n / max_exact) / math.log(REL_MAX_DIST / max_exact)
                              * (REL_BUCKETS - max_exact)) < max_exact + k:
            n += 1
        thr.append(n)
    return tuple(thr)


T5_THRESHOLDS = _t5_thresholds()


def _cparams(*sem):
    return pltpu.CompilerParams(dimension_semantics=sem, vmem_limit_bytes=VMEM_LIMIT)


def _silu(v):
    return v * jax.nn.sigmoid(v)


def _dot(a, b):
    return jnp.dot(a, b, preferred_element_type=F32)


def _dot_nt(a, b):
    return lax.dot_general(a, b, (((1,), (1,)), ((), ())), preferred_element_type=F32)


def _rms(v, g):
    return v * lax.rsqrt(jnp.mean(v * v, axis=-1, keepdims=True) + NORM_EPS) * g


def _mod_kernel(c_ref, w_ref, b_ref, o_ref):
    a = _silu(c_ref[...]).astype(BF)
    o_ref[...] = _dot(a, w_ref[...].astype(BF)) + b_ref[...]


def _modulation(c8, w_ada, b_ada, tn=512):
    rows, d = c8.shape
    n = w_ada.shape[1]
    return pl.pallas_call(
        _mod_kernel,
        grid=(n // tn,),
        in_specs=[pl.BlockSpec((rows, d), lambda j: (0, 0)),
                  pl.BlockSpec((d, tn), lambda j: (0, j)),
                  pl.BlockSpec((1, tn), lambda j: (0, j))],
        out_specs=pl.BlockSpec((rows, tn), lambda j: (0, j)),
        out_shape=jax.ShapeDtypeStruct((rows, n), F32),
        compiler_params=_cparams("parallel"),
    )(c8, w_ada, b_ada)


def _prenorm_kernel(x_ref, g_ref, mod_ref, h_ref, *, shift_row, scale_row):
    y = _rms(x_ref[0], g_ref[...])
    h = y * (1.0 + mod_ref[0, scale_row:scale_row + 1, :]) + mod_ref[0, shift_row:shift_row + 1, :]
    h_ref[0] = h.astype(BF)


def _prenorm(x, g, mod3, shift_row, scale_row, tr=512):
    b, s, d = x.shape
    return pl.pallas_call(
        functools.partial(_prenorm_kernel, shift_row=shift_row, scale_row=scale_row),
        grid=(b, s // tr),
        in_specs=[pl.BlockSpec((1, tr, d), lambda i, r: (i, r, 0)),
                  pl.BlockSpec((1, d), lambda i, r: (0, 0)),
                  pl.BlockSpec((1, 6, d), lambda i, r: (i, 0, 0))],
        out_specs=pl.BlockSpec((1, tr, d), lambda i, r: (i, r, 0)),
        out_shape=jax.ShapeDtypeStruct((b, s, d), BF),
        compiler_params=_cparams("parallel", "parallel"),
    )(x, g, mod3)


def _post_mix_kernel(y_ref, x_ref, gp_ref, gn_ref, mod_ref, xo_ref, h_ref, *, gate_row, shift_row, scale_row):
    x1 = x_ref[0] + mod_ref[0, gate_row:gate_row + 1, :] * _rms(y_ref[0].astype(F32), gp_ref[...])
    xo_ref[0] = x1
    h = _rms(x1, gn_ref[...]) * (1.0 + mod_ref[0, scale_row:scale_row + 1, :]) \
        + mod_ref[0, shift_row:shift_row + 1, :]
    h_ref[0] = h.astype(BF)


def _post_mix(y, x, g_post, g_next, mod3, gate_row, shift_row, scale_row, tr=256):
    b, s, d = x.shape
    row = pl.BlockSpec((1, tr, d), lambda i, r: (i, r, 0))
    vec = pl.BlockSpec((1, d), lambda i, r: (0, 0))
    return pl.pallas_call(
        functools.partial(_post_mix_kernel, gate_row=gate_row, shift_row=shift_row, scale_row=scale_row),
        grid=(b, s // tr),
        in_specs=[row, row, vec, vec, pl.BlockSpec((1, 6, d), lambda i, r: (i, 0, 0))],
        out_specs=[row, row],
        out_shape=[jax.ShapeDtypeStruct((b, s, d), F32), jax.ShapeDtypeStruct((b, s, d), BF)],
        compiler_params=_cparams("parallel", "parallel"),
    )(y, x, g_post, g_next, mod3)


def _post_ffn_kernel(y_ref, x_ref, gp_ref, mod_ref, xo_ref, *, gate_row):
    xo_ref[0] = x_ref[0] + mod_ref[0, gate_row:gate_row + 1, :] * _rms(y_ref[0].astype(F32), gp_ref[...])


def _post_ffn(y, x, g_post, mod3, gate_row, tr=256):
    b, s, d = x.shape
    row = pl.BlockSpec((1, tr, d), lambda i, r: (i, r, 0))
    return pl.pallas_call(
        functools.partial(_post_ffn_kernel, gate_row=gate_row),
        grid=(b, s // tr),
        in_specs=[row, row, pl.BlockSpec((1, d), lambda i, r: (0, 0)),
                  pl.BlockSpec((1, 6, d), lambda i, r: (i, 0, 0))],
        out_specs=row,
        out_shape=jax.ShapeDtypeStruct((b, s, d), F32),
        compiler_params=_cparams("parallel", "parallel"),
    )(y, x, g_post, mod3)


def _mm_scale_kernel(a_ref, w_ref, s_ref, o_ref):
    o_ref[...] = (_dot(a_ref[...], w_ref[...]) * s_ref[...]).astype(o_ref.dtype)


def _mm_kernel(a_ref, w_ref, o_ref):
    o_ref[...] = _dot(a_ref[...], w_ref[...]).astype(o_ref.dtype)


def _matmul(a, w, out_dtype, tm, tn, col_scale=None):
    m, k = a.shape
    n = w.shape[1]
    in_specs = [pl.BlockSpec((tm, k), lambda i, j: (i, 0)),
                pl.BlockSpec((k, tn), lambda i, j: (0, j))]
    args = [a, w]
    body = _mm_kernel
    if col_scale is not None:
        in_specs.append(pl.BlockSpec((1, tn), lambda i, j: (0, j)))
        args.append(col_scale)
        body = _mm_scale_kernel
    return pl.pallas_call(
        body,
        grid=(m // tm, n // tn),
        in_specs=in_specs,
        out_specs=pl.BlockSpec((tm, tn), lambda i, j: (i, j)),
        out_shape=jax.ShapeDtypeStruct((m, n), out_dtype),
        compiler_params=_cparams("parallel", "arbitrary"),
    )(*args)


def _mm_tail_kernel(a_ref, w_ref, wt_ref, s_ref, o_ref, *, n_main):
    j = pl.program_id(1)

    @pl.when(j < n_main)
    def _main():
        o_ref[...] = (_dot(a_ref[...], w_ref[...]) * s_ref[...]).astype(o_ref.dtype)

    @pl.when(j >= n_main)
    def _tail():
        o_ref[...] = (_dot(a_ref[...], wt_ref[...]) * s_ref[...]).astype(o_ref.dtype)


def _matmul_with_tail(a, w, n_w, w_tail, col_scale, out_dtype, tm, tn):
    m, k = a.shape
    n_main = n_w // tn
    n_tail = w_tail.shape[1] // tn
    n = n_w + w_tail.shape[1]
    return pl.pallas_call(
        functools.partial(_mm_tail_kernel, n_main=n_main),
        grid=(m // tm, n_main + n_tail),
        in_specs=[pl.BlockSpec((tm, k), lambda i, j: (i, 0)),
                  pl.BlockSpec((k, tn), lambda i, j: (0, jnp.minimum(j, n_main - 1))),
                  pl.BlockSpec((k, tn), lambda i, j: (0, jnp.maximum(j - n_main, 0)), pipeline_mode=pl.Buffered(1)),
                  pl.BlockSpec((1, tn), lambda i, j: (0, j))],
        out_specs=pl.BlockSpec((tm, tn), lambda i, j: (i, j)),
        out_shape=jax.ShapeDtypeStruct((m, n), out_dtype),
        compiler_params=_cparams("parallel", "arbitrary"),
    )(a, w, w_tail, col_scale)


def _mix_kernel(h_ref, a1_ref, a2_ref, w1_ref, w2_ref, wga_ref, wgb_ref, o_ref):
    h = h_ref[...]
    ya = _dot(a1_ref[...], w1_ref[...])
    yb = _dot(a2_ref[...], w2_ref[...])
    o = jax.nn.sigmoid(_dot(h, wga_ref[...])) * ya + jax.nn.sigmoid(_dot(h, wgb_ref[...])) * yb
    o_ref[...] = o.astype(o_ref.dtype)


def _gated_mix(h, u_conv, o_nsa, w_conv_out, w_nsa_out, w_gates, tm=1024, tn=256):
    m, k = h.shape
    ka = u_conv.shape[1]
    kb = o_nsa.shape[1]
    d_model = w_conv_out.shape[1]
    gb_block = d_model // tn
    return pl.pallas_call(
        _mix_kernel,
        grid=(m // tm, d_model // tn),
        in_specs=[pl.BlockSpec((tm, k), lambda i, j: (i, 0)),
                  pl.BlockSpec((tm, ka), lambda i, j: (i, 0)),
                  pl.BlockSpec((tm, kb), lambda i, j: (i, 0)),
                  pl.BlockSpec((ka, tn), lambda i, j: (0, j)),
                  pl.BlockSpec((kb, tn), lambda i, j: (0, j)),
                  pl.BlockSpec((k, tn), lambda i, j: (0, j)),
                  pl.BlockSpec((k, tn), lambda i, j: (0, j + gb_block))],
        out_specs=pl.BlockSpec((tm, tn), lambda i, j: (i, j)),
        out_shape=jax.ShapeDtypeStruct((m, d_model), BF),
        compiler_params=_cparams("parallel", "arbitrary"),
    )(h, u_conv, o_nsa, w_conv_out, w_nsa_out, w_gates, w_gates)


def _ffn_up_kernel(a_ref, wg_ref, wu_ref, o_ref):
    a = a_ref[...]
    o_ref[...] = (_silu(_dot(a, wg_ref[...])) * _dot(a, wu_ref[...])).astype(o_ref.dtype)


def _ffn_up(h, w_gate, w_up, tm=2048, tn=256):
    m, k = h.shape
    n = w_gate.shape[1]
    wspec = pl.BlockSpec((k, tn), lambda i, j: (0, j))
    return pl.pallas_call(
        _ffn_up_kernel,
        grid=(m // tm, n // tn),
        in_specs=[pl.BlockSpec((tm, k), lambda i, j: (i, 0)), wspec, wspec],
        out_specs=pl.BlockSpec((tm, tn), lambda i, j: (i, j)),
        out_shape=jax.ShapeDtypeStruct((m, n), BF),
        compiler_params=_cparams("parallel", "arbitrary"),
    )(h, w_gate, w_up)


def _conv_kernel(a_ref, b_ref, ah_ref, bh_ref, w_ref, bdw_ref, g_ref, bb_ref, o_ref, u_ref, sh_ref, v_ref, *, ts):
    i = pl.program_id(1)
    u_ref[CONV_HALO:CONV_HALO + ts, :] = a_ref[0].astype(F32) * jax.nn.sigmoid(b_ref[0].astype(F32))
    uh = ah_ref[0].astype(F32) * jax.nn.sigmoid(bh_ref[0].astype(F32))
    u_ref[0:CONV_HALO, :] = jnp.where(i > 0, uh, 0.0)

    rc, cc = CONV_ROWS, CONV_LANES
    first = CONV_HALO - (CONV_WIDTH - 1)
    span = ts + CONV_HALO - SUBLANES
    for c0 in range(0, CONV_DIM, cc):
        for s in range(1, SUBLANES):
            sh_ref[s - 1] = u_ref[s:s + span, c0:c0 + cc]

        def row_chunk(rb, carry, c0=c0):
            r0 = pl.multiple_of(rb * rc, rc)
            acc = jnp.zeros((rc, cc), F32)
            for k in range(CONV_WIDTH):
                s = (first + k) % SUBLANES
                base = first + k - s
                if s == 0:
                    tap = u_ref[pl.ds(r0 + base, rc), c0:c0 + cc]
                else:
                    tap = sh_ref[s - 1, pl.ds(r0 + base, rc), :]
                acc = acc + tap * w_ref[k:k + 1, c0:c0 + cc]
            v_ref[pl.ds(r0, rc), c0:c0 + cc] = acc + bdw_ref[:, c0:c0 + cc]
            return carry

        lax.fori_loop(0, ts // rc, row_chunk, 0)

    v = v_ref[...]
    mu = jnp.mean(v, axis=-1, keepdims=True)
    vc = v - mu
    y = vc * lax.rsqrt(jnp.mean(vc * vc, axis=-1, keepdims=True) + NORM_EPS) * g_ref[...] + bb_ref[...]
    o_ref[0] = _silu(y).astype(o_ref.dtype)


def _conv_module(proj, w_dw, b_dw, ln_g, ln_b, ts=256):
    b, s, _ = proj.shape
    hb = ts // CONV_HALO
    cur_a = pl.BlockSpec((1, ts, CONV_DIM), lambda i, r: (i, r, 0))
    cur_b = pl.BlockSpec((1, ts, CONV_DIM), lambda i, r: (i, r, 1))
    halo_a = pl.BlockSpec((1, CONV_HALO, CONV_DIM), lambda i, r: (i, jnp.maximum(r * hb - 1, 0), 0))
    halo_b = pl.BlockSpec((1, CONV_HALO, CONV_DIM), lambda i, r: (i, jnp.maximum(r * hb - 1, 0), 1))
    vec = pl.BlockSpec((1, CONV_DIM), lambda i, r: (0, 0))
    return pl.pallas_call(
        functools.partial(_conv_kernel, ts=ts),
        grid=(b, s // ts),
        in_specs=[cur_a, cur_b, halo_a, halo_b,
                  pl.BlockSpec((CONV_WIDTH, CONV_DIM), lambda i, r: (0, 0)), vec, vec, vec],
        out_specs=pl.BlockSpec((1, ts, CONV_DIM), lambda i, r: (i, r, 0)),
        out_shape=jax.ShapeDtypeStruct((b, s, CONV_DIM), BF),
        scratch_shapes=[pltpu.VMEM((CONV_HALO + ts, CONV_DIM), F32),
                        pltpu.VMEM((SUBLANES - 1, ts + CONV_HALO - SUBLANES, CONV_LANES), F32),
                        pltpu.VMEM((ts, CONV_DIM), F32)],
        compiler_params=_cparams("parallel", "arbitrary"),
    )(proj, proj, proj, proj, w_dw, b_dw, ln_g, ln_b)


def _compress_kernel(k_ref, pe_ref, w1_ref, w2_ref, o_ref, kf_ref):
    kf_ref[...] = k_ref[0].astype(F32)
    rows = kf_ref.shape[0] // CMP_STRIDE
    lo = jnp.zeros((rows, CMP_HID), F32)
    hi = jnp.zeros((rows, CMP_HID), F32)
    for l in range(CMP_STRIDE):
        tok = kf_ref[pl.ds(l, rows, stride=CMP_STRIDE), :].astype(BF)
        lo = lo + _dot(tok, w1_ref[l * HEAD_DIM:(l + 1) * HEAD_DIM, :])
        hi = hi + _dot(tok, w1_ref[(CMP_STRIDE + l) * HEAD_DIM:(CMP_STRIDE + l + 1) * HEAD_DIM, :])
    pe = _dot(pe_ref[...], w1_ref[...])
    hid = lo + pltpu.roll(hi, rows - 1, axis=0) + pe[0:1, :]
    act = 0.5 * hid * (1.0 + jnp.tanh(math.sqrt(2.0 / math.pi) * (hid + 0.044715 * hid * hid * hid)))
    o_ref[0] = _dot(act.astype(BF), w2_ref[...])


def _compress(proj, col_block, pe8, w1, w2):
    b, s, _ = proj.shape
    rows = s // CMP_STRIDE
    return pl.pallas_call(
        _compress_kernel,
        grid=(b, N_KV),
        in_specs=[pl.BlockSpec((1, s, HEAD_DIM), lambda i, g: (i, 0, col_block + g)),
                  pl.BlockSpec(pe8.shape, lambda i, g: (0, 0)),
                  pl.BlockSpec(w1.shape, lambda i, g: (0, 0)),
                  pl.BlockSpec(w2.shape, lambda i, g: (0, 0))],
        out_specs=pl.BlockSpec((1, rows, HEAD_DIM), lambda i, g: (i * N_KV + g, 0, 0)),
        out_shape=jax.ShapeDtypeStruct((b * N_KV, rows, HEAD_DIM), F32),
        scratch_shapes=[pltpu.VMEM((s, HEAD_DIM), F32)],
        compiler_params=_cparams("parallel", "parallel"),
    )(proj, pe8, w1, w2)


def _bias_tables_kernel(rb_ref, wb_ref, sb_ref, cf_ref):
    h = pl.program_id(0)
    last = rb_ref[REL_BUCKETS - 1, h]

    def rel(dist):
        n = jnp.maximum(dist, 0)
        bucket = jnp.full(n.shape, REL_BUCKETS // 2, jnp.int32)
        for t in T5_THRESHOLDS:
            bucket = bucket + (n >= t).astype(jnp.int32)
        bucket = jnp.where(n < REL_BUCKETS // 2, n, jnp.minimum(bucket, REL_BUCKETS - 1))
        out = jnp.zeros(n.shape, F32)
        for bk in range(REL_BUCKETS - 1):
            out = jnp.where(bucket == bk, rb_ref[bk, h] - last, out)
        return out * LOG2E

    def toeplitz(fn, offset, keys):
        width = keys + Q_TILE
        j = lax.broadcasted_iota(jnp.int32, (SUBLANES, width), 1)
        diag = fn(offset + Q_TILE - 1 - j)
        rows = jnp.concatenate([diag] * (Q_TILE // SUBLANES), axis=0)
        return pltpu.roll(rows, width - (Q_TILE - 1), 1, stride=1, stride_axis=0)[:, 0:keys]

    wb_ref[0] = toeplitz(lambda d: jnp.where((d >= 0) & (d < WINDOW), rel(d), NEG), WIN_PAD, WIN_KEYS)
    sb_ref[0] = toeplitz(lambda d: jnp.where(d >= 0, rel(d), NEG), SLC_PAD, SLC_LOCAL_KEYS)

    qi = lax.broadcasted_iota(jnp.int32, (Q_TILE, LANES), 0)
    col = lax.broadcasted_iota(jnp.int32, (Q_TILE, LANES), 1)
    mi = col & (CMP_BAND - 1)
    part = col >> (CMP_BAND.bit_length() - 1)
    dc = qi - CMP_STRIDE * (mi - CMP_BAND_LO) - (CMP_LEN - 1)
    val = jnp.where(dc >= 0, rel(dc), 0.0)
    p0 = val.astype(BF).astype(F32)
    r1 = val - p0
    p1 = r1.astype(BF).astype(F32)
    p2 = r1 - p1
    parts = jnp.where(part == 0, p0, jnp.where(part == 1, p1, jnp.where(part == 2, p2, 0.0)))
    cf_ref[0] = parts.astype(BF)


def _bias_tables(rel_bias):
    smem = pl.BlockSpec(memory_space=pltpu.SMEM)
    lk = SLC_LOCAL_KEYS
    return pl.pallas_call(
        _bias_tables_kernel,
        grid=(N_HEADS,),
        in_specs=[smem],
        out_specs=[pl.BlockSpec((1, Q_TILE, WIN_KEYS), lambda h: (h, 0, 0)),
                   pl.BlockSpec((1, Q_TILE, lk), lambda h: (h, 0, 0)),
                   pl.BlockSpec((1, Q_TILE, LANES), lambda h: (h, 0, 0))],
        out_shape=[jax.ShapeDtypeStruct((N_HEADS, Q_TILE, WIN_KEYS), F32),
                   jax.ShapeDtypeStruct((N_HEADS, Q_TILE, lk), F32),
                   jax.ShapeDtypeStruct((N_HEADS, Q_TILE, LANES), BF)],
        compiler_params=_cparams("arbitrary"),
    )(rel_bias)


def _softmax_parts(s):
    m = jnp.max(s, axis=-1, keepdims=True)
    e = jnp.exp2(s - m)
    return m, e, jnp.sum(e, axis=-1, keepdims=True)


def _lane_tiles(v, op):
    out = v[:, 0:LANES]
    for t in range(1, v.shape[1] // LANES):
        out = op(out, v[:, t * LANES:(t + 1) * LANES])
    return out


def _attn_kernel(*refs, seq, n_cast):
    (q_ref, ks_ref, vs_ref, kw_ref, vw_ref, kc_ref, vc_ref, ng_ref,
     wb_ref, sb_ref, cf_ref, ovt_ref, eye_ref, spread_ref) = refs[:N_ATTN_IN]
    cast_in = refs[N_ATTN_IN:N_ATTN_IN + n_cast]
    o_ref = refs[N_ATTN_IN + n_cast]
    cast_out = refs[N_ATTN_IN + n_cast + 1:N_ATTN_IN + 2 * n_cast + 1]
    kaug_ref, vsp_ref, kwp_ref, vwp_ref, key_ref, sfar_ref = refs[N_ATTN_IN + 2 * n_cast + 1:]

    for w_ref, wo_ref in zip(cast_in, cast_out):
        wo_ref[...] = w_ref[...].astype(BF)

    tile = pl.program_id(2)
    n_slc = seq // SLC_BLOCK
    ncp = seq // CMP_STRIDE
    rows = HPG * Q_TILE
    n_sel = min(N_SEL, n_slc)
    if n_slc != PAD_FLAG_COL or Q_TILE % LANES or 3 * CMP_BAND > LANES:
        raise NotImplementedError("selection layout expects 64 selection blocks and whole-lane query tiles")
    assert CMP_STRIDE * (CMP_BAND_LO + 1) - (CMP_LEN - 1) >= T5_THRESHOLDS[-1]
    assert CMP_STRIDE * (CMP_BAND - CMP_BAND_LO) + CMP_LEN - 1 >= Q_TILE

    @pl.when(tile == 0)
    def _stage_keys():
        col = lax.broadcasted_iota(jnp.int32, (SLC_PAD, LANES), 1)
        kaug_ref[0:SLC_PAD, 0:HEAD_DIM] = jnp.zeros((SLC_PAD, HEAD_DIM), BF)
        kaug_ref[0:SLC_PAD, HEAD_DIM:] = jnp.where(col == PAD_FLAG_COL, NEG, 0.0).astype(BF)
        kaug_ref[SLC_PAD:, 0:HEAD_DIM] = ks_ref[0]
        blk = lax.broadcasted_iota(jnp.int32, (seq, LANES), 0) >> (SLC_BLOCK.bit_length() - 1)
        colk = lax.broadcasted_iota(jnp.int32, (seq, LANES), 1)
        kaug_ref[SLC_PAD:, HEAD_DIM:] = jnp.where(colk == blk, NEG, 0.0).astype(BF)
        vsp_ref[:, HEAD_DIM:] = jnp.ones((SLC_PAD + seq, LANES), BF)
        vsp_ref[0:SLC_PAD, 0:HEAD_DIM] = jnp.zeros((SLC_PAD, HEAD_DIM), BF)
        vsp_ref[SLC_PAD:, 0:HEAD_DIM] = vs_ref[0]
        kwp_ref[0:WIN_PAD, :] = jnp.zeros((WIN_PAD, HEAD_DIM), BF)
        kwp_ref[WIN_PAD:, :] = kw_ref[0]
        vwp_ref[:, HEAD_DIM:] = jnp.ones((WIN_PAD + seq, LANES), BF)
        vwp_ref[0:WIN_PAD, 0:HEAD_DIM] = jnp.zeros((WIN_PAD, HEAD_DIM), BF)
        vwp_ref[WIN_PAD:, 0:HEAD_DIM] = vw_ref[0]

    q4 = jnp.concatenate([q_ref[0, :, hh * HEAD_DIM:(hh + 1) * HEAD_DIM] for hh in range(HPG)], axis=0)
    t0 = tile * Q_TILE
    r0 = pl.multiple_of(t0, Q_TILE)

    gts = jax.nn.sigmoid(ng_ref[0].astype(F32))
    g_hi = gts.astype(BF)
    g_lo = (gts - g_hi.astype(F32)).astype(BF)
    spread = spread_ref[...]
    gsp = _dot(g_hi, spread) + _dot(g_lo, spread)

    key_w = lax.broadcasted_iota(jnp.int32, (1, WIN_KEYS), 1)
    before_start = jnp.where(key_w < WIN_PAD - t0, NEG, 0.0)
    s_w = _dot_nt(q4, kwp_ref[pl.ds(r0, WIN_KEYS), :]) + wb_ref[...].reshape(rows, WIN_KEYS) + before_start
    e_w = jnp.exp2(s_w - jnp.max(s_w, axis=-1, keepdims=True))
    pv_w = _dot(e_w.astype(BF), vwp_ref[pl.ds(r0, WIN_KEYS), :])
    o_win = pv_w[:, 0:HEAD_DIM] * (1.0 / pv_w[:, HEAD_DIM:])

    kc = kc_ref[0].astype(BF)
    vc = vc_ref[0].astype(BF)
    r_io = lax.broadcasted_iota(jnp.int32, (LANES, ncp), 0)
    n_io = lax.broadcasted_iota(jnp.int32, (LANES, ncp), 1)
    band = (r_io < 3 * CMP_BAND) & (n_io == CMP_PER_TILE * tile + (r_io & (CMP_BAND - 1)) - CMP_BAND_LO)
    shift = jnp.where(band, 1.0, 0.0).astype(BF)
    cf4 = cf_ref[...].reshape(rows, LANES)
    lc = _dot_nt(q4, kc) + _dot(cf4, shift)
    qi_c = lax.broadcasted_iota(jnp.int32, (Q_TILE, ncp), 0)
    n_c = lax.broadcasted_iota(jnp.int32, (Q_TILE, ncp), 1)
    vis = (CMP_STRIDE * n_c + (CMP_LEN - 1)) <= (t0 + qi_c)
    lc3 = jnp.where(vis[None], lc.reshape(HPG, Q_TILE, ncp), NEG)
    _, e_c, l_c = _softmax_parts(lc3)
    p_c = jnp.where(vis[None], e_c * (1.0 / l_c), 0.0)
    o_cmp = _dot(p_c.reshape(rows, ncp).astype(BF), vc)

    psum = p_c[0]
    for hh in range(1, HPG):
        psum = psum + p_c[hh]
    p0 = psum.astype(BF)
    r1 = psum - p0.astype(F32)
    p1 = r1.astype(BF)
    p2 = (r1 - p1.astype(F32)).astype(BF)
    ovt = ovt_ref[...]
    imp_t = _dot_nt(ovt, p0) + _dot_nt(ovt, p1) + _dot_nt(ovt, p2)

    def gate(hh, branch):
        c = 3 * hh + branch
        return gsp[:, c * LANES:(c + 1) * LANES]

    gated_cw = [gate(hh, 0) * o_cmp[hh * Q_TILE:(hh + 1) * Q_TILE] + gate(hh, 2) * o_win[hh * Q_TILE:(hh + 1) * Q_TILE]
                for hh in range(HPG)]

    blk_t = lax.broadcasted_iota(jnp.int32, (n_slc, Q_TILE), 0)
    q_t = lax.broadcasted_iota(jnp.int32, (n_slc, Q_TILE), 1)
    cur_t = (t0 + q_t) >> (SLC_BLOCK.bit_length() - 1)
    forced_t = (blk_t == 0) | (blk_t == cur_t) | (blk_t == cur_t - 1)
    causal_t = blk_t <= cur_t
    key = jnp.where(forced_t, KEY_FORCED, jnp.where(causal_t, lax.bitcast_convert_type(imp_t, jnp.int32), KEY_FUTURE))
    key_ref[...] = key
    groups = [key[8 * a:8 * a + 8, :] for a in range(n_slc // 8)]
    ranks = [jnp.zeros((8, Q_TILE), jnp.int32) for _ in groups]
    sub = lax.broadcasted_iota(jnp.int32, (8, Q_TILE), 0)
    for bb in range(n_slc):
        kb = key_ref[bb:bb + 1, :]
        kb1 = kb + 1
        for a in range(n_slc // 8):
            if 8 * a > bb:
                ahead = kb1
            elif 8 * a + 7 < bb:
                ahead = kb
            else:
                ahead = jnp.where(sub > bb - 8 * a, kb1, kb)
            ranks[a] = ranks[a] + (ahead > groups[a]).astype(jnp.int32)
    rank = jnp.concatenate(ranks, axis=0)
    sel_t = (rank < n_sel) & causal_t
    first_local = (t0 - SLC_PAD) >> (SLC_BLOCK.bit_length() - 1)
    far_t = blk_t < first_local
    flag_rows = jnp.where(blk_t == 0, 1.0, 0.0)
    stack = jnp.concatenate([jnp.where(sel_t, 0.0, 1.0), flag_rows,
                             jnp.where(sel_t & far_t, 0.0, 1.0), flag_rows], axis=0).astype(BF)
    qmask = _dot_nt(eye_ref[...], stack).astype(BF)
    qa_loc = jnp.concatenate([q4, qmask[:, 0:LANES]], axis=1)
    qa_far = jnp.concatenate([q4, qmask[:, LANES:]], axis=1)

    lk = SLC_LOCAL_KEYS
    chunk = SLC_CHUNK_BLOCKS * SLC_BLOCK
    n_far = (first_local + SLC_CHUNK_BLOCKS - 1) // SLC_CHUNK_BLOCKS

    s_loc = _dot_nt(qa_loc, kaug_ref[pl.ds(r0, lk), :]) + sb_ref[...].reshape(rows, lk)

    def far_logits(c, m_part):
        rr = pl.multiple_of(SLC_PAD + c * chunk, SLC_BLOCK)
        s = _dot_nt(qa_far, kaug_ref[pl.ds(rr, chunk), :])
        sfar_ref[c] = s
        return jnp.maximum(m_part, _lane_tiles(s, jnp.maximum))

    m_part = lax.fori_loop(0, n_far, far_logits, _lane_tiles(s_loc, jnp.maximum))
    m_s = jnp.max(m_part, axis=-1, keepdims=True)

    e_loc = jnp.exp2(s_loc - m_s)
    acc_s = _dot(e_loc.astype(BF), vsp_ref[pl.ds(r0, lk), :])

    def far_values(c, acc):
        rr = pl.multiple_of(SLC_PAD + c * chunk, SLC_BLOCK)
        e = jnp.exp2(sfar_ref[c] - m_s)
        return acc + _dot(e.astype(BF), vsp_ref[pl.ds(rr, chunk), :])

    acc_s = lax.fori_loop(0, n_far, far_values, acc_s)
    o_slc = acc_s[:, 0:HEAD_DIM] * (1.0 / acc_s[:, HEAD_DIM:])

    for hh in range(HPG):
        o_h = gated_cw[hh] + gate(hh, 1) * o_slc[hh * Q_TILE:(hh + 1) * Q_TILE]
        o_ref[0, :, hh * HEAD_DIM:(hh + 1) * HEAD_DIM] = o_h.astype(o_ref.dtype)


N_ATTN_IN = 14


def _nsa_attention(proj, k_cmp, v_cmp, wb, sb, cf, to_cast):
    b, s, _ = proj.shape
    n_steps = b * N_KV * (s // Q_TILE)
    cast_specs = []
    for w in to_cast:
        slab = next(r for r in range(BF16_SUBLANES, w.shape[0] + 1, BF16_SUBLANES)
                    if w.shape[0] % r == 0 and w.shape[0] // r <= n_steps)
        last = w.shape[0] // slab - 1
        cast_specs.append(pl.BlockSpec(
            (slab, w.shape[1]),
            lambda i, g, j, last=last: (jnp.minimum((i * N_KV + g) * (s // Q_TILE) + j, last), 0)))
    ovt = _overlap_matrix(s)
    eye = jnp.asarray(np.tile(np.eye(Q_TILE), (HPG, 1)), BF)
    spread = jnp.asarray(np.repeat(np.eye(LANES)[:, :3 * HPG], LANES, axis=1), BF)
    n_chunks = -(-(s // SLC_BLOCK) // SLC_CHUNK_BLOCKS)
    q_blk = Q_OFF // (HPG * HEAD_DIM)
    kv_blk = KC_OFF // HEAD_DIM
    ng_blk = NG_OFF // LANES
    lk = SLC_LOCAL_KEYS
    ncp = s // CMP_STRIDE

    def kv_spec(idx):
        return pl.BlockSpec((1, s, HEAD_DIM), lambda i, g, j: (i, 0, kv_blk + idx * N_KV + g))

    cmp_spec = pl.BlockSpec((1, ncp, HEAD_DIM), lambda i, g, j: (i * N_KV + g, 0, 0))
    outs = pl.pallas_call(
        functools.partial(_attn_kernel, seq=s, n_cast=len(to_cast)),
        grid=(b, N_KV, s // Q_TILE),
        in_specs=[pl.BlockSpec((1, Q_TILE, HPG * HEAD_DIM), lambda i, g, j: (i, j, q_blk + g)),
                  kv_spec(2), kv_spec(3), kv_spec(4), kv_spec(5),
                  cmp_spec, cmp_spec,
                  pl.BlockSpec((1, Q_TILE, LANES), lambda i, g, j: (i, j, ng_blk + g)),
                  pl.BlockSpec((HPG, Q_TILE, WIN_KEYS), lambda i, g, j: (g, 0, 0)),
                  pl.BlockSpec((HPG, Q_TILE, lk), lambda i, g, j: (g, 0, 0)),
                  pl.BlockSpec((HPG, Q_TILE, LANES), lambda i, g, j: (g, 0, 0)),
                  pl.BlockSpec(ovt.shape, lambda i, g, j: (0, 0)),
                  pl.BlockSpec(eye.shape, lambda i, g, j: (0, 0)),
                  pl.BlockSpec(spread.shape, lambda i, g, j: (0, 0))] + cast_specs,
        out_specs=[pl.BlockSpec((1, Q_TILE, HPG * HEAD_DIM), lambda i, g, j: (i, j, g))] + cast_specs,
        out_shape=[jax.ShapeDtypeStruct((b, s, Q_WIDTH), BF)]
        + [jax.ShapeDtypeStruct(w.shape, BF) for w in to_cast],
        scratch_shapes=[pltpu.VMEM((SLC_PAD + s, 2 * HEAD_DIM), BF),
                        pltpu.VMEM((SLC_PAD + s, HEAD_DIM + LANES), BF),
                        pltpu.VMEM((WIN_PAD + s, HEAD_DIM), BF),
                        pltpu.VMEM((WIN_PAD + s, HEAD_DIM + LANES), BF),
                        pltpu.VMEM((s // SLC_BLOCK, Q_TILE), jnp.int32),
                        pltpu.VMEM((n_chunks, HPG * Q_TILE, SLC_CHUNK_BLOCKS * SLC_BLOCK), F32)],
        compiler_params=pltpu.CompilerParams(dimension_semantics=("arbitrary",) * 3,
                                             vmem_limit_bytes=ATTN_VMEM_LIMIT),
    )(proj, proj, proj, proj, proj, k_cmp, v_cmp, proj, wb, sb, cf, ovt, eye, spread, *to_cast)
    return outs[0], outs[1:]


def _overlap_matrix(s):
    ncp = s // CMP_STRIDE
    n_slc = s // SLC_BLOCK
    nc = (s - CMP_LEN) // CMP_STRIDE + 1
    i = np.arange(ncp)[None, :]
    jj = np.arange(n_slc)[:, None]
    ov = ((i * CMP_STRIDE < (jj + 1) * SLC_BLOCK) & (i * CMP_STRIDE + CMP_LEN > jj * SLC_BLOCK) & (i < nc))
    return jnp.asarray(ov, BF)


def kernel(x, c, w_ada, b_ada, g_pre_mix, g_post_mix, g_pre_ffn, g_post_ffn, w_in, w_dw, b_dw, conv_ln_g, conv_ln_b, w_conv_out, cmp_pe_k, cmp_pe_v, w_cmp_k1, w_cmp_k2, w_cmp_v1, w_cmp_v2, rel_bias, w_nsa_out, w_out, w_ffn_gate, w_ffn_up, w_ffn_down):
    b, s, d = x.shape
    m = b * s
    depth = w_ada.shape[0]
    c8 = jnp.zeros((8, d), F32).at[:b].set(c)
    wb, sb, cf = _bias_tables(rel_bias)
    col_scale = jnp.ones((1, NG_OFF + N_KV * LANES), F32).at[:, Q_OFF:KC_OFF].set(HEAD_DIM ** -0.5 * LOG2E)

    for l in range(depth):
        mod = _modulation(c8, w_ada[l], b_ada[l][None, :])
        mod3 = mod[:b].reshape(b, 6, d)

        w_all = w_in[l].astype(BF)
        w_ng = w_in[l][:, NG_OFF:MG_OFF].reshape(d, N_KV, 3 * HPG)
        w_ng = jnp.pad(w_ng, ((0, 0), (0, 0), (0, LANES - 3 * HPG))).reshape(d, N_KV * LANES)
        w_ng = w_ng.astype(BF)
        w_gates = w_all[:, MG_OFF:]

        h = _prenorm(x, g_pre_mix[l][None, :], mod3, 0, 1).reshape(m, d)
        proj = _matmul_with_tail(h, w_all, NG_OFF, w_ng, col_scale, BF, 2048, 512).reshape(b, s, -1)

        u_conv = _conv_module(proj, w_dw[l], b_dw[l][None, :], conv_ln_g[l][None, :], conv_ln_b[l][None, :])

        pe_k = jnp.broadcast_to(cmp_pe_k[l].reshape(1, -1), (8, CMP_LEN * HEAD_DIM)).astype(BF)
        pe_v = jnp.broadcast_to(cmp_pe_v[l].reshape(1, -1), (8, CMP_LEN * HEAD_DIM)).astype(BF)
        k_cmp = _compress(proj, KC_OFF // HEAD_DIM, pe_k, w_cmp_k1[l].astype(BF), w_cmp_k2[l].astype(BF))
        v_cmp = _compress(proj, (KC_OFF + KV_WIDTH) // HEAD_DIM, pe_v,
                          w_cmp_v1[l].astype(BF), w_cmp_v2[l].astype(BF))

        later = [w_conv_out[l], w_nsa_out[l], w_out[l], w_ffn_gate[l], w_ffn_up[l], w_ffn_down[l]]
        o_nsa, (wb_conv, wb_nsa, wb_out, wb_gate, wb_up, wb_down) = _nsa_attention(proj, k_cmp, v_cmp, wb, sb, cf, later)

        z = _gated_mix(h, u_conv.reshape(m, CONV_DIM), o_nsa.reshape(m, Q_WIDTH), wb_conv, wb_nsa, w_gates)
        mix = _matmul(z, wb_out, BF, 1024, 1024).reshape(b, s, d)
        x, h2 = _post_mix(mix, x, g_post_mix[l][None, :], g_pre_ffn[l][None, :], mod3, 2, 3, 4)

        ff = _ffn_up(h2.reshape(m, d), wb_gate, wb_up)
        f = _matmul(ff, wb_down, BF, 512, 512).reshape(b, s, d)
        x = _post_ffn(f, x, g_post_ffn[l][None, :], mod3, 5)
    return x
```

```python
import functools
import math

import numpy as np
import jax
import jax.numpy as jnp
from jax import lax
from jax.experimental import pallas as pl
from jax.experimental.pallas import tpu as pltpu

F32 = jnp.float32
BF = jnp.bfloat16

CONV_DIM = 2048
CONV_WIDTH = 31
N_HEADS = 16
N_KV = 4
HPG = N_HEADS // N_KV
HEAD_DIM = 128
CMP_LEN = 32
CMP_STRIDE = 16
CMP_HID = 512
SLC_BLOCK = 64
N_SEL = 16
WINDOW = 512
Q_BLOCK = 64
REL_BUCKETS = 32
REL_MAX_DIST = 128
NORM_EPS = 1e-6
NEG = -1e30
LOG2E = 1.4426950408889634

Q_WIDTH = N_HEADS * HEAD_DIM
KV_WIDTH = N_KV * HEAD_DIM
Q_OFF = 2 * CONV_DIM
KC_OFF = Q_OFF + Q_WIDTH
NG_OFF = KC_OFF + 6 * KV_WIDTH
MG_OFF = NG_OFF + 3 * N_HEADS

LANES = 128
SUBLANES = 8
BF16_SUBLANES = 16
VMEM_LIMIT = 56 * 1024 * 1024
ATTN_VMEM_LIMIT = 61 * 1024 * 1024

CONV_HALO = 32
CONV_ROWS = 64
CONV_LANES = 256

Q_TILE = 4 * Q_BLOCK
SLC_PAD = 2 * SLC_BLOCK
SLC_LOCAL_KEYS = SLC_PAD + Q_TILE
SLC_CHUNK_BLOCKS = 16
WIN_KEYS = WINDOW + Q_TILE
WIN_PAD = WINDOW
CMP_PER_TILE = Q_TILE // CMP_STRIDE
CMP_BAND = 32
CMP_BAND_LO = 9
PAD_FLAG_COL = 64
KEY_FORCED = 0x7F000000
KEY_FUTURE = -2


def _t5_thresholds():
    max_exact = REL_BUCKETS // 2
    thr = []
    for k in range(1, REL_BUCKETS - max_exact):
        n = max_exact
        while max_exact + int(math.log(n / max_exact) / math.log(REL_MAX_DIST / max_exact)
                              * (REL_BUCKETS - max_exact)) < max_exact + k:
            n += 1
        thr.append(n)
    return tuple(thr)


T5_THRESHOLDS = _t5_thresholds()


def _cparams(*sem):
    return pltpu.CompilerParams(dimension_semantics=sem, vmem_limit_bytes=VMEM_LIMIT)


def _silu(v):
    return v * jax.nn.sigmoid(v)


def _dot(a, b):
    return jnp.dot(a, b, preferred_element_type=F32)


def _dot_nt(a, b):
    return lax.dot_general(a, b, (((1,), (1,)), ((), ())), preferred_element_type=F32)


def _rms(v, g):
    return v * lax.rsqrt(jnp.mean(v * v, axis=-1, keepdims=True) + NORM_EPS) * g


def _mod_kernel(c_ref, w_ref, b_ref, o_ref):
    a = _silu(c_ref[...]).astype(BF)
    o_ref[...] = _dot(a, w_ref[...].astype(BF)) + b_ref[...]


def _modulation(c8, w_ada, b_ada, tn=512):
    rows, d = c8.shape
    n = w_ada.shape[1]
    return pl.pallas_call(
        _mod_kernel,
        grid=(n // tn,),
        in_specs=[pl.BlockSpec((rows, d), lambda j: (0, 0)),
                  pl.BlockSpec((d, tn), lambda j: (0, j)),
                  pl.BlockSpec((1, tn), lambda j: (0, j))],
        out_specs=pl.BlockSpec((rows, tn), lambda j: (0, j)),
        out_shape=jax.ShapeDtypeStruct((rows, n), F32),
        compiler_params=_cparams("parallel"),
    )(c8, w_ada, b_ada)


def _prenorm_kernel(x_ref, g_ref, mod_ref, h_ref, *, shift_row, scale_row):
    y = _rms(x_ref[0], g_ref[...])
    h = y * (1.0 + mod_ref[0, scale_row:scale_row + 1, :]) + mod_ref[0, shift_row:shift_row + 1, :]
    h_ref[0] = h.astype(BF)


def _prenorm(x, g, mod3, shift_row, scale_row, tr=512):
    b, s, d = x.shape
    return pl.pallas_call(
        functools.partial(_prenorm_kernel, shift_row=shift_row, scale_row=scale_row),
        grid=(b, s // tr),
        in_specs=[pl.BlockSpec((1, tr, d), lambda i, r: (i, r, 0)),
                  pl.BlockSpec((1, d), lambda i, r: (0, 0)),
                  pl.BlockSpec((1, 6, d), lambda i, r: (i, 0, 0))],
        out_specs=pl.BlockSpec((1, tr, d), lambda i, r: (i, r, 0)),
        out_shape=jax.ShapeDtypeStruct((b, s, d), BF),
        compiler_params=_cparams("parallel", "parallel"),
    )(x, g, mod3)


def _post_mix_kernel(y_ref, x_ref, gp_ref, gn_ref, mod_ref, xo_ref, h_ref, *, gate_row, shift_row, scale_row):
    x1 = x_ref[0] + mod_ref[0, gate_row:gate_row + 1, :] * _rms(y_ref[0].astype(F32), gp_ref[...])
    xo_ref[0] = x1
    h = _rms(x1, gn_ref[...]) * (1.0 + mod_ref[0, scale_row:scale_row + 1, :]) \
        + mod_ref[0, shift_row:shift_row + 1, :]
    h_ref[0] = h.astype(BF)


def _post_mix(y, x, g_post, g_next, mod3, gate_row, shift_row, scale_row, tr=256):
    b, s, d = x.shape
    row = pl.BlockSpec((1, tr, d), lambda i, r: (i, r, 0))
    vec = pl.BlockSpec((1, d), lambda i, r: (0, 0))
    return pl.pallas_call(
        functools.partial(_post_mix_kernel, gate_row=gate_row, shift_row=shift_row, scale_row=scale_row),
        grid=(b, s // tr),
        in_specs=[row, row, vec, vec, pl.BlockSpec((1, 6, d), lambda i, r: (i, 0, 0))],
        out_specs=[row, row],
        out_shape=[jax.ShapeDtypeStruct((b, s, d), F32), jax.ShapeDtypeStruct((b, s, d), BF)],
        compiler_params=_cparams("parallel", "parallel"),
    )(y, x, g_post, g_next, mod3)


def _post_ffn_kernel(y_ref, x_ref, gp_ref, mod_ref, xo_ref, *, gate_row):
    xo_ref[0] = x_ref[0] + mod_ref[0, gate_row:gate_row + 1, :] * _rms(y_ref[0].astype(F32), gp_ref[...])


def _post_ffn(y, x, g_post, mod3, gate_row, tr=256):
    b, s, d = x.shape
    row = pl.BlockSpec((1, tr, d), lambda i, r: (i, r, 0))
    return pl.pallas_call(
        functools.partial(_post_ffn_kernel, gate_row=gate_row),
        grid=(b, s // tr),
        in_specs=[row, row, pl.BlockSpec((1, d), lambda i, r: (0, 0)),
                  pl.BlockSpec((1, 6, d), lambda i, r: (i, 0, 0))],
        out_specs=row,
        out_shape=jax.ShapeDtypeStruct((b, s, d), F32),
        compiler_params=_cparams("parallel", "parallel"),
    )(y, x, g_post, mod3)


def _mm_scale_kernel(a_ref, w_ref, s_ref, o_ref):
    o_ref[...] = (_dot(a_ref[...], w_ref[...]) * s_ref[...]).astype(o_ref.dtype)


def _mm_kernel(a_ref, w_ref, o_ref):
    o_ref[...] = _dot(a_ref[...], w_ref[...]).astype(o_ref.dtype)


def _matmul(a, w, out_dtype, tm, tn, col_scale=None):
    m, k = a.shape
    n = w.shape[1]
    in_specs = [pl.BlockSpec((tm, k), lambda i, j: (i, 0)),
                pl.BlockSpec((k, tn), lambda i, j: (0, j))]
    args = [a, w]
    body = _mm_kernel
    if col_scale is not None:
        in_specs.append(pl.BlockSpec((1, tn), lambda i, j: (0, j)))
        args.append(col_scale)
        body = _mm_scale_kernel
    return pl.pallas_call(
        body,
        grid=(m // tm, n // tn),
        in_specs=in_specs,
        out_specs=pl.BlockSpec((tm, tn), lambda i, j: (i, j)),
        out_shape=jax.ShapeDtypeStruct((m, n), out_dtype),
        compiler_params=_cparams("parallel", "arbitrary"),
    )(*args)


def _mm_tail_kernel(a_ref, w_ref, wt_ref, s_ref, o_ref, *, n_main):
    j = pl.program_id(1)

    @pl.when(j < n_main)
    def _main():
        o_ref[...] = (_dot(a_ref[...], w_ref[...]) * s_ref[...]).astype(o_ref.dtype)

    @pl.when(j >= n_main)
    def _tail():
        o_ref[...] = (_dot(a_ref[...], wt_ref[...]) * s_ref[...]).astype(o_ref.dtype)


def _matmul_with_tail(a, w, n_w, w_tail, col_scale, out_dtype, tm, tn):
    m, k = a.shape
    n_main = n_w // tn
    n_tail = w_tail.shape[1] // tn
    n = n_w + w_tail.shape[1]
    return pl.pallas_call(
        functools.partial(_mm_tail_kernel, n_main=n_main),
        grid=(m // tm, n_main + n_tail),
        in_specs=[pl.BlockSpec((tm, k), lambda i, j: (i, 0)),
                  pl.BlockSpec((k, tn), lambda i, j: (0, jnp.minimum(j, n_main - 1))),
                  pl.BlockSpec((k, tn), lambda i, j: (0, jnp.maximum(j - n_main, 0)), pipeline_mode=pl.Buffered(1)),
                  pl.BlockSpec((1, tn), lambda i, j: (0, j))],
        out_specs=pl.BlockSpec((tm, tn), lambda i, j: (i, j)),
        out_shape=jax.ShapeDtypeStruct((m, n), out_dtype),
        compiler_params=_cparams("parallel", "arbitrary"),
    )(a, w, w_tail, col_scale)


def _mix_kernel(h_ref, a1_ref, a2_ref, w1_ref, w2_ref, wga_ref, wgb_ref, o_ref):
    h = h_ref[...]
    ya = _dot(a1_ref[...], w1_ref[...])
    yb = _dot(a2_ref[...], w2_ref[...])
    o = jax.nn.sigmoid(_dot(h, wga_ref[...])) * ya + jax.nn.sigmoid(_dot(h, wgb_ref[...])) * yb
    o_ref[...] = o.astype(o_ref.dtype)


def _gated_mix(h, u_conv, o_nsa, w_conv_out, w_nsa_out, w_gates, tm=1024, tn=256):
    m, k = h.shape
    ka = u_conv.shape[1]
    kb = o_nsa.shape[1]
    d_model = w_conv_out.shape[1]
    gb_block = d_model // tn
    return pl.pallas_call(
        _mix_kernel,
        grid=(m // tm, d_model // tn),
        in_specs=[pl.BlockSpec((tm, k), lambda i, j: (i, 0)),
                  pl.BlockSpec((tm, ka), lambda i, j: (i, 0)),
                  pl.BlockSpec((tm, kb), lambda i, j: (i, 0)),
                  pl.BlockSpec((ka, tn), lambda i, j: (0, j)),
                  pl.BlockSpec((kb, tn), lambda i, j: (0, j)),
                  pl.BlockSpec((k, tn), lambda i, j: (0, j)),
                  pl.BlockSpec((k, tn), lambda i, j: (0, j + gb_block))],
        out_specs=pl.BlockSpec((tm, tn), lambda i, j: (i, j)),
        out_shape=jax.ShapeDtypeStruct((m, d_model), BF),
        compiler_params=_cparams("parallel", "arbitrary"),
    )(h, u_conv, o_nsa, w_conv_out, w_nsa_out, w_gates, w_gates)


def _ffn_up_kernel(a_ref, wg_ref, wu_ref, o_ref):
    a = a_ref[...]
    o_ref[...] = (_silu(_dot(a, wg_ref[...])) * _dot(a, wu_ref[...])).astype(o_ref.dtype)


def _ffn_up(h, w_gate, w_up, tm=2048, tn=256):
    m, k = h.shape
    n = w_gate.shape[1]
    wspec = pl.BlockSpec((k, tn), lambda i, j: (0, j))
    return pl.pallas_call(
        _ffn_up_kernel,
        grid=(m // tm, n // tn),
        in_specs=[pl.BlockSpec((tm, k), lambda i, j: (i, 0)), wspec, wspec],
        out_specs=pl.BlockSpec((tm, tn), lambda i, j: (i, j)),
        out_shape=jax.ShapeDtypeStruct((m, n), BF),
        compiler_params=_cparams("parallel", "arbitrary"),
    )(h, w_gate, w_up)


def _conv_kernel(a_ref, b_ref, ah_ref, bh_ref, w_ref, bdw_ref, g_ref, bb_ref, o_ref, u_ref, sh_ref, v_ref, *, ts):
    i = pl.program_id(1)
    u_ref[CONV_HALO:CONV_HALO + ts, :] = a_ref[0].astype(F32) * jax.nn.sigmoid(b_ref[0].astype(F32))
    uh = ah_ref[0].astype(F32) * jax.nn.sigmoid(bh_ref[0].astype(F32))
    u_ref[0:CONV_HALO, :] = jnp.where(i > 0, uh, 0.0)

    rc, cc = CONV_ROWS, CONV_LANES
    first = CONV_HALO - (CONV_WIDTH - 1)
    span = ts + CONV_HALO - SUBLANES
    for c0 in range(0, CONV_DIM, cc):
        for s in range(1, SUBLANES):
            sh_ref[s - 1] = u_ref[s:s + span, c0:c0 + cc]

        def row_chunk(rb, carry, c0=c0):
            r0 = pl.multiple_of(rb * rc, rc)
            acc = jnp.zeros((rc, cc), F32)
            for k in range(CONV_WIDTH):
                s = (first + k) % SUBLANES
                base = first + k - s
                if s == 0:
                    tap = u_ref[pl.ds(r0 + base, rc), c0:c0 + cc]
                else:
                    tap = sh_ref[s - 1, pl.ds(r0 + base, rc), :]
                acc = acc + tap * w_ref[k:k + 1, c0:c0 + cc]
            v_ref[pl.ds(r0, rc), c0:c0 + cc] = acc + bdw_ref[:, c0:c0 + cc]
            return carry

        lax.fori_loop(0, ts // rc, row_chunk, 0)

    v = v_ref[...]
    mu = jnp.mean(v, axis=-1, keepdims=True)
    vc = v - mu
    y = vc * lax.rsqrt(jnp.mean(vc * vc, axis=-1, keepdims=True) + NORM_EPS) * g_ref[...] + bb_ref[...]
    o_ref[0] = _silu(y).astype(o_ref.dtype)


def _conv_module(proj, w_dw, b_dw, ln_g, ln_b, ts=256):
    b, s, _ = proj.shape
    hb = ts // CONV_HALO
    cur_a = pl.BlockSpec((1, ts, CONV_DIM), lambda i, r: (i, r, 0))
    cur_b = pl.BlockSpec((1, ts, CONV_DIM), lambda i, r: (i, r, 1))
    halo_a = pl.BlockSpec((1, CONV_HALO, CONV_DIM), lambda i, r: (i, jnp.maximum(r * hb - 1, 0), 0))
    halo_b = pl.BlockSpec((1, CONV_HALO, CONV_DIM), lambda i, r: (i, jnp.maximum(r * hb - 1, 0), 1))
    vec = pl.BlockSpec((1, CONV_DIM), lambda i, r: (0, 0))
    return pl.pallas_call(
        functools.partial(_conv_kernel, ts=ts),
        grid=(b, s // ts),
        in_specs=[cur_a, cur_b, halo_a, halo_b,
                  pl.BlockSpec((CONV_WIDTH, CONV_DIM), lambda i, r: (0, 0)), vec, vec, vec],
        out_specs=pl.BlockSpec((1, ts, CONV_DIM), lambda i, r: (i, r, 0)),
        out_shape=jax.ShapeDtypeStruct((b, s, CONV_DIM), BF),
        scratch_shapes=[pltpu.VMEM((CONV_HALO + ts, CONV_DIM), F32),
                        pltpu.VMEM((SUBLANES - 1, ts + CONV_HALO - SUBLANES, CONV_LANES), F32),
                        pltpu.VMEM((ts, CONV_DIM), F32)],
        compiler_params=_cparams("parallel", "arbitrary"),
    )(proj, proj, proj, proj, w_dw, b_dw, ln_g, ln_b)


def _compress_kernel(k_ref, pe_ref, w1_ref, w2_ref, o_ref, kf_ref):
    kf_ref[...] = k_ref[0].astype(F32)
    rows = kf_ref.shape[0] // CMP_STRIDE
    lo = jnp.zeros((rows, CMP_HID), F32)
    hi = jnp.zeros((rows, CMP_HID), F32)
    for l in range(CMP_STRIDE):
        tok = kf_ref[pl.ds(l, rows, stride=CMP_STRIDE), :].astype(BF)
        lo = lo + _dot(tok, w1_ref[l * HEAD_DIM:(l + 1) * HEAD_DIM, :])
        hi = hi + _dot(tok, w1_ref[(CMP_STRIDE + l) * HEAD_DIM:(CMP_STRIDE + l + 1) * HEAD_DIM, :])
    pe = _dot(pe_ref[...], w1_ref[...])
    hid = lo + pltpu.roll(hi, rows - 1, axis=0) + pe[0:1, :]
    act = 0.5 * hid * (1.0 + jnp.tanh(math.sqrt(2.0 / math.pi) * (hid + 0.044715 * hid * hid * hid)))
    o_ref[0] = _dot(act.astype(BF), w2_ref[...])


def _compress(proj, col_block, pe8, w1, w2):
    b, s, _ = proj.shape
    rows = s // CMP_STRIDE
    return pl.pallas_call(
        _compress_kernel,
        grid=(b, N_KV),
        in_specs=[pl.BlockSpec((1, s, HEAD_DIM), lambda i, g: (i, 0, col_block + g)),
                  pl.BlockSpec(pe8.shape, lambda i, g: (0, 0)),
                  pl.BlockSpec(w1.shape, lambda i, g: (0, 0)),
                  pl.BlockSpec(w2.shape, lambda i, g: (0, 0))],
        out_specs=pl.BlockSpec((1, rows, HEAD_DIM), lambda i, g: (i * N_KV + g, 0, 0)),
        out_shape=jax.ShapeDtypeStruct((b * N_KV, rows, HEAD_DIM), F32),
        scratch_shapes=[pltpu.VMEM((s, HEAD_DIM), F32)],
        compiler_params=_cparams("parallel", "parallel"),
    )(proj, pe8, w1, w2)


def _bias_tables_kernel(rb_ref, wb_ref, sb_ref, cf_ref):
    h = pl.program_id(0)
    last = rb_ref[REL_BUCKETS - 1, h]

    def rel(dist):
        n = jnp.maximum(dist, 0)
        bucket = jnp.full(n.shape, REL_BUCKETS // 2, jnp.int32)
        for t in T5_THRESHOLDS:
            bucket = bucket + (n >= t).astype(jnp.int32)
        bucket = jnp.where(n < REL_BUCKETS // 2, n, jnp.minimum(bucket, REL_BUCKETS - 1))
        out = jnp.zeros(n.shape, F32)
        for bk in range(REL_BUCKETS - 1):
            out = jnp.where(bucket == bk, rb_ref[bk, h] - last, out)
        return out * LOG2E

    def toeplitz(fn, offset, keys):
        width = keys + Q_TILE
        j = lax.broadcasted_iota(jnp.int32, (SUBLANES, width), 1)
        diag = fn(offset + Q_TILE - 1 - j)
        rows = jnp.concatenate([diag] * (Q_TILE // SUBLANES), axis=0)
        return pltpu.roll(rows, width - (Q_TILE - 1), 1, stride=1, stride_axis=0)[:, 0:keys]

    wb_ref[0] = toeplitz(lambda d: jnp.where((d >= 0) & (d < WINDOW), rel(d), NEG), WIN_PAD, WIN_KEYS)
    sb_ref[0] = toeplitz(lambda d: jnp.where(d >= 0, rel(d), NEG), SLC_PAD, SLC_LOCAL_KEYS)

    qi = lax.broadcasted_iota(jnp.int32, (Q_TILE, LANES), 0)
    col = lax.broadcasted_iota(jnp.int32, (Q_TILE, LANES), 1)
    mi = col & (CMP_BAND - 1)
    part = col >> (CMP_BAND.bit_length() - 1)
    dc = qi - CMP_STRIDE * (mi - CMP_BAND_LO) - (CMP_LEN - 1)
    val = jnp.where(dc >= 0, rel(dc), 0.0)
    p0 = val.astype(BF).astype(F32)
    r1 = val - p0
    p1 = r1.astype(BF).astype(F32)
    p2 = r1 - p1
    parts = jnp.where(part == 0, p0, jnp.where(part == 1, p1, jnp.where(part == 2, p2, 0.0)))
    cf_ref[0] = parts.astype(BF)


def _bias_tables(rel_bias):
    smem = pl.BlockSpec(memory_space=pltpu.SMEM)
    lk = SLC_LOCAL_KEYS
    return pl.pallas_call(
        _bias_tables_kernel,
        grid=(N_HEADS,),
        in_specs=[smem],
        out_specs=[pl.BlockSpec((1, Q_TILE, WIN_KEYS), lambda h: (h, 0, 0)),
                   pl.BlockSpec((1, Q_TILE, lk), lambda h: (h, 0, 0)),
                   pl.BlockSpec((1, Q_TILE, LANES), lambda h: (h, 0, 0))],
        out_shape=[jax.ShapeDtypeStruct((N_HEADS, Q_TILE, WIN_KEYS), F32),
                   jax.ShapeDtypeStruct((N_HEADS, Q_TILE, lk), F32),
                   jax.ShapeDtypeStruct((N_HEADS, Q_TILE, LANES), BF)],
        compiler_params=_cparams("arbitrary"),
    )(rel_bias)


def _softmax_parts(s):
    m = jnp.max(s, axis=-1, keepdims=True)
    e = jnp.exp2(s - m)
    return m, e, jnp.sum(e, axis=-1, keepdims=True)


def _lane_tiles(v, op):
    out = v[:, 0:LANES]
    for t in range(1, v.shape[1] // LANES):
        out = op(out, v[:, t * LANES:(t + 1) * LANES])
    return out


def _attn_kernel(*refs, seq, n_cast):
    (q_ref, ks_ref, vs_ref, kw_ref, vw_ref, kc_ref, vc_ref, ng_ref,
     wb_ref, sb_ref, cf_ref, ovt_ref, eye_ref, spread_ref) = refs[:N_ATTN_IN]
    cast_in = refs[N_ATTN_IN:N_ATTN_IN + n_cast]
    o_ref = refs[N_ATTN_IN + n_cast]
    cast_out = refs[N_ATTN_IN + n_cast + 1:N_ATTN_IN + 2 * n_cast + 1]
    kaug_ref, vsp_ref, kwp_ref, vwp_ref, key_ref, sfar_ref = refs[N_ATTN_IN + 2 * n_cast + 1:]

    for w_ref, wo_ref in zip(cast_in, cast_out):
        wo_ref[...] = w_ref[...].astype(BF)

    tile = pl.program_id(2)
    n_slc = seq // SLC_BLOCK
    ncp = seq // CMP_STRIDE
    rows = HPG * Q_TILE
    n_sel = min(N_SEL, n_slc)
    if n_slc != PAD_FLAG_COL or Q_TILE % LANES or 3 * CMP_BAND > LANES:
        raise NotImplementedError("selection layout expects 64 selection blocks and whole-lane query tiles")
    assert CMP_STRIDE * (CMP_BAND_LO + 1) - (CMP_LEN - 1) >= T5_THRESHOLDS[-1]
    assert CMP_STRIDE * (CMP_BAND - CMP_BAND_LO) + CMP_LEN - 1 >= Q_TILE

    @pl.when(tile == 0)
    def _stage_keys():
        col = lax.broadcasted_iota(jnp.int32, (SLC_PAD, LANES), 1)
        kaug_ref[0:SLC_PAD, 0:HEAD_DIM] = jnp.zeros((SLC_PAD, HEAD_DIM), BF)
        kaug_ref[0:SLC_PAD, HEAD_DIM:] = jnp.where(col == PAD_FLAG_COL, NEG, 0.0).astype(BF)
        kaug_ref[SLC_PAD:, 0:HEAD_DIM] = ks_ref[0]
        blk = lax.broadcasted_iota(jnp.int32, (seq, LANES), 0) >> (SLC_BLOCK.bit_length() - 1)
        colk = lax.broadcasted_iota(jnp.int32, (seq, LANES), 1)
        kaug_ref[SLC_PAD:, HEAD_DIM:] = jnp.where(colk == blk, NEG, 0.0).astype(BF)
        vsp_ref[:, HEAD_DIM:] = jnp.ones((SLC_PAD + seq, LANES), BF)
        vsp_ref[0:SLC_PAD, 0:HEAD_DIM] = jnp.zeros((SLC_PAD, HEAD_DIM), BF)
        vsp_ref[SLC_PAD:, 0:HEAD_DIM] = vs_ref[0]
        kwp_ref[0:WIN_PAD, :] = jnp.zeros((WIN_PAD, HEAD_DIM), BF)
        kwp_ref[WIN_PAD:, :] = kw_ref[0]
        vwp_ref[:, HEAD_DIM:] = jnp.ones((WIN_PAD + seq, LANES), BF)
        vwp_ref[0:WIN_PAD, 0:HEAD_DIM] = jnp.zeros((WIN_PAD, HEAD_DIM), BF)
        vwp_ref[WIN_PAD:, 0:HEAD_DIM] = vw_ref[0]

    q4 = jnp.concatenate([q_ref[0, :, hh * HEAD_DIM:(hh + 1) * HEAD_DIM] for hh in range(HPG)], axis=0)
    t0 = tile * Q_TILE
    r0 = pl.multiple_of(t0, Q_TILE)

    kc = kc_ref[0].astype(BF)
    vc = vc_ref[0].astype(BF)
    r_io = lax.broadcasted_iota(jnp.int32, (LANES, ncp), 0)
    n_io = lax.broadcasted_iota(jnp.int32, (LANES, ncp), 1)
    band = (r_io < 3 * CMP_BAND) & (n_io == CMP_PER_TILE * tile + (r_io & (CMP_BAND - 1)) - CMP_BAND_LO)
    shift = jnp.where(band, 1.0, 0.0).astype(BF)
    cf4 = cf_ref[...].reshape(rows, LANES)
    lc = _dot_nt(q4, kc) + _dot(cf4, shift)
    qi_c = lax.broadcasted_iota(jnp.int32, (Q_TILE, ncp), 0)
    n_c = lax.broadcasted_iota(jnp.int32, (Q_TILE, ncp), 1)
    vis = (CMP_STRIDE * n_c + (CMP_LEN - 1)) <= (t0 + qi_c)
    lc3 = jnp.where(vis[None], lc.reshape(HPG, Q_TILE, ncp), NEG)
    _, e_c, l_c = _softmax_parts(lc3)
    p_c = jnp.where(vis[None], e_c * (1.0 / l_c), 0.0)
    o_cmp = _dot(p_c.reshape(rows, ncp).astype(BF), vc)

    psum = p_c[0]
    for hh in range(1, HPG):
        psum = psum + p_c[hh]
    p0 = psum.astype(BF)
    r1 = psum - p0.astype(F32)
    p1 = r1.astype(BF)
    p2 = (r1 - p1.astype(F32)).astype(BF)
    ovt = ovt_ref[...]
    imp_t = _dot_nt(ovt, p0) + _dot_nt(ovt, p1) + _dot_nt(ovt, p2)

    gts = jax.nn.sigmoid(ng_ref[0].astype(F32))
    g_hi = gts.astype(BF)
    g_lo = (gts - g_hi.astype(F32)).astype(BF)
    spread = spread_ref[...]
    gsp = _dot(g_hi, spread) + _dot(g_lo, spread)

    key_w = lax.broadcasted_iota(jnp.int32, (1, WIN_KEYS), 1)
    before_start = jnp.where(key_w < WIN_PAD - t0, NEG, 0.0)
    s_w = _dot_nt(q4, kwp_ref[pl.ds(r0, WIN_KEYS), :]) + wb_ref[...].reshape(rows, WIN_KEYS) + before_start
    e_w = jnp.exp2(s_w - jnp.max(s_w, axis=-1, keepdims=True))
    pv_w = _dot(e_w.astype(BF), vwp_ref[pl.ds(r0, WIN_KEYS), :])
    o_win = pv_w[:, 0:HEAD_DIM] * (1.0 / pv_w[:, HEAD_DIM:])

    def gate(hh, branch):
        c = 3 * hh + branch
        return gsp[:, c * LANES:(c + 1) * LANES]

    gated_cw = [gate(hh, 0) * o_cmp[hh * Q_TILE:(hh + 1) * Q_TILE] + gate(hh, 2) * o_win[hh * Q_TILE:(hh + 1) * Q_TILE]
                for hh in range(HPG)]

    blk_t = lax.broadcasted_iota(jnp.int32, (n_slc, Q_TILE), 0)
    q_t = lax.broadcasted_iota(jnp.int32, (n_slc, Q_TILE), 1)
    cur_t = (t0 + q_t) >> (SLC_BLOCK.bit_length() - 1)
    forced_t = (blk_t == 0) | (blk_t == cur_t) | (blk_t == cur_t - 1)
    causal_t = blk_t <= cur_t
    key = jnp.where(forced_t, KEY_FORCED, jnp.where(causal_t, lax.bitcast_convert_type(imp_t, jnp.int32), KEY_FUTURE))
    key_ref[...] = key
    groups = [key[8 * a:8 * a + 8, :] for a in range(n_slc // 8)]
    ranks = [jnp.zeros((8, Q_TILE), jnp.int32) for _ in groups]
    sub = lax.broadcasted_iota(jnp.int32, (8, Q_TILE), 0)
    for bb in range(n_slc):
        kb = key_ref[bb:bb + 1, :]
        kb1 = kb + 1
        for a in range(n_slc // 8):
            if 8 * a > bb:
                ahead = kb1
            elif 8 * a + 7 < bb:
                ahead = kb
            else:
                ahead = jnp.where(sub > bb - 8 * a, kb1, kb)
            ranks[a] = ranks[a] + (ahead > groups[a]).astype(jnp.int32)
    rank = jnp.concatenate(ranks, axis=0)
    sel_t = (rank < n_sel) & causal_t
    first_local = (t0 - SLC_PAD) >> (SLC_BLOCK.bit_length() - 1)
    far_t = blk_t < first_local
    flag_rows = jnp.where(blk_t == 0, 1.0, 0.0)
    stack = jnp.concatenate([jnp.where(sel_t, 0.0, 1.0), flag_rows,
                             jnp.where(sel_t & far_t, 0.0, 1.0), flag_rows], axis=0).astype(BF)
    qmask = _dot_nt(eye_ref[...], stack).astype(BF)
    qa_loc = jnp.concatenate([q4, qmask[:, 0:LANES]], axis=1)
    qa_far = jnp.concatenate([q4, qmask[:, LANES:]], axis=1)

    lk = SLC_LOCAL_KEYS
    chunk = SLC_CHUNK_BLOCKS * SLC_BLOCK
    n_far = (first_local + SLC_CHUNK_BLOCKS - 1) // SLC_CHUNK_BLOCKS

    s_loc = _dot_nt(qa_loc, kaug_ref[pl.ds(r0, lk), :]) + sb_ref[...].reshape(rows, lk)

    def far_logits(c, m_part):
        rr = pl.multiple_of(SLC_PAD + c * chunk, SLC_BLOCK)
        s = _dot_nt(qa_far, kaug_ref[pl.ds(rr, chunk), :])
        sfar_ref[c] = s
        return jnp.maximum(m_part, _lane_tiles(s, jnp.maximum))

    m_part = lax.fori_loop(0, n_far, far_logits, _lane_tiles(s_loc, jnp.maximum))
    m_s = jnp.max(m_part, axis=-1, keepdims=True)

    e_loc = jnp.exp2(s_loc - m_s)
    acc_s = _dot(e_loc.astype(BF), vsp_ref[pl.ds(r0, lk), :])

    def far_values(c, acc):
        rr = pl.multiple_of(SLC_PAD + c * chunk, SLC_BLOCK)
        e = jnp.exp2(sfar_ref[c] - m_s)
        return acc + _dot(e.astype(BF), vsp_ref[pl.ds(rr, chunk), :])

    acc_s = lax.fori_loop(0, n_far, far_values, acc_s)
    o_slc = acc_s[:, 0:HEAD_DIM] * (1.0 / acc_s[:, HEAD_DIM:])

    for hh in range(HPG):
        o_h = gated_cw[hh] + gate(hh, 1) * o_slc[hh * Q_TILE:(hh + 1) * Q_TILE]
        o_ref[0, :, hh * HEAD_DIM:(hh + 1) * HEAD_DIM] = o_h.astype(o_ref.dtype)


N_ATTN_IN = 14


def _nsa_attention(proj, k_cmp, v_cmp, wb, sb, cf, to_cast):
    b, s, _ = proj.shape
    n_steps = b * N_KV * (s // Q_TILE)
    cast_specs = []
    for w in to_cast:
        slab = next(r for r in range(BF16_SUBLANES, w.shape[0] + 1, BF16_SUBLANES)
                    if w.shape[0] % r == 0 and w.shape[0] // r <= n_steps)
        last = w.shape[0] // slab - 1
        cast_specs.append(pl.BlockSpec(
            (slab, w.shape[1]),
            lambda i, g, j, last=last: (jnp.minimum((i * N_KV + g) * (s // Q_TILE) + j, last), 0)))
    ovt = _overlap_matrix(s)
    eye = jnp.asarray(np.tile(np.eye(Q_TILE), (HPG, 1)), BF)
    spread = jnp.asarray(np.repeat(np.eye(LANES)[:, :3 * HPG], LANES, axis=1), BF)
    n_chunks = -(-(s // SLC_BLOCK) // SLC_CHUNK_BLOCKS)
    q_blk = Q_OFF // (HPG * HEAD_DIM)
    kv_blk = KC_OFF // HEAD_DIM
    ng_blk = NG_OFF // LANES
    lk = SLC_LOCAL_KEYS
    ncp = s // CMP_STRIDE

    def kv_spec(idx):
        return pl.BlockSpec((1, s, HEAD_DIM), lambda i, g, j: (i, 0, kv_blk + idx * N_KV + g))

    cmp_spec = pl.BlockSpec((1, ncp, HEAD_DIM), lambda i, g, j: (i * N_KV + g, 0, 0))
    outs = pl.pallas_call(
        functools.partial(_attn_kernel, seq=s, n_cast=len(to_cast)),
        grid=(b, N_KV, s // Q_TILE),
        in_specs=[pl.BlockSpec((1, Q_TILE, HPG * HEAD_DIM), lambda i, g, j: (i, j, q_blk + g)),
                  kv_spec(2), kv_spec(3), kv_spec(4), kv_spec(5),
                  cmp_spec, cmp_spec,
                  pl.BlockSpec((1, Q_TILE, LANES), lambda i, g, j: (i, j, ng_blk + g)),
                  pl.BlockSpec((HPG, Q_TILE, WIN_KEYS), lambda i, g, j: (g, 0, 0)),
                  pl.BlockSpec((HPG, Q_TILE, lk), lambda i, g, j: (g, 0, 0)),
                  pl.BlockSpec((HPG, Q_TILE, LANES), lambda i, g, j: (g, 0, 0)),
                  pl.BlockSpec(ovt.shape, lambda i, g, j: (0, 0)),
                  pl.BlockSpec(eye.shape, lambda i, g, j: (0, 0)),
                  pl.BlockSpec(spread.shape, lambda i, g, j: (0, 0))] + cast_specs,
        out_specs=[pl.BlockSpec((1, Q_TILE, HPG * HEAD_DIM), lambda i, g, j: (i, j, g))] + cast_specs,
        out_shape=[jax.ShapeDtypeStruct((b, s, Q_WIDTH), BF)]
        + [jax.ShapeDtypeStruct(w.shape, BF) for w in to_cast],
        scratch_shapes=[pltpu.VMEM((SLC_PAD + s, 2 * HEAD_DIM), BF),
                        pltpu.VMEM((SLC_PAD + s, HEAD_DIM + LANES), BF),
                        pltpu.VMEM((WIN_PAD + s, HEAD_DIM), BF),
                        pltpu.VMEM((WIN_PAD + s, HEAD_DIM + LANES), BF),
                        pltpu.VMEM((s // SLC_BLOCK, Q_TILE), jnp.int32),
                        pltpu.VMEM((n_chunks, HPG * Q_TILE, SLC_CHUNK_BLOCKS * SLC_BLOCK), F32)],
        compiler_params=pltpu.CompilerParams(dimension_semantics=("arbitrary",) * 3,
                                             vmem_limit_bytes=ATTN_VMEM_LIMIT),
    )(proj, proj, proj, proj, proj, k_cmp, v_cmp, proj, wb, sb, cf, ovt, eye, spread, *to_cast)
    return outs[0], outs[1:]


def _overlap_matrix(s):
    ncp = s // CMP_STRIDE
    n_slc = s // SLC_BLOCK
    nc = (s - CMP_LEN) // CMP_STRIDE + 1
    i = np.arange(ncp)[None, :]
    jj = np.arange(n_slc)[:, None]
    ov = ((i * CMP_STRIDE < (jj + 1) * SLC_BLOCK) & (i * CMP_STRIDE + CMP_LEN > jj * SLC_BLOCK) & (i < nc))
    return jnp.asarray(ov, BF)


def kernel(x, c, w_ada, b_ada, g_pre_mix, g_post_mix, g_pre_ffn, g_post_ffn, w_in, w_dw, b_dw, conv_ln_g, conv_ln_b, w_conv_out, cmp_pe_k, cmp_pe_v, w_cmp_k1, w_cmp_k2, w_cmp_v1, w_cmp_v2, rel_bias, w_nsa_out, w_out, w_ffn_gate, w_ffn_up, w_ffn_down):
    b, s, d = x.shape
    m = b * s
    depth = w_ada.shape[0]
    c8 = jnp.zeros((8, d), F32).at[:b].set(c)
    wb, sb, cf = _bias_tables(rel_bias)
    col_scale = jnp.ones((1, NG_OFF + N_KV * LANES), F32).at[:, Q_OFF:KC_OFF].set(HEAD_DIM ** -0.5 * LOG2E)

    for l in range(depth):
        mod = _modulation(c8, w_ada[l], b_ada[l][None, :])
        mod3 = mod[:b].reshape(b, 6, d)

        w_all = w_in[l].astype(BF)
        w_ng = w_in[l][:, NG_OFF:MG_OFF].reshape(d, N_KV, 3 * HPG)
        w_ng = jnp.pad(w_ng, ((0, 0), (0, 0), (0, LANES - 3 * HPG))).reshape(d, N_KV * LANES)
        w_ng = w_ng.astype(BF)
        w_gates = w_all[:, MG_OFF:]

        h = _prenorm(x, g_pre_mix[l][None, :], mod3, 0, 1).reshape(m, d)
        proj = _matmul_with_tail(h, w_all, NG_OFF, w_ng, col_scale, BF, 2048, 512).reshape(b, s, -1)

        u_conv = _conv_module(proj, w_dw[l], b_dw[l][None, :], conv_ln_g[l][None, :], conv_ln_b[l][None, :])

        pe_k = jnp.broadcast_to(cmp_pe_k[l].reshape(1, -1), (8, CMP_LEN * HEAD_DIM)).astype(BF)
        pe_v = jnp.broadcast_to(cmp_pe_v[l].reshape(1, -1), (8, CMP_LEN * HEAD_DIM)).astype(BF)
        k_cmp = _compress(proj, KC_OFF // HEAD_DIM, pe_k, w_cmp_k1[l].astype(BF), w_cmp_k2[l].astype(BF))
        v_cmp = _compress(proj, (KC_OFF + KV_WIDTH) // HEAD_DIM, pe_v,
                          w_cmp_v1[l].astype(BF), w_cmp_v2[l].astype(BF))

        later = [w_conv_out[l], w_nsa_out[l], w_out[l], w_ffn_gate[l], w_ffn_up[l], w_ffn_down[l]]
        o_nsa, (wb_conv, wb_nsa, wb_out, wb_gate, wb_up, wb_down) = _nsa_attention(proj, k_cmp, v_cmp, wb, sb, cf, later)

        z = _gated_mix(h, u_conv.reshape(m, CONV_DIM), o_nsa.reshape(m, Q_WIDTH), wb_conv, wb_nsa, w_gates)
        mix = _matmul(z, wb_out, BF, 1024, 1024).reshape(b, s, d)
        x, h2 = _post_mix(mix, x, g_post_mix[l][None, :], g_pre_ffn[l][None, :], mod3, 2, 3, 4)

        ff = _ffn_up(h2.reshape(m, d), wb_gate, wb_up)
        f = _matmul(ff, wb_down, BF, 512, 512).reshape(b, s, d)
        x = _post_ffn(f, x, g_post_ffn[l][None, :], mod3, 5)
    return x
```

```python
import functools
import math

import numpy as np
import jax
import jax.numpy as jnp
from jax import lax
from jax.experimental import pallas as pl
from jax.experimental.pallas import tpu as pltpu

F32 = jnp.float32
BF = jnp.bfloat16

CONV_DIM = 2048
CONV_WIDTH = 31
N_HEADS = 16
N_KV = 4
HPG = N_HEADS // N_KV
HEAD_DIM = 128
CMP_LEN = 32
CMP_STRIDE = 16
CMP_HID = 512
SLC_BLOCK = 64
N_SEL = 16
WINDOW = 512
Q_BLOCK = 64
REL_BUCKETS = 32
REL_MAX_DIST = 128
NORM_EPS = 1e-6
NEG = -1e30
LOG2E = 1.4426950408889634

Q_WIDTH = N_HEADS * HEAD_DIM
KV_WIDTH = N_KV * HEAD_DIM
Q_OFF = 2 * CONV_DIM
KC_OFF = Q_OFF + Q_WIDTH
NG_OFF = KC_OFF + 6 * KV_WIDTH
MG_OFF = NG_OFF + 3 * N_HEADS

LANES = 128
SUBLANES = 8
BF16_SUBLANES = 16
VMEM_LIMIT = 56 * 1024 * 1024
ATTN_VMEM_LIMIT = 61 * 1024 * 1024

CONV_HALO = 32
CONV_ROWS = 64
CONV_LANES = 256

Q_TILE = 4 * Q_BLOCK
SLC_PAD = 2 * SLC_BLOCK
SLC_LOCAL_KEYS = SLC_PAD + Q_TILE
SLC_CHUNK_BLOCKS = 16
WIN_KEYS = WINDOW + Q_TILE
WIN_PAD = WINDOW
CMP_PER_TILE = Q_TILE // CMP_STRIDE
CMP_BAND = 32
CMP_BAND_LO = 9
PAD_FLAG_COL = 64
KEY_FORCED = 0x7F000000
KEY_FUTURE = -2


def _t5_thresholds():
    max_exact = REL_BUCKETS // 2
    thr = []
    for k in range(1, REL_BUCKETS - max_exact):
        n = max_exact
        while max_exact + int(math.log(n / max_exact) / math.log(REL_MAX_DIST / max_exact)
                              * (REL_BUCKETS - max_exact)) < max_exact + k:
            n += 1
        thr.append(n)
    return tuple(thr)


T5_THRESHOLDS = _t5_thresholds()


def _cparams(*sem):
    return pltpu.CompilerParams(dimension_semantics=sem, vmem_limit_bytes=VMEM_LIMIT)


def _silu(v):
    return v * jax.nn.sigmoid(v)


def _dot(a, b):
    return jnp.dot(a, b, preferred_element_type=F32)


def _dot_nt(a, b):
    return lax.dot_general(a, b, (((1,), (1,)), ((), ())), preferred_element_type=F32)


def _rms(v, g):
    return v * lax.rsqrt(jnp.mean(v * v, axis=-1, keepdims=True) + NORM_EPS) * g


def _mod_kernel(c_ref, w_ref, b_ref, o_ref):
    a = _silu(c_ref[...]).astype(BF)
    o_ref[...] = _dot(a, w_ref[...].astype(BF)) + b_ref[...]


def _modulation(c8, w_ada, b_ada, tn=512):
    rows, d = c8.shape
    n = w_ada.shape[1]
    return pl.pallas_call(
        _mod_kernel,
        grid=(n // tn,),
        in_specs=[pl.BlockSpec((rows, d), lambda j: (0, 0)),
                  pl.BlockSpec((d, tn), lambda j: (0, j)),
                  pl.BlockSpec((1, tn), lambda j: (0, j))],
        out_specs=pl.BlockSpec((rows, tn), lambda j: (0, j)),
        out_shape=jax.ShapeDtypeStruct((rows, n), F32),
        compiler_params=_cparams("parallel"),
    )(c8, w_ada, b_ada)


def _prenorm_kernel(x_ref, g_ref, mod_ref, h_ref, *, shift_row, scale_row):
    y = _rms(x_ref[0], g_ref[...])
    h = y * (1.0 + mod_ref[0, scale_row:scale_row + 1, :]) + mod_ref[0, shift_row:shift_row + 1, :]
    h_ref[0] = h.astype(BF)


def _prenorm(x, g, mod3, shift_row, scale_row, tr=512):
    b, s, d = x.shape
    return pl.pallas_call(
        functools.partial(_prenorm_kernel, shift_row=shift_row, scale_row=scale_row),
        grid=(b, s // tr),
        in_specs=[pl.BlockSpec((1, tr, d), lambda i, r: (i, r, 0)),
                  pl.BlockSpec((1, d), lambda i, r: (0, 0)),
                  pl.BlockSpec((1, 6, d), lambda i, r: (i, 0, 0))],
        out_specs=pl.BlockSpec((1, tr, d), lambda i, r: (i, r, 0)),
        out_shape=jax.ShapeDtypeStruct((b, s, d), BF),
        compiler_params=_cparams("parallel", "parallel"),
    )(x, g, mod3)


def _post_mix_kernel(y_ref, x_ref, gp_ref, gn_ref, mod_ref, xo_ref, h_ref, *, gate_row, shift_row, scale_row):
    x1 = x_ref[0] + mod_ref[0, gate_row:gate_row + 1, :] * _rms(y_ref[0].astype(F32), gp_ref[...])
    xo_ref[0] = x1
    h = _rms(x1, gn_ref[...]) * (1.0 + mod_ref[0, scale_row:scale_row + 1, :]) \
        + mod_ref[0, shift_row:shift_row + 1, :]
    h_ref[0] = h.astype(BF)


def _post_mix(y, x, g_post, g_next, mod3, gate_row, shift_row, scale_row, tr=256):
    b, s, d = x.shape
    row = pl.BlockSpec((1, tr, d), lambda i, r: (i, r, 0))
    vec = pl.BlockSpec((1, d), lambda i, r: (0, 0))
    return pl.pallas_call(
        functools.partial(_post_mix_kernel, gate_row=gate_row, shift_row=shift_row, scale_row=scale_row),
        grid=(b, s // tr),
        in_specs=[row, row, vec, vec, pl.BlockSpec((1, 6, d), lambda i, r: (i, 0, 0))],
        out_specs=[row, row],
        out_shape=[jax.ShapeDtypeStruct((b, s, d), F32), jax.ShapeDtypeStruct((b, s, d), BF)],
        compiler_params=_cparams("parallel", "parallel"),
    )(y, x, g_post, g_next, mod3)


def _post_ffn_kernel(y_ref, x_ref, gp_ref, mod_ref, xo_ref, *, gate_row):
    xo_ref[0] = x_ref[0] + mod_ref[0, gate_row:gate_row + 1, :] * _rms(y_ref[0].astype(F32), gp_ref[...])


def _post_ffn(y, x, g_post, mod3, gate_row, tr=256):
    b, s, d = x.shape
    row = pl.BlockSpec((1, tr, d), lambda i, r: (i, r, 0))
    return pl.pallas_call(
        functools.partial(_post_ffn_kernel, gate_row=gate_row),
        grid=(b, s // tr),
        in_specs=[row, row, pl.BlockSpec((1, d), lambda i, r: (0, 0)),
                  pl.BlockSpec((1, 6, d), lambda i, r: (i, 0, 0))],
        out_specs=row,
        out_shape=jax.ShapeDtypeStruct((b, s, d), F32),
        compiler_params=_cparams("parallel", "parallel"),
    )(y, x, g_post, mod3)


def _mm_scale_kernel(a_ref, w_ref, s_ref, o_ref):
    o_ref[...] = (_dot(a_ref[...], w_ref[...]) * s_ref[...]).astype(o_ref.dtype)


def _mm_kernel(a_ref, w_ref, o_ref):
    o_ref[...] = _dot(a_ref[...], w_ref[...]).astype(o_ref.dtype)


def _matmul(a, w, out_dtype, tm, tn, col_scale=None):
    m, k = a.shape
    n = w.shape[1]
    in_specs = [pl.BlockSpec((tm, k), lambda i, j: (i, 0)),
                pl.BlockSpec((k, tn), lambda i, j: (0, j))]
    args = [a, w]
    body = _mm_kernel
    if col_scale is not None:
        in_specs.append(pl.BlockSpec((1, tn), lambda i, j: (0, j)))
        args.append(col_scale)
        body = _mm_scale_kernel
    return pl.pallas_call(
        body,
        grid=(m // tm, n // tn),
        in_specs=in_specs,
        out_specs=pl.BlockSpec((tm, tn), lambda i, j: (i, j)),
        out_shape=jax.ShapeDtypeStruct((m, n), out_dtype),
        compiler_params=_cparams("parallel", "arbitrary"),
    )(*args)


def _mm_tail_kernel(a_ref, w_ref, wt_ref, s_ref, o_ref, *, n_main):
    j = pl.program_id(1)

    @pl.when(j < n_main)
    def _main():
        o_ref[...] = (_dot(a_ref[...], w_ref[...]) * s_ref[...]).astype(o_ref.dtype)

    @pl.when(j >= n_main)
    def _tail():
        o_ref[...] = (_dot(a_ref[...], wt_ref[...]) * s_ref[...]).astype(o_ref.dtype)


def _matmul_with_tail(a, w, n_w, w_tail, col_scale, out_dtype, tm, tn):
    m, k = a.shape
    n_main = n_w // tn
    n_tail = w_tail.shape[1] // tn
    n = n_w + w_tail.shape[1]
    return pl.pallas_call(
        functools.partial(_mm_tail_kernel, n_main=n_main),
        grid=(m // tm, n_main + n_tail),
        in_specs=[pl.BlockSpec((tm, k), lambda i, j: (i, 0)),
                  pl.BlockSpec((k, tn), lambda i, j: (0, jnp.minimum(j, n_main - 1))),
                  pl.BlockSpec((k, tn), lambda i, j: (0, jnp.maximum(j - n_main, 0)), pipeline_mode=pl.Buffered(1)),
                  pl.BlockSpec((1, tn), lambda i, j: (0, j))],
        out_specs=pl.BlockSpec((tm, tn), lambda i, j: (i, j)),
        out_shape=jax.ShapeDtypeStruct((m, n), out_dtype),
        compiler_params=_cparams("parallel", "arbitrary"),
    )(a, w, w_tail, col_scale)


def _mix_kernel(h_ref, a1_ref, a2_ref, w1_ref, w2_ref, wga_ref, wgb_ref, o_ref):
    h = h_ref[...]
    ya = _dot(a1_ref[...], w1_ref[...])
    yb = _dot(a2_ref[...], w2_ref[...])
    o = jax.nn.sigmoid(_dot(h, wga_ref[...])) * ya + jax.nn.sigmoid(_dot(h, wgb_ref[...])) * yb
    o_ref[...] = o.astype(o_ref.dtype)


def _gated_mix(h, u_conv, o_nsa, w_conv_out, w_nsa_out, w_gates, tm=1024, tn=256):
    m, k = h.shape
    ka = u_conv.shape[1]
    kb = o_nsa.shape[1]
    d_model = w_conv_out.shape[1]
    gb_block = d_model // tn
    return pl.pallas_call(
        _mix_kernel,
        grid=(m // tm, d_model // tn),
        in_specs=[pl.BlockSpec((tm, k), lambda i, j: (i, 0)),
                  pl.BlockSpec((tm, ka), lambda i, j: (i, 0)),
                  pl.BlockSpec((tm, kb), lambda i, j: (i, 0)),
                  pl.BlockSpec((ka, tn), lambda i, j: (0, j)),
                  pl.BlockSpec((kb, tn), lambda i, j: (0, j)),
                  pl.BlockSpec((k, tn), lambda i, j: (0, j)),
                  pl.BlockSpec((k, tn), lambda i, j: (0, j + gb_block))],
        out_specs=pl.BlockSpec((tm, tn), lambda i, j: (i, j)),
        out_shape=jax.ShapeDtypeStruct((m, d_model), BF),
        compiler_params=_cparams("parallel", "arbitrary"),
    )(h, u_conv, o_nsa, w_conv_out, w_nsa_out, w_gates, w_gates)


def _ffn_up_kernel(a_ref, wg_ref, wu_ref, o_ref):
    a = a_ref[...]
    o_ref[...] = (_silu(_dot(a, wg_ref[...])) * _dot(a, wu_ref[...])).astype(o_ref.dtype)


def _ffn_up(h, w_gate, w_up, tm=2048, tn=256):
    m, k = h.shape
    n = w_gate.shape[1]
    wspec = pl.BlockSpec((k, tn), lambda i, j: (0, j))
    return pl.pallas_call(
        _ffn_up_kernel,
        grid=(m // tm, n // tn),
        in_specs=[pl.BlockSpec((tm, k), lambda i, j: (i, 0)), wspec, wspec],
        out_specs=pl.BlockSpec((tm, tn), lambda i, j: (i, j)),
        out_shape=jax.ShapeDtypeStruct((m, n), BF),
        compiler_params=_cparams("parallel", "arbitrary"),
    )(h, w_gate, w_up)


def _conv_kernel(a_ref, b_ref, ah_ref, bh_ref, w_ref, bdw_ref, g_ref, bb_ref, o_ref, u_ref, sh_ref, v_ref, *, ts):
    i = pl.program_id(1)
    u_ref[CONV_HALO:CONV_HALO + ts, :] = a_ref[0].astype(F32) * jax.nn.sigmoid(b_ref[0].astype(F32))
    uh = ah_ref[0].astype(F32) * jax.nn.sigmoid(bh_ref[0].astype(F32))
    u_ref[0:CONV_HALO, :] = jnp.where(i > 0, uh, 0.0)

    rc, cc = CONV_ROWS, CONV_LANES
    first = CONV_HALO - (CONV_WIDTH - 1)
    span = ts + CONV_HALO - SUBLANES
    for c0 in range(0, CONV_DIM, cc):
        for s in range(1, SUBLANES):
            sh_ref[s - 1] = u_ref[s:s + span, c0:c0 + cc]

        def row_chunk(rb, carry, c0=c0):
            r0 = pl.multiple_of(rb * rc, rc)
            acc = jnp.zeros((rc, cc), F32)
            for k in range(CONV_WIDTH):
                s = (first + k) % SUBLANES
                base = first + k - s
                if s == 0:
                    tap = u_ref[pl.ds(r0 + base, rc), c0:c0 + cc]
                else:
                    tap = sh_ref[s - 1, pl.ds(r0 + base, rc), :]
                acc = acc + tap * w_ref[k:k + 1, c0:c0 + cc]
            v_ref[pl.ds(r0, rc), c0:c0 + cc] = acc + bdw_ref[:, c0:c0 + cc]
            return carry

        lax.fori_loop(0, ts // rc, row_chunk, 0)

    v = v_ref[...]
    mu = jnp.mean(v, axis=-1, keepdims=True)
    vc = v - mu
    y = vc * lax.rsqrt(jnp.mean(vc * vc, axis=-1, keepdims=True) + NORM_EPS) * g_ref[...] + bb_ref[...]
    o_ref[0] = _silu(y).astype(o_ref.dtype)


def _conv_module(proj, w_dw, b_dw, ln_g, ln_b, ts=256):
    b, s, _ = proj.shape
    hb = ts // CONV_HALO
    cur_a = pl.BlockSpec((1, ts, CONV_DIM), lambda i, r: (i, r, 0))
    cur_b = pl.BlockSpec((1, ts, CONV_DIM), lambda i, r: (i, r, 1))
    halo_a = pl.BlockSpec((1, CONV_HALO, CONV_DIM), lambda i, r: (i, jnp.maximum(r * hb - 1, 0), 0))
    halo_b = pl.BlockSpec((1, CONV_HALO, CONV_DIM), lambda i, r: (i, jnp.maximum(r * hb - 1, 0), 1))
    vec = pl.BlockSpec((1, CONV_DIM), lambda i, r: (0, 0))
    return pl.pallas_call(
        functools.partial(_conv_kernel, ts=ts),
        grid=(b, s // ts),
        in_specs=[cur_a, cur_b, halo_a, halo_b,
                  pl.BlockSpec((CONV_WIDTH, CONV_DIM), lambda i, r: (0, 0)), vec, vec, vec],
        out_specs=pl.BlockSpec((1, ts, CONV_DIM), lambda i, r: (i, r, 0)),
        out_shape=jax.ShapeDtypeStruct((b, s, CONV_DIM), BF),
        scratch_shapes=[pltpu.VMEM((CONV_HALO + ts, CONV_DIM), F32),
                        pltpu.VMEM((SUBLANES - 1, ts + CONV_HALO - SUBLANES, CONV_LANES), F32),
                        pltpu.VMEM((ts, CONV_DIM), F32)],
        compiler_params=_cparams("parallel", "arbitrary"),
    )(proj, proj, proj, proj, w_dw, b_dw, ln_g, ln_b)


def _compress_kernel(k_ref, pe_ref, w1_ref, w2_ref, o_ref, kf_ref):
    kf_ref[...] = k_ref[0].astype(F32)
    rows = kf_ref.shape[0] // CMP_STRIDE
    lo = jnp.zeros((rows, CMP_HID), F32)
    hi = jnp.zeros((rows, CMP_HID), F32)
    for l in range(CMP_STRIDE):
        tok = kf_ref[pl.ds(l, rows, stride=CMP_STRIDE), :].astype(BF)
        lo = lo + _dot(tok, w1_ref[l * HEAD_DIM:(l + 1) * HEAD_DIM, :])
        hi = hi + _dot(tok, w1_ref[(CMP_STRIDE + l) * HEAD_DIM:(CMP_STRIDE + l + 1) * HEAD_DIM, :])
    pe = _dot(pe_ref[...], w1_ref[...])
    hid = lo + pltpu.roll(hi, rows - 1, axis=0) + pe[0:1, :]
    act = 0.5 * hid * (1.0 + jnp.tanh(math.sqrt(2.0 / math.pi) * (hid + 0.044715 * hid * hid * hid)))
    o_ref[0] = _dot(act.astype(BF), w2_ref[...])


def _compress(proj, col_block, pe8, w1, w2):
    b, s, _ = proj.shape
    rows = s // CMP_STRIDE
    return pl.pallas_call(
        _compress_kernel,
        grid=(b, N_KV),
        in_specs=[pl.BlockSpec((1, s, HEAD_DIM), lambda i, g: (i, 0, col_block + g)),
                  pl.BlockSpec(pe8.shape, lambda i, g: (0, 0)),
                  pl.BlockSpec(w1.shape, lambda i, g: (0, 0)),
                  pl.BlockSpec(w2.shape, lambda i, g: (0, 0))],
        out_specs=pl.BlockSpec((1, rows, HEAD_DIM), lambda i, g: (i * N_KV + g, 0, 0)),
        out_shape=jax.ShapeDtypeStruct((b * N_KV, rows, HEAD_DIM), F32),
        scratch_shapes=[pltpu.VMEM((s, HEAD_DIM), F32)],
        compiler_params=_cparams("parallel", "parallel"),
    )(proj, pe8, w1, w2)


def _bias_tables_kernel(rb_ref, wb_ref, sb_ref, cf_ref):
    h = pl.program_id(0)
    last = rb_ref[REL_BUCKETS - 1, h]

    def rel(dist):
        n = jnp.maximum(dist, 0)
        bucket = jnp.full(n.shape, REL_BUCKETS // 2, jnp.int32)
        for t in T5_THRESHOLDS:
            bucket = bucket + (n >= t).astype(jnp.int32)
        bucket = jnp.where(n < REL_BUCKETS // 2, n, jnp.minimum(bucket, REL_BUCKETS - 1))
        out = jnp.zeros(n.shape, F32)
        for bk in range(REL_BUCKETS - 1):
            out = jnp.where(bucket == bk, rb_ref[bk, h] - last, out)
        return out * LOG2E

    def toeplitz(fn, offset, keys):
        width = keys + Q_TILE
        j = lax.broadcasted_iota(jnp.int32, (SUBLANES, width), 1)
        diag = fn(offset + Q_TILE - 1 - j)
        rows = jnp.concatenate([diag] * (Q_TILE // SUBLANES), axis=0)
        return pltpu.roll(rows, width - (Q_TILE - 1), 1, stride=1, stride_axis=0)[:, 0:keys]

    wb_ref[0] = toeplitz(lambda d: jnp.where((d >= 0) & (d < WINDOW), rel(d), NEG), WIN_PAD, WIN_KEYS)
    sb_ref[0] = toeplitz(lambda d: jnp.where(d >= 0, rel(d), NEG), SLC_PAD, SLC_LOCAL_KEYS)

    qi = lax.broadcasted_iota(jnp.int32, (Q_TILE, LANES), 0)
    col = lax.broadcasted_iota(jnp.int32, (Q_TILE, LANES), 1)
    mi = col & (CMP_BAND - 1)
    part = col >> (CMP_BAND.bit_length() - 1)
    dc = qi - CMP_STRIDE * (mi - CMP_BAND_LO) - (CMP_LEN - 1)
    val = jnp.where(dc >= 0, rel(dc), 0.0)
    p0 = val.astype(BF).astype(F32)
    r1 = val - p0
    p1 = r1.astype(BF).astype(F32)
    p2 = r1 - p1
    parts = jnp.where(part == 0, p0, jnp.where(part == 1, p1, jnp.where(part == 2, p2, 0.0)))
    cf_ref[0] = parts.astype(BF)


def _bias_tables(rel_bias):
    smem = pl.BlockSpec(memory_space=pltpu.SMEM)
    lk = SLC_LOCAL_KEYS
    return pl.pallas_call(
        _bias_tables_kernel,
        grid=(N_HEADS,),
        in_specs=[smem],
        out_specs=[pl.BlockSpec((1, Q_TILE, WIN_KEYS), lambda h: (h, 0, 0)),
                   pl.BlockSpec((1, Q_TILE, lk), lambda h: (h, 0, 0)),
                   pl.BlockSpec((1, Q_TILE, LANES), lambda h: (h, 0, 0))],
        out_shape=[jax.ShapeDtypeStruct((N_HEADS, Q_TILE, WIN_KEYS), F32),
                   jax.ShapeDtypeStruct((N_HEADS, Q_TILE, lk), F32),
                   jax.ShapeDtypeStruct((N_HEADS, Q_TILE, LANES), BF)],
        compiler_params=_cparams("arbitrary"),
    )(rel_bias)


def _softmax_parts(s):
    m = jnp.max(s, axis=-1, keepdims=True)
    e = jnp.exp2(s - m)
    return m, e, jnp.sum(e, axis=-1, keepdims=True)


def _lane_tiles(v, op):
    out = v[:, 0:LANES]
    for t in range(1, v.shape[1] // LANES):
        out = op(out, v[:, t * LANES:(t + 1) * LANES])
    return out


def _attn_kernel(*refs, seq, n_cast):
    (q_ref, ks_ref, vs_ref, kw_ref, vw_ref, kc_ref, vc_ref, ng_ref,
     wb_ref, sb_ref, cf_ref, ovt_ref, eye_ref, spread_ref) = refs[:N_ATTN_IN]
    cast_in = refs[N_ATTN_IN:N_ATTN_IN + n_cast]
    o_ref = refs[N_ATTN_IN + n_cast]
    cast_out = refs[N_ATTN_IN + n_cast + 1:N_ATTN_IN + 2 * n_cast + 1]
    kaug_ref, vsp_ref, kwp_ref, vwp_ref, key_ref, sfar_ref = refs[N_ATTN_IN + 2 * n_cast + 1:]

    for w_ref, wo_ref in zip(cast_in, cast_out):
        wo_ref[...] = w_ref[...].astype(BF)

    tile = pl.program_id(2)
    n_slc = seq // SLC_BLOCK
    ncp = seq // CMP_STRIDE
    rows = HPG * Q_TILE
    n_sel = min(N_SEL, n_slc)
    if n_slc != PAD_FLAG_COL or Q_TILE % LANES or 3 * CMP_BAND > LANES:
        raise NotImplementedError("selection layout expects 64 selection blocks and whole-lane query tiles")
    assert CMP_STRIDE * (CMP_BAND_LO + 1) - (CMP_LEN - 1) >= T5_THRESHOLDS[-1]
    assert CMP_STRIDE * (CMP_BAND - CMP_BAND_LO) + CMP_LEN - 1 >= Q_TILE

    @pl.when(tile == 0)
    def _stage_keys():
        col = lax.broadcasted_iota(jnp.int32, (SLC_PAD, LANES), 1)
        kaug_ref[0:SLC_PAD, 0:HEAD_DIM] = jnp.zeros((SLC_PAD, HEAD_DIM), BF)
        kaug_ref[0:SLC_PAD, HEAD_DIM:] = jnp.where(col == PAD_FLAG_COL, NEG, 0.0).astype(BF)
        kaug_ref[SLC_PAD:, 0:HEAD_DIM] = ks_ref[0]
        blk = lax.broadcasted_iota(jnp.int32, (seq, LANES), 0) >> (SLC_BLOCK.bit_length() - 1)
        colk = lax.broadcasted_iota(jnp.int32, (seq, LANES), 1)
        kaug_ref[SLC_PAD:, HEAD_DIM:] = jnp.where(colk == blk, NEG, 0.0).astype(BF)
        vsp_ref[:, HEAD_DIM:] = jnp.ones((SLC_PAD + seq, LANES), BF)
        vsp_ref[0:SLC_PAD, 0:HEAD_DIM] = jnp.zeros((SLC_PAD, HEAD_DIM), BF)
        vsp_ref[SLC_PAD:, 0:HEAD_DIM] = vs_ref[0]
        kwp_ref[0:WIN_PAD, :] = jnp.zeros((WIN_PAD, HEAD_DIM), BF)
        kwp_ref[WIN_PAD:, :] = kw_ref[0]
        vwp_ref[:, HEAD_DIM:] = jnp.ones((WIN_PAD + seq, LANES), BF)
        vwp_ref[0:WIN_PAD, 0:HEAD_DIM] = jnp.zeros((WIN_PAD, HEAD_DIM), BF)
        vwp_ref[WIN_PAD:, 0:HEAD_DIM] = vw_ref[0]

    q4 = jnp.concatenate([q_ref[0, :, hh * HEAD_DIM:(hh + 1) * HEAD_DIM] for hh in range(HPG)], axis=0)
    t0 = tile * Q_TILE
    r0 = pl.multiple_of(t0, Q_TILE)

    key_w = lax.broadcasted_iota(jnp.int32, (1, WIN_KEYS), 1)
    before_start = jnp.where(key_w < WIN_PAD - t0, NEG, 0.0)
    s_w = _dot_nt(q4, kwp_ref[pl.ds(r0, WIN_KEYS), :]) + wb_ref[...].reshape(rows, WIN_KEYS) + before_start

    kc = kc_ref[0].astype(BF)
    vc = vc_ref[0].astype(BF)
    r_io = lax.broadcasted_iota(jnp.int32, (LANES, ncp), 0)
    n_io = lax.broadcasted_iota(jnp.int32, (LANES, ncp), 1)
    band = (r_io < 3 * CMP_BAND) & (n_io == CMP_PER_TILE * tile + (r_io & (CMP_BAND - 1)) - CMP_BAND_LO)
    shift = jnp.where(band, 1.0, 0.0).astype(BF)
    cf4 = cf_ref[...].reshape(rows, LANES)
    lc = _dot_nt(q4, kc) + _dot(cf4, shift)
    qi_c = lax.broadcasted_iota(jnp.int32, (Q_TILE, ncp), 0)
    n_c = lax.broadcasted_iota(jnp.int32, (Q_TILE, ncp), 1)
    vis = (CMP_STRIDE * n_c + (CMP_LEN - 1)) <= (t0 + qi_c)
    lc3 = jnp.where(vis[None], lc.reshape(HPG, Q_TILE, ncp), NEG)
    _, e_c, l_c = _softmax_parts(lc3)
    p_c = jnp.where(vis[None], e_c * (1.0 / l_c), 0.0)
    o_cmp = _dot(p_c.reshape(rows, ncp).astype(BF), vc)

    psum = p_c[0]
    for hh in range(1, HPG):
        psum = psum + p_c[hh]
    p0 = psum.astype(BF)
    r1 = psum - p0.astype(F32)
    p1 = r1.astype(BF)
    p2 = (r1 - p1.astype(F32)).astype(BF)
    ovt = ovt_ref[...]
    imp_t = _dot_nt(ovt, p0) + _dot_nt(ovt, p1) + _dot_nt(ovt, p2)

    gts = jax.nn.sigmoid(ng_ref[0].astype(F32))
    g_hi = gts.astype(BF)
    g_lo = (gts - g_hi.astype(F32)).astype(BF)
    spread = spread_ref[...]
    gsp = _dot(g_hi, spread) + _dot(g_lo, spread)

    e_w = jnp.exp2(s_w - jnp.max(s_w, axis=-1, keepdims=True))
    pv_w = _dot(e_w.astype(BF), vwp_ref[pl.ds(r0, WIN_KEYS), :])
    o_win = pv_w[:, 0:HEAD_DIM] * (1.0 / pv_w[:, HEAD_DIM:])

    def gate(hh, branch):
        c = 3 * hh + branch
        return gsp[:, c * LANES:(c + 1) * LANES]

    gated_cw = [gate(hh, 0) * o_cmp[hh * Q_TILE:(hh + 1) * Q_TILE] + gate(hh, 2) * o_win[hh * Q_TILE:(hh + 1) * Q_TILE]
                for hh in range(HPG)]

    blk_t = lax.broadcasted_iota(jnp.int32, (n_slc, Q_TILE), 0)
    q_t = lax.broadcasted_iota(jnp.int32, (n_slc, Q_TILE), 1)
    cur_t = (t0 + q_t) >> (SLC_BLOCK.bit_length() - 1)
    forced_t = (blk_t == 0) | (blk_t == cur_t) | (blk_t == cur_t - 1)
    causal_t = blk_t <= cur_t
    key = jnp.where(forced_t, KEY_FORCED, jnp.where(causal_t, lax.bitcast_convert_type(imp_t, jnp.int32), KEY_FUTURE))
    key_ref[...] = key
    groups = [key[8 * a:8 * a + 8, :] for a in range(n_slc // 8)]
    ranks = [jnp.zeros((8, Q_TILE), jnp.int32) for _ in groups]
    sub = lax.broadcasted_iota(jnp.int32, (8, Q_TILE), 0)
    for bb in range(n_slc):
        kb = key_ref[bb:bb + 1, :]
        kb1 = kb + 1
        for a in range(n_slc // 8):
            if 8 * a > bb:
                ahead = kb1
            elif 8 * a + 7 < bb:
                ahead = kb
            else:
                ahead = jnp.where(sub > bb - 8 * a, kb1, kb)
            ranks[a] = ranks[a] + (ahead > groups[a]).astype(jnp.int32)
    rank = jnp.concatenate(ranks, axis=0)
    sel_t = (rank < n_sel) & causal_t
    first_local = (t0 - SLC_PAD) >> (SLC_BLOCK.bit_length() - 1)
    far_t = blk_t < first_local
    flag_rows = jnp.where(blk_t == 0, 1.0, 0.0)
    stack = jnp.concatenate([jnp.where(sel_t, 0.0, 1.0), flag_rows,
                             jnp.where(sel_t & far_t, 0.0, 1.0), flag_rows], axis=0).astype(BF)
    qmask = _dot_nt(eye_ref[...], stack).astype(BF)
    qa_loc = jnp.concatenate([q4, qmask[:, 0:LANES]], axis=1)
    qa_far = jnp.concatenate([q4, qmask[:, LANES:]], axis=1)

    lk = SLC_LOCAL_KEYS
    chunk = SLC_CHUNK_BLOCKS * SLC_BLOCK
    n_far = (first_local + SLC_CHUNK_BLOCKS - 1) // SLC_CHUNK_BLOCKS

    s_loc = _dot_nt(qa_loc, kaug_ref[pl.ds(r0, lk), :]) + sb_ref[...].reshape(rows, lk)

    def far_logits(c, m_part):
        rr = pl.multiple_of(SLC_PAD + c * chunk, SLC_BLOCK)
        s = _dot_nt(qa_far, kaug_ref[pl.ds(rr, chunk), :])
        sfar_ref[c] = s
        return jnp.maximum(m_part, _lane_tiles(s, jnp.maximum))

    m_part = lax.fori_loop(0, n_far, far_logits, _lane_tiles(s_loc, jnp.maximum))
    m_s = jnp.max(m_part, axis=-1, keepdims=True)

    e_loc = jnp.exp2(s_loc - m_s)
    acc_s = _dot(e_loc.astype(BF), vsp_ref[pl.ds(r0, lk), :])

    def far_values(c, acc):
        rr = pl.multiple_of(SLC_PAD + c * chunk, SLC_BLOCK)
        e = jnp.exp2(sfar_ref[c] - m_s)
        return acc + _dot(e.astype(BF), vsp_ref[pl.ds(rr, chunk), :])

    acc_s = lax.fori_loop(0, n_far, far_values, acc_s)
    o_slc = acc_s[:, 0:HEAD_DIM] * (1.0 / acc_s[:, HEAD_DIM:])

    for hh in range(HPG):
        o_h = gated_cw[hh] + gate(hh, 1) * o_slc[hh * Q_TILE:(hh + 1) * Q_TILE]
        o_ref[0, :, hh * HEAD_DIM:(hh + 1) * HEAD_DIM] = o_h.astype(o_ref.dtype)


N_ATTN_IN = 14


def _nsa_attention(proj, k_cmp, v_cmp, wb, sb, cf, to_cast):
    b, s, _ = proj.shape
    n_steps = b * N_KV * (s // Q_TILE)
    cast_specs = []
    for w in to_cast:
        slab = next(r for r in range(BF16_SUBLANES, w.shape[0] + 1, BF16_SUBLANES)
                    if w.shape[0] % r == 0 and w.shape[0] // r <= n_steps)
        last = w.shape[0] // slab - 1
        cast_specs.append(pl.BlockSpec(
            (slab, w.shape[1]),
            lambda i, g, j, last=last: (jnp.minimum((i * N_KV + g) * (s // Q_TILE) + j, last), 0)))
    ovt = _overlap_matrix(s)
    eye = jnp.asarray(np.tile(np.eye(Q_TILE), (HPG, 1)), BF)
    spread = jnp.asarray(np.repeat(np.eye(LANES)[:, :3 * HPG], LANES, axis=1), BF)
    n_chunks = -(-(s // SLC_BLOCK) // SLC_CHUNK_BLOCKS)
    q_blk = Q_OFF // (HPG * HEAD_DIM)
    kv_blk = KC_OFF // HEAD_DIM
    ng_blk = NG_OFF // LANES
    lk = SLC_LOCAL_KEYS
    ncp = s // CMP_STRIDE

    def kv_spec(idx):
        return pl.BlockSpec((1, s, HEAD_DIM), lambda i, g, j: (i, 0, kv_blk + idx * N_KV + g))

    cmp_spec = pl.BlockSpec((1, ncp, HEAD_DIM), lambda i, g, j: (i * N_KV + g, 0, 0))
    outs = pl.pallas_call(
        functools.partial(_attn_kernel, seq=s, n_cast=len(to_cast)),
        grid=(b, N_KV, s // Q_TILE),
        in_specs=[pl.BlockSpec((1, Q_TILE, HPG * HEAD_DIM), lambda i, g, j: (i, j, q_blk + g)),
                  kv_spec(2), kv_spec(3), kv_spec(4), kv_spec(5),
                  cmp_spec, cmp_spec,
                  pl.BlockSpec((1, Q_TILE, LANES), lambda i, g, j: (i, j, ng_blk + g)),
                  pl.BlockSpec((HPG, Q_TILE, WIN_KEYS), lambda i, g, j: (g, 0, 0)),
                  pl.BlockSpec((HPG, Q_TILE, lk), lambda i, g, j: (g, 0, 0)),
                  pl.BlockSpec((HPG, Q_TILE, LANES), lambda i, g, j: (g, 0, 0)),
                  pl.BlockSpec(ovt.shape, lambda i, g, j: (0, 0)),
                  pl.BlockSpec(eye.shape, lambda i, g, j: (0, 0)),
                  pl.BlockSpec(spread.shape, lambda i, g, j: (0, 0))] + cast_specs,
        out_specs=[pl.BlockSpec((1, Q_TILE, HPG * HEAD_DIM), lambda i, g, j: (i, j, g))] + cast_specs,
        out_shape=[jax.ShapeDtypeStruct((b, s, Q_WIDTH), BF)]
        + [jax.ShapeDtypeStruct(w.shape, BF) for w in to_cast],
        scratch_shapes=[pltpu.VMEM((SLC_PAD + s, 2 * HEAD_DIM), BF),
                        pltpu.VMEM((SLC_PAD + s, HEAD_DIM + LANES), BF),
                        pltpu.VMEM((WIN_PAD + s, HEAD_DIM), BF),
                        pltpu.VMEM((WIN_PAD + s, HEAD_DIM + LANES), BF),
                        pltpu.VMEM((s // SLC_BLOCK, Q_TILE), jnp.int32),
                        pltpu.VMEM((n_chunks, HPG * Q_TILE, SLC_CHUNK_BLOCKS * SLC_BLOCK), F32)],
        compiler_params=pltpu.CompilerParams(dimension_semantics=("arbitrary",) * 3,
                                             vmem_limit_bytes=ATTN_VMEM_LIMIT),
    )(proj, proj, proj, proj, proj, k_cmp, v_cmp, proj, wb, sb, cf, ovt, eye, spread, *to_cast)
    return outs[0], outs[1:]


def _overlap_matrix(s):
    ncp = s // CMP_STRIDE
    n_slc = s // SLC_BLOCK
    nc = (s - CMP_LEN) // CMP_STRIDE + 1
    i = np.arange(ncp)[None, :]
    jj = np.arange(n_slc)[:, None]
    ov = ((i * CMP_STRIDE < (jj + 1) * SLC_BLOCK) & (i * CMP_STRIDE + CMP_LEN > jj * SLC_BLOCK) & (i < nc))
    return jnp.asarray(ov, BF)


def kernel(x, c, w_ada, b_ada, g_pre_mix, g_post_mix, g_pre_ffn, g_post_ffn, w_in, w_dw, b_dw, conv_ln_g, conv_ln_b, w_conv_out, cmp_pe_k, cmp_pe_v, w_cmp_k1, w_cmp_k2, w_cmp_v1, w_cmp_v2, rel_bias, w_nsa_out, w_out, w_ffn_gate, w_ffn_up, w_ffn_down):
    b, s, d = x.shape
    m = b * s
    depth = w_ada.shape[0]
    c8 = jnp.zeros((8, d), F32).at[:b].set(c)
    wb, sb, cf = _bias_tables(rel_bias)
    col_scale = jnp.ones((1, NG_OFF + N_KV * LANES), F32).at[:, Q_OFF:KC_OFF].set(HEAD_DIM ** -0.5 * LOG2E)

    for l in range(depth):
        mod = _modulation(c8, w_ada[l], b_ada[l][None, :])
        mod3 = mod[:b].reshape(b, 6, d)

        w_all = w_in[l].astype(BF)
        w_ng = w_in[l][:, NG_OFF:MG_OFF].reshape(d, N_KV, 3 * HPG)
        w_ng = jnp.pad(w_ng, ((0, 0), (0, 0), (0, LANES - 3 * HPG))).reshape(d, N_KV * LANES)
        w_ng = w_ng.astype(BF)
        w_gates = w_all[:, MG_OFF:]

        h = _prenorm(x, g_pre_mix[l][None, :], mod3, 0, 1).reshape(m, d)
        proj = _matmul_with_tail(h, w_all, NG_OFF, w_ng, col_scale, BF, 2048, 512).reshape(b, s, -1)

        u_conv = _conv_module(proj, w_dw[l], b_dw[l][None, :], conv_ln_g[l][None, :], conv_ln_b[l][None, :])

        pe_k = jnp.broadcast_to(cmp_pe_k[l].reshape(1, -1), (8, CMP_LEN * HEAD_DIM)).astype(BF)
        pe_v = jnp.broadcast_to(cmp_pe_v[l].reshape(1, -1), (8, CMP_LEN * HEAD_DIM)).astype(BF)
        k_cmp = _compress(proj, KC_OFF // HEAD_DIM, pe_k, w_cmp_k1[l].astype(BF), w_cmp_k2[l].astype(BF))
        v_cmp = _compress(proj, (KC_OFF + KV_WIDTH) // HEAD_DIM, pe_v,
                          w_cmp_v1[l].astype(BF), w_cmp_v2[l].astype(BF))

        later = [w_conv_out[l], w_nsa_out[l], w_out[l], w_ffn_gate[l], w_ffn_up[l], w_ffn_down[l]]
        o_nsa, (wb_conv, wb_nsa, wb_out, wb_gate, wb_up, wb_down) = _nsa_attention(proj, k_cmp, v_cmp, wb, sb, cf, later)

        z = _gated_mix(h, u_conv.reshape(m, CONV_DIM), o_nsa.reshape(m, Q_WIDTH), wb_conv, wb_nsa, w_gates)
        mix = _matmul(z, wb_out, BF, 1024, 1024).reshape(b, s, d)
        x, h2 = _post_mix(mix, x, g_post_mix[l][None, :], g_pre_ffn[l][None, :], mod3, 2, 3, 4)

        ff = _ffn_up(h2.reshape(m, d), wb_gate, wb_up)
        f = _matmul(ff, wb_down, BF, 512, 512).reshape(b, s, d)
        x = _post_ffn(f, x, g_post_ffn[l][None, :], mod3, 5)
    return x
```

```python
import functools
import math

import numpy as np
import jax
import jax.numpy as jnp
from jax import lax
from jax.experimental import pallas as pl
from jax.experimental.pallas import tpu as pltpu

F32 = jnp.float32
BF = jnp.bfloat16

CONV_DIM = 2048
CONV_WIDTH = 31
N_HEADS = 16
N_KV = 4
HPG = N_HEADS // N_KV
HEAD_DIM = 128
CMP_LEN = 32
CMP_STRIDE = 16
CMP_HID = 512
SLC_BLOCK = 64
N_SEL = 16
WINDOW = 512
Q_BLOCK = 64
REL_BUCKETS = 32
REL_MAX_DIST = 128
NORM_EPS = 1e-6
NEG = -1e30
LOG2E = 1.4426950408889634

Q_WIDTH = N_HEADS * HEAD_DIM
KV_WIDTH = N_KV * HEAD_DIM
Q_OFF = 2 * CONV_DIM
KC_OFF = Q_OFF + Q_WIDTH
NG_OFF = KC_OFF + 6 * KV_WIDTH
MG_OFF = NG_OFF + 3 * N_HEADS

LANES = 128
SUBLANES = 8
BF16_SUBLANES = 16
VMEM_LIMIT = 56 * 1024 * 1024
ATTN_VMEM_LIMIT = 61 * 1024 * 1024

CONV_HALO = 32
CONV_ROWS = 64
CONV_LANES = 256

Q_TILE = 4 * Q_BLOCK
SLC_PAD = 2 * SLC_BLOCK
SLC_LOCAL_KEYS = SLC_PAD + Q_TILE
SLC_CHUNK_BLOCKS = 16
WIN_KEYS = WINDOW + Q_TILE
WIN_PAD = WINDOW
CMP_PER_TILE = Q_TILE // CMP_STRIDE
CMP_BAND = 32
CMP_BAND_LO = 9
PAD_FLAG_COL = 64
KEY_FORCED = 0x7F000000
KEY_FUTURE = -2


def _t5_thresholds():
    max_exact = REL_BUCKETS // 2
    thr = []
    for k in range(1, REL_BUCKETS - max_exact):
        n = max_exact
        while max_exact + int(math.log(n / max_exact) / math.log(REL_MAX_DIST / max_exact)
                              * (REL_BUCKETS - max_exact)) < max_exact + k:
            n += 1
        thr.append(n)
    return tuple(thr)


T5_THRESHOLDS = _t5_thresholds()


def _cparams(*sem):
    return pltpu.CompilerParams(dimension_semantics=sem, vmem_limit_bytes=VMEM_LIMIT)


def _silu(v):
    return v * jax.nn.sigmoid(v)


def _dot(a, b):
    return jnp.dot(a, b, preferred_element_type=F32)


def _dot_nt(a, b):
    return lax.dot_general(a, b, (((1,), (1,)), ((), ())), preferred_element_type=F32)


def _rms(v, g):
    return v * lax.rsqrt(jnp.mean(v * v, axis=-1, keepdims=True) + NORM_EPS) * g


def _mod_kernel(c_ref, w_ref, b_ref, o_ref):
    a = _silu(c_ref[...]).astype(BF)
    o_ref[...] = _dot(a, w_ref[...].astype(BF)) + b_ref[...]


def _modulation(c8, w_ada, b_ada, tn=512):
    rows, d = c8.shape
    n = w_ada.shape[1]
    return pl.pallas_call(
        _mod_kernel,
        grid=(n // tn,),
        in_specs=[pl.BlockSpec((rows, d), lambda j: (0, 0)),
                  pl.BlockSpec((d, tn), lambda j: (0, j)),
                  pl.BlockSpec((1, tn), lambda j: (0, j))],
        out_specs=pl.BlockSpec((rows, tn), lambda j: (0, j)),
        out_shape=jax.ShapeDtypeStruct((rows, n), F32),
        compiler_params=_cparams("parallel"),
    )(c8, w_ada, b_ada)


def _prenorm_kernel(x_ref, g_ref, mod_ref, h_ref, *, shift_row, scale_row):
    y = _rms(x_ref[0], g_ref[...])
    h = y * (1.0 + mod_ref[0, scale_row:scale_row + 1, :]) + mod_ref[0, shift_row:shift_row + 1, :]
    h_ref[0] = h.astype(BF)


def _prenorm(x, g, mod3, shift_row, scale_row, tr=512):
    b, s, d = x.shape
    return pl.pallas_call(
        functools.partial(_prenorm_kernel, shift_row=shift_row, scale_row=scale_row),
        grid=(b, s // tr),
        in_specs=[pl.BlockSpec((1, tr, d), lambda i, r: (i, r, 0)),
                  pl.BlockSpec((1, d), lambda i, r: (0, 0)),
                  pl.BlockSpec((1, 6, d), lambda i, r: (i, 0, 0))],
        out_specs=pl.BlockSpec((1, tr, d), lambda i, r: (i, r, 0)),
        out_shape=jax.ShapeDtypeStruct((b, s, d), BF),
        compiler_params=_cparams("parallel", "parallel"),
    )(x, g, mod3)


def _post_mix_kernel(y_ref, x_ref, gp_ref, gn_ref, mod_ref, xo_ref, h_ref, *, gate_row, shift_row, scale_row):
    x1 = x_ref[0] + mod_ref[0, gate_row:gate_row + 1, :] * _rms(y_ref[0].astype(F32), gp_ref[...])
    xo_ref[0] = x1
    h = _rms(x1, gn_ref[...]) * (1.0 + mod_ref[0, scale_row:scale_row + 1, :]) \
        + mod_ref[0, shift_row:shift_row + 1, :]
    h_ref[0] = h.astype(BF)


def _post_mix(y, x, g_post, g_next, mod3, gate_row, shift_row, scale_row, tr=256):
    b, s, d = x.shape
    row = pl.BlockSpec((1, tr, d), lambda i, r: (i, r, 0))
    vec = pl.BlockSpec((1, d), lambda i, r: (0, 0))
    return pl.pallas_call(
        functools.partial(_post_mix_kernel, gate_row=gate_row, shift_row=shift_row, scale_row=scale_row),
        grid=(b, s // tr),
        in_specs=[row, row, vec, vec, pl.BlockSpec((1, 6, d), lambda i, r: (i, 0, 0))],
        out_specs=[row, row],
        out_shape=[jax.ShapeDtypeStruct((b, s, d), F32), jax.ShapeDtypeStruct((b, s, d), BF)],
        compiler_params=_cparams("parallel", "parallel"),
    )(y, x, g_post, g_next, mod3)


def _post_ffn_kernel(y_ref, x_ref, gp_ref, mod_ref, xo_ref, *, gate_row):
    xo_ref[0] = x_ref[0] + mod_ref[0, gate_row:gate_row + 1, :] * _rms(y_ref[0].astype(F32), gp_ref[...])


def _post_ffn(y, x, g_post, mod3, gate_row, tr=256):
    b, s, d = x.shape
    row = pl.BlockSpec((1, tr, d), lambda i, r: (i, r, 0))
    return pl.pallas_call(
        functools.partial(_post_ffn_kernel, gate_row=gate_row),
        grid=(b, s // tr),
        in_specs=[row, row, pl.BlockSpec((1, d), lambda i, r: (0, 0)),
                  pl.BlockSpec((1, 6, d), lambda i, r: (i, 0, 0))],
        out_specs=row,
        out_shape=jax.ShapeDtypeStruct((b, s, d), F32),
        compiler_params=_cparams("parallel", "parallel"),
    )(y, x, g_post, mod3)


def _mm_scale_kernel(a_ref, w_ref, s_ref, o_ref):
    o_ref[...] = (_dot(a_ref[...], w_ref[...]) * s_ref[...]).astype(o_ref.dtype)


def _mm_kernel(a_ref, w_ref, o_ref):
    o_ref[...] = _dot(a_ref[...], w_ref[...]).astype(o_ref.dtype)


def _matmul(a, w, out_dtype, tm, tn, col_scale=None):
    m, k = a.shape
    n = w.shape[1]
    in_specs = [pl.BlockSpec((tm, k), lambda i, j: (i, 0)),
                pl.BlockSpec((k, tn), lambda i, j: (0, j))]
    args = [a, w]
    body = _mm_kernel
    if col_scale is not None:
        in_specs.append(pl.BlockSpec((1, tn), lambda i, j: (0, j)))
        args.append(col_scale)
        body = _mm_scale_kernel
    return pl.pallas_call(
        body,
        grid=(m // tm, n // tn),
        in_specs=in_specs,
        out_specs=pl.BlockSpec((tm, tn), lambda i, j: (i, j)),
        out_shape=jax.ShapeDtypeStruct((m, n), out_dtype),
        compiler_params=_cparams("parallel", "arbitrary"),
    )(*args)


def _mm_tail_kernel(a_ref, w_ref, wt_ref, s_ref, o_ref, *, n_main):
    j = pl.program_id(1)

    @pl.when(j < n_main)
    def _main():
        o_ref[...] = (_dot(a_ref[...], w_ref[...]) * s_ref[...]).astype(o_ref.dtype)

    @pl.when(j >= n_main)
    def _tail():
        o_ref[...] = (_dot(a_ref[...], wt_ref[...]) * s_ref[...]).astype(o_ref.dtype)


def _matmul_with_tail(a, w, n_w, w_tail, col_scale, out_dtype, tm, tn):
    m, k = a.shape
    n_main = n_w // tn
    n_tail = w_tail.shape[1] // tn
    n = n_w + w_tail.shape[1]
    return pl.pallas_call(
        functools.partial(_mm_tail_kernel, n_main=n_main),
        grid=(m // tm, n_main + n_tail),
        in_specs=[pl.BlockSpec((tm, k), lambda i, j: (i, 0)),
                  pl.BlockSpec((k, tn), lambda i, j: (0, jnp.minimum(j, n_main - 1))),
                  pl.BlockSpec((k, tn), lambda i, j: (0, jnp.maximum(j - n_main, 0)), pipeline_mode=pl.Buffered(1)),
                  pl.BlockSpec((1, tn), lambda i, j: (0, j))],
        out_specs=pl.BlockSpec((tm, tn), lambda i, j: (i, j)),
        out_shape=jax.ShapeDtypeStruct((m, n), out_dtype),
        compiler_params=_cparams("parallel", "arbitrary"),
    )(a, w, w_tail, col_scale)


def _mix_kernel(h_ref, a1_ref, a2_ref, w1_ref, w2_ref, wga_ref, wgb_ref, o_ref):
    h = h_ref[...]
    ya = _dot(a1_ref[...], w1_ref[...])
    yb = _dot(a2_ref[...], w2_ref[...])
    o = jax.nn.sigmoid(_dot(h, wga_ref[...])) * ya + jax.nn.sigmoid(_dot(h, wgb_ref[...])) * yb
    o_ref[...] = o.astype(o_ref.dtype)


def _gated_mix(h, u_conv, o_nsa, w_conv_out, w_nsa_out, w_gates, tm=1024, tn=256):
    m, k = h.shape
    ka = u_conv.shape[1]
    kb = o_nsa.shape[1]
    d_model = w_conv_out.shape[1]
    gb_block = d_model // tn
    return pl.pallas_call(
        _mix_kernel,
        grid=(m // tm, d_model // tn),
        in_specs=[pl.BlockSpec((tm, k), lambda i, j: (i, 0)),
                  pl.BlockSpec((tm, ka), lambda i, j: (i, 0)),
                  pl.BlockSpec((tm, kb), lambda i, j: (i, 0)),
                  pl.BlockSpec((ka, tn), lambda i, j: (0, j)),
                  pl.BlockSpec((kb, tn), lambda i, j: (0, j)),
                  pl.BlockSpec((k, tn), lambda i, j: (0, j)),
                  pl.BlockSpec((k, tn), lambda i, j: (0, j + gb_block))],
        out_specs=pl.BlockSpec((tm, tn), lambda i, j: (i, j)),
        out_shape=jax.ShapeDtypeStruct((m, d_model), BF),
        compiler_params=_cparams("parallel", "arbitrary"),
    )(h, u_conv, o_nsa, w_conv_out, w_nsa_out, w_gates, w_gates)


def _ffn_up_kernel(a_ref, wg_ref, wu_ref, o_ref):
    a = a_ref[...]
    o_ref[...] = (_silu(_dot(a, wg_ref[...])) * _dot(a, wu_ref[...])).astype(o_ref.dtype)


def _ffn_up(h, w_gate, w_up, tm=2048, tn=256):
    m, k = h.shape
    n = w_gate.shape[1]
    wspec = pl.BlockSpec((k, tn), lambda i, j: (0, j))
    return pl.pallas_call(
        _ffn_up_kernel,
        grid=(m // tm, n // tn),
        in_specs=[pl.BlockSpec((tm, k), lambda i, j: (i, 0)), wspec, wspec],
        out_specs=pl.BlockSpec((tm, tn), lambda i, j: (i, j)),
        out_shape=jax.ShapeDtypeStruct((m, n), BF),
        compiler_params=_cparams("parallel", "arbitrary"),
    )(h, w_gate, w_up)


def _conv_kernel(a_ref, b_ref, ah_ref, bh_ref, w_ref, bdw_ref, g_ref, bb_ref, o_ref, u_ref, sh_ref, v_ref, *, ts):
    i = pl.program_id(1)
    u_ref[CONV_HALO:CONV_HALO + ts, :] = a_ref[0].astype(F32) * jax.nn.sigmoid(b_ref[0].astype(F32))
    uh = ah_ref[0].astype(F32) * jax.nn.sigmoid(bh_ref[0].astype(F32))
    u_ref[0:CONV_HALO, :] = jnp.where(i > 0, uh, 0.0)

    rc, cc = CONV_ROWS, CONV_LANES
    first = CONV_HALO - (CONV_WIDTH - 1)
    span = ts + CONV_HALO - SUBLANES
    for c0 in range(0, CONV_DIM, cc):
        for s in range(1, SUBLANES):
            sh_ref[s - 1] = u_ref[s:s + span, c0:c0 + cc]

        def row_chunk(rb, carry, c0=c0):
            r0 = pl.multiple_of(rb * rc, rc)
            acc = jnp.zeros((rc, cc), F32)
            for k in range(CONV_WIDTH):
                s = (first + k) % SUBLANES
                base = first + k - s
                if s == 0:
                    tap = u_ref[pl.ds(r0 + base, rc), c0:c0 + cc]
                else:
                    tap = sh_ref[s - 1, pl.ds(r0 + base, rc), :]
                acc = acc + tap * w_ref[k:k + 1, c0:c0 + cc]
            v_ref[pl.ds(r0, rc), c0:c0 + cc] = acc + bdw_ref[:, c0:c0 + cc]
            return carry

        lax.fori_loop(0, ts // rc, row_chunk, 0)

    v = v_ref[...]
    mu = jnp.mean(v, axis=-1, keepdims=True)
    vc = v - mu
    y = vc * lax.rsqrt(jnp.mean(vc * vc, axis=-1, keepdims=True) + NORM_EPS) * g_ref[...] + bb_ref[...]
    o_ref[0] = _silu(y).astype(o_ref.dtype)


def _conv_module(proj, w_dw, b_dw, ln_g, ln_b, ts=512):
    b, s, _ = proj.shape
    hb = ts // CONV_HALO
    cur_a = pl.BlockSpec((1, ts, CONV_DIM), lambda i, r: (i, r, 0))
    cur_b = pl.BlockSpec((1, ts, CONV_DIM), lambda i, r: (i, r, 1))
    halo_a = pl.BlockSpec((1, CONV_HALO, CONV_DIM), lambda i, r: (i, jnp.maximum(r * hb - 1, 0), 0))
    halo_b = pl.BlockSpec((1, CONV_HALO, CONV_DIM), lambda i, r: (i, jnp.maximum(r * hb - 1, 0), 1))
    vec = pl.BlockSpec((1, CONV_DIM), lambda i, r: (0, 0))
    return pl.pallas_call(
        functools.partial(_conv_kernel, ts=ts),
        grid=(b, s // ts),
        in_specs=[cur_a, cur_b, halo_a, halo_b,
                  pl.BlockSpec((CONV_WIDTH, CONV_DIM), lambda i, r: (0, 0)), vec, vec, vec],
        out_specs=pl.BlockSpec((1, ts, CONV_DIM), lambda i, r: (i, r, 0)),
        out_shape=jax.ShapeDtypeStruct((b, s, CONV_DIM), BF),
        scratch_shapes=[pltpu.VMEM((CONV_HALO + ts, CONV_DIM), F32),
                        pltpu.VMEM((SUBLANES - 1, ts + CONV_HALO - SUBLANES, CONV_LANES), F32),
                        pltpu.VMEM((ts, CONV_DIM), F32)],
        compiler_params=_cparams("parallel", "arbitrary"),
    )(proj, proj, proj, proj, w_dw, b_dw, ln_g, ln_b)


def _compress_kernel(k_ref, pe_ref, w1_ref, w2_ref, o_ref, kf_ref):
    kf_ref[...] = k_ref[0].astype(F32)
    rows = kf_ref.shape[0] // CMP_STRIDE
    lo = jnp.zeros((rows, CMP_HID), F32)
    hi = jnp.zeros((rows, CMP_HID), F32)
    for l in range(CMP_STRIDE):
        tok = kf_ref[pl.ds(l, rows, stride=CMP_STRIDE), :].astype(BF)
        lo = lo + _dot(tok, w1_ref[l * HEAD_DIM:(l + 1) * HEAD_DIM, :])
        hi = hi + _dot(tok, w1_ref[(CMP_STRIDE + l) * HEAD_DIM:(CMP_STRIDE + l + 1) * HEAD_DIM, :])
    pe = _dot(pe_ref[...], w1_ref[...])
    hid = lo + pltpu.roll(hi, rows - 1, axis=0) + pe[0:1, :]
    act = 0.5 * hid * (1.0 + jnp.tanh(math.sqrt(2.0 / math.pi) * (hid + 0.044715 * hid * hid * hid)))
    o_ref[0] = _dot(act.astype(BF), w2_ref[...])


def _compress(proj, col_block, pe8, w1, w2):
    b, s, _ = proj.shape
    rows = s // CMP_STRIDE
    return pl.pallas_call(
        _compress_kernel,
        grid=(b, N_KV),
        in_specs=[pl.BlockSpec((1, s, HEAD_DIM), lambda i, g: (i, 0, col_block + g)),
                  pl.BlockSpec(pe8.shape, lambda i, g: (0, 0)),
                  pl.BlockSpec(w1.shape, lambda i, g: (0, 0)),
                  pl.BlockSpec(w2.shape, lambda i, g: (0, 0))],
        out_specs=pl.BlockSpec((1, rows, HEAD_DIM), lambda i, g: (i * N_KV + g, 0, 0)),
        out_shape=jax.ShapeDtypeStruct((b * N_KV, rows, HEAD_DIM), F32),
        scratch_shapes=[pltpu.VMEM((s, HEAD_DIM), F32)],
        compiler_params=_cparams("parallel", "parallel"),
    )(proj, pe8, w1, w2)


def _bias_tables_kernel(rb_ref, wb_ref, sb_ref, cf_ref):
    h = pl.program_id(0)
    last = rb_ref[REL_BUCKETS - 1, h]

    def rel(dist):
        n = jnp.maximum(dist, 0)
        bucket = jnp.full(n.shape, REL_BUCKETS // 2, jnp.int32)
        for t in T5_THRESHOLDS:
            bucket = bucket + (n >= t).astype(jnp.int32)
        bucket = jnp.where(n < REL_BUCKETS // 2, n, jnp.minimum(bucket, REL_BUCKETS - 1))
        out = jnp.zeros(n.shape, F32)
        for bk in range(REL_BUCKETS - 1):
            out = jnp.where(bucket == bk, rb_ref[bk, h] - last, out)
        return out * LOG2E

    def toeplitz(fn, offset, keys):
        width = keys + Q_TILE
        j = lax.broadcasted_iota(jnp.int32, (SUBLANES, width), 1)
        diag = fn(offset + Q_TILE - 1 - j)
        rows = jnp.concatenate([diag] * (Q_TILE // SUBLANES), axis=0)
        return pltpu.roll(rows, width - (Q_TILE - 1), 1, stride=1, stride_axis=0)[:, 0:keys]

    wb_ref[0] = toeplitz(lambda d: jnp.where((d >= 0) & (d < WINDOW), rel(d), NEG), WIN_PAD, WIN_KEYS)
    sb_ref[0] = toeplitz(lambda d: jnp.where(d >= 0, rel(d), NEG), SLC_PAD, SLC_LOCAL_KEYS)

    qi = lax.broadcasted_iota(jnp.int32, (Q_TILE, LANES), 0)
    col = lax.broadcasted_iota(jnp.int32, (Q_TILE, LANES), 1)
    mi = col & (CMP_BAND - 1)
    part = col >> (CMP_BAND.bit_length() - 1)
    dc = qi - CMP_STRIDE * (mi - CMP_BAND_LO) - (CMP_LEN - 1)
    val = jnp.where(dc >= 0, rel(dc), 0.0)
    p0 = val.astype(BF).astype(F32)
    r1 = val - p0
    p1 = r1.astype(BF).astype(F32)
    p2 = r1 - p1
    parts = jnp.where(part == 0, p0, jnp.where(part == 1, p1, jnp.where(part == 2, p2, 0.0)))
    cf_ref[0] = parts.astype(BF)


def _bias_tables(rel_bias):
    smem = pl.BlockSpec(memory_space=pltpu.SMEM)
    lk = SLC_LOCAL_KEYS
    return pl.pallas_call(
        _bias_tables_kernel,
        grid=(N_HEADS,),
        in_specs=[smem],
        out_specs=[pl.BlockSpec((1, Q_TILE, WIN_KEYS), lambda h: (h, 0, 0)),
                   pl.BlockSpec((1, Q_TILE, lk), lambda h: (h, 0, 0)),
                   pl.BlockSpec((1, Q_TILE, LANES), lambda h: (h, 0, 0))],
        out_shape=[jax.ShapeDtypeStruct((N_HEADS, Q_TILE, WIN_KEYS), F32),
                   jax.ShapeDtypeStruct((N_HEADS, Q_TILE, lk), F32),
                   jax.ShapeDtypeStruct((N_HEADS, Q_TILE, LANES), BF)],
        compiler_params=_cparams("arbitrary"),
    )(rel_bias)


def _softmax_parts(s):
    m = jnp.max(s, axis=-1, keepdims=True)
    e = jnp.exp2(s - m)
    return m, e, jnp.sum(e, axis=-1, keepdims=True)


def _lane_tiles(v, op):
    out = v[:, 0:LANES]
    for t in range(1, v.shape[1] // LANES):
        out = op(out, v[:, t * LANES:(t + 1) * LANES])
    return out


def _attn_kernel(*refs, seq, n_cast):
    (q_ref, ks_ref, vs_ref, kw_ref, vw_ref, kc_ref, vc_ref, ng_ref,
     wb_ref, sb_ref, cf_ref, ovt_ref, eye_ref, spread_ref) = refs[:N_ATTN_IN]
    cast_in = refs[N_ATTN_IN:N_ATTN_IN + n_cast]
    o_ref = refs[N_ATTN_IN + n_cast]
    cast_out = refs[N_ATTN_IN + n_cast + 1:N_ATTN_IN + 2 * n_cast + 1]
    kaug_ref, vsp_ref, kwp_ref, vwp_ref, key_ref, sfar_ref = refs[N_ATTN_IN + 2 * n_cast + 1:]

    for w_ref, wo_ref in zip(cast_in, cast_out):
        wo_ref[...] = w_ref[...].astype(BF)

    tile = pl.program_id(2)
    n_slc = seq // SLC_BLOCK
    ncp = seq // CMP_STRIDE
    rows = HPG * Q_TILE
    n_sel = min(N_SEL, n_slc)
    if n_slc != PAD_FLAG_COL or Q_TILE % LANES or 3 * CMP_BAND > LANES:
        raise NotImplementedError("selection layout expects 64 selection blocks and whole-lane query tiles")
    assert CMP_STRIDE * (CMP_BAND_LO + 1) - (CMP_LEN - 1) >= T5_THRESHOLDS[-1]
    assert CMP_STRIDE * (CMP_BAND - CMP_BAND_LO) + CMP_LEN - 1 >= Q_TILE

    @pl.when(tile == 0)
    def _stage_keys():
        col = lax.broadcasted_iota(jnp.int32, (SLC_PAD, LANES), 1)
        kaug_ref[0:SLC_PAD, 0:HEAD_DIM] = jnp.zeros((SLC_PAD, HEAD_DIM), BF)
        kaug_ref[0:SLC_PAD, HEAD_DIM:] = jnp.where(col == PAD_FLAG_COL, NEG, 0.0).astype(BF)
        kaug_ref[SLC_PAD:, 0:HEAD_DIM] = ks_ref[0]
        blk = lax.broadcasted_iota(jnp.int32, (seq, LANES), 0) >> (SLC_BLOCK.bit_length() - 1)
        colk = lax.broadcasted_iota(jnp.int32, (seq, LANES), 1)
        kaug_ref[SLC_PAD:, HEAD_DIM:] = jnp.where(colk == blk, NEG, 0.0).astype(BF)
        vsp_ref[:, HEAD_DIM:] = jnp.ones((SLC_PAD + seq, LANES), BF)
        vsp_ref[0:SLC_PAD, 0:HEAD_DIM] = jnp.zeros((SLC_PAD, HEAD_DIM), BF)
        vsp_ref[SLC_PAD:, 0:HEAD_DIM] = vs_ref[0]
        kwp_ref[0:WIN_PAD, :] = jnp.zeros((WIN_PAD, HEAD_DIM), BF)
        kwp_ref[WIN_PAD:, :] = kw_ref[0]
        vwp_ref[:, HEAD_DIM:] = jnp.ones((WIN_PAD + seq, LANES), BF)
        vwp_ref[0:WIN_PAD, 0:HEAD_DIM] = jnp.zeros((WIN_PAD, HEAD_DIM), BF)
        vwp_ref[WIN_PAD:, 0:HEAD_DIM] = vw_ref[0]

    q4 = jnp.concatenate([q_ref[0, :, hh * HEAD_DIM:(hh + 1) * HEAD_DIM] for hh in range(HPG)], axis=0)
    t0 = tile * Q_TILE
    r0 = pl.multiple_of(t0, Q_TILE)

    key_w = lax.broadcasted_iota(jnp.int32, (1, WIN_KEYS), 1)
    before_start = jnp.where(key_w < WIN_PAD - t0, NEG, 0.0)
    s_w = _dot_nt(q4, kwp_ref[pl.ds(r0, WIN_KEYS), :]) + wb_ref[...].reshape(rows, WIN_KEYS) + before_start

    kc = kc_ref[0].astype(BF)
    vc = vc_ref[0].astype(BF)
    r_io = lax.broadcasted_iota(jnp.int32, (LANES, ncp), 0)
    n_io = lax.broadcasted_iota(jnp.int32, (LANES, ncp), 1)
    band = (r_io < 3 * CMP_BAND) & (n_io == CMP_PER_TILE * tile + (r_io & (CMP_BAND - 1)) - CMP_BAND_LO)
    shift = jnp.where(band, 1.0, 0.0).astype(BF)
    cf4 = cf_ref[...].reshape(rows, LANES)
    lc = _dot_nt(q4, kc) + _dot(cf4, shift)
    qi_c = lax.broadcasted_iota(jnp.int32, (Q_TILE, ncp), 0)
    n_c = lax.broadcasted_iota(jnp.int32, (Q_TILE, ncp), 1)
    vis = (CMP_STRIDE * n_c + (CMP_LEN - 1)) <= (t0 + qi_c)
    lc3 = jnp.where(vis[None], lc.reshape(HPG, Q_TILE, ncp), NEG)
    _, e_c, l_c = _softmax_parts(lc3)
    p_c = jnp.where(vis[None], e_c * (1.0 / l_c), 0.0)
    o_cmp = _dot(p_c.reshape(rows, ncp).astype(BF), vc)

    psum = p_c[0]
    for hh in range(1, HPG):
        psum = psum + p_c[hh]
    p0 = psum.astype(BF)
    r1 = psum - p0.astype(F32)
    p1 = r1.astype(BF)
    p2 = (r1 - p1.astype(F32)).astype(BF)
    ovt = ovt_ref[...]
    imp_t = _dot_nt(ovt, p0) + _dot_nt(ovt, p1) + _dot_nt(ovt, p2)

    gts = jax.nn.sigmoid(ng_ref[0].astype(F32))
    g_hi = gts.astype(BF)
    g_lo = (gts - g_hi.astype(F32)).astype(BF)
    spread = spread_ref[...]
    gsp = _dot(g_hi, spread) + _dot(g_lo, spread)

    e_w = jnp.exp2(s_w - jnp.max(s_w, axis=-1, keepdims=True))
    pv_w = _dot(e_w.astype(BF), vwp_ref[pl.ds(r0, WIN_KEYS), :])
    o_win = pv_w[:, 0:HEAD_DIM] * (1.0 / pv_w[:, HEAD_DIM:])

    def gate(hh, branch):
        c = 3 * hh + branch
        return gsp[:, c * LANES:(c + 1) * LANES]

    gated_cw = [gate(hh, 0) * o_cmp[hh * Q_TILE:(hh + 1) * Q_TILE] + gate(hh, 2) * o_win[hh * Q_TILE:(hh + 1) * Q_TILE]
                for hh in range(HPG)]

    blk_t = lax.broadcasted_iota(jnp.int32, (n_slc, Q_TILE), 0)
    q_t = lax.broadcasted_iota(jnp.int32, (n_slc, Q_TILE), 1)
    cur_t = (t0 + q_t) >> (SLC_BLOCK.bit_length() - 1)
    forced_t = (blk_t == 0) | (blk_t == cur_t) | (blk_t == cur_t - 1)
    causal_t = blk_t <= cur_t
    key = jnp.where(forced_t, KEY_FORCED, jnp.where(causal_t, lax.bitcast_convert_type(imp_t, jnp.int32), KEY_FUTURE))
    key_ref[...] = key
    groups = [key[8 * a:8 * a + 8, :] for a in range(n_slc // 8)]
    ranks = [jnp.zeros((8, Q_TILE), jnp.int32) for _ in groups]
    sub = lax.broadcasted_iota(jnp.int32, (8, Q_TILE), 0)
    for bb in range(n_slc):
        kb = key_ref[bb:bb + 1, :]
        kb1 = kb + 1
        for a in range(n_slc // 8):
            if 8 * a > bb:
                ahead = kb1
            elif 8 * a + 7 < bb:
                ahead = kb
            else:
                ahead = jnp.where(sub > bb - 8 * a, kb1, kb)
            ranks[a] = ranks[a] + (ahead > groups[a]).astype(jnp.int32)
    rank = jnp.concatenate(ranks, axis=0)
    sel_t = (rank < n_sel) & causal_t
    first_local = (t0 - SLC_PAD) >> (SLC_BLOCK.bit_length() - 1)
    far_t = blk_t < first_local
    flag_rows = jnp.where(blk_t == 0, 1.0, 0.0)
    stack = jnp.concatenate([jnp.where(sel_t, 0.0, 1.0), flag_rows,
                             jnp.where(sel_t & far_t, 0.0, 1.0), flag_rows], axis=0).astype(BF)
    qmask = _dot_nt(eye_ref[...], stack).astype(BF)
    qa_loc = jnp.concatenate([q4, qmask[:, 0:LANES]], axis=1)
    qa_far = jnp.concatenate([q4, qmask[:, LANES:]], axis=1)

    lk = SLC_LOCAL_KEYS
    chunk = SLC_CHUNK_BLOCKS * SLC_BLOCK
    n_far = (first_local + SLC_CHUNK_BLOCKS - 1) // SLC_CHUNK_BLOCKS

    s_loc = _dot_nt(qa_loc, kaug_ref[pl.ds(r0, lk), :]) + sb_ref[...].reshape(rows, lk)

    def far_logits(c, m_part):
        rr = pl.multiple_of(SLC_PAD + c * chunk, SLC_BLOCK)
        s = _dot_nt(qa_far, kaug_ref[pl.ds(rr, chunk), :])
        sfar_ref[c] = s
        return jnp.maximum(m_part, _lane_tiles(s, jnp.maximum))

    m_part = lax.fori_loop(0, n_far, far_logits, _lane_tiles(s_loc, jnp.maximum))
    m_s = jnp.max(m_part, axis=-1, keepdims=True)

    e_loc = jnp.exp2(s_loc - m_s)
    acc_s = _dot(e_loc.astype(BF), vsp_ref[pl.ds(r0, lk), :])

    def far_values(c, acc):
        rr = pl.multiple_of(SLC_PAD + c * chunk, SLC_BLOCK)
        e = jnp.exp2(sfar_ref[c] - m_s)
        return acc + _dot(e.astype(BF), vsp_ref[pl.ds(rr, chunk), :])

    acc_s = lax.fori_loop(0, n_far, far_values, acc_s)
    o_slc = acc_s[:, 0:HEAD_DIM] * (1.0 / acc_s[:, HEAD_DIM:])

    for hh in range(HPG):
        o_h = gated_cw[hh] + gate(hh, 1) * o_slc[hh * Q_TILE:(hh + 1) * Q_TILE]
        o_ref[0, :, hh * HEAD_DIM:(hh + 1) * HEAD_DIM] = o_h.astype(o_ref.dtype)


N_ATTN_IN = 14


def _nsa_attention(proj, k_cmp, v_cmp, wb, sb, cf, to_cast):
    b, s, _ = proj.shape
    n_steps = b * N_KV * (s // Q_TILE)
    cast_specs = []
    for w in to_cast:
        slab = next(r for r in range(BF16_SUBLANES, w.shape[0] + 1, BF16_SUBLANES)
                    if w.shape[0] % r == 0 and w.shape[0] // r <= n_steps)
        last = w.shape[0] // slab - 1
        cast_specs.append(pl.BlockSpec(
            (slab, w.shape[1]),
            lambda i, g, j, last=last: (jnp.minimum((i * N_KV + g) * (s // Q_TILE) + j, last), 0)))
    ovt = _overlap_matrix(s)
    eye = jnp.asarray(np.tile(np.eye(Q_TILE), (HPG, 1)), BF)
    spread = jnp.asarray(np.repeat(np.eye(LANES)[:, :3 * HPG], LANES, axis=1), BF)
    n_chunks = -(-(s // SLC_BLOCK) // SLC_CHUNK_BLOCKS)
    q_blk = Q_OFF // (HPG * HEAD_DIM)
    kv_blk = KC_OFF // HEAD_DIM
    ng_blk = NG_OFF // LANES
    lk = SLC_LOCAL_KEYS
    ncp = s // CMP_STRIDE

    def kv_spec(idx):
        return pl.BlockSpec((1, s, HEAD_DIM), lambda i, g, j: (i, 0, kv_blk + idx * N_KV + g))

    cmp_spec = pl.BlockSpec((1, ncp, HEAD_DIM), lambda i, g, j: (i * N_KV + g, 0, 0))
    outs = pl.pallas_call(
        functools.partial(_attn_kernel, seq=s, n_cast=len(to_cast)),
        grid=(b, N_KV, s // Q_TILE),
        in_specs=[pl.BlockSpec((1, Q_TILE, HPG * HEAD_DIM), lambda i, g, j: (i, j, q_blk + g)),
                  kv_spec(2), kv_spec(3), kv_spec(4), kv_spec(5),
                  cmp_spec, cmp_spec,
                  pl.BlockSpec((1, Q_TILE, LANES), lambda i, g, j: (i, j, ng_blk + g)),
                  pl.BlockSpec((HPG, Q_TILE, WIN_KEYS), lambda i, g, j: (g, 0, 0)),
                  pl.BlockSpec((HPG, Q_TILE, lk), lambda i, g, j: (g, 0, 0)),
                  pl.BlockSpec((HPG, Q_TILE, LANES), lambda i, g, j: (g, 0, 0)),
                  pl.BlockSpec(ovt.shape, lambda i, g, j: (0, 0)),
                  pl.BlockSpec(eye.shape, lambda i, g, j: (0, 0)),
                  pl.BlockSpec(spread.shape, lambda i, g, j: (0, 0))] + cast_specs,
        out_specs=[pl.BlockSpec((1, Q_TILE, HPG * HEAD_DIM), lambda i, g, j: (i, j, g))] + cast_specs,
        out_shape=[jax.ShapeDtypeStruct((b, s, Q_WIDTH), BF)]
        + [jax.ShapeDtypeStruct(w.shape, BF) for w in to_cast],
        scratch_shapes=[pltpu.VMEM((SLC_PAD + s, 2 * HEAD_DIM), BF),
                        pltpu.VMEM((SLC_PAD + s, HEAD_DIM + LANES), BF),
                        pltpu.VMEM((WIN_PAD + s, HEAD_DIM), BF),
                        pltpu.VMEM((WIN_PAD + s, HEAD_DIM + LANES), BF),
                        pltpu.VMEM((s // SLC_BLOCK, Q_TILE), jnp.int32),
                        pltpu.VMEM((n_chunks, HPG * Q_TILE, SLC_CHUNK_BLOCKS * SLC_BLOCK), F32)],
        compiler_params=pltpu.CompilerParams(dimension_semantics=("arbitrary",) * 3,
                                             vmem_limit_bytes=ATTN_VMEM_LIMIT),
    )(proj, proj, proj, proj, proj, k_cmp, v_cmp, proj, wb, sb, cf, ovt, eye, spread, *to_cast)
    return outs[0], outs[1:]


def _overlap_matrix(s):
    ncp = s // CMP_STRIDE
    n_slc = s // SLC_BLOCK
    nc = (s - CMP_LEN) // CMP_STRIDE + 1
    i = np.arange(ncp)[None, :]
    jj = np.arange(n_slc)[:, None]
    ov = ((i * CMP_STRIDE < (jj + 1) * SLC_BLOCK) & (i * CMP_STRIDE + CMP_LEN > jj * SLC_BLOCK) & (i < nc))
    return jnp.asarray(ov, BF)


def kernel(x, c, w_ada, b_ada, g_pre_mix, g_post_mix, g_pre_ffn, g_post_ffn, w_in, w_dw, b_dw, conv_ln_g, conv_ln_b, w_conv_out, cmp_pe_k, cmp_pe_v, w_cmp_k1, w_cmp_k2, w_cmp_v1, w_cmp_v2, rel_bias, w_nsa_out, w_out, w_ffn_gate, w_ffn_up, w_ffn_down):
    b, s, d = x.shape
    m = b * s
    depth = w_ada.shape[0]
    c8 = jnp.zeros((8, d), F32).at[:b].set(c)
    wb, sb, cf = _bias_tables(rel_bias)
    col_scale = jnp.ones((1, NG_OFF + N_KV * LANES), F32).at[:, Q_OFF:KC_OFF].set(HEAD_DIM ** -0.5 * LOG2E)

    for l in range(depth):
        mod = _modulation(c8, w_ada[l], b_ada[l][None, :])
        mod3 = mod[:b].reshape(b, 6, d)

        w_all = w_in[l].astype(BF)
        w_ng = w_in[l][:, NG_OFF:MG_OFF].reshape(d, N_KV, 3 * HPG)
        w_ng = jnp.pad(w_ng, ((0, 0), (0, 0), (0, LANES - 3 * HPG))).reshape(d, N_KV * LANES)
        w_ng = w_ng.astype(BF)
        w_gates = w_all[:, MG_OFF:]

        h = _prenorm(x, g_pre_mix[l][None, :], mod3, 0, 1).reshape(m, d)
        proj = _matmul_with_tail(h, w_all, NG_OFF, w_ng, col_scale, BF, 2048, 512).reshape(b, s, -1)

        u_conv = _conv_module(proj, w_dw[l], b_dw[l][None, :], conv_ln_g[l][None, :], conv_ln_b[l][None, :])

        pe_k = jnp.broadcast_to(cmp_pe_k[l].reshape(1, -1), (8, CMP_LEN * HEAD_DIM)).astype(BF)
        pe_v = jnp.broadcast_to(cmp_pe_v[l].reshape(1, -1), (8, CMP_LEN * HEAD_DIM)).astype(BF)
        k_cmp = _compress(proj, KC_OFF // HEAD_DIM, pe_k, w_cmp_k1[l].astype(BF), w_cmp_k2[l].astype(BF))
        v_cmp = _compress(proj, (KC_OFF + KV_WIDTH) // HEAD_DIM, pe_v,
                          w_cmp_v1[l].astype(BF), w_cmp_v2[l].astype(BF))

        later = [w_conv_out[l], w_nsa_out[l], w_out[l], w_ffn_gate[l], w_ffn_up[l], w_ffn_down[l]]
        o_nsa, (wb_conv, wb_nsa, wb_out, wb_gate, wb_up, wb_down) = _nsa_attention(proj, k_cmp, v_cmp, wb, sb, cf, later)

        z = _gated_mix(h, u_conv.reshape(m, CONV_DIM), o_nsa.reshape(m, Q_WIDTH), wb_conv, wb_nsa, w_gates)
        mix = _matmul(z, wb_out, BF, 1024, 1024).reshape(b, s, d)
        x, h2 = _post_mix(mix, x, g_post_mix[l][None, :], g_pre_ffn[l][None, :], mod3, 2, 3, 4)

        ff = _ffn_up(h2.reshape(m, d), wb_gate, wb_up)
        f = _matmul(ff, wb_down, BF, 512, 512).reshape(b, s, d)
        x = _post_ffn(f, x, g_post_ffn[l][None, :], mod3, 5)
    return x
```
